```python
import math
import jax, jax.numpy as jnp
from jax import lax
import numpy as np

D_MODEL = 1024
BATCH = 4
SEQ = 4096
DEPTH = 2
DEC_BATCH = 128
DEC_SEQ = 8
PAST_LEN = 2048
PAGE_SIZE = 128

N_EVEN = (DEPTH + 1) // 2
N_ODD = DEPTH // 2

DIL_PAIRS = ((128, 1), (512, 4), (2048, 16))
N_DIL = 3
A_HEADS = 8
A_HEAD_DIM = 64
A_WIDTH = A_HEADS * A_HEAD_DIM
A_BLOCK = 128
NUM_BUCKETS = 32
MAX_DISTANCE = 2048
RNN_WIDTH = 512
RNN_BLOCKS = 8
RNN_BLOCK_W = RNN_WIDTH // RNN_BLOCKS
CONV_W = 4
RG_C = 8.0
EVEN_IN = N_DIL * 3 * A_WIDTH + 2 * RNN_WIDTH
C_QK_HEADS = 8
C_V_HEADS = 16
C_DK = 128
C_DV = 128
C_QK_W = C_QK_HEADS * C_DK
C_V_W = C_V_HEADS * C_DV
C_CONV_DIM = 2 * C_QK_W + C_V_W
C_CHUNK = 64
ODD_IN = C_CONV_DIM + C_V_W + 2 * C_V_HEADS
N_GROUPS = 4
EXPERTS_PER_GROUP = 8
TOP_K_IN_GROUP = 2
EXPERT_FF = 256
EPS = 1e-6
F32 = jnp.float32
NEG_INF = -1e30

kernel_name = 'hybrid_dilated_rglru_gdn_hmoe_step'


def _rmsnorm(x, g):
    xf = x.astype(F32)
    y = xf * lax.rsqrt(jnp.mean(xf * xf, axis=-1, keepdims=True) + EPS)
    return (y * g.astype(F32)).astype(x.dtype)


def _l2norm(x):
    return x * lax.rsqrt(jnp.sum(x * x, axis=-1, keepdims=True) + EPS)


def _t5_bucket(dist):
    max_exact = NUM_BUCKETS // 2
    d = np.maximum(dist, 1).astype(np.float32)
    large = max_exact + (np.log(d / max_exact) / np.log(MAX_DISTANCE / max_exact) * (NUM_BUCKETS - max_exact)).astype(np.int32)
    large = np.minimum(large, NUM_BUCKETS - 1)
    return np.where(dist < max_exact, dist, large).astype(np.int32)


def _causal_conv(buf_x, w, T):
    out = buf_x[:, 0:T] * w[0]
    for j in range(1, CONV_W):
        out = out + buf_x[:, j:j + T] * w[j]
    return out


def _dilated_attn_prompt(q, k, v, bias_tab, window, dil):
    B, S, H, Dh = q.shape
    span = window // dil
    L = S // dil
    nb = -(-L // A_BLOCK)
    Lp = nb * A_BLOCK
    nprev = -(-span // A_BLOCK)
    nk = (nprev + 1) * A_BLOCK

    def to_sub(t):
        t = t.reshape(B, L, dil, H, Dh).transpose(0, 2, 1, 3, 4)
        return jnp.pad(t, ((0, 0), (0, 0), (0, Lp - L), (0, 0), (0, 0)))

    def band(t):
        tp = jnp.pad(t, ((0, 0), (0, 0), (nprev * A_BLOCK, 0), (0, 0), (0, 0)))
        tb = tp.reshape(B, dil, nb + nprev, A_BLOCK, H, Dh)
        return jnp.concatenate([tb[:, :, p:p + nb] for p in range(nprev + 1)], axis=3)

    qb = to_sub(q).reshape(B, dil, nb, A_BLOCK, H, Dh)
    kb = band(to_sub(k))
    vb = band(to_sub(v))
    qi = np.arange(A_BLOCK)[:, None]
    km = np.arange(nk)[None, :]
    delta = nprev * A_BLOCK + qi - km
    key_pos = (np.arange(nb)[:, None, None] - nprev) * A_BLOCK + km[None]
    valid = ((delta >= 0) & (delta <= span))[None] & (key_pos >= 0)
    bias = jnp.take(bias_tab, _t5_bucket(np.clip(delta, 0, span) * dil), axis=0)
    s = jnp.einsum('brnqhd,brnkhd->brnhqk', qb, kb).astype(F32) * (Dh ** -0.5)
    s = s + jnp.transpose(bias, (2, 0, 1)).astype(F32)
    s = jnp.where(valid[:, None], s, NEG_INF)
    m = jnp.max(s, axis=-1, keepdims=True)
    p = jnp.exp(s - m)
    den = jnp.sum(p, axis=-1)
    o = jnp.einsum('brnhqk,brnkhd->brnqhd', p, vb.astype(F32))
    o = o / jnp.swapaxes(den, -1, -2)[..., None]
    lse = jnp.swapaxes(m[..., 0] + jnp.log(den), -1, -2)
    o = o.reshape(B, dil, Lp, H, Dh)[:, :, :L].transpose(0, 2, 1, 3, 4).reshape(B, S, H, Dh)
    lse = lse.reshape(B, dil, Lp, H)[:, :, :L].transpose(0, 2, 1, 3).reshape(B, S, H)
    return o, lse


def _dilated_attn_sample(q, k_ext, v_ext, bias_tab, window, dil):
    B, T, H, Dh = q.shape
    Lc = k_ext.shape[1] - T
    span = window // dil
    steps = np.arange(span + 1)
    idx = Lc + np.arange(T)[:, None] - dil * steps[None, :]
    valid = idx >= 0
    idx = np.maximum(idx, 0)
    kg = jnp.take(k_ext, idx, axis=1)
    vg = jnp.take(v_ext, idx, axis=1)
    bias = jnp.take(bias_tab, _t5_bucket(steps * dil), axis=0)
    s = jnp.einsum('bthd,btkhd->bhtk', q, kg).astype(F32) * (Dh ** -0.5)
    s = s + bias.T.astype(F32)[None, :, None, :]
    s = jnp.where(valid[None, None], s, NEG_INF)
    m = jnp.max(s, axis=-1, keepdims=True)
    p = jnp.exp(s - m)
    den = jnp.sum(p, axis=-1)
    o = jnp.einsum('bhtk,btkhd->bthd', p, vg.astype(F32)) / jnp.swapaxes(den, 1, 2)[..., None]
    lse = jnp.swapaxes(m[..., 0] + jnp.log(den), 1, 2)
    return o, lse


def _rg_lru(xc, wa, ba, wx, bx, lam, h0):
    N, T, R = xc.shape
    xf = xc.astype(F32)
    xb = xf.reshape(N, T, RNN_BLOCKS, RNN_BLOCK_W)
    r = jax.nn.sigmoid(jnp.einsum('ntbi,bij->ntbj', xb, wa.astype(F32)).reshape(N, T, R) + ba.astype(F32))
    ig = jax.nn.sigmoid(jnp.einsum('ntbi,bij->ntbj', xb, wx.astype(F32)).reshape(N, T, R) + bx.astype(F32))
    log_a = -RG_C * r * jax.nn.softplus(-lam.astype(F32))
    a = jnp.exp(log_a)
    b = jnp.sqrt(-jnp.expm1(2.0 * log_a)) * (ig * xf)
    b = b.at[:, 0].add(a[:, 0] * h0.astype(F32))

    def comb(left, right):
        al, bl = left
        ar, br = right
        return al * ar, ar * bl + br

    _, h = lax.associative_scan(comb, (a, b), axis=1)
    return h, h[:, -1]


def _even_mixer(h, w_in, conv_w, conv_b, rg_wa, rg_ba, rg_wx, rg_bx, rg_lambda, w_out, t5_bias, kv_bufs, h0, conv0):
    N, T, _ = h.shape
    proj = h @ w_in
    a_cols = N_DIL * 3 * A_WIDTH
    qkv = proj[..., :a_cols].reshape(N, T, N_DIL, 3, A_HEADS, A_HEAD_DIM)
    xb = proj[..., a_cols:a_cols + RNN_WIDTH]
    gb = proj[..., a_cols + RNN_WIDTH:]
    outs, lses, new_kv = [], [], []
    for g, (win, dil) in enumerate(DIL_PAIRS):
        q = qkv[:, :, g, 0]
        kv = qkv[:, :, g, 1:3]
        if kv_bufs is None:
            o, lse = _dilated_attn_prompt(q, kv[:, :, 0], kv[:, :, 1], t5_bias[:, g], win, dil)
            new_kv.append(kv[:, T - min(win, T):])
        else:
            ext = jnp.concatenate([kv_bufs[g].astype(kv.dtype), kv], axis=1)
            o, lse = _dilated_attn_sample(q, ext[:, :, 0], ext[:, :, 1], t5_bias[:, g], win, dil)
            new_kv.append(kv)
        outs.append(o)
        lses.append(lse)
    mix_w = jax.nn.softmax(jnp.stack(lses, axis=0), axis=0)
    o_a = jnp.einsum('gnth,gnthd->nthd', mix_w, jnp.stack(outs, axis=0)).reshape(N, T, A_WIDTH)
    conv_in = jnp.concatenate([conv0.astype(xb.dtype), xb], axis=1)
    xc = _causal_conv(conv_in, conv_w, T) + conv_b
    hs, h_last = _rg_lru(xc, rg_wa, rg_ba, rg_wx, rg_bx, rg_lambda, h0)
    y_b = hs * jax.nn.gelu(gb.astype(F32))
    mixed = jnp.concatenate([o_a, y_b], axis=-1).astype(h.dtype) @ w_out
    return mixed, new_kv, h_last, conv_in[:, -(CONV_W - 1):]


def _gated_delta(q, k, v, beta, g, S0):
    N, T, H, Dk = q.shape
    Dv = v.shape[-1]
    C = C_CHUNK
    nc = -(-T // C)
    pad = nc * C - T

    def chunks(t):
        t = jnp.pad(t.astype(F32), ((0, 0), (0, pad)) + ((0, 0),) * (t.ndim - 2))
        t = t.reshape((N, nc, C) + t.shape[2:])
        return jnp.moveaxis(jnp.moveaxis(t, 1, 0), 3, 2)

    qc, kc, vc, bc, gc = chunks(q), chunks(k), chunks(v), chunks(beta), chunks(g)
    G = jnp.cumsum(gc, axis=-1)
    incl = np.tril(np.ones((C, C), dtype=bool))
    strict = np.tril(np.ones((C, C), dtype=bool), -1)
    decay = jnp.exp(jnp.where(incl, G[..., :, None] - G[..., None, :], -jnp.inf))
    kk = jnp.einsum('zbhtd,zbhjd->zbhtj', kc, kc)
    A = jnp.eye(C, dtype=F32) + jnp.where(strict, bc[..., :, None] * kk * decay, 0.0)
    gam = jnp.exp(G)
    U0 = lax.linalg.triangular_solve(A, bc[..., None] * vc, left_side=True, lower=True, unit_diagonal=True)
    W = lax.linalg.triangular_solve(A, (bc * gam)[..., None] * kc, left_side=True, lower=True, unit_diagonal=True)
    qk = jnp.einsum('zbhtd,zbhjd->zbhtj', qc, kc) * decay
    qg = qc * gam[..., None]
    kend = kc * jnp.exp(G[..., -1:] - G)[..., None]
    gend = jnp.exp(G[..., -1])

    def step(S, xs):
        u0, w_, qk_, qg_, ke_, ge_ = xs
        u = u0 - jnp.einsum('bhtk,bhkv->bhtv', w_, S)
        o = jnp.einsum('bhtk,bhkv->bhtv', qg_, S) + jnp.einsum('bhtj,bhjv->bhtv', qk_, u)
        S = S * ge_[..., None, None] + jnp.einsum('bhjk,bhjv->bhkv', ke_, u)
        return S, o

    S_last, o = lax.scan(step, S0.astype(F32), (U0, W, qk, qg, kend, gend))
    o = jnp.moveaxis(o, 0, 1)
    o = jnp.swapaxes(o, 2, 3).reshape(N, nc * C, H, Dv)[:, :T]
    return o, S_last


def _odd_mixer(h, w_in, conv_w, a_log, dt_bias, onorm_w, w_out, S0, conv0):
    N, T, _ = h.shape
    proj = h @ w_in
    qkv = proj[..., :C_CONV_DIM]
    z = proj[..., C_CONV_DIM:C_CONV_DIM + C_V_W]
    b_raw = proj[..., C_CONV_DIM + C_V_W:C_CONV_DIM + C_V_W + C_V_HEADS]
    a_raw = proj[..., C_CONV_DIM + C_V_W + C_V_HEADS:]
    conv_in = jnp.concatenate([conv0.astype(qkv.dtype), qkv], axis=1)
    qkv = jax.nn.silu(_causal_conv(conv_in, conv_w, T).astype(F32))
    q = qkv[..., :C_QK_W].reshape(N, T, C_QK_HEADS, C_DK)
    k = qkv[..., C_QK_W:2 * C_QK_W].reshape(N, T, C_QK_HEADS, C_DK)
    v = qkv[..., 2 * C_QK_W:].reshape(N, T, C_V_HEADS, C_DV)
    rep = C_V_HEADS // C_QK_HEADS
    q = jnp.repeat(_l2norm(q) * (C_DK ** -0.5), rep, axis=2)
    k = jnp.repeat(_l2norm(k), rep, axis=2)
    beta = jax.nn.sigmoid(b_raw.astype(F32))
    g = -jnp.exp(a_log.astype(F32)) * jax.nn.softplus(a_raw.astype(F32) + dt_bias.astype(F32))
    o, S_last = _gated_delta(q, k, v, beta, g, S0)
    o = _rmsnorm(o, onorm_w) * jax.nn.silu(z.astype(F32).reshape(N, T, C_V_HEADS, C_DV))
    y = o.reshape(N, T, C_V_W).astype(h.dtype) @ w_out
    return y, S_last, conv_in[:, -(CONV_W - 1):]


def _hier_moe(h, rg_w, rg_b, re_w, re_b, w_gate, w_up, w_down):
    N, T, D = h.shape
    x = h.reshape(N * T, D)
    coarse = (x @ rg_w).astype(F32) + rg_b.astype(F32)
    grp = jnp.argmax(coarse, axis=-1)
    p_grp = jnp.max(jax.nn.softmax(coarse, axis=-1), axis=-1, keepdims=True)
    grp_oh = jax.nn.one_hot(grp, N_GROUPS, dtype=F32)
    fine = jnp.einsum('md,dge->mge', x, re_w).astype(F32) + re_b.astype(F32)
    fine = jnp.einsum('mge,mg->me', fine, grp_oh)
    top_v, top_i = lax.top_k(fine, TOP_K_IN_GROUP)
    top_w = jax.nn.softmax(top_v, axis=-1) * p_grp
    w_exp = jnp.einsum('mk,mke->me', top_w, jax.nn.one_hot(top_i, EXPERTS_PER_GROUP, dtype=F32))
    gate = grp_oh[:, :, None] * w_exp[:, None, :]
    y = jnp.zeros((N * T, D), F32)
    for gi in range(N_GROUPS):
        hg = jnp.einsum('md,edf->mef', x, w_gate[gi]).astype(F32)
        hu = jnp.einsum('md,edf->mef', x, w_up[gi]).astype(F32)
        act = (jax.nn.silu(hg) * hu * gate[:, gi, :, None]).astype(x.dtype)
        y = y + jnp.einsum('mef,efd->md', act, w_down[gi]).astype(F32)
    return y.astype(h.dtype).reshape(N, T, D)


def _trunk(x, w, st):
    prompt = st is None
    N, T, _ = x.shape
    new_a = [[] for _ in range(N_DIL)]
    new_bh, new_bconv, new_cS, new_cconv = [], [], [], []
    for layer in range(DEPTH):
        i = layer // 2
        h = _rmsnorm(x, w['norm_mix'][layer])
        if layer % 2 == 0:
            if prompt:
                bufs = None
                h0 = jnp.zeros((N, RNN_WIDTH), F32)
                c0 = jnp.zeros((N, CONV_W - 1, RNN_WIDTH), x.dtype)
            else:
                bufs = [st['a'][g][i] for g in range(N_DIL)]
                h0 = st['bh'][i]
                c0 = st['bconv'][i]
            mix, kvs, bh, bconv = _even_mixer(h, w['e_w_in'][i], w['e_conv_w'][i], w['e_conv_b'][i], w['e_rg_wa'][i], w['e_rg_ba'][i], w['e_rg_wx'][i], w['e_rg_bx'][i], w['e_rg_lambda'][i], w['e_w_out'][i], w['t5_bias'], bufs, h0, c0)
            for g in range(N_DIL):
                new_a[g].append(kvs[g])
            new_bh.append(bh)
            new_bconv.append(bconv)
        else:
            if prompt:
                S0 = jnp.zeros((N, C_V_HEADS, C_DK, C_DV), F32)
                c0 = jnp.zeros((N, CONV_W - 1, C_CONV_DIM), x.dtype)
            else:
                S0 = st['cS'][i]
                c0 = st['cconv'][i]
            mix, cS, cconv = _odd_mixer(h, w['o_w_in'][i], w['o_conv_w'][i], w['o_a_log'][i], w['o_dt_bias'][i], w['o_onorm_w'][i], w['o_w_out'][i], S0, c0)
            new_cS.append(cS)
            new_cconv.append(cconv)
        x = x + mix.astype(x.dtype)
        x = x + _hier_moe(_rmsnorm(x, w['norm_ffn'][layer]), w['moe_rg_w'][layer], w['moe_rg_b'][layer], w['moe_re_w'][layer], w['moe_re_b'][layer], w['moe_w_gate'][layer], w['moe_w_up'][layer], w['moe_w_down'][layer]).astype(x.dtype)
    y = _rmsnorm(x, w['norm_final'])
    a_out = [jnp.stack(lst, axis=0) for lst in new_a]
    return y, a_out, jnp.stack(new_bh, axis=0), jnp.stack(new_bconv, axis=0), jnp.stack(new_cS, axis=0), jnp.stack(new_cconv, axis=0)


def setup_inputs(seed: int = 0) -> dict:
    key = jax.random.key(seed)
    keys = iter(jax.random.split(key, 48))

    def nrm(shape, scale):
        return jax.random.normal(next(keys), shape, jnp.float32) * scale

    def unif(shape, lo, hi):
        return jax.random.uniform(next(keys), shape, jnp.float32, lo, hi)

    D = D_MODEL
    inp = {}
    inp['x_prompt'] = nrm((BATCH, SEQ, D), 1.0)
    inp['x_sample'] = nrm((DEC_BATCH, DEC_SEQ, D), 1.0)
    for g, (win, _) in enumerate(DIL_PAIRS):
        inp['cache_a_g%d_kv' % g] = nrm((N_EVEN, DEC_BATCH, min(win, PAST_LEN), 2, A_HEADS, A_HEAD_DIM), 1.0)
    inp['state_b_h'] = nrm((N_EVEN, DEC_BATCH, RNN_WIDTH), 0.5)
    inp['state_b_conv'] = nrm((N_EVEN, DEC_BATCH, CONV_W - 1, RNN_WIDTH), 1.0)
    inp['state_c_S'] = nrm((N_ODD, DEC_BATCH, C_V_HEADS, C_DK, C_DV), C_DK ** -0.5)
    inp['state_c_conv'] = nrm((N_ODD, DEC_BATCH, CONV_W - 1, C_CONV_DIM), 1.0)
    inp['t5_bias'] = nrm((NUM_BUCKETS, N_DIL, A_HEADS), 0.2)
    inp['norm_mix'] = 1.0 + nrm((DEPTH, D), 0.02)
    inp['norm_ffn'] = 1.0 + nrm((DEPTH, D), 0.02)
    inp['norm_final'] = 1.0 + nrm((D,), 0.02)
    inp['e_w_in'] = nrm((N_EVEN, D, EVEN_IN), D ** -0.5)
    inp['e_conv_w'] = nrm((N_EVEN, CONV_W, RNN_WIDTH), CONV_W ** -0.5)
    inp['e_conv_b'] = nrm((N_EVEN, RNN_WIDTH), 0.01)
    inp['e_rg_wa'] = nrm((N_EVEN, RNN_BLOCKS, RNN_BLOCK_W, RNN_BLOCK_W), RNN_BLOCK_W ** -0.5)
    inp['e_rg_ba'] = nrm((N_EVEN, RNN_WIDTH), 0.01)
    inp['e_rg_wx'] = nrm((N_EVEN, RNN_BLOCKS, RNN_BLOCK_W, RNN_BLOCK_W), RNN_BLOCK_W ** -0.5)
    inp['e_rg_bx'] = nrm((N_EVEN, RNN_WIDTH), 0.01)
    a_c = unif((N_EVEN, RNN_WIDTH), 0.9, 0.999)
    a_base = a_c ** (1.0 / RG_C)
    inp['e_rg_lambda'] = jnp.log(a_base) - jnp.log1p(-a_base)
    inp['e_w_out'] = nrm((N_EVEN, A_WIDTH + RNN_WIDTH, D), (A_WIDTH + RNN_WIDTH) ** -0.5)
    inp['o_w_in'] = nrm((N_ODD, D, ODD_IN), D ** -0.5)
    inp['o_conv_w'] = nrm((N_ODD, CONV_W, C_CONV_DIM), CONV_W ** -0.5)
    inp['o_a_log'] = jnp.log(unif((N_ODD, C_V_HEADS), 1.0, 16.0))
    dt = jnp.exp(unif((N_ODD, C_V_HEADS), math.log(1e-3), math.log(1e-1)))
    inp['o_dt_bias'] = dt + jnp.log(-jnp.expm1(-dt))
    inp['o_onorm_w'] = 1.0 + nrm((N_ODD, C_DV), 0.02)
    inp['o_w_out'] = nrm((N_ODD, C_V_W, D), C_V_W ** -0.5)
    inp['moe_rg_w'] = nrm((DEPTH, D, N_GROUPS), D ** -0.5)
    inp['moe_rg_b'] = nrm((DEPTH, N_GROUPS), 0.01)
    inp['moe_re_w'] = nrm((DEPTH, D, N_GROUPS, EXPERTS_PER_GROUP), D ** -0.5)
    inp['moe_re_b'] = nrm((DEPTH, N_GROUPS, EXPERTS_PER_GROUP), 0.01)
    inp['moe_w_gate'] = nrm((DEPTH, N_GROUPS, EXPERTS_PER_GROUP, D, EXPERT_FF), D ** -0.5)
    inp['moe_w_up'] = nrm((DEPTH, N_GROUPS, EXPERTS_PER_GROUP, D, EXPERT_FF), D ** -0.5)
    inp['moe_w_down'] = nrm((DEPTH, N_GROUPS, EXPERTS_PER_GROUP, EXPERT_FF, D), EXPERT_FF ** -0.5)
    return inp


def reference(x_prompt, x_sample, cache_a_g0_kv, cache_a_g1_kv, cache_a_g2_kv, state_b_h, state_b_conv, state_c_S, state_c_conv, t5_bias, norm_mix, norm_ffn, norm_final, e_w_in, e_conv_w, e_conv_b, e_rg_wa, e_rg_ba, e_rg_wx, e_rg_bx, e_rg_lambda, e_w_out, o_w_in, o_conv_w, o_a_log, o_dt_bias, o_onorm_w, o_w_out, moe_rg_w, moe_rg_b, moe_re_w, moe_re_b, moe_w_gate, moe_w_up, moe_w_down):
    w = dict(t5_bias=t5_bias, norm_mix=norm_mix, norm_ffn=norm_ffn, norm_final=norm_final,
             e_w_in=e_w_in, e_conv_w=e_conv_w, e_conv_b=e_conv_b, e_rg_wa=e_rg_wa, e_rg_ba=e_rg_ba,
             e_rg_wx=e_rg_wx, e_rg_bx=e_rg_bx, e_rg_lambda=e_rg_lambda, e_w_out=e_w_out,
             o_w_in=o_w_in, o_conv_w=o_conv_w, o_a_log=o_a_log, o_dt_bias=o_dt_bias, o_onorm_w=o_onorm_w, o_w_out=o_w_out,
             moe_rg_w=moe_rg_w, moe_rg_b=moe_rg_b, moe_re_w=moe_re_w, moe_re_b=moe_re_b,
             moe_w_gate=moe_w_gate, moe_w_up=moe_w_up, moe_w_down=moe_w_down)
    st = dict(a=(cache_a_g0_kv, cache_a_g1_kv, cache_a_g2_kv), bh=state_b_h, bconv=state_b_conv, cS=state_c_S, cconv=state_c_conv)
    y_prompt, a_p, new_b_h_prompt, new_b_conv_prompt, new_c_S_prompt, new_c_conv_prompt = _trunk(x_prompt, w, None)
    y_sample, a_s, new_b_h_sample, new_b_conv_sample, new_c_S_sample, new_c_conv_sample = _trunk(x_sample, w, st)
    new_a_g0_prompt, new_a_g1_prompt, new_a_g2_prompt = a_p
    new_a_g0_sample, new_a_g1_sample, new_a_g2_sample = a_s
    return (y_prompt, y_sample, new_a_g0_prompt, new_a_g0_sample, new_a_g1_prompt, new_a_g1_sample, new_a_g2_prompt, new_a_g2_sample, new_b_h_prompt, new_b_h_sample, new_b_conv_prompt, new_b_conv_sample, new_c_S_prompt, new_c_S_sample, new_c_conv_prompt, new_c_conv_sample)
```

```python
import functools
import math

import jax
import jax.numpy as jnp
import numpy as np
from jax import lax
from jax.experimental import pallas as pl
from jax.experimental.pallas import tpu as pltpu

F32 = jnp.float32
BF16 = jnp.bfloat16
EPS = 1e-6
NEG_INF = -1e30

D_MODEL = 1024
DIL_PAIRS = ((128, 1), (512, 4), (2048, 16))
N_DIL = 3
A_HEADS = 8
A_HEAD_DIM = 64
A_WIDTH = A_HEADS * A_HEAD_DIM
A_BLOCK = 128
SPAN = 128
NUM_BUCKETS = 32
MAX_DISTANCE = 2048
RNN_WIDTH = 512
RNN_BLOCKS = 8
CONV_W = 4
RG_C = 8.0
EVEN_IN = N_DIL * 3 * A_WIDTH + 2 * RNN_WIDTH
XB_COL = N_DIL * 3 * A_WIDTH
GB_COL = XB_COL + RNN_WIDTH
C_QK_HEADS = 8
C_V_HEADS = 16
C_DK = 128
C_DV = 128
C_QK_W = C_QK_HEADS * C_DK
C_V_W = C_V_HEADS * C_DV
C_CONV_DIM = 2 * C_QK_W + C_V_W
ODD_IN = C_CONV_DIM + C_V_W + 2 * C_V_HEADS
ODD_IN_PAD = 6272
Z_COL = C_CONV_DIM
BG_COL = C_CONV_DIM + C_V_W
GDN_BASE = 8
N_GROUPS = 4
EXPERTS_PER_GROUP = 8
N_EXPERTS = N_GROUPS * EXPERTS_PER_GROUP
EXPERT_FF = 256
ROUTE_LANES = 128
MOE_TILE = 256

VMEM_LIMIT = 56 * 1024 * 1024


def _cparams(n_grid):
    return pltpu.CompilerParams(dimension_semantics=("arbitrary",) * n_grid,
                                vmem_limit_bytes=VMEM_LIMIT)


def _rms(x, g):
    return x * lax.rsqrt(jnp.mean(x * x, axis=-1, keepdims=True) + EPS) * g


def _silu(x):
    return x * jax.nn.sigmoid(x)


def _softplus(x):
    return jnp.maximum(x, 0.0) + jnp.log1p(jnp.exp(-jnp.abs(x)))


def _dot(a, b):
    return jnp.dot(a, b, preferred_element_type=F32)


def _dot_nt(a, b):
    return lax.dot_general(a, b, (((1,), (1,)), ((), ())), preferred_element_type=F32)


def _dot_tn(a, b):
    return lax.dot_general(a, b, (((0,), (0,)), ((), ())), preferred_element_type=F32)


def _dot_f32(a, b):
    return jnp.dot(a, b, preferred_element_type=F32, precision=lax.Precision.HIGHEST)


def _row_tile(*counts):
    for t in (256, 128, 64, 32, 16, 8):
        if all(c % t == 0 for c in counts):
            return t
    raise ValueError("token counts must be multiples of 8")


def _norm_proj_body(*refs, col_chunk, n_add):
    x_ref = refs[0]
    add_refs = refs[1:1 + n_add]
    g_ref, w_ref = refs[1 + n_add:3 + n_add]
    outs = refs[3 + n_add:]
    x = x_ref[...]
    for a in add_refs:
        x = x + a[...]
    if n_add:
        outs[0][...] = x
    o_ref = outs[-1]
    hb = _rms(x, g_ref[...]).astype(BF16)
    for c0 in range(0, o_ref.shape[1], col_chunk):
        o_ref[:, c0:c0 + col_chunk] = _dot(hb, w_ref[:, c0:c0 + col_chunk])


def _norm_proj(x, adds, g, w, tm, col_chunk):
    m, d = x.shape
    n = w.shape[1]
    row = pl.BlockSpec((tm, d), lambda i: (i, 0))
    out_shape = [jax.ShapeDtypeStruct((m, n), F32)]
    out_specs = [pl.BlockSpec((tm, n), lambda i: (i, 0))]
    if adds:
        out_shape = [jax.ShapeDtypeStruct((m, d), F32)] + out_shape
        out_specs = [row] + out_specs
    return pl.pallas_call(
        functools.partial(_norm_proj_body, col_chunk=col_chunk, n_add=len(adds)),
        grid=(m // tm,),
        in_specs=[row] * (1 + len(adds)) + [pl.BlockSpec((1, d), lambda i: (0, 0)),
                                            pl.BlockSpec((d, n), lambda i: (0, 0))],
        out_specs=out_specs, out_shape=out_shape,
        compiler_params=_cparams(1), name="norm_proj",
    )(x, *adds, g.reshape(1, d), w)


def _t5_bucket(dist):
    max_exact = NUM_BUCKETS // 2
    d = np.maximum(dist, 1).astype(np.float32)
    large = max_exact + (np.log(d / max_exact) / np.log(MAX_DISTANCE / max_exact)
                         * (NUM_BUCKETS - max_exact)).astype(np.int32)
    large = np.minimum(large, NUM_BUCKETS - 1)
    return np.where(dist < max_exact, dist, large).astype(np.int32)


def _prompt_bias(tab, dil):
    qi = np.arange(A_BLOCK)[:, None]
    km = np.arange(2 * A_BLOCK)[None, :]
    delta = A_BLOCK + qi - km
    valid = (delta >= 0) & (delta <= SPAN)
    bias = jnp.take(tab, _t5_bucket(np.clip(delta, 0, SPAN) * dil), axis=0)
    bias = jnp.where(valid[..., None], bias.astype(F32), NEG_INF)
    return jnp.transpose(bias, (2, 0, 1))


def _attn_prompt_body(q_ref, kp_ref, ko_ref, vp_ref, vo_ref, bias_ref, o_ref, lse_ref):
    first = pl.program_id(2) == 0
    scale = A_HEAD_DIM ** -0.5
    q = q_ref[...].astype(BF16)
    k = jnp.concatenate([kp_ref[...], ko_ref[...]], axis=0).astype(BF16)
    v = jnp.concatenate([vp_ref[...], vo_ref[...]], axis=0).astype(BF16)
    km = lax.broadcasted_iota(jnp.int32, (1, 2 * A_BLOCK), 1)
    no_prev = jnp.logical_and(first, km < A_BLOCK)
    for h in range(A_HEADS):
        sl = slice(h * A_HEAD_DIM, (h + 1) * A_HEAD_DIM)
        s = _dot_nt(q[:, sl], k[:, sl]) * scale + bias_ref[h]
        s = jnp.where(no_prev, NEG_INF, s)
        m = jnp.max(s, axis=-1, keepdims=True)
        p = jnp.exp(s - m)
        den = jnp.sum(p, axis=-1, keepdims=True)
        o_ref[:, sl] = _dot(p.astype(BF16), v[:, sl]) / den
        lse_ref[:, sl] = jnp.broadcast_to(m + jnp.log(den), (A_BLOCK, A_HEAD_DIM))


def _attn_prompt(proj, tab, g, dil, n_batch, seq, m_total):
    sub_len = seq // dil
    nb = sub_len // A_BLOCK
    n_col = EVEN_IN // A_WIDTH
    view = proj.reshape(m_total // dil, dil * EVEN_IN)

    def spec(j, prev):
        def index(b, r, i):
            blk = jnp.maximum(i - 1, 0) if prev else i
            return (b * nb + blk, r * n_col + g * 3 + j)
        return pl.BlockSpec((A_BLOCK, A_WIDTH), index)

    out_spec = pl.BlockSpec((A_BLOCK, A_WIDTH), lambda b, r, i: (b * nb + i, r))
    out_sds = jax.ShapeDtypeStruct((m_total // dil, dil * A_WIDTH), F32)
    o, lse = pl.pallas_call(
        _attn_prompt_body,
        grid=(n_batch, dil, nb),
        in_specs=[spec(0, False), spec(1, True), spec(1, False), spec(2, True), spec(2, False),
                  pl.BlockSpec((A_HEADS, A_BLOCK, 2 * A_BLOCK), lambda b, r, i: (0, 0, 0))],
        out_specs=[out_spec, out_spec], out_shape=[out_sds, out_sds],
        compiler_params=_cparams(3), name="attn_prompt_g%d" % g,
    )(view, view, view, view, view, _prompt_bias(tab, dil))
    return o.reshape(m_total, A_WIDTH), lse.reshape(m_total, A_WIDTH)


def _sample_bias(tab, dil, cache_len, key_index, t_new):
    t = np.arange(t_new)[:, None]
    dist = cache_len + t - key_index[None, :]
    valid = (dist >= 0) & (dist % dil == 0) & (dist <= SPAN * dil)
    bias = jnp.take(tab, _t5_bucket(np.clip(dist, 0, SPAN * dil)), axis=0)
    bias = jnp.where(valid[..., None], bias.astype(F32), NEG_INF)
    return jnp.transpose(bias, (2, 0, 1)).reshape(A_HEADS * t_new, key_index.shape[0])


def _attn_sample_body(new_ref, c0_ref, c1_ref, c2_ref, b0_ref, b1_ref, b2_ref, bn_ref, *rest, t_new, n_res):
    out_refs = rest[2 * N_DIL:]
    scale = A_HEAD_DIM ** -0.5
    rows = A_HEADS * t_new
    head_of_row = lax.broadcasted_iota(jnp.int32, (rows, A_WIDTH), 0) // t_new
    head_of_lane = lax.broadcasted_iota(jnp.int32, (rows, A_WIDTH), 1) // A_HEAD_DIM
    own = head_of_row == head_of_lane
    caches = (c0_ref, c1_ref, c2_ref)
    biases = (b0_ref, b1_ref, b2_ref)
    kv_w = 2 * A_WIDTH
    for g in range(N_DIL):
        base = g * 3 * A_WIDTH
        q = new_ref[:, base:base + A_WIDTH]
        k_new = new_ref[:, base + A_WIDTH:base + 2 * A_WIDTH].astype(BF16)
        v_new = new_ref[:, base + 2 * A_WIDTH:base + 3 * A_WIDTH].astype(BF16)
        q_bd = jnp.where(own, jnp.concatenate([q] * A_HEADS, axis=0), 0.0).astype(BF16)
        pieces = n_res if g == 2 else 1
        n_keys = caches[g].shape[1]
        ks, vs, ss = [], [], []
        for r in range(pieces):
            kc = caches[g][0, :, r * kv_w:r * kv_w + A_WIDTH].astype(BF16)
            vc = caches[g][0, :, r * kv_w + A_WIDTH:(r + 1) * kv_w].astype(BF16)
            ks.append(kc)
            vs.append(vc)
            ss.append(_dot_nt(q_bd, kc) * scale + biases[g][:, r * n_keys:(r + 1) * n_keys])
        ss.append(_dot_nt(q_bd, k_new) * scale + bn_ref[g])
        vs.append(v_new)
        m = ss[0].max(axis=-1, keepdims=True)
        for s in ss[1:]:
            m = jnp.maximum(m, s.max(axis=-1, keepdims=True))
        den = jnp.zeros_like(m)
        acc = jnp.zeros((rows, A_WIDTH), F32)
        for s, v in zip(ss, vs):
            p = jnp.exp(s - m)
            den = den + jnp.sum(p, axis=-1, keepdims=True)
            acc = acc + _dot(p.astype(BF16), v)
        acc = jnp.where(own, acc / den, 0.0)
        lse = jnp.where(own, m + jnp.log(den), 0.0)
        o = acc[0:t_new]
        l = lse[0:t_new]
        for h in range(1, A_HEADS):
            o = o + acc[h * t_new:(h + 1) * t_new]
            l = l + lse[h * t_new:(h + 1) * t_new]
        out_refs[2 * g][...] = o
        out_refs[2 * g + 1][...] = l


def _attn_sample(proj, caches, t5_bias, prev_outs, n_seq, t_new, row0):
    cache_lens = [c.shape[1] for c in caches]
    dil2 = DIL_PAIRS[2][1]
    n_res = min(dil2, t_new)
    assert cache_lens[2] % dil2 == 0 and t_new <= dil2
    kv_w = 2 * A_WIDTH
    c2 = caches[2].reshape(n_seq, cache_lens[2] // dil2, dil2 * kv_w)
    idx2 = (np.arange(cache_lens[2] // dil2)[None, :] * dil2 + np.arange(n_res)[:, None]).reshape(-1)
    key_idx = [np.arange(cache_lens[0]), np.arange(cache_lens[1]), idx2]
    biases = [_sample_bias(t5_bias[:, g], DIL_PAIRS[g][1], cache_lens[g], key_idx[g], t_new) for g in range(N_DIL)]
    bias_new = jnp.stack([_sample_bias(t5_bias[:, g], DIL_PAIRS[g][1], cache_lens[g],
                                       cache_lens[g] + np.arange(t_new), t_new) for g in range(N_DIL)])
    rows = A_HEADS * t_new
    blk0 = row0 // t_new
    full = lambda a: pl.BlockSpec(a.shape, lambda b: (0,) * a.ndim)
    out_spec = pl.BlockSpec((t_new, A_WIDTH), lambda b: (blk0 + b, 0))
    n_prev = len(prev_outs)
    return pl.pallas_call(
        functools.partial(_attn_sample_body, t_new=t_new, n_res=n_res),
        grid=(n_seq,),
        in_specs=[pl.BlockSpec((t_new, N_DIL * 3 * A_WIDTH), lambda b: (blk0 + b, 0)),
                  pl.BlockSpec((1, cache_lens[0], kv_w), lambda b: (b, 0, 0)),
                  pl.BlockSpec((1, cache_lens[1], kv_w), lambda b: (b, 0, 0)),
                  pl.BlockSpec((1, cache_lens[2] // dil2, n_res * kv_w), lambda b: (b, 0, 0)),
                  full(biases[0]), full(biases[1]), full(biases[2]), full(bias_new)]
                 + [pl.BlockSpec(memory_space=pl.ANY)] * n_prev,
        out_specs=[out_spec] * n_prev,
        out_shape=[jax.ShapeDtypeStruct(a.shape, a.dtype) for a in prev_outs],
        input_output_aliases={8 + i: i for i in range(n_prev)},
        compiler_params=_cparams(1), name="attn_sample",
    )(proj, caches[0], caches[1], c2, *biases, bias_new, *prev_outs)


def _shift_rows(x, s, fill, axis):
    t = lax.broadcasted_iota(jnp.int32, x.shape, axis)
    return jnp.where(t >= s, pltpu.roll(x, s, axis), fill)


def _linear_scan(a, b, axis):
    n = a.shape[axis]
    s = 1
    while s < n:
        b = b + a * _shift_rows(b, s, 0.0, axis)
        a = a * _shift_rows(a, s, 1.0, axis)
        s *= 2
    return a, b


def _rglru_gates(xc, wa_ref, wx_ref, vec_ref):
    xcb = xc.astype(BF16)
    r = jax.nn.sigmoid(_dot(xcb, wa_ref[...]) + vec_ref[1:2])
    ig = jax.nn.sigmoid(_dot(xcb, wx_ref[...]) + vec_ref[2:3])
    log_a = -RG_C * r * _softplus(-vec_ref[3:4])
    a = jnp.exp(log_a)
    b = jnp.sqrt(1.0 - jnp.exp(2.0 * log_a)) * (ig * xc)
    return a, b


def _gelu(x):
    return 0.5 * x * (1.0 + jnp.tanh(math.sqrt(2.0 / math.pi) * (x + 0.044715 * (x * x * x))))


def _rglru_prompt_body(xb_ref, gb_ref, cw_ref, wa_ref, wx_ref, vec_ref, y_ref, hl_ref, tail_ref, h_ref):
    @pl.when(pl.program_id(1) == 0)
    def _():
        tail_ref[...] = jnp.zeros_like(tail_ref)
        h_ref[...] = jnp.zeros_like(h_ref)

    x = xb_ref[...]
    tt = x.shape[0]
    xe = jnp.concatenate([tail_ref[...], x], axis=0)
    xc = vec_ref[0:1] + x * cw_ref[CONV_W - 1:CONV_W]
    for j in range(1, CONV_W):
        xc = xc + pltpu.roll(xe, j, 0)[8:8 + tt] * cw_ref[CONV_W - 1 - j:CONV_W - j]
    tail_ref[...] = x[tt - 8:tt]
    a, b = _rglru_gates(xc, wa_ref, wx_ref, vec_ref)
    a_cum, h = _linear_scan(a, b, 0)
    h = h + a_cum * h_ref[...]
    h_ref[...] = h[tt - 1:tt]
    hl_ref[0] = h[tt - 1:tt]
    y_ref[...] = h * _gelu(gb_ref[...])


def _rglru_sample_body(xb_ref, gb_ref, c0_ref, h0_ref, cw_ref, wa_ref, wx_ref, vec_ref, prev_ref, y_ref, hl_ref, *, t_new):
    del prev_ref
    x = xb_ref[...]
    rows = x.shape[0]
    ns = rows // t_new
    x3 = x.reshape(ns, t_new, RNN_WIDTH)
    xe = jnp.concatenate([c0_ref[...], x3], axis=1)
    xc = vec_ref[0:1] + x3 * cw_ref[CONV_W - 1:CONV_W]
    for j in range(1, CONV_W):
        xc = xc + pltpu.roll(xe, j, 1)[:, 8:8 + t_new] * cw_ref[CONV_W - 1 - j:CONV_W - j]
    a, b = _rglru_gates(xc.reshape(rows, RNN_WIDTH), wa_ref, wx_ref, vec_ref)
    a_cum, h = _linear_scan(a.reshape(ns, t_new, RNN_WIDTH), b.reshape(ns, t_new, RNN_WIDTH), 1)
    h = h + a_cum * h0_ref[...][:, None, :]
    hl_ref[...] = h[:, t_new - 1, :]
    y_ref[...] = h.reshape(rows, RNN_WIDTH) * _gelu(gb_ref[...])


def _block_diag(w):
    nb, bi, bo = w.shape
    eye = jnp.eye(nb, dtype=w.dtype)
    return (w[:, :, None, :] * eye[:, None, :, None]).reshape(nb * bi, nb * bo)


def _rglru(proj, conv0, h0, conv_w, conv_b, wa, ba, wx, bx, lam, n_batch, seq, n_seq, t_new):
    m_total = proj.shape[0]
    mp = n_batch * seq
    wa_bd = _block_diag(wa).astype(BF16)
    wx_bd = _block_diag(wx).astype(BF16)
    vec = jnp.stack([conv_b, ba, bx, lam]).astype(F32)
    xcol = XB_COL // RNN_WIDTH
    gcol = GB_COL // RNN_WIDTH
    full2 = lambda a, nd: pl.BlockSpec(a.shape, lambda *_: (0,) * a.ndim)
    tt = _row_tile(seq)
    nt = seq // tt
    w_specs2 = [pl.BlockSpec(a.shape, lambda b, i: (0, 0)) for a in (conv_w, wa_bd, wx_bd, vec)]
    y, hl_p = pl.pallas_call(
        _rglru_prompt_body,
        grid=(n_batch, nt),
        in_specs=[pl.BlockSpec((tt, RNN_WIDTH), lambda b, i: (b * nt + i, xcol)),
                  pl.BlockSpec((tt, RNN_WIDTH), lambda b, i: (b * nt + i, gcol))] + w_specs2,
        out_specs=[pl.BlockSpec((tt, RNN_WIDTH), lambda b, i: (b * nt + i, 0)),
                   pl.BlockSpec((1, 1, RNN_WIDTH), lambda b, i: (b, 0, 0))],
        out_shape=[jax.ShapeDtypeStruct((m_total, RNN_WIDTH), F32),
                   jax.ShapeDtypeStruct((n_batch, 1, RNN_WIDTH), F32)],
        scratch_shapes=[pltpu.VMEM((8, RNN_WIDTH), F32), pltpu.VMEM((1, RNN_WIDTH), F32)],
        compiler_params=_cparams(2), name="rglru_prompt",
    )(proj, proj, conv_w, wa_bd, wx_bd, vec)
    del full2
    ts = 32 if n_seq % 32 == 0 else 8
    rows = ts * t_new
    blk0 = mp // rows
    c0p = jnp.pad(conv0, ((0, 0), (8 - (CONV_W - 1), 0), (0, 0)))
    w_specs1 = [pl.BlockSpec(a.shape, lambda i: (0, 0)) for a in (conv_w, wa_bd, wx_bd, vec)]
    y, hl_s = pl.pallas_call(
        functools.partial(_rglru_sample_body, t_new=t_new),
        grid=(n_seq // ts,),
        in_specs=[pl.BlockSpec((rows, RNN_WIDTH), lambda i: (blk0 + i, xcol)),
                  pl.BlockSpec((rows, RNN_WIDTH), lambda i: (blk0 + i, gcol)),
                  pl.BlockSpec((ts, 8, RNN_WIDTH), lambda i: (i, 0, 0)),
                  pl.BlockSpec((ts, RNN_WIDTH), lambda i: (i, 0))] + w_specs1
                 + [pl.BlockSpec(memory_space=pl.ANY)],
        out_specs=[pl.BlockSpec((rows, RNN_WIDTH), lambda i: (blk0 + i, 0)),
                   pl.BlockSpec((ts, RNN_WIDTH), lambda i: (i, 0))],
        out_shape=[jax.ShapeDtypeStruct((m_total, RNN_WIDTH), F32),
                   jax.ShapeDtypeStruct((n_seq, RNN_WIDTH), F32)],
        input_output_aliases={8: 0},
        compiler_params=_cparams(1), name="rglru_sample",
    )(proj, proj, c0p, h0, conv_w, wa_bd, wx_bd, vec, y)
    return y, hl_p.reshape(n_batch, RNN_WIDTH), hl_s


def _route(x1, gf_ref, wr_ref, br_ref, hn_ref, route_ref):
    hn = _rms(x1, gf_ref[...])
    hn_ref[...] = hn.astype(BF16)
    logits = _dot_f32(hn, wr_ref[...]) + br_ref[...]
    lane = lax.broadcasted_iota(jnp.int32, logits.shape, 1)
    is_coarse = lane < N_GROUPS
    coarse = jnp.where(is_coarse, logits, -jnp.inf)
    cmax = jnp.max(coarse, axis=-1, keepdims=True)
    grp = jnp.min(jnp.where(coarse == cmax, lane, ROUTE_LANES), axis=-1, keepdims=True)
    p_grp = 1.0 / jnp.sum(jnp.where(is_coarse, jnp.exp(logits - cmax), 0.0), axis=-1, keepdims=True)
    expert = lane - N_GROUPS
    in_grp = (lane >= N_GROUPS) & (expert < N_EXPERTS) & (expert // EXPERTS_PER_GROUP == grp)
    fine = jnp.where(in_grp, logits, -jnp.inf)
    v1 = jnp.max(fine, axis=-1, keepdims=True)
    i1 = jnp.min(jnp.where(fine == v1, lane, ROUTE_LANES), axis=-1, keepdims=True)
    fine2 = jnp.where(lane == i1, -jnp.inf, fine)
    v2 = jnp.max(fine2, axis=-1, keepdims=True)
    i2 = jnp.min(jnp.where(fine2 == v2, lane, ROUTE_LANES), axis=-1, keepdims=True)
    e2 = jnp.exp(v2 - v1)
    w1 = p_grp / (1.0 + e2)
    w2 = p_grp * e2 / (1.0 + e2)
    route = jnp.where(lane == 0, (i1 - N_GROUPS).astype(F32), 0.0)
    route = jnp.where(lane == 1, (i2 - N_GROUPS).astype(F32), route)
    route = jnp.where(lane == 2, w1, route)
    route = jnp.where(lane == 3, w2, route)
    route_ref[...] = route


def _even_out_body(o0, l0, o1, l1, o2, l2, yb_ref, x_ref, wo_ref, gf_ref, wr_ref, br_ref, x1_ref, hn_ref, route_ref):
    ls = (l0[...], l1[...], l2[...])
    mx = jnp.maximum(jnp.maximum(ls[0], ls[1]), ls[2])
    es = [jnp.exp(l - mx) for l in ls]
    tot = es[0] + es[1] + es[2]
    o_a = (es[0] * o0[...] + es[1] * o1[...] + es[2] * o2[...]) / tot
    cat = jnp.concatenate([o_a, yb_ref[...]], axis=-1).astype(BF16)
    x1 = x_ref[...] + _dot(cat, wo_ref[...])
    x1_ref[...] = x1
    _route(x1, gf_ref, wr_ref, br_ref, hn_ref, route_ref)


def _router_weights(rg_w, rg_b, re_w, re_b):
    d = rg_w.shape[0]
    wr = jnp.concatenate([rg_w, re_w.reshape(d, N_EXPERTS)], axis=1)
    br = jnp.concatenate([rg_b, re_b.reshape(N_EXPERTS)])
    pad = ROUTE_LANES - wr.shape[1]
    return jnp.pad(wr, ((0, 0), (0, pad))).astype(F32), jnp.pad(br, (0, pad)).reshape(1, ROUTE_LANES).astype(F32)


def _mix_out(body, row_inputs, x, consts, tm, name):
    m, d = x.shape
    const_spec = lambda a: pl.BlockSpec(a.shape, lambda i: (0,) * a.ndim)
    row_spec = lambda a: pl.BlockSpec((tm, a.shape[1]), lambda i: (i, 0))
    in_specs = []
    args = []
    for a in row_inputs:
        if isinstance(a, tuple):
            arr, width, col = a
            in_specs.append(pl.BlockSpec((tm, width), lambda i, col=col: (i, col)))
            args.append(arr)
        else:
            in_specs.append(row_spec(a))
            args.append(a)
    in_specs.append(row_spec(x))
    args.append(x)
    for c in consts:
        in_specs.append(const_spec(c))
        args.append(c)
    return pl.pallas_call(
        body, grid=(m // tm,), in_specs=in_specs,
        out_specs=[pl.BlockSpec((tm, d), lambda i: (i, 0)), pl.BlockSpec((tm, d), lambda i: (i, 0)),
                   pl.BlockSpec((tm, ROUTE_LANES), lambda i: (i, 0))],
        out_shape=[jax.ShapeDtypeStruct((m, d), F32), jax.ShapeDtypeStruct((m, d), BF16),
                   jax.ShapeDtypeStruct((m, ROUTE_LANES), F32)],
        compiler_params=_cparams(1), name=name,
    )(*args)


def _moe_ffn_body(te_ref, xs_ref, gw_ref, wg_ref, wu_ref, wd_ref, o_ref):
    del te_ref
    x = xs_ref[...]
    hg = _dot(x, wg_ref[0])
    hu = _dot(x, wu_ref[0])
    act = (_silu(hg) * hu * gw_ref[...]).astype(BF16)
    o_ref[...] = _dot(act, wd_ref[0])


def _moe(hn, route, w_gate, w_up, w_down):
    m, d = hn.shape
    tm = MOE_TILE
    e_idx = jnp.concatenate([route[:, 0], route[:, 1]]).astype(jnp.int32)
    gate_w = jnp.concatenate([route[:, 2], route[:, 3]])
    tok = jnp.concatenate([jnp.arange(m, dtype=jnp.int32)] * 2)
    n_assign = 2 * m
    n_slots = n_assign + N_EXPERTS * tm
    n_tiles = n_slots // tm
    order = jnp.argsort(e_idx, stable=True)
    e_sorted = e_idx[order]
    counts = jnp.zeros((N_EXPERTS,), jnp.int32).at[e_idx].add(1)
    padded = ((counts + tm - 1) // tm) * tm
    pad_end = jnp.cumsum(padded)
    pad_start = pad_end - padded
    start = jnp.cumsum(counts) - counts
    slot_sorted = pad_start[e_sorted] + jnp.arange(n_assign, dtype=jnp.int32) - start[e_sorted]
    slot_tok = jnp.zeros((n_slots,), jnp.int32).at[slot_sorted].set(tok[order])
    slot_w = jnp.zeros((n_slots,), F32).at[slot_sorted].set(gate_w[order])
    slot_of = jnp.zeros((n_assign,), jnp.int32).at[order].set(slot_sorted)
    tile_expert = jnp.searchsorted(pad_end, jnp.arange(n_tiles, dtype=jnp.int32) * tm, side="right")
    tile_expert = jnp.minimum(tile_expert, N_EXPERTS - 1).astype(jnp.int32)
    xs = jnp.take(hn, slot_tok, axis=0)
    ys = pl.pallas_call(
        _moe_ffn_body,
        grid_spec=pltpu.PrefetchScalarGridSpec(
            num_scalar_prefetch=1, grid=(n_tiles,),
            in_specs=[pl.BlockSpec((tm, d), lambda i, te: (i, 0)),
                      pl.BlockSpec((tm, 1), lambda i, te: (i, 0)),
                      pl.BlockSpec((1, d, EXPERT_FF), lambda i, te: (te[i], 0, 0)),
                      pl.BlockSpec((1, d, EXPERT_FF), lambda i, te: (te[i], 0, 0)),
                      pl.BlockSpec((1, EXPERT_FF, d), lambda i, te: (te[i], 0, 0))],
            out_specs=pl.BlockSpec((tm, d), lambda i, te: (i, 0))),
        out_shape=jax.ShapeDtypeStruct((n_slots, d), F32),
        compiler_params=_cparams(1), name="moe_ffn",
    )(tile_expert, xs, slot_w.reshape(n_slots, 1), w_gate, w_up, w_down)
    return jnp.take(ys, slot_of[:m], axis=0), jnp.take(ys, slot_of[m:], axis=0)


def _gdn_prep_math(xe, tt, cw_ref, bg, av_ref):
    acc = xe[8:8 + tt] * cw_ref[CONV_W - 1:CONV_W]
    for j in range(1, CONV_W):
        acc = acc + pltpu.roll(xe, j, 0)[8:8 + tt] * cw_ref[CONV_W - 1 - j:CONV_W - j]
    qkv = _silu(acc)
    outs = []
    for h in range(2 * C_QK_HEADS):
        xh = qkv[:, h * C_DK:(h + 1) * C_DK]
        xh = xh * lax.rsqrt(jnp.sum(xh * xh, axis=-1, keepdims=True) + EPS)
        if h < C_QK_HEADS:
            xh = xh * (C_DK ** -0.5)
        outs.append(xh)
    qk = jnp.concatenate(outs, axis=-1)
    v = qkv[:, 2 * C_QK_W:]
    lane = lax.broadcasted_iota(jnp.int32, bg.shape, 1)
    gdec = -jnp.exp(av_ref[0:1]) * _softplus(bg + av_ref[1:2])
    bgo = jnp.where(lane < C_V_HEADS, jax.nn.sigmoid(bg), gdec)
    return qk, v, bgo


def _gdn_prep_prompt_body(x_ref, bg_ref, cw_ref, av_ref, qk_ref, v_ref, bgo_ref, tail_ref):
    @pl.when(pl.program_id(1) == 0)
    def _():
        tail_ref[...] = jnp.zeros_like(tail_ref)

    x = x_ref[...]
    tt = x.shape[0]
    xe = jnp.concatenate([tail_ref[...], x], axis=0)
    tail_ref[...] = x[tt - 8:tt]
    qk, v, bgo = _gdn_prep_math(xe, tt, cw_ref, bg_ref[...], av_ref)
    qk_ref[...] = qk
    v_ref[...] = v
    bgo_ref[...] = bgo


def _gdn_prep_sample_body(x_ref, bg_ref, c0_ref, cw_ref, av_ref, p0, p1, p2, qk_ref, v_ref, bgo_ref, *, t_new):
    del p0, p1, p2
    xe = jnp.concatenate([c0_ref[0], x_ref[...]], axis=0)
    qk, v, bgo = _gdn_prep_math(xe, t_new, cw_ref, bg_ref[...], av_ref)
    qk_ref[...] = qk
    v_ref[...] = v
    bgo_ref[...] = bgo


def _gdn_prep(proj, conv0, conv_w, a_log, dt_bias, n_batch, seq, n_seq, t_new):
    m_total = proj.shape[0]
    mp = n_batch * seq
    av = jnp.zeros((2, 128), F32)
    av = av.at[0, C_V_HEADS:2 * C_V_HEADS].set(a_log).at[1, C_V_HEADS:2 * C_V_HEADS].set(dt_bias)
    bg_col = BG_COL // 128
    tt = min(_row_tile(seq), 128)
    nt = seq // tt
    out_shape = [jax.ShapeDtypeStruct((m_total, 2 * C_QK_W), F32),
                 jax.ShapeDtypeStruct((m_total, C_V_W), F32),
                 jax.ShapeDtypeStruct((m_total, 128), F32)]
    outs = pl.pallas_call(
        _gdn_prep_prompt_body,
        grid=(n_batch, nt),
        in_specs=[pl.BlockSpec((tt, C_CONV_DIM), lambda b, i: (b * nt + i, 0)),
                  pl.BlockSpec((tt, 128), lambda b, i: (b * nt + i, bg_col)),
                  pl.BlockSpec(conv_w.shape, lambda b, i: (0, 0)),
                  pl.BlockSpec(av.shape, lambda b, i: (0, 0))],
        out_specs=[pl.BlockSpec((tt, 2 * C_QK_W), lambda b, i: (b * nt + i, 0)),
                   pl.BlockSpec((tt, C_V_W), lambda b, i: (b * nt + i, 0)),
                   pl.BlockSpec((tt, 128), lambda b, i: (b * nt + i, 0))],
        out_shape=out_shape,
        scratch_shapes=[pltpu.VMEM((8, C_CONV_DIM), F32)],
        compiler_params=_cparams(2), name="gdn_prep_prompt",
    )(proj, proj, conv_w, av)
    blk0 = mp // t_new
    c0p = jnp.pad(conv0, ((0, 0), (8 - (CONV_W - 1), 0), (0, 0)))
    return pl.pallas_call(
        functools.partial(_gdn_prep_sample_body, t_new=t_new),
        grid=(n_seq,),
        in_specs=[pl.BlockSpec((t_new, C_CONV_DIM), lambda i: (blk0 + i, 0)),
                  pl.BlockSpec((t_new, 128), lambda i: (blk0 + i, bg_col)),
                  pl.BlockSpec((1, 8, C_CONV_DIM), lambda i: (i, 0, 0)),
                  pl.BlockSpec(conv_w.shape, lambda i: (0, 0)),
                  pl.BlockSpec(av.shape, lambda i: (0, 0))] + [pl.BlockSpec(memory_space=pl.ANY)] * 3,
        out_specs=[pl.BlockSpec((t_new, 2 * C_QK_W), lambda i: (blk0 + i, 0)),
                   pl.BlockSpec((t_new, C_V_W), lambda i: (blk0 + i, 0)),
                   pl.BlockSpec((t_new, 128), lambda i: (blk0 + i, 0))],
        out_shape=out_shape,
        input_output_aliases={5: 0, 6: 1, 7: 2},
        compiler_params=_cparams(1), name="gdn_prep_sample",
    )(proj, proj, c0p, conv_w, av, *outs)


def _unit_lower_inverse(l_bd, size):
    n = l_bd.shape[0]
    ri = lax.broadcasted_iota(jnp.int32, (n, n), 0)
    ci = lax.broadcasted_iota(jnp.int32, (n, n), 1)
    eye = jnp.where(ri == ci, 1.0, 0.0).astype(F32)
    x = jnp.where(ri // GDN_BASE == ci // GDN_BASE, -l_bd, 0.0)
    t = eye + x
    p = x
    s = 2
    while s < GDN_BASE:
        pb = p.astype(BF16)
        p = _dot(pb, pb)
        t = t + _dot(t.astype(BF16), p.astype(BF16))
        s *= 2
    s = GDN_BASE
    while s < size:
        off = jnp.where(((ri // s) % 2 == 1) & (ci // s == ri // s - 1), l_bd, 0.0)
        tb = t.astype(BF16)
        t = t - _dot(tb, _dot(off.astype(BF16), tb).astype(BF16))
        s *= 2
    return t


def _gdn_chunk_body(qk_ref, v_ref, bg_ref, nw_ref, *rest, chunk, pack, has_state):
    if has_state:
        s0_ref, _prev, o_ref, sout_ref, s_ref = rest
    else:
        o_ref, sout_ref, s_ref = rest
    c = chunk

    @pl.when(pl.program_id(1) == 0)
    def _():
        if has_state:
            s_ref[...] = s0_ref[0]
        else:
            s_ref[...] = jnp.zeros_like(s_ref)

    rep = C_V_HEADS // C_QK_HEADS
    beta = bg_ref[:, 0:C_V_HEADS]
    g = bg_ref[:, C_V_HEADS:2 * C_V_HEADS]
    ti = lax.broadcasted_iota(jnp.int32, (c, c), 0)
    tj = lax.broadcasted_iota(jnp.int32, (c, c), 1)
    tril = jnp.where(ti >= tj, 1.0, 0.0).astype(F32)
    gc = _dot_f32(tril, g)
    g_last = gc[c - 1:c]
    gam = jnp.exp(gc)
    kdec = jnp.exp(g_last - gc)
    g_end = jnp.exp(g_last)
    n = pack * c
    ri = lax.broadcasted_iota(jnp.int32, (n, n), 0)
    ci = lax.broadcasted_iota(jnp.int32, (n, n), 1)
    same = ri // c == ci // c
    incl = same & (ri >= ci)
    strict = same & (ri > ci)
    for p0 in range(0, C_V_HEADS, pack):
        heads = range(p0, p0 + pack)
        col = lambda a: jnp.concatenate([a[:, h:h + 1] for h in heads], axis=0)
        g_col = col(gc)
        g_row = jnp.sum(jnp.where(ri == ci, g_col, 0.0), axis=0, keepdims=True)
        b_col = col(beta)
        gam_col = col(gam)
        k_st = jnp.concatenate([qk_ref[:, C_QK_W + (h // rep) * C_DK:C_QK_W + (h // rep + 1) * C_DK] for h in heads], axis=0)
        q_st = jnp.concatenate([qk_ref[:, (h // rep) * C_DK:(h // rep + 1) * C_DK] for h in heads], axis=0)
        v_st = jnp.concatenate([v_ref[:, h * C_DV:(h + 1) * C_DV] for h in heads], axis=0)
        kb = k_st.astype(BF16)
        diff = g_col - g_row
        decay = jnp.exp(jnp.where(incl, diff, -jnp.inf))
        l_bd = jnp.where(strict, b_col * _dot_nt(kb, kb) * decay, 0.0)
        qk_bd = _dot_nt(q_st.astype(BF16), kb) * decay
        t_inv = _unit_lower_inverse(l_bd, c)
        rhs = jnp.concatenate([b_col * v_st, (b_col * gam_col) * k_st], axis=1).astype(BF16)
        uw = _dot(t_inv.astype(BF16), rhs)
        u0 = uw[:, 0:C_DV]
        w = uw[:, C_DV:]
        qg = q_st * gam_col
        kend = k_st * col(kdec)
        us = []
        qs = []
        for i, h in enumerate(heads):
            s_h = s_ref[h]
            rows = slice(i * c, (i + 1) * c)
            lhs = jnp.concatenate([w[rows], qg[rows]], axis=0).astype(BF16)
            ws = _dot(lhs, s_h.astype(BF16))
            us.append(u0[rows] - ws[0:c])
            qs.append(ws[c:])
        u_st = jnp.concatenate(us, axis=0)
        ub = u_st.astype(BF16)
        o_st = jnp.concatenate(qs, axis=0) + _dot(qk_bd.astype(BF16), ub)
        kendb = kend.astype(BF16)
        for i, h in enumerate(heads):
            rows = slice(i * c, (i + 1) * c)
            s_ref[h] = s_ref[h] * g_end[:, h:h + 1] + _dot_tn(kendb[rows], ub[rows])
            o_h = o_st[rows]
            o_ref[:, h * C_DV:(h + 1) * C_DV] = _rms(o_h, nw_ref[...])

    @pl.when(pl.program_id(1) == pl.num_programs(1) - 1)
    def _():
        sout_ref[0] = s_ref[...]


def _gdn_chunks(qk, v, bg, onorm_w, s0, n_batch, seq, n_seq, t_new, chunk):
    m_total = qk.shape[0]
    mp = n_batch * seq
    nc = seq // chunk
    nw = onorm_w.reshape(1, C_DV).astype(F32)
    state = (C_V_HEADS, C_DK, C_DV)
    o, s_p = pl.pallas_call(
        functools.partial(_gdn_chunk_body, chunk=chunk, pack=256 // chunk, has_state=False),
        grid=(n_batch, nc),
        in_specs=[pl.BlockSpec((chunk, 2 * C_QK_W), lambda b, i: (b * nc + i, 0)),
                  pl.BlockSpec((chunk, C_V_W), lambda b, i: (b * nc + i, 0)),
                  pl.BlockSpec((chunk, 128), lambda b, i: (b * nc + i, 0)),
                  pl.BlockSpec((1, C_DV), lambda b, i: (0, 0))],
        out_specs=[pl.BlockSpec((chunk, C_V_W), lambda b, i: (b * nc + i, 0)),
                   pl.BlockSpec((1,) + state, lambda b, i: (b, 0, 0, 0))],
        out_shape=[jax.ShapeDtypeStruct((m_total, C_V_W), F32),
                   jax.ShapeDtypeStruct((n_batch,) + state, F32)],
        scratch_shapes=[pltpu.VMEM(state, F32)],
        compiler_params=_cparams(2), name="gdn_prompt",
    )(qk, v, bg, nw)
    blk0 = mp // t_new
    o, s_s = pl.pallas_call(
        functools.partial(_gdn_chunk_body, chunk=t_new, pack=C_V_HEADS, has_state=True),
        grid=(n_seq, 1),
        in_specs=[pl.BlockSpec((t_new, 2 * C_QK_W), lambda b, i: (blk0 + b, 0)),
                  pl.BlockSpec((t_new, C_V_W), lambda b, i: (blk0 + b, 0)),
                  pl.BlockSpec((t_new, 128), lambda b, i: (blk0 + b, 0)),
                  pl.BlockSpec((1, C_DV), lambda b, i: (0, 0)),
                  pl.BlockSpec((1,) + state, lambda b, i: (b, 0, 0, 0)),
                  pl.BlockSpec(memory_space=pl.ANY)],
        out_specs=[pl.BlockSpec((t_new, C_V_W), lambda b, i: (blk0 + b, 0)),
                   pl.BlockSpec((1,) + state, lambda b, i: (b, 0, 0, 0))],
        out_shape=[jax.ShapeDtypeStruct((m_total, C_V_W), F32),
                   jax.ShapeDtypeStruct((n_seq,) + state, F32)],
        scratch_shapes=[pltpu.VMEM(state, F32)],
        input_output_aliases={5: 0},
        compiler_params=_cparams(2), name="gdn_sample",
    )(qk, v, bg, nw, s0, o)
    return o, s_p, s_s


def _odd_out_body(o_ref, z_ref, x_ref, wo_ref, gf_ref, wr_ref, br_ref, x1_ref, hn_ref, route_ref):
    y = (o_ref[...] * _silu(z_ref[...])).astype(BF16)
    x1 = x_ref[...] + _dot(y, wo_ref[...])
    x1_ref[...] = x1
    _route(x1, gf_ref, wr_ref, br_ref, hn_ref, route_ref)


def _final_body(x_ref, a_ref, b_ref, g_ref, y_ref):
    y_ref[...] = _rms(x_ref[...] + a_ref[...] + b_ref[...], g_ref[...])


def _final_norm(x, adds, g, tm, row0, n_rows):
    d = x.shape[1]
    blk0 = row0 // tm
    row = pl.BlockSpec((tm, d), lambda i: (blk0 + i, 0))
    return pl.pallas_call(
        _final_body, grid=(n_rows // tm,),
        in_specs=[row, row, row, pl.BlockSpec((1, d), lambda i: (0, 0))],
        out_specs=pl.BlockSpec((tm, d), lambda i: (i, 0)),
        out_shape=jax.ShapeDtypeStruct((n_rows, d), F32),
        compiler_params=_cparams(1), name="final_norm",
    )(x, *adds, g.reshape(1, d))


def kernel(x_prompt, x_sample, cache_a_g0_kv, cache_a_g1_kv, cache_a_g2_kv, state_b_h, state_b_conv, state_c_S, state_c_conv, t5_bias, norm_mix, norm_ffn, norm_final, e_w_in, e_conv_w, e_conv_b, e_rg_wa, e_rg_ba, e_rg_wx, e_rg_bx, e_rg_lambda, e_w_out, o_w_in, o_conv_w, o_a_log, o_dt_bias, o_onorm_w, o_w_out, moe_rg_w, moe_rg_b, moe_re_w, moe_re_b, moe_w_gate, moe_w_up, moe_w_down):
    n_batch, seq, d = x_prompt.shape
    n_seq, t_new, _ = x_sample.shape
    mp = n_batch * seq
    ms = n_seq * t_new
    m = mp + ms
    assert t_new == 8 and seq % (DIL_PAIRS[2][1] * A_BLOCK) == 0
    assert e_w_in.shape[0] == 1 and o_w_in.shape[0] == 1
    tm = _row_tile(mp, ms)
    x = jnp.concatenate([x_prompt.reshape(mp, d), x_sample.reshape(ms, d)], axis=0)

    def moe_weights(layer):
        shp = (N_EXPERTS, d, EXPERT_FF)
        return (moe_w_gate[layer].reshape(shp).astype(BF16), moe_w_up[layer].reshape(shp).astype(BF16),
                moe_w_down[layer].reshape(N_EXPERTS, EXPERT_FF, d).astype(BF16))

    (proj,) = _norm_proj(x, (), norm_mix[0], e_w_in[0].astype(BF16), tm, 512)
    attn = []
    for g, (_, dil) in enumerate(DIL_PAIRS):
        attn.extend(_attn_prompt(proj, t5_bias[:, g], g, dil, n_batch, seq, m))
    caches = [c[0].reshape(n_seq, c.shape[2], 2 * A_WIDTH) for c in (cache_a_g0_kv, cache_a_g1_kv, cache_a_g2_kv)]
    attn = _attn_sample(proj, caches, t5_bias, attn, n_seq, t_new, mp)
    y_b, bh_p, bh_s = _rglru(proj, state_b_conv[0], state_b_h[0], e_conv_w[0], e_conv_b[0], e_rg_wa[0], e_rg_ba[0],
                             e_rg_wx[0], e_rg_bx[0], e_rg_lambda[0], n_batch, seq, n_seq, t_new)
    wr, br = _router_weights(moe_rg_w[0], moe_rg_b[0], moe_re_w[0], moe_re_b[0])
    x1, hn, route = _mix_out(_even_out_body, list(attn) + [y_b], x,
                             [e_w_out[0].astype(BF16), norm_ffn[0].reshape(1, d), wr, br], tm, "even_out")
    y0, y1 = _moe(hn, route, *moe_weights(0))

    w_in1 = jnp.pad(o_w_in[0], ((0, 0), (0, ODD_IN_PAD - ODD_IN))).astype(BF16)
    x2, proj2 = _norm_proj(x1, (y0, y1), norm_mix[1], w_in1, tm, 896)
    qk, v, bg = _gdn_prep(proj2, state_c_conv[0], o_conv_w[0], o_a_log[0], o_dt_bias[0], n_batch, seq, n_seq, t_new)
    o_c, cs_p, cs_s = _gdn_chunks(qk, v, bg, o_onorm_w[0], state_c_S[0], n_batch, seq, n_seq, t_new, 64)
    wr, br = _router_weights(moe_rg_w[1], moe_rg_b[1], moe_re_w[1], moe_re_b[1])
    x3, hn, route = _mix_out(_odd_out_body, [o_c, (proj2, C_V_W, Z_COL // C_V_W)], x2,
                             [o_w_out[0].astype(BF16), norm_ffn[1].reshape(1, d), wr, br], tm, "odd_out")
    y0, y1 = _moe(hn, route, *moe_weights(1))
    y_p = _final_norm(x3, (y0, y1), norm_final, tm, 0, mp).reshape(n_batch, seq, d)
    y_s = _final_norm(x3, (y0, y1), norm_final, tm, mp, ms).reshape(n_seq, t_new, d)

    proj_p = proj[:mp].reshape(n_batch, seq, EVEN_IN)
    proj_s = proj[mp:].reshape(n_seq, t_new, EVEN_IN)
    new_a = []
    for g, (win, _) in enumerate(DIL_PAIRS):
        c0 = g * 3 * A_WIDTH + A_WIDTH
        keep = min(win, seq)
        new_a.append(proj_p[:, seq - keep:, c0:c0 + 2 * A_WIDTH].reshape(1, n_batch, keep, 2, A_HEADS, A_HEAD_DIM))
        new_a.append(proj_s[:, :, c0:c0 + 2 * A_WIDTH].reshape(1, n_seq, t_new, 2, A_HEADS, A_HEAD_DIM))
    keep = CONV_W - 1
    bconv_p = proj_p[:, seq - keep:, XB_COL:XB_COL + RNN_WIDTH][None]
    bconv_s = proj_s[:, t_new - keep:, XB_COL:XB_COL + RNN_WIDTH][None]
    cconv_p = proj2[:mp].reshape(n_batch, seq, ODD_IN_PAD)[:, seq - keep:, :C_CONV_DIM][None]
    cconv_s = proj2[mp:].reshape(n_seq, t_new, ODD_IN_PAD)[:, t_new - keep:, :C_CONV_DIM][None]
    return (y_p, y_s, *new_a, bh_p[None], bh_s[None], bconv_p, bconv_s, cs_p[None], cs_s[None], cconv_p, cconv_s)
```

```python
import functools
import math

import jax
import jax.numpy as jnp
import numpy as np
from jax import lax
from jax.experimental import pallas as pl
from jax.experimental.pallas import tpu as pltpu

F32 = jnp.float32
BF16 = jnp.bfloat16
EPS = 1e-6
NEG_INF = -1e30

D_MODEL = 1024
DIL_PAIRS = ((128, 1), (512, 4), (2048, 16))
N_DIL = 3
A_HEADS = 8
A_HEAD_DIM = 64
A_WIDTH = A_HEADS * A_HEAD_DIM
A_BLOCK = 128
SPAN = 128
NUM_BUCKETS = 32
MAX_DISTANCE = 2048
RNN_WIDTH = 512
RNN_BLOCKS = 8
CONV_W = 4
RG_C = 8.0
EVEN_IN = N_DIL * 3 * A_WIDTH + 2 * RNN_WIDTH
XB_COL = N_DIL * 3 * A_WIDTH
GB_COL = XB_COL + RNN_WIDTH
C_QK_HEADS = 8
C_V_HEADS = 16
C_DK = 128
C_DV = 128
C_QK_W = C_QK_HEADS * C_DK
C_V_W = C_V_HEADS * C_DV
C_CONV_DIM = 2 * C_QK_W + C_V_W
ODD_IN = C_CONV_DIM + C_V_W + 2 * C_V_HEADS
ODD_IN_PAD = 6272
Z_COL = C_CONV_DIM
BG_COL = C_CONV_DIM + C_V_W
GDN_BASE = 8
N_GROUPS = 4
EXPERTS_PER_GROUP = 8
N_EXPERTS = N_GROUPS * EXPERTS_PER_GROUP
EXPERT_FF = 256
ROUTE_LANES = 128
MOE_TILE = 256

LANES = 128
VMEM_LIMIT = 56 * 1024 * 1024


def _cparams(n_grid):
    return pltpu.CompilerParams(dimension_semantics=("arbitrary",) * n_grid,
                                vmem_limit_bytes=VMEM_LIMIT)


def _rms(x, g):
    return x * lax.rsqrt(jnp.mean(x * x, axis=-1, keepdims=True) + EPS) * g


def _silu(x):
    return x * jax.nn.sigmoid(x)


def _softplus(x):
    return jnp.maximum(x, 0.0) + jnp.log1p(jnp.exp(-jnp.abs(x)))


def _dot(a, b):
    return jnp.dot(a, b, preferred_element_type=F32)


def _dot_nt(a, b):
    return lax.dot_general(a, b, (((1,), (1,)), ((), ())), preferred_element_type=F32)


def _dot_tn(a, b):
    return lax.dot_general(a, b, (((0,), (0,)), ((), ())), preferred_element_type=F32)


def _dot_f32(a, b):
    return jnp.dot(a, b, preferred_element_type=F32, precision=lax.Precision.HIGHEST)


def _row_tile(*counts):
    for t in (256, 128, 64, 32, 16, 8):
        if all(c % t == 0 for c in counts):
            return t
    raise ValueError("token counts must be multiples of 8")


def _norm_proj_body(*refs, col_chunk, n_add):
    x_ref = refs[0]
    add_refs = refs[1:1 + n_add]
    g_ref, w_ref = refs[1 + n_add:3 + n_add]
    outs = refs[3 + n_add:]
    x = x_ref[...]
    for a in add_refs:
        x = x + a[...]
    if n_add:
        outs[0][...] = x
    o_ref = outs[-1]
    hb = _rms(x, g_ref[...]).astype(BF16)
    for c0 in range(0, o_ref.shape[1], col_chunk):
        o_ref[:, c0:c0 + col_chunk] = _dot(hb, w_ref[:, c0:c0 + col_chunk])


def _norm_proj(x, adds, g, w, tm, col_chunk):
    m, d = x.shape
    n = w.shape[1]
    row = pl.BlockSpec((tm, d), lambda i: (i, 0))
    out_shape = [jax.ShapeDtypeStruct((m, n), F32)]
    out_specs = [pl.BlockSpec((tm, n), lambda i: (i, 0))]
    if adds:
        out_shape = [jax.ShapeDtypeStruct((m, d), F32)] + out_shape
        out_specs = [row] + out_specs
    return pl.pallas_call(
        functools.partial(_norm_proj_body, col_chunk=col_chunk, n_add=len(adds)),
        grid=(m // tm,),
        in_specs=[row] * (1 + len(adds)) + [pl.BlockSpec((1, d), lambda i: (0, 0)),
                                            pl.BlockSpec((d, n), lambda i: (0, 0))],
        out_specs=out_specs, out_shape=out_shape,
        compiler_params=_cparams(1), name="norm_proj",
    )(x, *adds, g.reshape(1, d), w)


def _norm_proj_even_body(x_ref, g_ref, w_ref, o_ref, rm0_ref, rm1_ref, rm2_ref, lane_ref):
    tm = x_ref.shape[0]
    hb = _rms(x_ref[...], g_ref[...]).astype(BF16)
    rm_refs = (rm0_ref, rm1_ref, rm2_ref)
    for c in range(EVEN_IN // A_WIDTH):
        cols = slice(c * A_WIDTH, (c + 1) * A_WIDTH)
        res = _dot(hb, w_ref[:, cols])
        o_ref[:, cols] = res
        if c < 3 * N_DIL:
            g, j = divmod(c, 3)
            dil = DIL_PAIRS[g][1]
            if dil == 1:
                rm_refs[g][0, 0, :, j * A_WIDTH:(j + 1) * A_WIDTH] = res.astype(BF16)
            else:
                for t in range(A_WIDTH // LANES):
                    lane_ref[t] = res[:, t * LANES:(t + 1) * LANES]
                for r in range(dil):
                    part = [lane_ref[t, pl.ds(r, tm // dil, stride=dil), :] for t in range(A_WIDTH // LANES)]
                    rm_refs[g][0, r, :, j * A_WIDTH:(j + 1) * A_WIDTH] = jnp.concatenate(part, axis=1).astype(BF16)


def _norm_proj_even(x, g, w, tm, n_batch, seq):
    m, d = x.shape
    n = w.shape[1]
    tps = seq // tm
    npt = n_batch * tps

    def rm_index(i):
        return (jnp.where(i < npt, i // tps, n_batch), 0, jnp.where(i < npt, i % tps, i - npt), 0)

    rm_shapes = [jax.ShapeDtypeStruct((n_batch + 1, dil, seq // dil, 3 * A_WIDTH), BF16) for _, dil in DIL_PAIRS]
    rm_specs = [pl.BlockSpec((1, dil, tm // dil, 3 * A_WIDTH), rm_index) for _, dil in DIL_PAIRS]
    return pl.pallas_call(
        _norm_proj_even_body,
        grid=(m // tm,),
        in_specs=[pl.BlockSpec((tm, d), lambda i: (i, 0)), pl.BlockSpec((1, d), lambda i: (0, 0)),
                  pl.BlockSpec((d, n), lambda i: (0, 0))],
        out_specs=[pl.BlockSpec((tm, n), lambda i: (i, 0))] + rm_specs,
        out_shape=[jax.ShapeDtypeStruct((m, n), F32)] + rm_shapes,
        scratch_shapes=[pltpu.VMEM((A_WIDTH // LANES, tm, LANES), F32)],
        compiler_params=_cparams(1), name="norm_proj_even",
    )(x, g.reshape(1, d), w)


def _t5_bucket(dist):
    max_exact = NUM_BUCKETS // 2
    d = np.maximum(dist, 1).astype(np.float32)
    large = max_exact + (np.log(d / max_exact) / np.log(MAX_DISTANCE / max_exact)
                         * (NUM_BUCKETS - max_exact)).astype(np.int32)
    large = np.minimum(large, NUM_BUCKETS - 1)
    return np.where(dist < max_exact, dist, large).astype(np.int32)


def _prompt_bias(tab, dil):
    qi = np.arange(A_BLOCK)[:, None]
    km = np.arange(2 * A_BLOCK)[None, :]
    delta = A_BLOCK + qi - km
    valid = (delta >= 0) & (delta <= SPAN)
    bias = jnp.take(tab, _t5_bucket(np.clip(delta, 0, SPAN) * dil), axis=0)
    bias = jnp.where(valid[..., None], bias.astype(F32), NEG_INF)
    return jnp.transpose(bias, (2, 0, 1))


def _attn_prompt_body(q_ref, kp_ref, ko_ref, vp_ref, vo_ref, bias_ref, o_ref, lse_ref):
    first = pl.program_id(2) == 0
    scale = A_HEAD_DIM ** -0.5
    q = q_ref[0, 0]
    k = jnp.concatenate([kp_ref[0, 0], ko_ref[0, 0]], axis=0)
    v = jnp.concatenate([vp_ref[0, 0], vo_ref[0, 0]], axis=0)
    o_ref = o_ref.at[0, 0]
    lse_ref = lse_ref.at[0, 0]
    km = lax.broadcasted_iota(jnp.int32, (1, 2 * A_BLOCK), 1)
    no_prev = jnp.logical_and(first, km < A_BLOCK)
    for h in range(A_HEADS):
        sl = slice(h * A_HEAD_DIM, (h + 1) * A_HEAD_DIM)
        s = _dot_nt(q[:, sl], k[:, sl]) * scale + bias_ref[h]
        s = jnp.where(no_prev, NEG_INF, s)
        m = jnp.max(s, axis=-1, keepdims=True)
        p = jnp.exp(s - m)
        den = jnp.sum(p, axis=-1, keepdims=True)
        o_ref[:, sl] = _dot(p.astype(BF16), v[:, sl]) / den
        lse_ref[:, sl] = jnp.broadcast_to(m + jnp.log(den), (A_BLOCK, A_HEAD_DIM))


def _attn_prompt(qkv_rm, tab, g, dil, n_batch, seq):
    sub_len = seq // dil
    nb = sub_len // A_BLOCK

    def spec(j, prev):
        def index(b, r, i):
            return (b, r, jnp.maximum(i - 1, 0) if prev else i, j)
        return pl.BlockSpec((1, 1, A_BLOCK, A_WIDTH), index)

    out_spec = pl.BlockSpec((1, 1, A_BLOCK, A_WIDTH), lambda b, r, i: (b, r, i, 0))
    out_sds = jax.ShapeDtypeStruct((n_batch, dil, sub_len, A_WIDTH), F32)
    return pl.pallas_call(
        _attn_prompt_body,
        grid=(n_batch, dil, nb),
        in_specs=[spec(0, False), spec(1, True), spec(1, False), spec(2, True), spec(2, False),
                  pl.BlockSpec((A_HEADS, A_BLOCK, 2 * A_BLOCK), lambda b, r, i: (0, 0, 0))],
        out_specs=[out_spec, out_spec], out_shape=[out_sds, out_sds],
        compiler_params=_cparams(3), name="attn_prompt_g%d" % g,
    )(qkv_rm, qkv_rm, qkv_rm, qkv_rm, qkv_rm, _prompt_bias(tab, dil))


def _sample_bias(tab, dil, cache_len, key_index, t_new):
    t = np.arange(t_new)[:, None]
    dist = cache_len + t - key_index[None, :]
    valid = (dist >= 0) & (dist % dil == 0) & (dist <= SPAN * dil)
    bias = jnp.take(tab, _t5_bucket(np.clip(dist, 0, SPAN * dil)), axis=0)
    bias = jnp.where(valid[..., None], bias.astype(F32), NEG_INF)
    return jnp.transpose(bias, (2, 0, 1)).reshape(A_HEADS * t_new, key_index.shape[0])


def _attn_sample_body(new_ref, c0_ref, c1_ref, c2_ref, b0_ref, b1_ref, b2_ref, bn_ref, *out_refs, t_new, n_res):
    scale = A_HEAD_DIM ** -0.5
    rows = A_HEADS * t_new
    head_of_row = lax.broadcasted_iota(jnp.int32, (rows, A_WIDTH), 0) // t_new
    head_of_lane = lax.broadcasted_iota(jnp.int32, (rows, A_WIDTH), 1) // A_HEAD_DIM
    own = head_of_row == head_of_lane
    caches = (c0_ref, c1_ref, c2_ref)
    biases = (b0_ref, b1_ref, b2_ref)
    kv_w = 2 * A_WIDTH
    for g in range(N_DIL):
        base = g * 3 * A_WIDTH
        q = new_ref[:, base:base + A_WIDTH]
        k_new = new_ref[:, base + A_WIDTH:base + 2 * A_WIDTH].astype(BF16)
        v_new = new_ref[:, base + 2 * A_WIDTH:base + 3 * A_WIDTH].astype(BF16)
        q_bd = jnp.where(own, jnp.concatenate([q] * A_HEADS, axis=0), 0.0).astype(BF16)
        pieces = n_res if g == 2 else 1
        n_keys = caches[g].shape[1]
        ks, vs, ss = [], [], []
        for r in range(pieces):
            kc = caches[g][0, :, r * kv_w:r * kv_w + A_WIDTH].astype(BF16)
            vc = caches[g][0, :, r * kv_w + A_WIDTH:(r + 1) * kv_w].astype(BF16)
            ks.append(kc)
            vs.append(vc)
            ss.append(_dot_nt(q_bd, kc) * scale + biases[g][:, r * n_keys:(r + 1) * n_keys])
        ss.append(_dot_nt(q_bd, k_new) * scale + bn_ref[g])
        vs.append(v_new)
        m = ss[0].max(axis=-1, keepdims=True)
        for s in ss[1:]:
            m = jnp.maximum(m, s.max(axis=-1, keepdims=True))
        den = jnp.zeros_like(m)
        acc = jnp.zeros((rows, A_WIDTH), F32)
        for s, v in zip(ss, vs):
            p = jnp.exp(s - m)
            den = den + jnp.sum(p, axis=-1, keepdims=True)
            acc = acc + _dot(p.astype(BF16), v)
        acc = jnp.where(own, acc / den, 0.0)
        lse = jnp.where(own, m + jnp.log(den), 0.0)
        o = acc[0:t_new]
        l = lse[0:t_new]
        for h in range(1, A_HEADS):
            o = o + acc[h * t_new:(h + 1) * t_new]
            l = l + lse[h * t_new:(h + 1) * t_new]
        out_refs[2 * g][...] = o
        out_refs[2 * g + 1][...] = l


def _attn_sample(proj, caches, t5_bias, n_seq, t_new, row0):
    cache_lens = [c.shape[1] for c in caches]
    dil2 = DIL_PAIRS[2][1]
    n_res = min(dil2, t_new)
    assert cache_lens[2] % dil2 == 0 and t_new <= dil2
    kv_w = 2 * A_WIDTH
    c2 = caches[2].reshape(n_seq, cache_lens[2] // dil2, dil2 * kv_w)
    idx2 = (np.arange(cache_lens[2] // dil2)[None, :] * dil2 + np.arange(n_res)[:, None]).reshape(-1)
    key_idx = [np.arange(cache_lens[0]), np.arange(cache_lens[1]), idx2]
    biases = [_sample_bias(t5_bias[:, g], DIL_PAIRS[g][1], cache_lens[g], key_idx[g], t_new) for g in range(N_DIL)]
    bias_new = jnp.stack([_sample_bias(t5_bias[:, g], DIL_PAIRS[g][1], cache_lens[g],
                                       cache_lens[g] + np.arange(t_new), t_new) for g in range(N_DIL)])
    rows = A_HEADS * t_new
    blk0 = row0 // t_new
    full = lambda a: pl.BlockSpec(a.shape, lambda b: (0,) * a.ndim)
    out_spec = pl.BlockSpec((t_new, A_WIDTH), lambda b: (b, 0))
    return pl.pallas_call(
        functools.partial(_attn_sample_body, t_new=t_new, n_res=n_res),
        grid=(n_seq,),
        in_specs=[pl.BlockSpec((t_new, N_DIL * 3 * A_WIDTH), lambda b: (blk0 + b, 0)),
                  pl.BlockSpec((1, cache_lens[0], kv_w), lambda b: (b, 0, 0)),
                  pl.BlockSpec((1, cache_lens[1], kv_w), lambda b: (b, 0, 0)),
                  pl.BlockSpec((1, cache_lens[2] // dil2, n_res * kv_w), lambda b: (b, 0, 0)),
                  full(biases[0]), full(biases[1]), full(biases[2]), full(bias_new)],
        out_specs=[out_spec] * (2 * N_DIL),
        out_shape=[jax.ShapeDtypeStruct((n_seq * t_new, A_WIDTH), F32)] * (2 * N_DIL),
        compiler_params=_cparams(1), name="attn_sample",
    )(proj, caches[0], caches[1], c2, *biases, bias_new)


def _shift_rows(x, s, fill, axis):
    t = lax.broadcasted_iota(jnp.int32, x.shape, axis)
    return jnp.where(t >= s, pltpu.roll(x, s, axis), fill)


def _linear_scan(a, b, axis):
    n = a.shape[axis]
    s = 1
    while s < n:
        b = b + a * _shift_rows(b, s, 0.0, axis)
        a = a * _shift_rows(a, s, 1.0, axis)
        s *= 2
    return a, b


def _rglru_gates(xc, wa_ref, wx_ref, vec_ref):
    xcb = xc.astype(BF16)
    r = jax.nn.sigmoid(_dot(xcb, wa_ref[...]) + vec_ref[1:2])
    ig = jax.nn.sigmoid(_dot(xcb, wx_ref[...]) + vec_ref[2:3])
    log_a = -RG_C * r * _softplus(-vec_ref[3:4])
    a = jnp.exp(log_a)
    b = jnp.sqrt(1.0 - jnp.exp(2.0 * log_a)) * (ig * xc)
    return a, b


def _gelu(x):
    return 0.5 * x * (1.0 + jnp.tanh(math.sqrt(2.0 / math.pi) * (x + 0.044715 * (x * x * x))))


def _rglru_prompt_body(xb_ref, gb_ref, cw_ref, wa_ref, wx_ref, vec_ref, y_ref, hl_ref, tail_ref, h_ref):
    @pl.when(pl.program_id(1) == 0)
    def _():
        tail_ref[...] = jnp.zeros_like(tail_ref)
        h_ref[...] = jnp.zeros_like(h_ref)

    x = xb_ref[...]
    tt = x.shape[0]
    xe = jnp.concatenate([tail_ref[...], x], axis=0)
    xc = vec_ref[0:1] + x * cw_ref[CONV_W - 1:CONV_W]
    for j in range(1, CONV_W):
        xc = xc + pltpu.roll(xe, j, 0)[8:8 + tt] * cw_ref[CONV_W - 1 - j:CONV_W - j]
    tail_ref[...] = x[tt - 8:tt]
    a, b = _rglru_gates(xc, wa_ref, wx_ref, vec_ref)
    a_cum, h = _linear_scan(a, b, 0)
    h = h + a_cum * h_ref[...]
    h_ref[...] = h[tt - 1:tt]
    hl_ref[0] = h[tt - 1:tt]
    y_ref[...] = h * _gelu(gb_ref[...])


def _rglru_sample_body(xb_ref, gb_ref, c0_ref, h0_ref, cw_ref, wa_ref, wx_ref, vec_ref, prev_ref, y_ref, hl_ref, *, t_new):
    del prev_ref
    x = xb_ref[...]
    rows = x.shape[0]
    ns = rows // t_new
    x3 = x.reshape(ns, t_new, RNN_WIDTH)
    xe = jnp.concatenate([c0_ref[...], x3], axis=1)
    xc = vec_ref[0:1] + x3 * cw_ref[CONV_W - 1:CONV_W]
    for j in range(1, CONV_W):
        xc = xc + pltpu.roll(xe, j, 1)[:, 8:8 + t_new] * cw_ref[CONV_W - 1 - j:CONV_W - j]
    a, b = _rglru_gates(xc.reshape(rows, RNN_WIDTH), wa_ref, wx_ref, vec_ref)
    a_cum, h = _linear_scan(a.reshape(ns, t_new, RNN_WIDTH), b.reshape(ns, t_new, RNN_WIDTH), 1)
    h = h + a_cum * h0_ref[...][:, None, :]
    hl_ref[...] = h[:, t_new - 1, :]
    y_ref[...] = h.reshape(rows, RNN_WIDTH) * _gelu(gb_ref[...])


def _block_diag(w):
    nb, bi, bo = w.shape
    eye = jnp.eye(nb, dtype=w.dtype)
    return (w[:, :, None, :] * eye[:, None, :, None]).reshape(nb * bi, nb * bo)


def _rglru(proj, conv0, h0, conv_w, conv_b, wa, ba, wx, bx, lam, n_batch, seq, n_seq, t_new):
    m_total = proj.shape[0]
    mp = n_batch * seq
    wa_bd = _block_diag(wa).astype(BF16)
    wx_bd = _block_diag(wx).astype(BF16)
    vec = jnp.stack([conv_b, ba, bx, lam]).astype(F32)
    xcol = XB_COL // RNN_WIDTH
    gcol = GB_COL // RNN_WIDTH
    full2 = lambda a, nd: pl.BlockSpec(a.shape, lambda *_: (0,) * a.ndim)
    tt = _row_tile(seq)
    nt = seq // tt
    w_specs2 = [pl.BlockSpec(a.shape, lambda b, i: (0, 0)) for a in (conv_w, wa_bd, wx_bd, vec)]
    y, hl_p = pl.pallas_call(
        _rglru_prompt_body,
        grid=(n_batch, nt),
        in_specs=[pl.BlockSpec((tt, RNN_WIDTH), lambda b, i: (b * nt + i, xcol)),
                  pl.BlockSpec((tt, RNN_WIDTH), lambda b, i: (b * nt + i, gcol))] + w_specs2,
        out_specs=[pl.BlockSpec((tt, RNN_WIDTH), lambda b, i: (b * nt + i, 0)),
                   pl.BlockSpec((1, 1, RNN_WIDTH), lambda b, i: (b, 0, 0))],
        out_shape=[jax.ShapeDtypeStruct((m_total, RNN_WIDTH), F32),
                   jax.ShapeDtypeStruct((n_batch, 1, RNN_WIDTH), F32)],
        scratch_shapes=[pltpu.VMEM((8, RNN_WIDTH), F32), pltpu.VMEM((1, RNN_WIDTH), F32)],
        compiler_params=_cparams(2), name="rglru_prompt",
    )(proj, proj, conv_w, wa_bd, wx_bd, vec)
    del full2
    ts = 32 if n_seq % 32 == 0 else 8
    rows = ts * t_new
    blk0 = mp // rows
    c0p = jnp.pad(conv0, ((0, 0), (8 - (CONV_W - 1), 0), (0, 0)))
    w_specs1 = [pl.BlockSpec(a.shape, lambda i: (0, 0)) for a in (conv_w, wa_bd, wx_bd, vec)]
    y, hl_s = pl.pallas_call(
        functools.partial(_rglru_sample_body, t_new=t_new),
        grid=(n_seq // ts,),
        in_specs=[pl.BlockSpec((rows, RNN_WIDTH), lambda i: (blk0 + i, xcol)),
                  pl.BlockSpec((rows, RNN_WIDTH), lambda i: (blk0 + i, gcol)),
                  pl.BlockSpec((ts, 8, RNN_WIDTH), lambda i: (i, 0, 0)),
                  pl.BlockSpec((ts, RNN_WIDTH), lambda i: (i, 0))] + w_specs1
                 + [pl.BlockSpec(memory_space=pl.ANY)],
        out_specs=[pl.BlockSpec((rows, RNN_WIDTH), lambda i: (blk0 + i, 0)),
                   pl.BlockSpec((ts, RNN_WIDTH), lambda i: (i, 0))],
        out_shape=[jax.ShapeDtypeStruct((m_total, RNN_WIDTH), F32),
                   jax.ShapeDtypeStruct((n_seq, RNN_WIDTH), F32)],
        input_output_aliases={8: 0},
        compiler_params=_cparams(1), name="rglru_sample",
    )(proj, proj, c0p, h0, conv_w, wa_bd, wx_bd, vec, y)
    return y, hl_p.reshape(n_batch, RNN_WIDTH), hl_s


def _pack_bf16_pair(x):
    w = x.shape[1] // 2
    hi = lax.bitcast_convert_type(x[:, :w].astype(BF16).astype(F32), jnp.uint32)
    lo = lax.bitcast_convert_type(x[:, w:].astype(BF16).astype(F32), jnp.uint32)
    return hi | (lo >> 16)


def _unpack_bf16_pair(p):
    hi = lax.bitcast_convert_type(p & jnp.uint32(0xFFFF0000), F32).astype(BF16)
    lo = lax.bitcast_convert_type(p << 16, F32).astype(BF16)
    return hi, lo


def _route(x1, gf_ref, wr_ref, br_ref, hn_ref, route_ref, counts_ref, run_ref):
    @pl.when(pl.program_id(0) == 0)
    def _():
        run_ref[...] = jnp.zeros_like(run_ref)

    hn = _rms(x1, gf_ref[...])
    hn_ref[...] = _pack_bf16_pair(hn)
    logits = _dot_f32(hn, wr_ref[...]) + br_ref[...]
    lane = lax.broadcasted_iota(jnp.int32, logits.shape, 1)
    is_coarse = lane < N_GROUPS
    coarse = jnp.where(is_coarse, logits, -jnp.inf)
    cmax = jnp.max(coarse, axis=-1, keepdims=True)
    grp = jnp.min(jnp.where(coarse == cmax, lane, ROUTE_LANES), axis=-1, keepdims=True)
    p_grp = 1.0 / jnp.sum(jnp.where(is_coarse, jnp.exp(logits - cmax), 0.0), axis=-1, keepdims=True)
    expert = lane - N_GROUPS
    in_grp = (lane >= N_GROUPS) & (expert < N_EXPERTS) & (expert // EXPERTS_PER_GROUP == grp)
    fine = jnp.where(in_grp, logits, -jnp.inf)
    v1 = jnp.max(fine, axis=-1, keepdims=True)
    i1 = jnp.min(jnp.where(fine == v1, lane, ROUTE_LANES), axis=-1, keepdims=True)
    fine2 = jnp.where(lane == i1, -jnp.inf, fine)
    v2 = jnp.max(fine2, axis=-1, keepdims=True)
    i2 = jnp.min(jnp.where(fine2 == v2, lane, ROUTE_LANES), axis=-1, keepdims=True)
    e2 = jnp.exp(v2 - v1)
    w1 = p_grp / (1.0 + e2)
    w2 = p_grp * e2 / (1.0 + e2)
    tm = logits.shape[0]
    sel = jnp.where(lane == i1, 1.0, jnp.where(lane == i2, 1.0, 0.0))
    ri = lax.broadcasted_iota(jnp.int32, (tm, tm), 0)
    ci = lax.broadcasted_iota(jnp.int32, (tm, tm), 1)
    earlier = jnp.where(ri > ci, 1.0, 0.0).astype(BF16)
    before = _dot(earlier, sel.astype(BF16)) + run_ref[...]
    r1 = jnp.sum(jnp.where(lane == i1, before, 0.0), axis=-1, keepdims=True)
    r2 = jnp.sum(jnp.where(lane == i2, before, 0.0), axis=-1, keepdims=True)
    run_ref[...] = run_ref[...] + jnp.sum(sel, axis=0, keepdims=True)
    counts_ref[...] = run_ref[...]
    route = jnp.where(lane == 0, (i1 - N_GROUPS).astype(F32), 0.0)
    route = jnp.where(lane == 1, (i2 - N_GROUPS).astype(F32), route)
    route = jnp.where(lane == 2, w1, route)
    route = jnp.where(lane == 3, w2, route)
    route = jnp.where(lane == 4, r1, route)
    route = jnp.where(lane == 5, r2, route)
    route_ref[...] = route


def _merge_groups(os, ls):
    mx = jnp.maximum(jnp.maximum(ls[0], ls[1]), ls[2])
    es = [jnp.exp(l - mx) for l in ls]
    return (es[0] * os[0] + es[1] * os[1] + es[2] * os[2]) / (es[0] + es[1] + es[2])


def _even_out_body(o0, l0, o1, l1, o2, l2, so0, sl0, so1, sl1, so2, sl2, yb_ref, x_ref, wo_ref, gf_ref, wr_ref, br_ref,
                   x1_ref, hn_ref, route_ref, counts_ref, n1o, n1l, n2o, n2l, oa_ref, run_ref, *, n_prompt_tiles):
    i = pl.program_id(0)
    tm = x_ref.shape[0]

    @pl.when(i < n_prompt_tiles)
    def _():
        nat = []
        for src, dst, dil in ((o1, n1o, DIL_PAIRS[1][1]), (l1, n1l, DIL_PAIRS[1][1]),
                              (o2, n2o, DIL_PAIRS[2][1]), (l2, n2l, DIL_PAIRS[2][1])):
            for r in range(dil):
                blk = src[0, r]
                for t in range(A_WIDTH // LANES):
                    dst[t, pl.ds(r, tm // dil, stride=dil), :] = blk[:, t * LANES:(t + 1) * LANES]
            nat.append(jnp.concatenate([dst[t] for t in range(A_WIDTH // LANES)], axis=1))
        oa_ref[...] = _merge_groups((o0[0, 0], nat[0], nat[2]), (l0[0, 0], nat[1], nat[3]))

    @pl.when(i >= n_prompt_tiles)
    def _():
        oa_ref[...] = _merge_groups((so0[...], so1[...], so2[...]), (sl0[...], sl1[...], sl2[...]))

    cat = jnp.concatenate([oa_ref[...], yb_ref[...]], axis=-1).astype(BF16)
    x1 = x_ref[...] + _dot(cat, wo_ref[...])
    x1_ref[...] = x1
    _route(x1, gf_ref, wr_ref, br_ref, hn_ref, route_ref, counts_ref, run_ref)


def _router_weights(rg_w, rg_b, re_w, re_b):
    d = rg_w.shape[0]
    wr = jnp.concatenate([rg_w, re_w.reshape(d, N_EXPERTS)], axis=1)
    br = jnp.concatenate([rg_b, re_b.reshape(N_EXPERTS)])
    pad = ROUTE_LANES - wr.shape[1]
    return jnp.pad(wr, ((0, 0), (0, pad))).astype(F32), jnp.pad(br, (0, pad)).reshape(1, ROUTE_LANES).astype(F32)


def _mix_out_call(body, in_specs, args, consts, m, d, tm, scratch, name):
    const_spec = lambda a: pl.BlockSpec(a.shape, lambda i: (0,) * a.ndim)
    return pl.pallas_call(
        body, grid=(m // tm,), in_specs=list(in_specs) + [const_spec(c) for c in consts],
        out_specs=[pl.BlockSpec((tm, d), lambda i: (i, 0)), pl.BlockSpec((tm, d // 2), lambda i: (i, 0)),
                   pl.BlockSpec((tm, ROUTE_LANES), lambda i: (i, 0)), pl.BlockSpec((1, ROUTE_LANES), lambda i: (0, 0))],
        out_shape=[jax.ShapeDtypeStruct((m, d), F32), jax.ShapeDtypeStruct((m, d // 2), jnp.uint32),
                   jax.ShapeDtypeStruct((m, ROUTE_LANES), F32), jax.ShapeDtypeStruct((1, ROUTE_LANES), F32)],
        scratch_shapes=list(scratch) + [pltpu.VMEM((1, ROUTE_LANES), F32)],
        compiler_params=_cparams(1), name=name,
    )(*args, *consts)


def _even_out(attn_p, attn_s, y_b, x, consts, tm, n_batch, seq):
    m, d = x.shape
    tps = seq // tm
    npt = n_batch * tps
    in_specs = []
    for g, (_, dil) in enumerate(DIL_PAIRS):
        def index(i):
            return (jnp.minimum(i // tps, n_batch - 1), 0, jnp.where(i < npt, i % tps, 0), 0)
        in_specs += [pl.BlockSpec((1, dil, tm // dil, A_WIDTH), index)] * 2
    in_specs += [pl.BlockSpec((tm, A_WIDTH), lambda i: (jnp.maximum(i - npt, 0), 0))] * (2 * N_DIL)
    in_specs += [pl.BlockSpec((tm, RNN_WIDTH), lambda i: (i, 0)), pl.BlockSpec((tm, d), lambda i: (i, 0))]
    scratch = [pltpu.VMEM((A_WIDTH // LANES, tm, LANES), F32)] * 4 + [pltpu.VMEM((tm, A_WIDTH), F32)]
    return _mix_out_call(functools.partial(_even_out_body, n_prompt_tiles=npt), in_specs,
                         list(attn_p) + list(attn_s) + [y_b, x], consts, m, d, tm, scratch, "even_out")


def _odd_out(o_c, proj2, x, consts, tm):
    m, d = x.shape
    in_specs = [pl.BlockSpec((tm, C_V_W), lambda i: (i, 0)),
                pl.BlockSpec((tm, C_V_W), lambda i: (i, Z_COL // C_V_W)),
                pl.BlockSpec((tm, d), lambda i: (i, 0))]
    return _mix_out_call(_odd_out_body, in_specs, [o_c, proj2, x], consts, m, d, tm, [], "odd_out")


def _row_copy(src_ref, src_row, dst_ref, dst_row, sem):
    return pltpu.make_async_copy(src_ref.at[pl.ds(src_row, 1)], dst_ref.at[pl.ds(dst_row, 1)], sem)


def _moe_dispatch_body(s0_ref, s1_ref, fill_ref, used_ref, hn_ref, xs_ref, zero_ref, fill_sem, row_sem):
    i = pl.program_id(0)
    tm = hn_ref.shape[0]

    def fill_copy(e):
        return pltpu.make_async_copy(zero_ref, xs_ref.at[pl.ds(pl.multiple_of(fill_ref[e], MOE_TILE), MOE_TILE)], fill_sem)

    @pl.when(i == 0)
    def _():
        zero_ref[...] = jnp.zeros_like(zero_ref)
        for e in range(N_EXPERTS):
            @pl.when(used_ref[e] > 0)
            def _():
                fill_copy(e).start()
        for e in range(N_EXPERTS):
            @pl.when(used_ref[e] > 0)
            def _():
                fill_copy(e).wait()

    base = i * tm

    def scatter_row(j, carry):
        _row_copy(hn_ref, j, xs_ref, s0_ref[base + j], row_sem).start()
        _row_copy(hn_ref, j, xs_ref, s1_ref[base + j], row_sem).start()
        return carry

    lax.fori_loop(0, tm, scatter_row, 0, unroll=8)
    for _ in range(2):
        pltpu.make_async_copy(hn_ref, xs_ref.at[pl.ds(0, tm)], row_sem).wait()


def _moe_ffn_body(te_ref, nu_ref, xs_ref, wg_ref, wu_ref, wd_ref, o_ref):
    del te_ref

    @pl.when(pl.program_id(0) < nu_ref[0])
    def _():
        xa, xb = _unpack_bf16_pair(xs_ref[...])
        half = xa.shape[1]
        hg = _dot(xa, wg_ref[0, :half]) + _dot(xb, wg_ref[0, half:])
        hu = _dot(xa, wu_ref[0, :half]) + _dot(xb, wu_ref[0, half:])
        o_ref[...] = _dot((_silu(hg) * hu).astype(BF16), wd_ref[0])

    @pl.when(pl.program_id(0) >= nu_ref[0])
    def _():
        o_ref[...] = jnp.zeros_like(o_ref)


def _moe_combine_body(s0_ref, s1_ref, route_ref, ys_ref, y_ref, buf_ref, sem):
    tm = y_ref.shape[0]
    base = pl.program_id(0) * tm

    def gather_row(j, carry):
        _row_copy(ys_ref, s0_ref[base + j], buf_ref.at[0], j, sem).start()
        _row_copy(ys_ref, s1_ref[base + j], buf_ref.at[1], j, sem).start()
        return carry

    lax.fori_loop(0, tm, gather_row, 0, unroll=8)
    for k in range(2):
        pltpu.make_async_copy(ys_ref.at[pl.ds(0, tm)], buf_ref.at[k], sem).wait()
    y_ref[...] = route_ref[:, 2:3] * buf_ref[0] + route_ref[:, 3:4] * buf_ref[1]


def _moe(hn, route, counts, w_gate, w_up, w_down, tm):
    m = hn.shape[0]
    d = w_gate.shape[1]
    tile = MOE_TILE
    n_slots = 2 * m + N_EXPERTS * tile
    n_tiles = n_slots // tile
    cnt = counts[0, N_GROUPS:N_GROUPS + N_EXPERTS].astype(jnp.int32)
    padded = ((cnt + tile - 1) // tile) * tile
    pad_end = jnp.cumsum(padded)
    pad_start = pad_end - padded
    experts = jnp.arange(N_EXPERTS, dtype=jnp.int32)

    def slots(e, pos):
        start = jnp.sum(jnp.where(e.astype(jnp.int32)[:, None] == experts[None, :], pad_start[None, :], 0), axis=1)
        return (start + pos.astype(jnp.int32)).astype(jnp.int32)

    slot0 = slots(route[:, 0], route[:, 4])
    slot1 = slots(route[:, 1], route[:, 5])
    tile_start = jnp.arange(n_tiles, dtype=jnp.int32) * tile
    tile_expert = jnp.sum((tile_start[:, None] >= pad_end[None, :]).astype(jnp.int32), axis=1)
    tile_expert = jnp.minimum(tile_expert, N_EXPERTS - 1).astype(jnp.int32)
    n_used = (pad_end[-1:] // tile).astype(jnp.int32)
    fill_start = jnp.maximum(pad_end - tile, 0).astype(jnp.int32)

    xs = pl.pallas_call(
        _moe_dispatch_body,
        grid_spec=pltpu.PrefetchScalarGridSpec(
            num_scalar_prefetch=4, grid=(m // tm,),
            in_specs=[pl.BlockSpec((tm, d // 2), lambda i, *_: (i, 0))],
            out_specs=pl.BlockSpec(memory_space=pl.ANY),
            scratch_shapes=[pltpu.VMEM((tile, d // 2), jnp.uint32), pltpu.SemaphoreType.DMA(()),
                            pltpu.SemaphoreType.DMA(())]),
        out_shape=jax.ShapeDtypeStruct((n_slots, d // 2), jnp.uint32),
        compiler_params=_cparams(1), name="moe_dispatch",
    )(slot0, slot1, fill_start, cnt, hn)

    def used(i, nu):
        return jnp.minimum(i, nu[0] - 1)

    ys = pl.pallas_call(
        _moe_ffn_body,
        grid_spec=pltpu.PrefetchScalarGridSpec(
            num_scalar_prefetch=2, grid=(n_tiles,),
            in_specs=[pl.BlockSpec((tile, d // 2), lambda i, te, nu: (used(i, nu), 0)),
                      pl.BlockSpec((1, d, EXPERT_FF), lambda i, te, nu: (te[used(i, nu)], 0, 0)),
                      pl.BlockSpec((1, d, EXPERT_FF), lambda i, te, nu: (te[used(i, nu)], 0, 0)),
                      pl.BlockSpec((1, EXPERT_FF, d), lambda i, te, nu: (te[used(i, nu)], 0, 0))],
            out_specs=pl.BlockSpec((tile, d), lambda i, te, nu: (i, 0))),
        out_shape=jax.ShapeDtypeStruct((n_slots, d), F32),
        compiler_params=_cparams(1), name="moe_ffn",
    )(tile_expert, n_used, xs, w_gate, w_up, w_down)

    return pl.pallas_call(
        _moe_combine_body,
        grid_spec=pltpu.PrefetchScalarGridSpec(
            num_scalar_prefetch=2, grid=(m // tm,),
            in_specs=[pl.BlockSpec((tm, ROUTE_LANES), lambda i, *_: (i, 0)),
                      pl.BlockSpec(memory_space=pl.ANY)],
            out_specs=pl.BlockSpec((tm, d), lambda i, *_: (i, 0)),
            scratch_shapes=[pltpu.VMEM((2, tm, d), F32), pltpu.SemaphoreType.DMA(())]),
        out_shape=jax.ShapeDtypeStruct((m, d), F32),
        compiler_params=_cparams(1), name="moe_combine",
    )(slot0, slot1, route, ys)


def _gdn_prep_math(xe, tt, cw_ref, bg, av_ref):
    acc = xe[8:8 + tt] * cw_ref[CONV_W - 1:CONV_W]
    for j in range(1, CONV_W):
        acc = acc + pltpu.roll(xe, j, 0)[8:8 + tt] * cw_ref[CONV_W - 1 - j:CONV_W - j]
    qkv = _silu(acc)
    outs = []
    for h in range(2 * C_QK_HEADS):
        xh = qkv[:, h * C_DK:(h + 1) * C_DK]
        xh = xh * lax.rsqrt(jnp.sum(xh * xh, axis=-1, keepdims=True) + EPS)
        if h < C_QK_HEADS:
            xh = xh * (C_DK ** -0.5)
        outs.append(xh)
    qk = jnp.concatenate(outs, axis=-1)
    v = qkv[:, 2 * C_QK_W:]
    lane = lax.broadcasted_iota(jnp.int32, bg.shape, 1)
    gdec = -jnp.exp(av_ref[0:1]) * _softplus(bg + av_ref[1:2])
    bgo = jnp.where(lane < C_V_HEADS, jax.nn.sigmoid(bg), gdec)
    return qk, v, bgo


def _gdn_prep_prompt_body(x_ref, bg_ref, cw_ref, av_ref, qk_ref, v_ref, bgo_ref, tail_ref):
    @pl.when(pl.program_id(1) == 0)
    def _():
        tail_ref[...] = jnp.zeros_like(tail_ref)

    x = x_ref[...]
    tt = x.shape[0]
    xe = jnp.concatenate([tail_ref[...], x], axis=0)
    tail_ref[...] = x[tt - 8:tt]
    qk, v, bgo = _gdn_prep_math(xe, tt, cw_ref, bg_ref[...], av_ref)
    qk_ref[...] = qk
    v_ref[...] = v
    bgo_ref[...] = bgo


def _gdn_prep_sample_body(x_ref, bg_ref, c0_ref, cw_ref, av_ref, p0, p1, p2, qk_ref, v_ref, bgo_ref, *, t_new):
    del p0, p1, p2
    xe = jnp.concatenate([c0_ref[0], x_ref[...]], axis=0)
    qk, v, bgo = _gdn_prep_math(xe, t_new, cw_ref, bg_ref[...], av_ref)
    qk_ref[...] = qk
    v_ref[...] = v
    bgo_ref[...] = bgo


def _gdn_prep(proj, conv0, conv_w, a_log, dt_bias, n_batch, seq, n_seq, t_new):
    m_total = proj.shape[0]
    mp = n_batch * seq
    av = jnp.zeros((2, 128), F32)
    av = av.at[0, C_V_HEADS:2 * C_V_HEADS].set(a_log).at[1, C_V_HEADS:2 * C_V_HEADS].set(dt_bias)
    bg_col = BG_COL // 128
    tt = min(_row_tile(seq), 128)
    nt = seq // tt
    out_shape = [jax.ShapeDtypeStruct((m_total, 2 * C_QK_W), F32),
                 jax.ShapeDtypeStruct((m_total, C_V_W), F32),
                 jax.ShapeDtypeStruct((m_total, 128), F32)]
    outs = pl.pallas_call(
        _gdn_prep_prompt_body,
        grid=(n_batch, nt),
        in_specs=[pl.BlockSpec((tt, C_CONV_DIM), lambda b, i: (b * nt + i, 0)),
                  pl.BlockSpec((tt, 128), lambda b, i: (b * nt + i, bg_col)),
                  pl.BlockSpec(conv_w.shape, lambda b, i: (0, 0)),
                  pl.BlockSpec(av.shape, lambda b, i: (0, 0))],
        out_specs=[pl.BlockSpec((tt, 2 * C_QK_W), lambda b, i: (b * nt + i, 0)),
                   pl.BlockSpec((tt, C_V_W), lambda b, i: (b * nt + i, 0)),
                   pl.BlockSpec((tt, 128), lambda b, i: (b * nt + i, 0))],
        out_shape=out_shape,
        scratch_shapes=[pltpu.VMEM((8, C_CONV_DIM), F32)],
        compiler_params=_cparams(2), name="gdn_prep_prompt",
    )(proj, proj, conv_w, av)
    blk0 = mp // t_new
    c0p = jnp.pad(conv0, ((0, 0), (8 - (CONV_W - 1), 0), (0, 0)))
    return pl.pallas_call(
        functools.partial(_gdn_prep_sample_body, t_new=t_new),
        grid=(n_seq,),
        in_specs=[pl.BlockSpec((t_new, C_CONV_DIM), lambda i: (blk0 + i, 0)),
                  pl.BlockSpec((t_new, 128), lambda i: (blk0 + i, bg_col)),
                  pl.BlockSpec((1, 8, C_CONV_DIM), lambda i: (i, 0, 0)),
                  pl.BlockSpec(conv_w.shape, lambda i: (0, 0)),
                  pl.BlockSpec(av.shape, lambda i: (0, 0))] + [pl.BlockSpec(memory_space=pl.ANY)] * 3,
        out_specs=[pl.BlockSpec((t_new, 2 * C_QK_W), lambda i: (blk0 + i, 0)),
                   pl.BlockSpec((t_new, C_V_W), lambda i: (blk0 + i, 0)),
                   pl.BlockSpec((t_new, 128), lambda i: (blk0 + i, 0))],
        out_shape=out_shape,
        input_output_aliases={5: 0, 6: 1, 7: 2},
        compiler_params=_cparams(1), name="gdn_prep_sample",
    )(proj, proj, c0p, conv_w, av, *outs)


def _unit_lower_inverse(l_bd, size):
    n = l_bd.shape[0]
    ri = lax.broadcasted_iota(jnp.int32, (n, n), 0)
    ci = lax.broadcasted_iota(jnp.int32, (n, n), 1)
    eye = jnp.where(ri == ci, 1.0, 0.0).astype(F32)
    x = jnp.where(ri // GDN_BASE == ci // GDN_BASE, -l_bd, 0.0)
    t = eye + x
    p = x
    s = 2
    while s < GDN_BASE:
        pb = p.astype(BF16)
        p = _dot(pb, pb)
        t = t + _dot(t.astype(BF16), p.astype(BF16))
        s *= 2
    s = GDN_BASE
    while s < size:
        off = jnp.where(((ri // s) % 2 == 1) & (ci // s == ri // s - 1), l_bd, 0.0)
        tb = t.astype(BF16)
        t = t - _dot(tb, _dot(off.astype(BF16), tb).astype(BF16))
        s *= 2
    return t


def _gdn_chunk_body(qk_ref, v_ref, bg_ref, nw_ref, *rest, chunk, pack, has_state):
    if has_state:
        s0_ref, _prev, o_ref, sout_ref, s_ref = rest
    else:
        o_ref, sout_ref, s_ref = rest
    c = chunk

    @pl.when(pl.program_id(1) == 0)
    def _():
        if has_state:
            s_ref[...] = s0_ref[0]
        else:
            s_ref[...] = jnp.zeros_like(s_ref)

    rep = C_V_HEADS // C_QK_HEADS
    beta = bg_ref[:, 0:C_V_HEADS]
    g = bg_ref[:, C_V_HEADS:2 * C_V_HEADS]
    ti = lax.broadcasted_iota(jnp.int32, (c, c), 0)
    tj = lax.broadcasted_iota(jnp.int32, (c, c), 1)
    tril = jnp.where(ti >= tj, 1.0, 0.0).astype(F32)
    gc = _dot_f32(tril, g)
    g_last = gc[c - 1:c]
    gam = jnp.exp(gc)
    kdec = jnp.exp(g_last - gc)
    g_end = jnp.exp(g_last)
    n = pack * c
    ri = lax.broadcasted_iota(jnp.int32, (n, n), 0)
    ci = lax.broadcasted_iota(jnp.int32, (n, n), 1)
    same = ri // c == ci // c
    incl = same & (ri >= ci)
    strict = same & (ri > ci)
    for p0 in range(0, C_V_HEADS, pack):
        heads = range(p0, p0 + pack)
        col = lambda a: jnp.concatenate([a[:, h:h + 1] for h in heads], axis=0)
        g_col = col(gc)
        g_row = jnp.sum(jnp.where(ri == ci, g_col, 0.0), axis=0, keepdims=True)
        b_col = col(beta)
        gam_col = col(gam)
        k_st = jnp.concatenate([qk_ref[:, C_QK_W + (h // rep) * C_DK:C_QK_W + (h // rep + 1) * C_DK] for h in heads], axis=0)
        q_st = jnp.concatenate([qk_ref[:, (h // rep) * C_DK:(h // rep + 1) * C_DK] for h in heads], axis=0)
        v_st = jnp.concatenate([v_ref[:, h * C_DV:(h + 1) * C_DV] for h in heads], axis=0)
        kb = k_st.astype(BF16)
        diff = g_col - g_row
        decay = jnp.exp(jnp.where(incl, diff, -jnp.inf))
        l_bd = jnp.where(strict, b_col * _dot_nt(kb, kb) * decay, 0.0)
        qk_bd = _dot_nt(q_st.astype(BF16), kb) * decay
        t_inv = _unit_lower_inverse(l_bd, c)
        rhs = jnp.concatenate([b_col * v_st, (b_col * gam_col) * k_st], axis=1).astype(BF16)
        uw = _dot(t_inv.astype(BF16), rhs)
        u0 = uw[:, 0:C_DV]
        w = uw[:, C_DV:]
        qg = q_st * gam_col
        kend = k_st * col(kdec)
        us = []
        qs = []
        for i, h in enumerate(heads):
            s_h = s_ref[h]
            rows = slice(i * c, (i + 1) * c)
            lhs = jnp.concatenate([w[rows], qg[rows]], axis=0).astype(BF16)
            ws = _dot(lhs, s_h.astype(BF16))
            us.append(u0[rows] - ws[0:c])
            qs.append(ws[c:])
        u_st = jnp.concatenate(us, axis=0)
        ub = u_st.astype(BF16)
        o_st = jnp.concatenate(qs, axis=0) + _dot(qk_bd.astype(BF16), ub)
        kendb = kend.astype(BF16)
        for i, h in enumerate(heads):
            rows = slice(i * c, (i + 1) * c)
            s_ref[h] = s_ref[h] * g_end[:, h:h + 1] + _dot_tn(kendb[rows], ub[rows])
            o_h = o_st[rows]
            o_ref[:, h * C_DV:(h + 1) * C_DV] = _rms(o_h, nw_ref[...])

    @pl.when(pl.program_id(1) == pl.num_programs(1) - 1)
    def _():
        sout_ref[0] = s_ref[...]


def _gdn_chunks(qk, v, bg, onorm_w, s0, n_batch, seq, n_seq, t_new, chunk):
    m_total = qk.shape[0]
    mp = n_batch * seq
    nc = seq // chunk
    nw = onorm_w.reshape(1, C_DV).astype(F32)
    state = (C_V_HEADS, C_DK, C_DV)
    o, s_p = pl.pallas_call(
        functools.partial(_gdn_chunk_body, chunk=chunk, pack=256 // chunk, has_state=False),
        grid=(n_batch, nc),
        in_specs=[pl.BlockSpec((chunk, 2 * C_QK_W), lambda b, i: (b * nc + i, 0)),
                  pl.BlockSpec((chunk, C_V_W), lambda b, i: (b * nc + i, 0)),
                  pl.BlockSpec((chunk, 128), lambda b, i: (b * nc + i, 0)),
                  pl.BlockSpec((1, C_DV), lambda b, i: (0, 0))],
        out_specs=[pl.BlockSpec((chunk, C_V_W), lambda b, i: (b * nc + i, 0)),
                   pl.BlockSpec((1,) + state, lambda b, i: (b, 0, 0, 0))],
        out_shape=[jax.ShapeDtypeStruct((m_total, C_V_W), F32),
                   jax.ShapeDtypeStruct((n_batch,) + state, F32)],
        scratch_shapes=[pltpu.VMEM(state, F32)],
        compiler_params=_cparams(2), name="gdn_prompt",
    )(qk, v, bg, nw)
    blk0 = mp // t_new
    o, s_s = pl.pallas_call(
        functools.partial(_gdn_chunk_body, chunk=t_new, pack=C_V_HEADS, has_state=True),
        grid=(n_seq, 1),
        in_specs=[pl.BlockSpec((t_new, 2 * C_QK_W), lambda b, i: (blk0 + b, 0)),
                  pl.BlockSpec((t_new, C_V_W), lambda b, i: (blk0 + b, 0)),
                  pl.BlockSpec((t_new, 128), lambda b, i: (blk0 + b, 0)),
                  pl.BlockSpec((1, C_DV), lambda b, i: (0, 0)),
                  pl.BlockSpec((1,) + state, lambda b, i: (b, 0, 0, 0)),
                  pl.BlockSpec(memory_space=pl.ANY)],
        out_specs=[pl.BlockSpec((t_new, C_V_W), lambda b, i: (blk0 + b, 0)),
                   pl.BlockSpec((1,) + state, lambda b, i: (b, 0, 0, 0))],
        out_shape=[jax.ShapeDtypeStruct((m_total, C_V_W), F32),
                   jax.ShapeDtypeStruct((n_seq,) + state, F32)],
        scratch_shapes=[pltpu.VMEM(state, F32)],
        input_output_aliases={5: 0},
        compiler_params=_cparams(2), name="gdn_sample",
    )(qk, v, bg, nw, s0, o)
    return o, s_p, s_s


def _odd_out_body(o_ref, z_ref, x_ref, wo_ref, gf_ref, wr_ref, br_ref, x1_ref, hn_ref, route_ref, counts_ref, run_ref):
    y = (o_ref[...] * _silu(z_ref[...])).astype(BF16)
    x1 = x_ref[...] + _dot(y, wo_ref[...])
    x1_ref[...] = x1
    _route(x1, gf_ref, wr_ref, br_ref, hn_ref, route_ref, counts_ref, run_ref)


def _final_body(x_ref, a_ref, g_ref, y_ref):
    y_ref[...] = _rms(x_ref[...] + a_ref[...], g_ref[...])


def _final_norm(x, add, g, tm, row0, n_rows):
    d = x.shape[1]
    blk0 = row0 // tm
    row = pl.BlockSpec((tm, d), lambda i: (blk0 + i, 0))
    return pl.pallas_call(
        _final_body, grid=(n_rows // tm,),
        in_specs=[row, row, pl.BlockSpec((1, d), lambda i: (0, 0))],
        out_specs=pl.BlockSpec((tm, d), lambda i: (i, 0)),
        out_shape=jax.ShapeDtypeStruct((n_rows, d), F32),
        compiler_params=_cparams(1), name="final_norm",
    )(x, add, g.reshape(1, d))


def kernel(x_prompt, x_sample, cache_a_g0_kv, cache_a_g1_kv, cache_a_g2_kv, state_b_h, state_b_conv, state_c_S, state_c_conv, t5_bias, norm_mix, norm_ffn, norm_final, e_w_in, e_conv_w, e_conv_b, e_rg_wa, e_rg_ba, e_rg_wx, e_rg_bx, e_rg_lambda, e_w_out, o_w_in, o_conv_w, o_a_log, o_dt_bias, o_onorm_w, o_w_out, moe_rg_w, moe_rg_b, moe_re_w, moe_re_b, moe_w_gate, moe_w_up, moe_w_down):
    n_batch, seq, d = x_prompt.shape
    n_seq, t_new, _ = x_sample.shape
    mp = n_batch * seq
    ms = n_seq * t_new
    m = mp + ms
    assert t_new == 8 and seq % (DIL_PAIRS[2][1] * A_BLOCK) == 0
    assert e_w_in.shape[0] == 1 and o_w_in.shape[0] == 1
    tm = _row_tile(mp, ms)
    x = jnp.concatenate([x_prompt.reshape(mp, d), x_sample.reshape(ms, d)], axis=0)

    def moe_weights(layer):
        shp = (N_EXPERTS, d, EXPERT_FF)
        return (moe_w_gate[layer].reshape(shp).astype(BF16), moe_w_up[layer].reshape(shp).astype(BF16),
                moe_w_down[layer].reshape(N_EXPERTS, EXPERT_FF, d).astype(BF16))

    proj, *qkv_rm = _norm_proj_even(x, norm_mix[0], e_w_in[0].astype(BF16), tm, n_batch, seq)
    attn_p = []
    for g, (_, dil) in enumerate(DIL_PAIRS):
        attn_p.extend(_attn_prompt(qkv_rm[g], t5_bias[:, g], g, dil, n_batch, seq))
    caches = [c[0].reshape(n_seq, c.shape[2], 2 * A_WIDTH) for c in (cache_a_g0_kv, cache_a_g1_kv, cache_a_g2_kv)]
    attn_s = _attn_sample(proj, caches, t5_bias, n_seq, t_new, mp)
    y_b, bh_p, bh_s = _rglru(proj, state_b_conv[0], state_b_h[0], e_conv_w[0], e_conv_b[0], e_rg_wa[0], e_rg_ba[0],
                             e_rg_wx[0], e_rg_bx[0], e_rg_lambda[0], n_batch, seq, n_seq, t_new)
    wr, br = _router_weights(moe_rg_w[0], moe_rg_b[0], moe_re_w[0], moe_re_b[0])
    x1, hn, route, counts = _even_out(attn_p, attn_s, y_b, x, [e_w_out[0].astype(BF16), norm_ffn[0].reshape(1, d), wr, br],
                                      tm, n_batch, seq)
    y_moe = _moe(hn, route, counts, *moe_weights(0), tm)

    w_in1 = jnp.pad(o_w_in[0], ((0, 0), (0, ODD_IN_PAD - ODD_IN))).astype(BF16)
    x2, proj2 = _norm_proj(x1, (y_moe,), norm_mix[1], w_in1, tm, 896)
    qk, v, bg = _gdn_prep(proj2, state_c_conv[0], o_conv_w[0], o_a_log[0], o_dt_bias[0], n_batch, seq, n_seq, t_new)
    o_c, cs_p, cs_s = _gdn_chunks(qk, v, bg, o_onorm_w[0], state_c_S[0], n_batch, seq, n_seq, t_new, 64)
    wr, br = _router_weights(moe_rg_w[1], moe_rg_b[1], moe_re_w[1], moe_re_b[1])
    x3, hn, route, counts = _odd_out(o_c, proj2, x2, [o_w_out[0].astype(BF16), norm_ffn[1].reshape(1, d), wr, br], tm)
    y_moe = _moe(hn, route, counts, *moe_weights(1), tm)
    y_p = _final_norm(x3, y_moe, norm_final, tm, 0, mp).reshape(n_batch, seq, d)
    y_s = _final_norm(x3, y_moe, norm_final, tm, mp, ms).reshape(n_seq, t_new, d)

    proj_p = proj[:mp].reshape(n_batch, seq, EVEN_IN)
    proj_s = proj[mp:].reshape(n_seq, t_new, EVEN_IN)
    new_a = []
    for g, (win, _) in enumerate(DIL_PAIRS):
        c0 = g * 3 * A_WIDTH + A_WIDTH
        keep = min(win, seq)
        new_a.append(proj_p[:, seq - keep:, c0:c0 + 2 * A_WIDTH].reshape(1, n_batch, keep, 2, A_HEADS, A_HEAD_DIM))
        new_a.append(proj_s[:, :, c0:c0 + 2 * A_WIDTH].reshape(1, n_seq, t_new, 2, A_HEADS, A_HEAD_DIM))
    keep = CONV_W - 1
    bconv_p = proj_p[:, seq - keep:, XB_COL:XB_COL + RNN_WIDTH][None]
    bconv_s = proj_s[:, t_new - keep:, XB_COL:XB_COL + RNN_WIDTH][None]
    cconv_p = proj2[:mp].reshape(n_batch, seq, ODD_IN_PAD)[:, seq - keep:, :C_CONV_DIM][None]
    cconv_s = proj2[mp:].reshape(n_seq, t_new, ODD_IN_PAD)[:, t_new - keep:, :C_CONV_DIM][None]
    return (y_p, y_s, *new_a, bh_p[None], bh_s[None], bconv_p, bconv_s, cs_p[None], cs_s[None], cconv_p, cconv_s)
```

```python
import functools
import math

import jax
import jax.numpy as jnp
import numpy as np
from jax import lax
from jax.experimental import pallas as pl
from jax.experimental.pallas import tpu as pltpu

F32 = jnp.float32
BF16 = jnp.bfloat16
EPS = 1e-6
NEG_INF = -1e30

D_MODEL = 1024
DIL_PAIRS = ((128, 1), (512, 4), (2048, 16))
N_DIL = 3
A_HEADS = 8
A_HEAD_DIM = 64
A_WIDTH = A_HEADS * A_HEAD_DIM
A_BLOCK = 128
SPAN = 128
NUM_BUCKETS = 32
MAX_DISTANCE = 2048
RNN_WIDTH = 512
RNN_BLOCKS = 8
CONV_W = 4
RG_C = 8.0
EVEN_IN = N_DIL * 3 * A_WIDTH + 2 * RNN_WIDTH
XB_COL = N_DIL * 3 * A_WIDTH
GB_COL = XB_COL + RNN_WIDTH
C_QK_HEADS = 8
C_V_HEADS = 16
C_DK = 128
C_DV = 128
C_QK_W = C_QK_HEADS * C_DK
C_V_W = C_V_HEADS * C_DV
C_CONV_DIM = 2 * C_QK_W + C_V_W
ODD_IN = C_CONV_DIM + C_V_W + 2 * C_V_HEADS
ODD_IN_PAD = 6272
Z_COL = C_CONV_DIM
BG_COL = C_CONV_DIM + C_V_W
GDN_BASE = 8
N_GROUPS = 4
EXPERTS_PER_GROUP = 8
N_EXPERTS = N_GROUPS * EXPERTS_PER_GROUP
EXPERT_FF = 256
ROUTE_LANES = 128
MOE_TILE = 256

LANES = 128
VMEM_LIMIT = 56 * 1024 * 1024


def _cparams(n_grid):
    return pltpu.CompilerParams(dimension_semantics=("arbitrary",) * n_grid,
                                vmem_limit_bytes=VMEM_LIMIT)


def _rms(x, g):
    return x * lax.rsqrt(jnp.mean(x * x, axis=-1, keepdims=True) + EPS) * g


def _silu(x):
    return x * jax.nn.sigmoid(x)


def _softplus(x):
    return jnp.maximum(x, 0.0) + jnp.log1p(jnp.exp(-jnp.abs(x)))


def _dot(a, b):
    return jnp.dot(a, b, preferred_element_type=F32)


def _dot_nt(a, b):
    return lax.dot_general(a, b, (((1,), (1,)), ((), ())), preferred_element_type=F32)


def _dot_tn(a, b):
    return lax.dot_general(a, b, (((0,), (0,)), ((), ())), preferred_element_type=F32)


def _dot_f32(a, b):
    return jnp.dot(a, b, preferred_element_type=F32, precision=lax.Precision.HIGHEST)


def _row_tile(*counts):
    for t in (256, 128, 64, 32, 16, 8):
        if all(c % t == 0 for c in counts):
            return t
    raise ValueError("token counts must be multiples of 8")


def _norm_proj_body(*refs, col_chunk, n_add):
    x_ref = refs[0]
    add_refs = refs[1:1 + n_add]
    g_ref, w_ref = refs[1 + n_add:3 + n_add]
    outs = refs[3 + n_add:]
    x = x_ref[...]
    for a in add_refs:
        x = x + a[...]
    if n_add:
        outs[0][...] = x
    o_ref = outs[-1]
    hb = _rms(x, g_ref[...]).astype(BF16)
    for c0 in range(0, o_ref.shape[1], col_chunk):
        o_ref[:, c0:c0 + col_chunk] = _dot(hb, w_ref[:, c0:c0 + col_chunk])


def _norm_proj(x, adds, g, w, tm, col_chunk):
    m, d = x.shape
    n = w.shape[1]
    row = pl.BlockSpec((tm, d), lambda i: (i, 0))
    out_shape = [jax.ShapeDtypeStruct((m, n), F32)]
    out_specs = [pl.BlockSpec((tm, n), lambda i: (i, 0))]
    if adds:
        out_shape = [jax.ShapeDtypeStruct((m, d), F32)] + out_shape
        out_specs = [row] + out_specs
    return pl.pallas_call(
        functools.partial(_norm_proj_body, col_chunk=col_chunk, n_add=len(adds)),
        grid=(m // tm,),
        in_specs=[row] * (1 + len(adds)) + [pl.BlockSpec((1, d), lambda i: (0, 0)),
                                            pl.BlockSpec((d, n), lambda i: (0, 0))],
        out_specs=out_specs, out_shape=out_shape,
        compiler_params=_cparams(1), name="norm_proj",
    )(x, *adds, g.reshape(1, d), w)


def _norm_proj_even_body(x_ref, g_ref, w_ref, o_ref, rm0_ref, rm1_ref, rm2_ref, lane_ref):
    tm = x_ref.shape[0]
    hb = _rms(x_ref[...], g_ref[...]).astype(BF16)
    rm_refs = (rm0_ref, rm1_ref, rm2_ref)
    for c in range(EVEN_IN // A_WIDTH):
        cols = slice(c * A_WIDTH, (c + 1) * A_WIDTH)
        res = _dot(hb, w_ref[:, cols])
        o_ref[:, cols] = res
        if c < 3 * N_DIL:
            g, j = divmod(c, 3)
            dil = DIL_PAIRS[g][1]
            if dil == 1:
                rm_refs[g][0, 0, :, j * A_WIDTH:(j + 1) * A_WIDTH] = res.astype(BF16)
            else:
                for t in range(A_WIDTH // LANES):
                    lane_ref[t] = res[:, t * LANES:(t + 1) * LANES]
                for r in range(dil):
                    part = [lane_ref[t, pl.ds(r, tm // dil, stride=dil), :] for t in range(A_WIDTH // LANES)]
                    rm_refs[g][0, r, :, j * A_WIDTH:(j + 1) * A_WIDTH] = jnp.concatenate(part, axis=1).astype(BF16)


def _norm_proj_even(x, g, w, tm, n_batch, seq):
    m, d = x.shape
    n = w.shape[1]
    tps = seq // tm
    npt = n_batch * tps

    def rm_index(i):
        return (jnp.where(i < npt, i // tps, n_batch), 0, jnp.where(i < npt, i % tps, i - npt), 0)

    rm_shapes = [jax.ShapeDtypeStruct((n_batch + 1, dil, seq // dil, 3 * A_WIDTH), BF16) for _, dil in DIL_PAIRS]
    rm_specs = [pl.BlockSpec((1, dil, tm // dil, 3 * A_WIDTH), rm_index) for _, dil in DIL_PAIRS]
    return pl.pallas_call(
        _norm_proj_even_body,
        grid=(m // tm,),
        in_specs=[pl.BlockSpec((tm, d), lambda i: (i, 0)), pl.BlockSpec((1, d), lambda i: (0, 0)),
                  pl.BlockSpec((d, n), lambda i: (0, 0))],
        out_specs=[pl.BlockSpec((tm, n), lambda i: (i, 0))] + rm_specs,
        out_shape=[jax.ShapeDtypeStruct((m, n), F32)] + rm_shapes,
        scratch_shapes=[pltpu.VMEM((A_WIDTH // LANES, tm, LANES), F32)],
        compiler_params=_cparams(1), name="norm_proj_even",
    )(x, g.reshape(1, d), w)


def _t5_bucket(dist):
    max_exact = NUM_BUCKETS // 2
    d = np.maximum(dist, 1).astype(np.float32)
    large = max_exact + (np.log(d / max_exact) / np.log(MAX_DISTANCE / max_exact)
                         * (NUM_BUCKETS - max_exact)).astype(np.int32)
    large = np.minimum(large, NUM_BUCKETS - 1)
    return np.where(dist < max_exact, dist, large).astype(np.int32)


def _bucket_lookup(tab, buckets):
    onehot = jnp.asarray(buckets[..., None, None] == np.arange(NUM_BUCKETS)[:, None])
    return jnp.sum(jnp.where(onehot, tab.astype(F32), 0.0), axis=-2)


def _prompt_bias(tab, dil):
    qi = np.arange(A_BLOCK)[:, None]
    km = np.arange(2 * A_BLOCK)[None, :]
    delta = A_BLOCK + qi - km
    valid = (delta >= 0) & (delta <= SPAN)
    bias = _bucket_lookup(tab, _t5_bucket(np.clip(delta, 0, SPAN) * dil))
    bias = jnp.where(valid[..., None], bias, NEG_INF)
    return jnp.transpose(bias, (2, 0, 1))


def _attn_prompt_body(q_ref, kp_ref, ko_ref, vp_ref, vo_ref, bias_ref, o_ref, lse_ref):
    first = pl.program_id(2) == 0
    scale = A_HEAD_DIM ** -0.5
    q = q_ref[0, 0]
    k = jnp.concatenate([kp_ref[0, 0], ko_ref[0, 0]], axis=0)
    v = jnp.concatenate([vp_ref[0, 0], vo_ref[0, 0]], axis=0)
    o_ref = o_ref.at[0, 0]
    lse_ref = lse_ref.at[0, 0]
    km = lax.broadcasted_iota(jnp.int32, (1, 2 * A_BLOCK), 1)
    no_prev = jnp.logical_and(first, km < A_BLOCK)
    heads_per_tile = LANES // A_HEAD_DIM
    head_of_lane = lax.broadcasted_iota(jnp.int32, (1, LANES), 1) // A_HEAD_DIM
    for t in range(A_WIDTH // LANES):
        sl = slice(t * LANES, (t + 1) * LANES)
        q_t, k_t, v_t = q[:, sl], k[:, sl], v[:, sl]
        o_t = jnp.zeros((A_BLOCK, LANES), F32)
        lse_t = jnp.zeros((A_BLOCK, LANES), F32)
        for j in range(heads_per_tile):
            mine = head_of_lane == j
            s = _dot_nt(jnp.where(mine, q_t, jnp.zeros_like(q_t)), k_t) * scale + bias_ref[t * heads_per_tile + j]
            s = jnp.where(no_prev, NEG_INF, s)
            m = jnp.max(s, axis=-1, keepdims=True)
            p = jnp.exp(s - m)
            den = jnp.sum(p, axis=-1, keepdims=True)
            o_t = jnp.where(mine, _dot(p.astype(BF16), v_t) / den, o_t)
            lse_t = jnp.where(mine, m + jnp.log(den), lse_t)
        o_ref[:, sl] = o_t
        lse_ref[:, sl] = lse_t


def _attn_prompt(qkv_rm, tab, g, dil, n_batch, seq):
    sub_len = seq // dil
    nb = sub_len // A_BLOCK

    def spec(j, prev):
        def index(b, r, i):
            return (b, r, jnp.maximum(i - 1, 0) if prev else i, j)
        return pl.BlockSpec((1, 1, A_BLOCK, A_WIDTH), index)

    out_spec = pl.BlockSpec((1, 1, A_BLOCK, A_WIDTH), lambda b, r, i: (b, r, i, 0))
    out_sds = jax.ShapeDtypeStruct((n_batch, dil, sub_len, A_WIDTH), F32)
    return pl.pallas_call(
        _attn_prompt_body,
        grid=(n_batch, dil, nb),
        in_specs=[spec(0, False), spec(1, True), spec(1, False), spec(2, True), spec(2, False),
                  pl.BlockSpec((A_HEADS, A_BLOCK, 2 * A_BLOCK), lambda b, r, i: (0, 0, 0))],
        out_specs=[out_spec, out_spec], out_shape=[out_sds, out_sds],
        compiler_params=_cparams(3), name="attn_prompt_g%d" % g,
    )(qkv_rm, qkv_rm, qkv_rm, qkv_rm, qkv_rm, _prompt_bias(tab, dil))


def _sample_bias(tab, dil, cache_len, key_index, t_new):
    t = np.arange(t_new)[:, None]
    dist = cache_len + t - key_index[None, :]
    valid = (dist >= 0) & (dist % dil == 0) & (dist <= SPAN * dil)
    bias = _bucket_lookup(tab, _t5_bucket(np.clip(dist, 0, SPAN * dil)))
    bias = jnp.where(valid[..., None], bias, NEG_INF)
    return jnp.transpose(bias, (2, 0, 1)).reshape(A_HEADS * t_new, key_index.shape[0])


def _attn_sample_body(new_ref, c0_ref, c1_ref, c2_ref, b0_ref, b1_ref, b2_ref, bn_ref, *out_refs, t_new, n_res):
    scale = A_HEAD_DIM ** -0.5
    rows = A_HEADS * t_new
    head_of_row = lax.broadcasted_iota(jnp.int32, (rows, A_WIDTH), 0) // t_new
    head_of_lane = lax.broadcasted_iota(jnp.int32, (rows, A_WIDTH), 1) // A_HEAD_DIM
    own = head_of_row == head_of_lane
    caches = (c0_ref, c1_ref, c2_ref)
    biases = (b0_ref, b1_ref, b2_ref)
    kv_w = 2 * A_WIDTH
    for g in range(N_DIL):
        base = g * 3 * A_WIDTH
        q = new_ref[:, base:base + A_WIDTH]
        k_new = new_ref[:, base + A_WIDTH:base + 2 * A_WIDTH].astype(BF16)
        v_new = new_ref[:, base + 2 * A_WIDTH:base + 3 * A_WIDTH].astype(BF16)
        q_bd = jnp.where(own, jnp.concatenate([q] * A_HEADS, axis=0), 0.0).astype(BF16)
        pieces = n_res if g == 2 else 1
        n_keys = caches[g].shape[1]
        ks, vs, ss = [], [], []
        for r in range(pieces):
            kc = caches[g][0, :, r * kv_w:r * kv_w + A_WIDTH].astype(BF16)
            vc = caches[g][0, :, r * kv_w + A_WIDTH:(r + 1) * kv_w].astype(BF16)
            ks.append(kc)
            vs.append(vc)
            ss.append(_dot_nt(q_bd, kc) * scale + biases[g][:, r * n_keys:(r + 1) * n_keys])
        ss.append(_dot_nt(q_bd, k_new) * scale + bn_ref[g])
        vs.append(v_new)
        m = ss[0].max(axis=-1, keepdims=True)
        for s in ss[1:]:
            m = jnp.maximum(m, s.max(axis=-1, keepdims=True))
        den = jnp.zeros_like(m)
        acc = jnp.zeros((rows, A_WIDTH), F32)
        for s, v in zip(ss, vs):
            p = jnp.exp(s - m)
            den = den + jnp.sum(p, axis=-1, keepdims=True)
            acc = acc + _dot(p.astype(BF16), v)
        acc = jnp.where(own, acc / den, 0.0)
        lse = jnp.where(own, m + jnp.log(den), 0.0)
        o = acc[0:t_new]
        l = lse[0:t_new]
        for h in range(1, A_HEADS):
            o = o + acc[h * t_new:(h + 1) * t_new]
            l = l + lse[h * t_new:(h + 1) * t_new]
        out_refs[2 * g][...] = o
        out_refs[2 * g + 1][...] = l


def _attn_sample(proj, caches, t5_bias, n_seq, t_new, row0):
    cache_lens = [c.shape[1] for c in caches]
    dil2 = DIL_PAIRS[2][1]
    n_res = min(dil2, t_new)
    assert cache_lens[2] % dil2 == 0 and t_new <= dil2
    kv_w = 2 * A_WIDTH
    c2 = caches[2].reshape(n_seq, cache_lens[2] // dil2, dil2, kv_w)[:, :, :n_res].astype(BF16)
    c2 = c2.reshape(n_seq, cache_lens[2] // dil2, n_res * kv_w)
    caches = [caches[0].astype(BF16).reshape(n_seq, cache_lens[0], kv_w),
              caches[1].astype(BF16).reshape(n_seq, cache_lens[1], kv_w)]
    idx2 = (np.arange(cache_lens[2] // dil2)[None, :] * dil2 + np.arange(n_res)[:, None]).reshape(-1)
    key_idx = [np.arange(cache_lens[0]), np.arange(cache_lens[1]), idx2]
    biases = [_sample_bias(t5_bias[:, g], DIL_PAIRS[g][1], cache_lens[g], key_idx[g], t_new) for g in range(N_DIL)]
    bias_new = jnp.stack([_sample_bias(t5_bias[:, g], DIL_PAIRS[g][1], cache_lens[g],
                                       cache_lens[g] + np.arange(t_new), t_new) for g in range(N_DIL)])
    rows = A_HEADS * t_new
    blk0 = row0 // t_new
    full = lambda a: pl.BlockSpec(a.shape, lambda b: (0,) * a.ndim)
    out_spec = pl.BlockSpec((t_new, A_WIDTH), lambda b: (b, 0))
    return pl.pallas_call(
        functools.partial(_attn_sample_body, t_new=t_new, n_res=n_res),
        grid=(n_seq,),
        in_specs=[pl.BlockSpec((t_new, N_DIL * 3 * A_WIDTH), lambda b: (blk0 + b, 0)),
                  pl.BlockSpec((1, cache_lens[0], kv_w), lambda b: (b, 0, 0)),
                  pl.BlockSpec((1, cache_lens[1], kv_w), lambda b: (b, 0, 0)),
                  pl.BlockSpec((1, cache_lens[2] // dil2, n_res * kv_w), lambda b: (b, 0, 0)),
                  full(biases[0]), full(biases[1]), full(biases[2]), full(bias_new)],
        out_specs=[out_spec] * (2 * N_DIL),
        out_shape=[jax.ShapeDtypeStruct((n_seq * t_new, A_WIDTH), F32)] * (2 * N_DIL),
        compiler_params=_cparams(1), name="attn_sample",
    )(proj, caches[0], caches[1], c2, *biases, bias_new)


def _shift_rows(x, s, fill, axis):
    t = lax.broadcasted_iota(jnp.int32, x.shape, axis)
    return jnp.where(t >= s, pltpu.roll(x, s, axis), fill)


def _linear_scan(a, b, axis):
    n = a.shape[axis]
    s = 1
    while s < n:
        b = b + a * _shift_rows(b, s, 0.0, axis)
        a = a * _shift_rows(a, s, 1.0, axis)
        s *= 2
    return a, b


def _rglru_gates(xc, wa_ref, wx_ref, vec_ref):
    xcb = xc.astype(BF16)
    r = jax.nn.sigmoid(_dot(xcb, wa_ref[...]) + vec_ref[1:2])
    ig = jax.nn.sigmoid(_dot(xcb, wx_ref[...]) + vec_ref[2:3])
    log_a = -RG_C * r * _softplus(-vec_ref[3:4])
    a = jnp.exp(log_a)
    b = jnp.sqrt(1.0 - jnp.exp(2.0 * log_a)) * (ig * xc)
    return a, b


def _gelu(x):
    return 0.5 * x * (1.0 + jnp.tanh(math.sqrt(2.0 / math.pi) * (x + 0.044715 * (x * x * x))))


def _rglru_prompt_body(xb_ref, gb_ref, cw_ref, wa_ref, wx_ref, vec_ref, y_ref, hl_ref, tail_ref, h_ref):
    @pl.when(pl.program_id(1) == 0)
    def _():
        tail_ref[...] = jnp.zeros_like(tail_ref)
        h_ref[...] = jnp.zeros_like(h_ref)

    x = xb_ref[...]
    tt = x.shape[0]
    xe = jnp.concatenate([tail_ref[...], x], axis=0)
    xc = vec_ref[0:1] + x * cw_ref[CONV_W - 1:CONV_W]
    for j in range(1, CONV_W):
        xc = xc + pltpu.roll(xe, j, 0)[8:8 + tt] * cw_ref[CONV_W - 1 - j:CONV_W - j]
    tail_ref[...] = x[tt - 8:tt]
    a, b = _rglru_gates(xc, wa_ref, wx_ref, vec_ref)
    a_cum, h = _linear_scan(a, b, 0)
    h = h + a_cum * h_ref[...]
    h_ref[...] = h[tt - 1:tt]
    hl_ref[0] = h[tt - 1:tt]
    y_ref[...] = h * _gelu(gb_ref[...])


def _rglru_sample_body(xb_ref, gb_ref, c0_ref, h0_ref, cw_ref, wa_ref, wx_ref, vec_ref, prev_ref, y_ref, hl_ref, *, t_new):
    del prev_ref
    x = xb_ref[...]
    rows = x.shape[0]
    ns = rows // t_new
    x3 = x.reshape(ns, t_new, RNN_WIDTH)
    xe = jnp.concatenate([c0_ref[...], x3], axis=1)
    xc = vec_ref[0:1] + x3 * cw_ref[CONV_W - 1:CONV_W]
    for j in range(1, CONV_W):
        xc = xc + pltpu.roll(xe, j, 1)[:, 8:8 + t_new] * cw_ref[CONV_W - 1 - j:CONV_W - j]
    a, b = _rglru_gates(xc.reshape(rows, RNN_WIDTH), wa_ref, wx_ref, vec_ref)
    a_cum, h = _linear_scan(a.reshape(ns, t_new, RNN_WIDTH), b.reshape(ns, t_new, RNN_WIDTH), 1)
    h = h + a_cum * h0_ref[...][:, None, :]
    hl_ref[...] = h[:, t_new - 1, :]
    y_ref[...] = h.reshape(rows, RNN_WIDTH) * _gelu(gb_ref[...])


def _block_diag(w):
    nb, bi, bo = w.shape
    eye = jnp.eye(nb, dtype=w.dtype)
    return (w[:, :, None, :] * eye[:, None, :, None]).reshape(nb * bi, nb * bo)


def _rglru(proj, conv0, h0, conv_w, conv_b, wa, ba, wx, bx, lam, n_batch, seq, n_seq, t_new):
    m_total = proj.shape[0]
    mp = n_batch * seq
    wa_bd = _block_diag(wa).astype(BF16)
    wx_bd = _block_diag(wx).astype(BF16)
    vec = jnp.stack([conv_b, ba, bx, lam]).astype(F32)
    xcol = XB_COL // RNN_WIDTH
    gcol = GB_COL // RNN_WIDTH
    full2 = lambda a, nd: pl.BlockSpec(a.shape, lambda *_: (0,) * a.ndim)
    tt = _row_tile(seq)
    nt = seq // tt
    w_specs2 = [pl.BlockSpec(a.shape, lambda b, i: (0, 0)) for a in (conv_w, wa_bd, wx_bd, vec)]
    y, hl_p = pl.pallas_call(
        _rglru_prompt_body,
        grid=(n_batch, nt),
        in_specs=[pl.BlockSpec((tt, RNN_WIDTH), lambda b, i: (b * nt + i, xcol)),
                  pl.BlockSpec((tt, RNN_WIDTH), lambda b, i: (b * nt + i, gcol))] + w_specs2,
        out_specs=[pl.BlockSpec((tt, RNN_WIDTH), lambda b, i: (b * nt + i, 0)),
                   pl.BlockSpec((1, 1, RNN_WIDTH), lambda b, i: (b, 0, 0))],
        out_shape=[jax.ShapeDtypeStruct((m_total, RNN_WIDTH), F32),
                   jax.ShapeDtypeStruct((n_batch, 1, RNN_WIDTH), F32)],
        scratch_shapes=[pltpu.VMEM((8, RNN_WIDTH), F32), pltpu.VMEM((1, RNN_WIDTH), F32)],
        compiler_params=_cparams(2), name="rglru_prompt",
    )(proj, proj, conv_w, wa_bd, wx_bd, vec)
    del full2
    ts = 32 if n_seq % 32 == 0 else 8
    rows = ts * t_new
    blk0 = mp // rows
    c0p = jnp.pad(conv0, ((0, 0), (8 - (CONV_W - 1), 0), (0, 0)))
    w_specs1 = [pl.BlockSpec(a.shape, lambda i: (0, 0)) for a in (conv_w, wa_bd, wx_bd, vec)]
    y, hl_s = pl.pallas_call(
        functools.partial(_rglru_sample_body, t_new=t_new),
        grid=(n_seq // ts,),
        in_specs=[pl.BlockSpec((rows, RNN_WIDTH), lambda i: (blk0 + i, xcol)),
                  pl.BlockSpec((rows, RNN_WIDTH), lambda i: (blk0 + i, gcol)),
                  pl.BlockSpec((ts, 8, RNN_WIDTH), lambda i: (i, 0, 0)),
                  pl.BlockSpec((ts, RNN_WIDTH), lambda i: (i, 0))] + w_specs1
                 + [pl.BlockSpec(memory_space=pl.ANY)],
        out_specs=[pl.BlockSpec((rows, RNN_WIDTH), lambda i: (blk0 + i, 0)),
                   pl.BlockSpec((ts, RNN_WIDTH), lambda i: (i, 0))],
        out_shape=[jax.ShapeDtypeStruct((m_total, RNN_WIDTH), F32),
                   jax.ShapeDtypeStruct((n_seq, RNN_WIDTH), F32)],
        input_output_aliases={8: 0},
        compiler_params=_cparams(1), name="rglru_sample",
    )(proj, proj, c0p, h0, conv_w, wa_bd, wx_bd, vec, y)
    return y, hl_p.reshape(n_batch, RNN_WIDTH), hl_s


def _pack_bf16_pair(x):
    w = x.shape[1] // 2
    hi = lax.bitcast_convert_type(x[:, :w].astype(BF16).astype(F32), jnp.uint32)
    lo = lax.bitcast_convert_type(x[:, w:].astype(BF16).astype(F32), jnp.uint32)
    return hi | (lo >> 16)


def _unpack_bf16_pair(p):
    hi = lax.bitcast_convert_type(p & jnp.uint32(0xFFFF0000), F32).astype(BF16)
    lo = lax.bitcast_convert_type(p << 16, F32).astype(BF16)
    return hi, lo


def _route(x1, gf_ref, wr_ref, br_ref, hn_ref, route_ref, counts_ref, run_ref):
    @pl.when(pl.program_id(0) == 0)
    def _():
        run_ref[...] = jnp.zeros_like(run_ref)

    hn = _rms(x1, gf_ref[...])
    hn_ref[...] = _pack_bf16_pair(hn)
    hn_hi = hn.astype(BF16)
    hn_lo = (hn - hn_hi.astype(F32)).astype(BF16)
    logits = _dot(hn_hi, wr_ref[0]) + (_dot(hn_hi, wr_ref[1]) + _dot(hn_lo, wr_ref[0])) + br_ref[...]
    lane = lax.broadcasted_iota(jnp.int32, logits.shape, 1)
    is_coarse = lane < N_GROUPS
    coarse = jnp.where(is_coarse, logits, -jnp.inf)
    cmax = jnp.max(coarse, axis=-1, keepdims=True)
    grp = jnp.min(jnp.where(coarse == cmax, lane, ROUTE_LANES), axis=-1, keepdims=True)
    p_grp = 1.0 / jnp.sum(jnp.where(is_coarse, jnp.exp(logits - cmax), 0.0), axis=-1, keepdims=True)
    expert = lane - N_GROUPS
    in_grp = (lane >= N_GROUPS) & (expert < N_EXPERTS) & (expert // EXPERTS_PER_GROUP == grp)
    fine = jnp.where(in_grp, logits, -jnp.inf)
    v1 = jnp.max(fine, axis=-1, keepdims=True)
    i1 = jnp.min(jnp.where(fine == v1, lane, ROUTE_LANES), axis=-1, keepdims=True)
    fine2 = jnp.where(lane == i1, -jnp.inf, fine)
    v2 = jnp.max(fine2, axis=-1, keepdims=True)
    i2 = jnp.min(jnp.where(fine2 == v2, lane, ROUTE_LANES), axis=-1, keepdims=True)
    e2 = jnp.exp(v2 - v1)
    w1 = p_grp / (1.0 + e2)
    w2 = p_grp * e2 / (1.0 + e2)
    tm = logits.shape[0]
    sel = jnp.where(lane == i1, 1.0, jnp.where(lane == i2, 1.0, 0.0))
    ri = lax.broadcasted_iota(jnp.int32, (tm, tm), 0)
    ci = lax.broadcasted_iota(jnp.int32, (tm, tm), 1)
    earlier = jnp.where(ri > ci, 1.0, 0.0).astype(BF16)
    before = _dot(earlier, sel.astype(BF16)) + run_ref[...]
    r1 = jnp.sum(jnp.where(lane == i1, before, 0.0), axis=-1, keepdims=True)
    r2 = jnp.sum(jnp.where(lane == i2, before, 0.0), axis=-1, keepdims=True)
    run_ref[...] = run_ref[...] + jnp.sum(sel, axis=0, keepdims=True)
    counts_ref[...] = run_ref[...]
    route = jnp.where(lane == 0, (i1 - N_GROUPS).astype(F32), 0.0)
    route = jnp.where(lane == 1, (i2 - N_GROUPS).astype(F32), route)
    route = jnp.where(lane == 2, w1, route)
    route = jnp.where(lane == 3, w2, route)
    route = jnp.where(lane == 4, r1, route)
    route = jnp.where(lane == 5, r2, route)
    route_ref[...] = route


def _merge_groups(os, ls):
    mx = jnp.maximum(jnp.maximum(ls[0], ls[1]), ls[2])
    es = [jnp.exp(l - mx) for l in ls]
    return (es[0] * os[0] + es[1] * os[1] + es[2] * os[2]) / (es[0] + es[1] + es[2])


def _even_out_body(o0, l0, o1, l1, o2, l2, so0, sl0, so1, sl1, so2, sl2, yb_ref, x_ref, wo_ref, gf_ref, wr_ref, br_ref,
                   x1_ref, hn_ref, route_ref, counts_ref, n1o, n1l, n2o, n2l, oa_ref, run_ref, *, n_prompt_tiles):
    i = pl.program_id(0)
    tm = x_ref.shape[0]

    @pl.when(i < n_prompt_tiles)
    def _():
        nat = []
        for src, dst, dil in ((o1, n1o, DIL_PAIRS[1][1]), (l1, n1l, DIL_PAIRS[1][1]),
                              (o2, n2o, DIL_PAIRS[2][1]), (l2, n2l, DIL_PAIRS[2][1])):
            for r in range(dil):
                blk = src[0, r]
                for t in range(A_WIDTH // LANES):
                    dst[t, pl.ds(r, tm // dil, stride=dil), :] = blk[:, t * LANES:(t + 1) * LANES]
            nat.append(jnp.concatenate([dst[t] for t in range(A_WIDTH // LANES)], axis=1))
        oa_ref[...] = _merge_groups((o0[0, 0], nat[0], nat[2]), (l0[0, 0], nat[1], nat[3]))

    @pl.when(i >= n_prompt_tiles)
    def _():
        oa_ref[...] = _merge_groups((so0[...], so1[...], so2[...]), (sl0[...], sl1[...], sl2[...]))

    cat = jnp.concatenate([oa_ref[...], yb_ref[...]], axis=-1).astype(BF16)
    x1 = x_ref[...] + _dot(cat, wo_ref[...])
    x1_ref[...] = x1
    _route(x1, gf_ref, wr_ref, br_ref, hn_ref, route_ref, counts_ref, run_ref)


def _router_weights(rg_w, rg_b, re_w, re_b):
    d = rg_w.shape[0]
    wr = jnp.concatenate([rg_w, re_w.reshape(d, N_EXPERTS)], axis=1)
    br = jnp.concatenate([rg_b, re_b.reshape(N_EXPERTS)])
    pad = ROUTE_LANES - wr.shape[1]
    wr = jnp.pad(wr, ((0, 0), (0, pad))).astype(F32)
    wr_hi = wr.astype(BF16)
    wr_lo = (wr - wr_hi.astype(F32)).astype(BF16)
    return jnp.stack([wr_hi, wr_lo]), jnp.pad(br, (0, pad)).reshape(1, ROUTE_LANES).astype(F32)


def _mix_out_call(body, in_specs, args, consts, m, d, tm, scratch, name):
    const_spec = lambda a: pl.BlockSpec(a.shape, lambda i: (0,) * a.ndim)
    return pl.pallas_call(
        body, grid=(m // tm,), in_specs=list(in_specs) + [const_spec(c) for c in consts],
        out_specs=[pl.BlockSpec((tm, d), lambda i: (i, 0)), pl.BlockSpec((tm, d // 2), lambda i: (i, 0)),
                   pl.BlockSpec((tm, ROUTE_LANES), lambda i: (i, 0)), pl.BlockSpec((1, ROUTE_LANES), lambda i: (0, 0))],
        out_shape=[jax.ShapeDtypeStruct((m, d), F32), jax.ShapeDtypeStruct((m, d // 2), jnp.uint32),
                   jax.ShapeDtypeStruct((m, ROUTE_LANES), F32), jax.ShapeDtypeStruct((1, ROUTE_LANES), F32)],
        scratch_shapes=list(scratch) + [pltpu.VMEM((1, ROUTE_LANES), F32)],
        compiler_params=_cparams(1), name=name,
    )(*args, *consts)


def _even_out(attn_p, attn_s, y_b, x, consts, tm, n_batch, seq):
    m, d = x.shape
    tps = seq // tm
    npt = n_batch * tps
    in_specs = []
    for g, (_, dil) in enumerate(DIL_PAIRS):
        def index(i):
            return (jnp.minimum(i // tps, n_batch - 1), 0, jnp.where(i < npt, i % tps, 0), 0)
        in_specs += [pl.BlockSpec((1, dil, tm // dil, A_WIDTH), index)] * 2
    in_specs += [pl.BlockSpec((tm, A_WIDTH), lambda i: (jnp.maximum(i - npt, 0), 0))] * (2 * N_DIL)
    in_specs += [pl.BlockSpec((tm, RNN_WIDTH), lambda i: (i, 0)), pl.BlockSpec((tm, d), lambda i: (i, 0))]
    scratch = [pltpu.VMEM((A_WIDTH // LANES, tm, LANES), F32)] * 4 + [pltpu.VMEM((tm, A_WIDTH), F32)]
    return _mix_out_call(functools.partial(_even_out_body, n_prompt_tiles=npt), in_specs,
                         list(attn_p) + list(attn_s) + [y_b, x], consts, m, d, tm, scratch, "even_out")


def _odd_out(o_c, proj2, x, consts, tm):
    m, d = x.shape
    in_specs = [pl.BlockSpec((tm, C_V_W), lambda i: (i, 0)),
                pl.BlockSpec((tm, C_V_W), lambda i: (i, Z_COL // C_V_W)),
                pl.BlockSpec((tm, d), lambda i: (i, 0))]
    return _mix_out_call(_odd_out_body, in_specs, [o_c, proj2, x], consts, m, d, tm, [], "odd_out")


def _row_copy(src_ref, src_row, dst_ref, dst_row, sem):
    return pltpu.make_async_copy(src_ref.at[pl.ds(src_row, 1)], dst_ref.at[pl.ds(dst_row, 1)], sem)


def _moe_dispatch_body(s0_ref, s1_ref, fill_ref, used_ref, hn_ref, xs_ref, zero_ref, fill_sem, row_sem):
    i = pl.program_id(0)
    tm = hn_ref.shape[0]

    def fill_copy(e):
        return pltpu.make_async_copy(zero_ref, xs_ref.at[pl.ds(pl.multiple_of(fill_ref[e], MOE_TILE), MOE_TILE)], fill_sem)

    @pl.when(i == 0)
    def _():
        zero_ref[...] = jnp.zeros_like(zero_ref)
        for e in range(N_EXPERTS):
            @pl.when(used_ref[e] > 0)
            def _():
                fill_copy(e).start()
        for e in range(N_EXPERTS):
            @pl.when(used_ref[e] > 0)
            def _():
                fill_copy(e).wait()

    base = i * tm

    def scatter_row(j, carry):
        _row_copy(hn_ref, j, xs_ref, s0_ref[base + j], row_sem).start()
        _row_copy(hn_ref, j, xs_ref, s1_ref[base + j], row_sem).start()
        return carry

    lax.fori_loop(0, tm, scatter_row, 0, unroll=8)
    for _ in range(2):
        pltpu.make_async_copy(hn_ref, xs_ref.at[pl.ds(0, tm)], row_sem).wait()


def _moe_ffn_body(te_ref, nu_ref, xs_ref, wg_ref, wu_ref, wd_ref, o_ref):
    del te_ref

    @pl.when(pl.program_id(0) < nu_ref[0])
    def _():
        xa, xb = _unpack_bf16_pair(xs_ref[...])
        half = xa.shape[1]
        hg = _dot(xa, wg_ref[0, :half]) + _dot(xb, wg_ref[0, half:])
        hu = _dot(xa, wu_ref[0, :half]) + _dot(xb, wu_ref[0, half:])
        o_ref[...] = _dot((_silu(hg) * hu).astype(BF16), wd_ref[0])

    @pl.when(pl.program_id(0) >= nu_ref[0])
    def _():
        o_ref[...] = jnp.zeros_like(o_ref)


def _moe_combine_body(s0_ref, s1_ref, route_ref, ys_ref, y_ref, buf_ref, sem):
    tm = y_ref.shape[0]
    base = pl.program_id(0) * tm

    def gather_row(j, carry):
        _row_copy(ys_ref, s0_ref[base + j], buf_ref.at[0], j, sem).start()
        _row_copy(ys_ref, s1_ref[base + j], buf_ref.at[1], j, sem).start()
        return carry

    lax.fori_loop(0, tm, gather_row, 0, unroll=8)
    for k in range(2):
        pltpu.make_async_copy(ys_ref.at[pl.ds(0, tm)], buf_ref.at[k], sem).wait()
    y_ref[...] = route_ref[:, 2:3] * buf_ref[0] + route_ref[:, 3:4] * buf_ref[1]


def _moe(hn, route, counts, w_gate, w_up, w_down, tm):
    m = hn.shape[0]
    d = w_gate.shape[1]
    tile = MOE_TILE
    n_slots = 2 * m + N_EXPERTS * tile
    n_tiles = n_slots // tile
    cnt = counts[0, N_GROUPS:N_GROUPS + N_EXPERTS].astype(jnp.int32)
    padded = ((cnt + tile - 1) // tile) * tile
    pad_end = jnp.cumsum(padded)
    pad_start = pad_end - padded
    experts = jnp.arange(N_EXPERTS, dtype=jnp.int32)

    def slots(e, pos):
        start = jnp.sum(jnp.where(e.astype(jnp.int32)[:, None] == experts[None, :], pad_start[None, :], 0), axis=1)
        return (start + pos.astype(jnp.int32)).astype(jnp.int32)

    slot0 = slots(route[:, 0], route[:, 4])
    slot1 = slots(route[:, 1], route[:, 5])
    tile_start = jnp.arange(n_tiles, dtype=jnp.int32) * tile
    tile_expert = jnp.sum((tile_start[:, None] >= pad_end[None, :]).astype(jnp.int32), axis=1)
    tile_expert = jnp.minimum(tile_expert, N_EXPERTS - 1).astype(jnp.int32)
    n_used = (pad_end[-1:] // tile).astype(jnp.int32)
    fill_start = jnp.maximum(pad_end - tile, 0).astype(jnp.int32)

    xs = pl.pallas_call(
        _moe_dispatch_body,
        grid_spec=pltpu.PrefetchScalarGridSpec(
            num_scalar_prefetch=4, grid=(m // tm,),
            in_specs=[pl.BlockSpec((tm, d // 2), lambda i, *_: (i, 0))],
            out_specs=pl.BlockSpec(memory_space=pl.ANY),
            scratch_shapes=[pltpu.VMEM((tile, d // 2), jnp.uint32), pltpu.SemaphoreType.DMA(()),
                            pltpu.SemaphoreType.DMA(())]),
        out_shape=jax.ShapeDtypeStruct((n_slots, d // 2), jnp.uint32),
        compiler_params=_cparams(1), name="moe_dispatch",
    )(slot0, slot1, fill_start, cnt, hn)

    def used(i, nu):
        return jnp.minimum(i, jnp.maximum(nu[0] - 1, 0))

    ys = pl.pallas_call(
        _moe_ffn_body,
        grid_spec=pltpu.PrefetchScalarGridSpec(
            num_scalar_prefetch=2, grid=(n_tiles,),
            in_specs=[pl.BlockSpec((tile, d // 2), lambda i, te, nu: (used(i, nu), 0)),
                      pl.BlockSpec((1, d, EXPERT_FF), lambda i, te, nu: (te[used(i, nu)], 0, 0)),
                      pl.BlockSpec((1, d, EXPERT_FF), lambda i, te, nu: (te[used(i, nu)], 0, 0)),
                      pl.BlockSpec((1, EXPERT_FF, d), lambda i, te, nu: (te[used(i, nu)], 0, 0))],
            out_specs=pl.BlockSpec((tile, d), lambda i, te, nu: (i, 0))),
        out_shape=jax.ShapeDtypeStruct((n_slots, d), F32),
        compiler_params=_cparams(1), name="moe_ffn",
    )(tile_expert, n_used, xs, w_gate, w_up, w_down)

    return pl.pallas_call(
        _moe_combine_body,
        grid_spec=pltpu.PrefetchScalarGridSpec(
            num_scalar_prefetch=2, grid=(m // tm,),
            in_specs=[pl.BlockSpec((tm, ROUTE_LANES), lambda i, *_: (i, 0)),
                      pl.BlockSpec(memory_space=pl.ANY)],
            out_specs=pl.BlockSpec((tm, d), lambda i, *_: (i, 0)),
            scratch_shapes=[pltpu.VMEM((2, tm, d), F32), pltpu.SemaphoreType.DMA(())]),
        out_shape=jax.ShapeDtypeStruct((m, d), F32),
        compiler_params=_cparams(1), name="moe_combine",
    )(slot0, slot1, route, ys)


def _gdn_prep_math(xe, tt, cw_ref, bg, av_ref):
    acc = xe[8:8 + tt] * cw_ref[CONV_W - 1:CONV_W]
    for j in range(1, CONV_W):
        acc = acc + pltpu.roll(xe, j, 0)[8:8 + tt] * cw_ref[CONV_W - 1 - j:CONV_W - j]
    qkv = _silu(acc)
    outs = []
    for h in range(2 * C_QK_HEADS):
        xh = qkv[:, h * C_DK:(h + 1) * C_DK]
        xh = xh * lax.rsqrt(jnp.sum(xh * xh, axis=-1, keepdims=True) + EPS)
        if h < C_QK_HEADS:
            xh = xh * (C_DK ** -0.5)
        outs.append(xh)
    qk = jnp.concatenate(outs, axis=-1)
    v = qkv[:, 2 * C_QK_W:]
    lane = lax.broadcasted_iota(jnp.int32, bg.shape, 1)
    gdec = -jnp.exp(av_ref[0:1]) * _softplus(bg + av_ref[1:2])
    bgo = jnp.where(lane < C_V_HEADS, jax.nn.sigmoid(bg), gdec)
    return qk, v, bgo


def _gdn_prep_prompt_body(x_ref, bg_ref, cw_ref, av_ref, qk_ref, v_ref, bgo_ref, tail_ref):
    @pl.when(pl.program_id(1) == 0)
    def _():
        tail_ref[...] = jnp.zeros_like(tail_ref)

    x = x_ref[...]
    tt = x.shape[0]
    xe = jnp.concatenate([tail_ref[...], x], axis=0)
    tail_ref[...] = x[tt - 8:tt]
    qk, v, bgo = _gdn_prep_math(xe, tt, cw_ref, bg_ref[...], av_ref)
    qk_ref[...] = qk
    v_ref[...] = v
    bgo_ref[...] = bgo


def _gdn_prep_sample_body(x_ref, bg_ref, c0_ref, cw_ref, av_ref, p0, p1, p2, qk_ref, v_ref, bgo_ref, *, t_new):
    del p0, p1, p2
    xe = jnp.concatenate([c0_ref[0], x_ref[...]], axis=0)
    qk, v, bgo = _gdn_prep_math(xe, t_new, cw_ref, bg_ref[...], av_ref)
    qk_ref[...] = qk
    v_ref[...] = v
    bgo_ref[...] = bgo


def _gdn_prep(proj, conv0, conv_w, a_log, dt_bias, n_batch, seq, n_seq, t_new):
    m_total = proj.shape[0]
    mp = n_batch * seq
    av = jnp.zeros((2, 128), F32)
    av = av.at[0, C_V_HEADS:2 * C_V_HEADS].set(a_log).at[1, C_V_HEADS:2 * C_V_HEADS].set(dt_bias)
    bg_col = BG_COL // 128
    tt = min(_row_tile(seq), 128)
    nt = seq // tt
    out_shape = [jax.ShapeDtypeStruct((m_total, 2 * C_QK_W), F32),
                 jax.ShapeDtypeStruct((m_total, C_V_W), F32),
                 jax.ShapeDtypeStruct((m_total, 128), F32)]
    outs = pl.pallas_call(
        _gdn_prep_prompt_body,
        grid=(n_batch, nt),
        in_specs=[pl.BlockSpec((tt, C_CONV_DIM), lambda b, i: (b * nt + i, 0)),
                  pl.BlockSpec((tt, 128), lambda b, i: (b * nt + i, bg_col)),
                  pl.BlockSpec(conv_w.shape, lambda b, i: (0, 0)),
                  pl.BlockSpec(av.shape, lambda b, i: (0, 0))],
        out_specs=[pl.BlockSpec((tt, 2 * C_QK_W), lambda b, i: (b * nt + i, 0)),
                   pl.BlockSpec((tt, C_V_W), lambda b, i: (b * nt + i, 0)),
                   pl.BlockSpec((tt, 128), lambda b, i: (b * nt + i, 0))],
        out_shape=out_shape,
        scratch_shapes=[pltpu.VMEM((8, C_CONV_DIM), F32)],
        compiler_params=_cparams(2), name="gdn_prep_prompt",
    )(proj, proj, conv_w, av)
    blk0 = mp // t_new
    c0p = jnp.pad(conv0, ((0, 0), (8 - (CONV_W - 1), 0), (0, 0)))
    return pl.pallas_call(
        functools.partial(_gdn_prep_sample_body, t_new=t_new),
        grid=(n_seq,),
        in_specs=[pl.BlockSpec((t_new, C_CONV_DIM), lambda i: (blk0 + i, 0)),
                  pl.BlockSpec((t_new, 128), lambda i: (blk0 + i, bg_col)),
                  pl.BlockSpec((1, 8, C_CONV_DIM), lambda i: (i, 0, 0)),
                  pl.BlockSpec(conv_w.shape, lambda i: (0, 0)),
                  pl.BlockSpec(av.shape, lambda i: (0, 0))] + [pl.BlockSpec(memory_space=pl.ANY)] * 3,
        out_specs=[pl.BlockSpec((t_new, 2 * C_QK_W), lambda i: (blk0 + i, 0)),
                   pl.BlockSpec((t_new, C_V_W), lambda i: (blk0 + i, 0)),
                   pl.BlockSpec((t_new, 128), lambda i: (blk0 + i, 0))],
        out_shape=out_shape,
        input_output_aliases={5: 0, 6: 1, 7: 2},
        compiler_params=_cparams(1), name="gdn_prep_sample",
    )(proj, proj, c0p, conv_w, av, *outs)


def _unit_lower_inverses(l_bds, size):
    n = l_bds[0].shape[0]
    ri = lax.broadcasted_iota(jnp.int32, (n, n), 0)
    ci = lax.broadcasted_iota(jnp.int32, (n, n), 1)
    eye = jnp.where(ri == ci, 1.0, 0.0).astype(F32)
    base = ri // GDN_BASE == ci // GDN_BASE
    ps = [jnp.where(base, -l, 0.0) for l in l_bds]
    ts = [eye + p for p in ps]
    s = 2
    while s < GDN_BASE:
        ps = [_dot(p.astype(BF16), p.astype(BF16)) for p in ps]
        ts = [t + _dot(t.astype(BF16), p.astype(BF16)) for t, p in zip(ts, ps)]
        s *= 2
    s = GDN_BASE
    while s < size:
        lower_left = ((ri // s) % 2 == 1) & (ci // s == ri // s - 1)
        tbs = [t.astype(BF16) for t in ts]
        mids = [_dot(jnp.where(lower_left, l, 0.0).astype(BF16), tb).astype(BF16) for l, tb in zip(l_bds, tbs)]
        ts = [t - _dot(tb, mid) for t, tb, mid in zip(ts, tbs, mids)]
        s *= 2
    return ts


def _gdn_chunk_body(qk_ref, v_ref, bg_ref, nw_ref, *rest, chunk, pack, has_state):
    if has_state:
        s0_ref, _prev, o_ref, sout_ref, s_ref = rest
    else:
        o_ref, sout_ref, s_ref = rest
    c = chunk

    @pl.when(pl.program_id(1) == 0)
    def _():
        if has_state:
            s_ref[...] = s0_ref[0]
        else:
            s_ref[...] = jnp.zeros_like(s_ref)

    rep = C_V_HEADS // C_QK_HEADS
    beta = bg_ref[:, 0:C_V_HEADS]
    g = bg_ref[:, C_V_HEADS:2 * C_V_HEADS]
    ti = lax.broadcasted_iota(jnp.int32, (c, c), 0)
    tj = lax.broadcasted_iota(jnp.int32, (c, c), 1)
    tril = jnp.where(ti >= tj, 1.0, 0.0).astype(F32)
    gc = _dot_f32(tril, g)
    g_last = gc[c - 1:c]
    gam = jnp.exp(gc)
    kdec = jnp.exp(g_last - gc)
    g_end = jnp.exp(g_last)
    n = pack * c
    ri = lax.broadcasted_iota(jnp.int32, (n, n), 0)
    ci = lax.broadcasted_iota(jnp.int32, (n, n), 1)
    same = ri // c == ci // c
    incl = same & (ri >= ci)
    strict = same & (ri > ci)
    packs = [range(p0, p0 + pack) for p0 in range(0, C_V_HEADS, pack)]

    def col(a, heads):
        return jnp.concatenate([a[:, h:h + 1] for h in heads], axis=0)

    l_bds, qk_bds, rhss, qgs, kends = [], [], [], [], []
    for heads in packs:
        g_col = col(gc, heads)
        g_row = jnp.sum(jnp.where(ri == ci, g_col, 0.0), axis=0, keepdims=True)
        b_col = col(beta, heads)
        gam_col = col(gam, heads)
        k_st = jnp.concatenate([qk_ref[:, C_QK_W + (h // rep) * C_DK:C_QK_W + (h // rep + 1) * C_DK] for h in heads], axis=0)
        q_st = jnp.concatenate([qk_ref[:, (h // rep) * C_DK:(h // rep + 1) * C_DK] for h in heads], axis=0)
        v_st = jnp.concatenate([v_ref[:, h * C_DV:(h + 1) * C_DV] for h in heads], axis=0)
        kb = k_st.astype(BF16)
        decay = jnp.exp(jnp.where(incl, g_col - g_row, -jnp.inf))
        l_bds.append(jnp.where(strict, b_col * _dot_nt(kb, kb) * decay, 0.0))
        qk_bds.append((_dot_nt(q_st.astype(BF16), kb) * decay).astype(BF16))
        rhss.append(jnp.concatenate([b_col * v_st, (b_col * gam_col) * k_st], axis=1).astype(BF16))
        qgs.append(q_st * gam_col)
        kends.append((k_st * col(kdec, heads)).astype(BF16))
    t_invs = _unit_lower_inverses(l_bds, c)
    uws = [_dot(t.astype(BF16), rhs) for t, rhs in zip(t_invs, rhss)]
    u_sts, q_sts = [], []
    for heads, uw, qg in zip(packs, uws, qgs):
        us, qs = [], []
        for i, h in enumerate(heads):
            rows = slice(i * c, (i + 1) * c)
            lhs = jnp.concatenate([uw[rows, C_DV:], qg[rows]], axis=0).astype(BF16)
            ws = _dot(lhs, s_ref[h].astype(BF16))
            us.append(uw[rows, 0:C_DV] - ws[0:c])
            qs.append(ws[c:])
        u_sts.append(jnp.concatenate(us, axis=0).astype(BF16))
        q_sts.append(jnp.concatenate(qs, axis=0))
    o_sts = [q + _dot(qk_bd, ub) for q, qk_bd, ub in zip(q_sts, qk_bds, u_sts)]
    for heads, o_st, ub, kendb in zip(packs, o_sts, u_sts, kends):
        for i, h in enumerate(heads):
            rows = slice(i * c, (i + 1) * c)
            s_ref[h] = s_ref[h] * g_end[:, h:h + 1] + _dot_tn(kendb[rows], ub[rows])
            o_ref[:, h * C_DV:(h + 1) * C_DV] = _rms(o_st[rows], nw_ref[...])

    @pl.when(pl.program_id(1) == pl.num_programs(1) - 1)
    def _():
        sout_ref[0] = s_ref[...]


def _gdn_chunks(qk, v, bg, onorm_w, s0, n_batch, seq, n_seq, t_new, chunk):
    m_total = qk.shape[0]
    mp = n_batch * seq
    nc = seq // chunk
    nw = onorm_w.reshape(1, C_DV).astype(F32)
    state = (C_V_HEADS, C_DK, C_DV)
    o, s_p = pl.pallas_call(
        functools.partial(_gdn_chunk_body, chunk=chunk, pack=256 // chunk, has_state=False),
        grid=(n_batch, nc),
        in_specs=[pl.BlockSpec((chunk, 2 * C_QK_W), lambda b, i: (b * nc + i, 0)),
                  pl.BlockSpec((chunk, C_V_W), lambda b, i: (b * nc + i, 0)),
                  pl.BlockSpec((chunk, 128), lambda b, i: (b * nc + i, 0)),
                  pl.BlockSpec((1, C_DV), lambda b, i: (0, 0))],
        out_specs=[pl.BlockSpec((chunk, C_V_W), lambda b, i: (b * nc + i, 0)),
                   pl.BlockSpec((1,) + state, lambda b, i: (b, 0, 0, 0))],
        out_shape=[jax.ShapeDtypeStruct((m_total, C_V_W), F32),
                   jax.ShapeDtypeStruct((n_batch,) + state, F32)],
        scratch_shapes=[pltpu.VMEM(state, F32)],
        compiler_params=_cparams(2), name="gdn_prompt",
    )(qk, v, bg, nw)
    blk0 = mp // t_new
    o, s_s = pl.pallas_call(
        functools.partial(_gdn_chunk_body, chunk=t_new, pack=C_V_HEADS, has_state=True),
        grid=(n_seq, 1),
        in_specs=[pl.BlockSpec((t_new, 2 * C_QK_W), lambda b, i: (blk0 + b, 0)),
                  pl.BlockSpec((t_new, C_V_W), lambda b, i: (blk0 + b, 0)),
                  pl.BlockSpec((t_new, 128), lambda b, i: (blk0 + b, 0)),
                  pl.BlockSpec((1, C_DV), lambda b, i: (0, 0)),
                  pl.BlockSpec((1,) + state, lambda b, i: (b, 0, 0, 0)),
                  pl.BlockSpec(memory_space=pl.ANY)],
        out_specs=[pl.BlockSpec((t_new, C_V_W), lambda b, i: (blk0 + b, 0)),
                   pl.BlockSpec((1,) + state, lambda b, i: (b, 0, 0, 0))],
        out_shape=[jax.ShapeDtypeStruct((m_total, C_V_W), F32),
                   jax.ShapeDtypeStruct((n_seq,) + state, F32)],
        scratch_shapes=[pltpu.VMEM(state, F32)],
        input_output_aliases={5: 0},
        compiler_params=_cparams(2), name="gdn_sample",
    )(qk, v, bg, nw, s0, o)
    return o, s_p, s_s


def _odd_out_body(o_ref, z_ref, x_ref, wo_ref, gf_ref, wr_ref, br_ref, x1_ref, hn_ref, route_ref, counts_ref, run_ref):
    y = (o_ref[...] * _silu(z_ref[...])).astype(BF16)
    x1 = x_ref[...] + _dot(y, wo_ref[...])
    x1_ref[...] = x1
    _route(x1, gf_ref, wr_ref, br_ref, hn_ref, route_ref, counts_ref, run_ref)


def _final_body(x_ref, a_ref, g_ref, y_ref):
    y_ref[...] = _rms(x_ref[...] + a_ref[...], g_ref[...])


def _final_norm(x, add, g, tm, row0, n_rows):
    d = x.shape[1]
    blk0 = row0 // tm
    row = pl.BlockSpec((tm, d), lambda i: (blk0 + i, 0))
    return pl.pallas_call(
        _final_body, grid=(n_rows // tm,),
        in_specs=[row, row, pl.BlockSpec((1, d), lambda i: (0, 0))],
        out_specs=pl.BlockSpec((tm, d), lambda i: (i, 0)),
        out_shape=jax.ShapeDtypeStruct((n_rows, d), F32),
        compiler_params=_cparams(1), name="final_norm",
    )(x, add, g.reshape(1, d))


def kernel(x_prompt, x_sample, cache_a_g0_kv, cache_a_g1_kv, cache_a_g2_kv, state_b_h, state_b_conv, state_c_S, state_c_conv, t5_bias, norm_mix, norm_ffn, norm_final, e_w_in, e_conv_w, e_conv_b, e_rg_wa, e_rg_ba, e_rg_wx, e_rg_bx, e_rg_lambda, e_w_out, o_w_in, o_conv_w, o_a_log, o_dt_bias, o_onorm_w, o_w_out, moe_rg_w, moe_rg_b, moe_re_w, moe_re_b, moe_w_gate, moe_w_up, moe_w_down):
    n_batch, seq, d = x_prompt.shape
    n_seq, t_new, _ = x_sample.shape
    mp = n_batch * seq
    ms = n_seq * t_new
    m = mp + ms
    assert t_new == 8 and seq % (DIL_PAIRS[2][1] * A_BLOCK) == 0
    assert e_w_in.shape[0] == 1 and o_w_in.shape[0] == 1
    tm = _row_tile(mp, ms)
    x = jnp.concatenate([x_prompt.reshape(mp, d), x_sample.reshape(ms, d)], axis=0)

    def moe_weights(layer):
        shp = (N_EXPERTS, d, EXPERT_FF)
        return (moe_w_gate[layer].reshape(shp).astype(BF16), moe_w_up[layer].reshape(shp).astype(BF16),
                moe_w_down[layer].reshape(N_EXPERTS, EXPERT_FF, d).astype(BF16))

    proj, *qkv_rm = _norm_proj_even(x, norm_mix[0], e_w_in[0].astype(BF16), tm, n_batch, seq)
    attn_p = []
    for g, (_, dil) in enumerate(DIL_PAIRS):
        attn_p.extend(_attn_prompt(qkv_rm[g], t5_bias[:, g], g, dil, n_batch, seq))
    caches = [c[0] for c in (cache_a_g0_kv, cache_a_g1_kv, cache_a_g2_kv)]
    attn_s = _attn_sample(proj, caches, t5_bias, n_seq, t_new, mp)
    y_b, bh_p, bh_s = _rglru(proj, state_b_conv[0], state_b_h[0], e_conv_w[0], e_conv_b[0], e_rg_wa[0], e_rg_ba[0],
                             e_rg_wx[0], e_rg_bx[0], e_rg_lambda[0], n_batch, seq, n_seq, t_new)
    wr, br = _router_weights(moe_rg_w[0], moe_rg_b[0], moe_re_w[0], moe_re_b[0])
    x1, hn, route, counts = _even_out(attn_p, attn_s, y_b, x, [e_w_out[0].astype(BF16), norm_ffn[0].reshape(1, d), wr, br],
                                      tm, n_batch, seq)
    y_moe = _moe(hn, route, counts, *moe_weights(0), tm)

    w_in1 = jnp.pad(o_w_in[0], ((0, 0), (0, ODD_IN_PAD - ODD_IN))).astype(BF16)
    x2, proj2 = _norm_proj(x1, (y_moe,), norm_mix[1], w_in1, tm, 896)
    qk, v, bg = _gdn_prep(proj2, state_c_conv[0], o_conv_w[0], o_a_log[0], o_dt_bias[0], n_batch, seq, n_seq, t_new)
    o_c, cs_p, cs_s = _gdn_chunks(qk, v, bg, o_onorm_w[0], state_c_S[0], n_batch, seq, n_seq, t_new, 64)
    wr, br = _router_weights(moe_rg_w[1], moe_rg_b[1], moe_re_w[1], moe_re_b[1])
    x3, hn, route, counts = _odd_out(o_c, proj2, x2, [o_w_out[0].astype(BF16), norm_ffn[1].reshape(1, d), wr, br], tm)
    y_moe = _moe(hn, route, counts, *moe_weights(1), tm)
    y_p = _final_norm(x3, y_moe, norm_final, tm, 0, mp).reshape(n_batch, seq, d)
    y_s = _final_norm(x3, y_moe, norm_final, tm, mp, ms).reshape(n_seq, t_new, d)

    def prompt_tail(a, keep, c0, c1):
        return jnp.stack([lax.slice(a, ((b + 1) * seq - keep, c0), ((b + 1) * seq, c1)) for b in range(n_batch)])

    def sample_rows(a, keep, c0, c1):
        return lax.slice(a, (mp, c0), (m, c1)).reshape(n_seq, t_new, c1 - c0)[:, t_new - keep:]

    new_a = []
    for g, (win, _) in enumerate(DIL_PAIRS):
        c0 = g * 3 * A_WIDTH + A_WIDTH
        keep = min(win, seq)
        new_a.append(prompt_tail(proj, keep, c0, c0 + 2 * A_WIDTH).reshape(1, n_batch, keep, 2, A_HEADS, A_HEAD_DIM))
        new_a.append(sample_rows(proj, t_new, c0, c0 + 2 * A_WIDTH).reshape(1, n_seq, t_new, 2, A_HEADS, A_HEAD_DIM))
    keep = CONV_W - 1
    bconv_p = prompt_tail(proj, keep, XB_COL, XB_COL + RNN_WIDTH)[None]
    bconv_s = sample_rows(proj, keep, XB_COL, XB_COL + RNN_WIDTH)[None]
    cconv_p = prompt_tail(proj2, keep, 0, C_CONV_DIM)[None]
    cconv_s = sample_rows(proj2, keep, 0, C_CONV_DIM)[None]
    return (y_p, y_s, *new_a, bh_p[None], bh_s[None], bconv_p, bconv_s, cs_p[None], cs_s[None], cconv_p, cconv_s)
```

```python
import functools
import math

import jax
import jax.numpy as jnp
import numpy as np
from jax import lax
from jax.experimental import pallas as pl
from jax.experimental.pallas import tpu as pltpu

F32 = jnp.float32
BF16 = jnp.bfloat16
EPS = 1e-6
NEG_INF = -1e30

D_MODEL = 1024
DIL_PAIRS = ((128, 1), (512, 4), (2048, 16))
N_DIL = 3
A_HEADS = 8
A_HEAD_DIM = 64
A_WIDTH = A_HEADS * A_HEAD_DIM
A_BLOCK = 128
SPAN = 128
NUM_BUCKETS = 32
MAX_DISTANCE = 2048
RNN_WIDTH = 512
RNN_BLOCKS = 8
CONV_W = 4
RG_C = 8.0
EVEN_IN = N_DIL * 3 * A_WIDTH + 2 * RNN_WIDTH
XB_COL = N_DIL * 3 * A_WIDTH
GB_COL = XB_COL + RNN_WIDTH
C_QK_HEADS = 8
C_V_HEADS = 16
C_DK = 128
C_DV = 128
C_QK_W = C_QK_HEADS * C_DK
C_V_W = C_V_HEADS * C_DV
C_CONV_DIM = 2 * C_QK_W + C_V_W
ODD_IN = C_CONV_DIM + C_V_W + 2 * C_V_HEADS
ODD_IN_PAD = 6272
Z_COL = C_CONV_DIM
BG_COL = C_CONV_DIM + C_V_W
GDN_BASE = 8
N_GROUPS = 4
EXPERTS_PER_GROUP = 8
N_EXPERTS = N_GROUPS * EXPERTS_PER_GROUP
EXPERT_FF = 256
ROUTE_LANES = 128
MOE_TILE = 256

LANES = 128
VMEM_LIMIT = 56 * 1024 * 1024


def _cparams(n_grid):
    return pltpu.CompilerParams(dimension_semantics=("arbitrary",) * n_grid,
                                vmem_limit_bytes=VMEM_LIMIT)


def _rms(x, g):
    return x * lax.rsqrt(jnp.mean(x * x, axis=-1, keepdims=True) + EPS) * g


def _silu(x):
    return x * jax.nn.sigmoid(x)


def _softplus(x):
    return jnp.maximum(x, 0.0) + jnp.log1p(jnp.exp(-jnp.abs(x)))


def _dot(a, b):
    return jnp.dot(a, b, preferred_element_type=F32)


def _dot_nt(a, b):
    return lax.dot_general(a, b, (((1,), (1,)), ((), ())), preferred_element_type=F32)


def _dot_tn(a, b):
    return lax.dot_general(a, b, (((0,), (0,)), ((), ())), preferred_element_type=F32)


def _dot_f32(a, b):
    return jnp.dot(a, b, preferred_element_type=F32, precision=lax.Precision.HIGHEST)


def _row_tile(*counts):
    for t in (256, 128, 64, 32, 16, 8):
        if all(c % t == 0 for c in counts):
            return t
    raise ValueError("token counts must be multiples of 8")


def _norm_proj_body(*refs, col_chunk, n_add):
    x_ref = refs[0]
    add_refs = refs[1:1 + n_add]
    g_ref, w_ref = refs[1 + n_add:3 + n_add]
    outs = refs[3 + n_add:]
    x = x_ref[...]
    for a in add_refs:
        x = x + a[...]
    if n_add:
        outs[0][...] = x
    o_ref = outs[-1]
    hb = _rms(x, g_ref[...]).astype(BF16)
    for c0 in range(0, o_ref.shape[1], col_chunk):
        o_ref[:, c0:c0 + col_chunk] = _dot(hb, w_ref[:, c0:c0 + col_chunk])


def _norm_proj(x, adds, g, w, tm, col_chunk):
    m, d = x.shape
    n = w.shape[1]
    row = pl.BlockSpec((tm, d), lambda i: (i, 0))
    out_shape = [jax.ShapeDtypeStruct((m, n), F32)]
    out_specs = [pl.BlockSpec((tm, n), lambda i: (i, 0))]
    if adds:
        out_shape = [jax.ShapeDtypeStruct((m, d), F32)] + out_shape
        out_specs = [row] + out_specs
    return pl.pallas_call(
        functools.partial(_norm_proj_body, col_chunk=col_chunk, n_add=len(adds)),
        grid=(m // tm,),
        in_specs=[row] * (1 + len(adds)) + [pl.BlockSpec((1, d), lambda i: (0, 0)),
                                            pl.BlockSpec((d, n), lambda i: (0, 0))],
        out_specs=out_specs, out_shape=out_shape,
        compiler_params=_cparams(1), name="norm_proj",
    )(x, *adds, g.reshape(1, d), w)


def _norm_proj_even_body(x_ref, g_ref, w_ref, o_ref, rm0_ref, rm1_ref, rm2_ref, lane_ref):
    tm = x_ref.shape[0]
    hb = _rms(x_ref[...], g_ref[...]).astype(BF16)
    rm_refs = (rm0_ref, rm1_ref, rm2_ref)
    for c in range(EVEN_IN // A_WIDTH):
        cols = slice(c * A_WIDTH, (c + 1) * A_WIDTH)
        res = _dot(hb, w_ref[:, cols])
        o_ref[:, cols] = res
        if c < 3 * N_DIL:
            g, j = divmod(c, 3)
            dil = DIL_PAIRS[g][1]
            if dil == 1:
                rm_refs[g][0, 0, :, j * A_WIDTH:(j + 1) * A_WIDTH] = res.astype(BF16)
            else:
                for t in range(A_WIDTH // LANES):
                    lane_ref[t] = res[:, t * LANES:(t + 1) * LANES]
                for r in range(dil):
                    part = [lane_ref[t, pl.ds(r, tm // dil, stride=dil), :] for t in range(A_WIDTH // LANES)]
                    rm_refs[g][0, r, :, j * A_WIDTH:(j + 1) * A_WIDTH] = jnp.concatenate(part, axis=1).astype(BF16)


def _norm_proj_even(x, g, w, tm, n_batch, seq):
    m, d = x.shape
    n = w.shape[1]
    tps = seq // tm
    npt = n_batch * tps

    def rm_index(i):
        return (jnp.where(i < npt, i // tps, n_batch), 0, jnp.where(i < npt, i % tps, i - npt), 0)

    rm_shapes = [jax.ShapeDtypeStruct((n_batch + 1, dil, seq // dil, 3 * A_WIDTH), BF16) for _, dil in DIL_PAIRS]
    rm_specs = [pl.BlockSpec((1, dil, tm // dil, 3 * A_WIDTH), rm_index) for _, dil in DIL_PAIRS]
    return pl.pallas_call(
        _norm_proj_even_body,
        grid=(m // tm,),
        in_specs=[pl.BlockSpec((tm, d), lambda i: (i, 0)), pl.BlockSpec((1, d), lambda i: (0, 0)),
                  pl.BlockSpec((d, n), lambda i: (0, 0))],
        out_specs=[pl.BlockSpec((tm, n), lambda i: (i, 0))] + rm_specs,
        out_shape=[jax.ShapeDtypeStruct((m, n), F32)] + rm_shapes,
        scratch_shapes=[pltpu.VMEM((A_WIDTH // LANES, tm, LANES), F32)],
        compiler_params=_cparams(1), name="norm_proj_even",
    )(x, g.reshape(1, d), w)


def _t5_bucket(dist):
    max_exact = NUM_BUCKETS // 2
    d = np.maximum(dist, 1).astype(np.float32)
    large = max_exact + (np.log(d / max_exact) / np.log(MAX_DISTANCE / max_exact)
                         * (NUM_BUCKETS - max_exact)).astype(np.int32)
    large = np.minimum(large, NUM_BUCKETS - 1)
    return np.where(dist < max_exact, dist, large).astype(np.int32)


def _bucket_lookup(tab, buckets):
    onehot = jnp.asarray(buckets[..., None, None] == np.arange(NUM_BUCKETS)[:, None])
    return jnp.sum(jnp.where(onehot, tab.astype(F32), 0.0), axis=-2)


def _prompt_bias(tab, dil):
    qi = np.arange(A_BLOCK)[:, None]
    km = np.arange(2 * A_BLOCK)[None, :]
    delta = A_BLOCK + qi - km
    valid = (delta >= 0) & (delta <= SPAN)
    bias = _bucket_lookup(tab, _t5_bucket(np.clip(delta, 0, SPAN) * dil))
    bias = jnp.where(valid[..., None], bias, NEG_INF)
    return jnp.transpose(bias, (2, 0, 1))


def _attn_prompt_body(q_ref, kp_ref, ko_ref, vp_ref, vo_ref, bias_ref, o_ref, lse_ref):
    first = pl.program_id(2) == 0
    scale = A_HEAD_DIM ** -0.5
    q = q_ref[0, 0]
    k = jnp.concatenate([kp_ref[0, 0], ko_ref[0, 0]], axis=0)
    v = jnp.concatenate([vp_ref[0, 0], vo_ref[0, 0]], axis=0)
    o_ref = o_ref.at[0, 0]
    lse_ref = lse_ref.at[0, 0]
    km = lax.broadcasted_iota(jnp.int32, (1, 2 * A_BLOCK), 1)
    no_prev = jnp.logical_and(first, km < A_BLOCK)
    heads_per_tile = LANES // A_HEAD_DIM
    head_of_lane = lax.broadcasted_iota(jnp.int32, (1, LANES), 1) // A_HEAD_DIM
    for t in range(A_WIDTH // LANES):
        sl = slice(t * LANES, (t + 1) * LANES)
        q_t, k_t, v_t = q[:, sl], k[:, sl], v[:, sl]
        o_t = jnp.zeros((A_BLOCK, LANES), F32)
        lse_t = jnp.zeros((A_BLOCK, LANES), F32)
        for j in range(heads_per_tile):
            mine = head_of_lane == j
            s = _dot_nt(jnp.where(mine, q_t, jnp.zeros_like(q_t)), k_t) * scale + bias_ref[t * heads_per_tile + j]
            s = jnp.where(no_prev, NEG_INF, s)
            m = jnp.max(s, axis=-1, keepdims=True)
            p = jnp.exp(s - m)
            den = jnp.sum(p, axis=-1, keepdims=True)
            o_t = jnp.where(mine, _dot(p.astype(BF16), v_t) / den, o_t)
            lse_t = jnp.where(mine, m + jnp.log(den), lse_t)
        o_ref[:, sl] = o_t
        lse_ref[:, sl] = lse_t


def _attn_prompt(qkv_rm, tab, g, dil, n_batch, seq):
    sub_len = seq // dil
    nb = sub_len // A_BLOCK

    def spec(j, prev):
        def index(b, r, i):
            return (b, r, jnp.maximum(i - 1, 0) if prev else i, j)
        return pl.BlockSpec((1, 1, A_BLOCK, A_WIDTH), index)

    out_spec = pl.BlockSpec((1, 1, A_BLOCK, A_WIDTH), lambda b, r, i: (b, r, i, 0))
    out_sds = jax.ShapeDtypeStruct((n_batch, dil, sub_len, A_WIDTH), F32)
    return pl.pallas_call(
        _attn_prompt_body,
        grid=(n_batch, dil, nb),
        in_specs=[spec(0, False), spec(1, True), spec(1, False), spec(2, True), spec(2, False),
                  pl.BlockSpec((A_HEADS, A_BLOCK, 2 * A_BLOCK), lambda b, r, i: (0, 0, 0))],
        out_specs=[out_spec, out_spec], out_shape=[out_sds, out_sds],
        compiler_params=_cparams(3), name="attn_prompt_g%d" % g,
    )(qkv_rm, qkv_rm, qkv_rm, qkv_rm, qkv_rm, _prompt_bias(tab, dil))


def _attn_sample_body(q_ref, n0_ref, n1_ref, n2_ref, c0_ref, c1_ref, c2_ref, bias_ref, *out_refs, t_new):
    scale = A_HEAD_DIM ** -0.5
    ones = jnp.ones((A_HEAD_DIM, LANES), BF16)
    caches = (c0_ref, c1_ref, c2_ref)
    news = (n0_ref, n1_ref, n2_ref)

    def lane_sums(x):
        n = x.shape[0]
        return _dot(x.reshape(n * A_HEADS, A_HEAD_DIM).astype(BF16), ones).reshape(n, A_HEADS, LANES)

    for g in range(N_DIL):
        dil = DIL_PAIRS[g][1]
        cache, new = caches[g], news[g]
        n_rows = cache.shape[2]
        for t in range(t_new):
            q = q_ref[0, g, t] * scale
            res = t % dil
            n_new = t // dil + 1
            n_cache = SPAN + 1 - n_new
            lo = n_rows - n_cache
            new_steps = range(n_new - 1, -1, -1)
            k_all = jnp.concatenate([cache[0, 0, lo:n_rows, res, 0]] + [new[0, t - dil * s, 0][None] for s in new_steps])
            v_all = jnp.concatenate([cache[0, 0, lo:n_rows, res, 1]] + [new[0, t - dil * s, 1][None] for s in new_steps])
            s_all = lane_sums(k_all * q) + bias_ref[g]
            m = jnp.max(s_all, axis=0)
            p = jnp.exp(s_all - m)
            den = jnp.sum(p, axis=0)
            o = jnp.sum(p[:, :, :A_HEAD_DIM] * v_all, axis=0) / den[:, :A_HEAD_DIM]
            out_refs[2 * g][0, t] = o
            out_refs[2 * g + 1][0, t] = (m + jnp.log(den))[:, :A_HEAD_DIM]


def _attn_sample(q_new, kv_new, caches, t5_bias, n_seq, t_new):
    head = (A_HEADS, A_HEAD_DIM)
    views, cache_specs = [], []
    for g, (win, dil) in enumerate(DIL_PAIRS):
        cache_len = caches[g].shape[2]
        assert cache_len == win and win == SPAN * dil, "cache must hold exactly one window"
        n_res = min(dil, t_new)
        views.append(caches[g].reshape(1, n_seq, SPAN, dil, 2, *head))
        cache_specs.append(pl.BlockSpec((1, 1, SPAN, n_res, 2) + head, lambda b: (0, b, 0, 0, 0, 0, 0)))
    steps = SPAN - np.arange(SPAN + 1)
    bias = jnp.stack([_bucket_lookup(t5_bias[:, g], _t5_bucket(steps * dil)) for g, (_, dil) in enumerate(DIL_PAIRS)])
    bias = jnp.broadcast_to(bias[..., None], bias.shape + (LANES,))
    new_spec = pl.BlockSpec((None, 1, t_new, 2) + head, lambda b: (0, b, 0, 0, 0, 0))
    out_spec = pl.BlockSpec((1, t_new) + head, lambda b: (b, 0, 0, 0))
    outs = pl.pallas_call(
        functools.partial(_attn_sample_body, t_new=t_new),
        grid=(n_seq,),
        in_specs=[pl.BlockSpec((1, N_DIL, t_new) + head, lambda b: (b, 0, 0, 0, 0)), new_spec, new_spec, new_spec]
                 + cache_specs + [pl.BlockSpec(bias.shape, lambda b: (0, 0, 0, 0))],
        out_specs=[out_spec] * (2 * N_DIL),
        out_shape=[jax.ShapeDtypeStruct((n_seq, t_new) + head, F32)] * (2 * N_DIL),
        compiler_params=_cparams(1), name="attn_sample",
    )(q_new, *kv_new, *views, bias)
    return [o.reshape(n_seq * t_new, A_WIDTH) for o in outs]


def _shift_rows(x, s, fill, axis):
    t = lax.broadcasted_iota(jnp.int32, x.shape, axis)
    return jnp.where(t >= s, pltpu.roll(x, s, axis), fill)


def _linear_scan(a, b, axis):
    n = a.shape[axis]
    s = 1
    while s < n:
        b = b + a * _shift_rows(b, s, 0.0, axis)
        a = a * _shift_rows(a, s, 1.0, axis)
        s *= 2
    return a, b


def _rglru_gates(xc, wa_ref, wx_ref, vec_ref):
    xcb = xc.astype(BF16)
    r = jax.nn.sigmoid(_dot(xcb, wa_ref[...]) + vec_ref[1:2])
    ig = jax.nn.sigmoid(_dot(xcb, wx_ref[...]) + vec_ref[2:3])
    log_a = -RG_C * r * _softplus(-vec_ref[3:4])
    a = jnp.exp(log_a)
    b = jnp.sqrt(1.0 - jnp.exp(2.0 * log_a)) * (ig * xc)
    return a, b


def _gelu(x):
    return 0.5 * x * (1.0 + jnp.tanh(math.sqrt(2.0 / math.pi) * (x + 0.044715 * (x * x * x))))


def _rglru_prompt_body(xb_ref, gb_ref, cw_ref, wa_ref, wx_ref, vec_ref, y_ref, hl_ref, tail_ref, h_ref):
    @pl.when(pl.program_id(1) == 0)
    def _():
        tail_ref[...] = jnp.zeros_like(tail_ref)
        h_ref[...] = jnp.zeros_like(h_ref)

    x = xb_ref[...]
    tt = x.shape[0]
    xe = jnp.concatenate([tail_ref[...], x], axis=0)
    xc = vec_ref[0:1] + x * cw_ref[CONV_W - 1:CONV_W]
    for j in range(1, CONV_W):
        xc = xc + pltpu.roll(xe, j, 0)[8:8 + tt] * cw_ref[CONV_W - 1 - j:CONV_W - j]
    tail_ref[...] = x[tt - 8:tt]
    a, b = _rglru_gates(xc, wa_ref, wx_ref, vec_ref)
    a_cum, h = _linear_scan(a, b, 0)
    h = h + a_cum * h_ref[...]
    h_ref[...] = h[tt - 1:tt]
    hl_ref[0] = h[tt - 1:tt]
    y_ref[...] = h * _gelu(gb_ref[...])


def _rglru_sample_body(xb_ref, gb_ref, c0_ref, h0_ref, cw_ref, wa_ref, wx_ref, vec_ref, prev_ref, y_ref, hl_ref, *, t_new):
    del prev_ref
    x = xb_ref[...]
    rows = x.shape[0]
    ns = rows // t_new
    x3 = x.reshape(ns, t_new, RNN_WIDTH)
    xe = jnp.concatenate([c0_ref[...], x3], axis=1)
    xc = vec_ref[0:1] + x3 * cw_ref[CONV_W - 1:CONV_W]
    for j in range(1, CONV_W):
        xc = xc + pltpu.roll(xe, j, 1)[:, 8:8 + t_new] * cw_ref[CONV_W - 1 - j:CONV_W - j]
    a, b = _rglru_gates(xc.reshape(rows, RNN_WIDTH), wa_ref, wx_ref, vec_ref)
    a_cum, h = _linear_scan(a.reshape(ns, t_new, RNN_WIDTH), b.reshape(ns, t_new, RNN_WIDTH), 1)
    h = h + a_cum * h0_ref[...][:, None, :]
    hl_ref[...] = h[:, t_new - 1, :]
    y_ref[...] = h.reshape(rows, RNN_WIDTH) * _gelu(gb_ref[...])


def _block_diag(w):
    nb, bi, bo = w.shape
    eye = jnp.eye(nb, dtype=w.dtype)
    return (w[:, :, None, :] * eye[:, None, :, None]).reshape(nb * bi, nb * bo)


def _rglru(proj, conv0, h0, conv_w, conv_b, wa, ba, wx, bx, lam, n_batch, seq, n_seq, t_new):
    m_total = proj.shape[0]
    mp = n_batch * seq
    wa_bd = _block_diag(wa).astype(BF16)
    wx_bd = _block_diag(wx).astype(BF16)
    vec = jnp.stack([conv_b, ba, bx, lam]).astype(F32)
    xcol = XB_COL // RNN_WIDTH
    gcol = GB_COL // RNN_WIDTH
    full2 = lambda a, nd: pl.BlockSpec(a.shape, lambda *_: (0,) * a.ndim)
    tt = _row_tile(seq)
    nt = seq // tt
    w_specs2 = [pl.BlockSpec(a.shape, lambda b, i: (0, 0)) for a in (conv_w, wa_bd, wx_bd, vec)]
    y, hl_p = pl.pallas_call(
        _rglru_prompt_body,
        grid=(n_batch, nt),
        in_specs=[pl.BlockSpec((tt, RNN_WIDTH), lambda b, i: (b * nt + i, xcol)),
                  pl.BlockSpec((tt, RNN_WIDTH), lambda b, i: (b * nt + i, gcol))] + w_specs2,
        out_specs=[pl.BlockSpec((tt, RNN_WIDTH), lambda b, i: (b * nt + i, 0)),
                   pl.BlockSpec((1, 1, RNN_WIDTH), lambda b, i: (b, 0, 0))],
        out_shape=[jax.ShapeDtypeStruct((m_total, RNN_WIDTH), F32),
                   jax.ShapeDtypeStruct((n_batch, 1, RNN_WIDTH), F32)],
        scratch_shapes=[pltpu.VMEM((8, RNN_WIDTH), F32), pltpu.VMEM((1, RNN_WIDTH), F32)],
        compiler_params=_cparams(2), name="rglru_prompt",
    )(proj, proj, conv_w, wa_bd, wx_bd, vec)
    del full2
    ts = 32 if n_seq % 32 == 0 else 8
    rows = ts * t_new
    blk0 = mp // rows
    c0p = jnp.pad(conv0, ((0, 0), (8 - (CONV_W - 1), 0), (0, 0)))
    w_specs1 = [pl.BlockSpec(a.shape, lambda i: (0, 0)) for a in (conv_w, wa_bd, wx_bd, vec)]
    y, hl_s = pl.pallas_call(
        functools.partial(_rglru_sample_body, t_new=t_new),
        grid=(n_seq // ts,),
        in_specs=[pl.BlockSpec((rows, RNN_WIDTH), lambda i: (blk0 + i, xcol)),
                  pl.BlockSpec((rows, RNN_WIDTH), lambda i: (blk0 + i, gcol)),
                  pl.BlockSpec((ts, 8, RNN_WIDTH), lambda i: (i, 0, 0)),
                  pl.BlockSpec((ts, RNN_WIDTH), lambda i: (i, 0))] + w_specs1
                 + [pl.BlockSpec(memory_space=pl.ANY)],
        out_specs=[pl.BlockSpec((rows, RNN_WIDTH), lambda i: (blk0 + i, 0)),
                   pl.BlockSpec((ts, RNN_WIDTH), lambda i: (i, 0))],
        out_shape=[jax.ShapeDtypeStruct((m_total, RNN_WIDTH), F32),
                   jax.ShapeDtypeStruct((n_seq, RNN_WIDTH), F32)],
        input_output_aliases={8: 0},
        compiler_params=_cparams(1), name="rglru_sample",
    )(proj, proj, c0p, h0, conv_w, wa_bd, wx_bd, vec, y)
    return y, hl_p.reshape(n_batch, RNN_WIDTH), hl_s


def _pack_bf16_pair(x):
    w = x.shape[1] // 2
    hi = lax.bitcast_convert_type(x[:, :w].astype(BF16).astype(F32), jnp.uint32)
    lo = lax.bitcast_convert_type(x[:, w:].astype(BF16).astype(F32), jnp.uint32)
    return hi | (lo >> 16)


def _unpack_bf16_pair(p):
    hi = lax.bitcast_convert_type(p & jnp.uint32(0xFFFF0000), F32).astype(BF16)
    lo = lax.bitcast_convert_type(p << 16, F32).astype(BF16)
    return hi, lo


def _route(x1, gf_ref, wr_ref, br_ref, hn_ref, route_ref, counts_ref, run_ref):
    @pl.when(pl.program_id(0) == 0)
    def _():
        run_ref[...] = jnp.zeros_like(run_ref)

    hn = _rms(x1, gf_ref[...])
    hn_ref[...] = _pack_bf16_pair(hn)
    hn_hi = hn.astype(BF16)
    hn_lo = (hn - hn_hi.astype(F32)).astype(BF16)
    logits = _dot(hn_hi, wr_ref[0]) + (_dot(hn_hi, wr_ref[1]) + _dot(hn_lo, wr_ref[0])) + br_ref[...]
    lane = lax.broadcasted_iota(jnp.int32, logits.shape, 1)
    is_coarse = lane < N_GROUPS
    coarse = jnp.where(is_coarse, logits, -jnp.inf)
    cmax = jnp.max(coarse, axis=-1, keepdims=True)
    grp = jnp.min(jnp.where(coarse == cmax, lane, ROUTE_LANES), axis=-1, keepdims=True)
    p_grp = 1.0 / jnp.sum(jnp.where(is_coarse, jnp.exp(logits - cmax), 0.0), axis=-1, keepdims=True)
    expert = lane - N_GROUPS
    in_grp = (lane >= N_GROUPS) & (expert < N_EXPERTS) & (expert // EXPERTS_PER_GROUP == grp)
    fine = jnp.where(in_grp, logits, -jnp.inf)
    v1 = jnp.max(fine, axis=-1, keepdims=True)
    i1 = jnp.min(jnp.where(fine == v1, lane, ROUTE_LANES), axis=-1, keepdims=True)
    fine2 = jnp.where(lane == i1, -jnp.inf, fine)
    v2 = jnp.max(fine2, axis=-1, keepdims=True)
    i2 = jnp.min(jnp.where(fine2 == v2, lane, ROUTE_LANES), axis=-1, keepdims=True)
    e2 = jnp.exp(v2 - v1)
    w1 = p_grp / (1.0 + e2)
    w2 = p_grp * e2 / (1.0 + e2)
    tm = logits.shape[0]
    sel = jnp.where(lane == i1, 1.0, jnp.where(lane == i2, 1.0, 0.0))
    ri = lax.broadcasted_iota(jnp.int32, (tm, tm), 0)
    ci = lax.broadcasted_iota(jnp.int32, (tm, tm), 1)
    earlier = jnp.where(ri > ci, 1.0, 0.0).astype(BF16)
    before = _dot(earlier, sel.astype(BF16)) + run_ref[...]
    r1 = jnp.sum(jnp.where(lane == i1, before, 0.0), axis=-1, keepdims=True)
    r2 = jnp.sum(jnp.where(lane == i2, before, 0.0), axis=-1, keepdims=True)
    run_ref[...] = run_ref[...] + jnp.sum(sel, axis=0, keepdims=True)
    counts_ref[...] = run_ref[...]
    route = jnp.where(lane == 0, (i1 - N_GROUPS).astype(F32), 0.0)
    route = jnp.where(lane == 1, (i2 - N_GROUPS).astype(F32), route)
    route = jnp.where(lane == 2, w1, route)
    route = jnp.where(lane == 3, w2, route)
    route = jnp.where(lane == 4, r1, route)
    route = jnp.where(lane == 5, r2, route)
    route_ref[...] = route


def _merge_groups(os, ls):
    mx = jnp.maximum(jnp.maximum(ls[0], ls[1]), ls[2])
    es = [jnp.exp(l - mx) for l in ls]
    return (es[0] * os[0] + es[1] * os[1] + es[2] * os[2]) / (es[0] + es[1] + es[2])


def _even_out_body(o0, l0, o1, l1, o2, l2, so0, sl0, so1, sl1, so2, sl2, yb_ref, x_ref, wo_ref, gf_ref, wr_ref, br_ref,
                   x1_ref, hn_ref, route_ref, counts_ref, n1o, n1l, n2o, n2l, oa_ref, run_ref, *, n_prompt_tiles):
    i = pl.program_id(0)
    tm = x_ref.shape[0]

    @pl.when(i < n_prompt_tiles)
    def _():
        nat = []
        for src, dst, dil in ((o1, n1o, DIL_PAIRS[1][1]), (l1, n1l, DIL_PAIRS[1][1]),
                              (o2, n2o, DIL_PAIRS[2][1]), (l2, n2l, DIL_PAIRS[2][1])):
            for r in range(dil):
                blk = src[0, r]
                for t in range(A_WIDTH // LANES):
                    dst[t, pl.ds(r, tm // dil, stride=dil), :] = blk[:, t * LANES:(t + 1) * LANES]
            nat.append(jnp.concatenate([dst[t] for t in range(A_WIDTH // LANES)], axis=1))
        oa_ref[...] = _merge_groups((o0[0, 0], nat[0], nat[2]), (l0[0, 0], nat[1], nat[3]))

    @pl.when(i >= n_prompt_tiles)
    def _():
        oa_ref[...] = _merge_groups((so0[...], so1[...], so2[...]), (sl0[...], sl1[...], sl2[...]))

    cat = jnp.concatenate([oa_ref[...], yb_ref[...]], axis=-1).astype(BF16)
    x1 = x_ref[...] + _dot(cat, wo_ref[...])
    x1_ref[...] = x1
    _route(x1, gf_ref, wr_ref, br_ref, hn_ref, route_ref, counts_ref, run_ref)


def _router_weights(rg_w, rg_b, re_w, re_b):
    d = rg_w.shape[0]
    wr = jnp.concatenate([rg_w, re_w.reshape(d, N_EXPERTS)], axis=1)
    br = jnp.concatenate([rg_b, re_b.reshape(N_EXPERTS)])
    pad = ROUTE_LANES - wr.shape[1]
    wr = jnp.pad(wr, ((0, 0), (0, pad))).astype(F32)
    wr_hi = wr.astype(BF16)
    wr_lo = (wr - wr_hi.astype(F32)).astype(BF16)
    return jnp.stack([wr_hi, wr_lo]), jnp.pad(br, (0, pad)).reshape(1, ROUTE_LANES).astype(F32)


def _mix_out_call(body, in_specs, args, consts, m, d, tm, scratch, name):
    const_spec = lambda a: pl.BlockSpec(a.shape, lambda i: (0,) * a.ndim)
    return pl.pallas_call(
        body, grid=(m // tm,), in_specs=list(in_specs) + [const_spec(c) for c in consts],
        out_specs=[pl.BlockSpec((tm, d), lambda i: (i, 0)), pl.BlockSpec((tm, d // 2), lambda i: (i, 0)),
                   pl.BlockSpec((tm, ROUTE_LANES), lambda i: (i, 0)), pl.BlockSpec((1, ROUTE_LANES), lambda i: (0, 0))],
        out_shape=[jax.ShapeDtypeStruct((m, d), F32), jax.ShapeDtypeStruct((m, d // 2), jnp.uint32),
                   jax.ShapeDtypeStruct((m, ROUTE_LANES), F32), jax.ShapeDtypeStruct((1, ROUTE_LANES), F32)],
        scratch_shapes=list(scratch) + [pltpu.VMEM((1, ROUTE_LANES), F32)],
        compiler_params=_cparams(1), name=name,
    )(*args, *consts)


def _even_out(attn_p, attn_s, y_b, x, consts, tm, n_batch, seq):
    m, d = x.shape
    tps = seq // tm
    npt = n_batch * tps
    in_specs = []
    for g, (_, dil) in enumerate(DIL_PAIRS):
        def index(i):
            return (jnp.minimum(i // tps, n_batch - 1), 0, jnp.where(i < npt, i % tps, 0), 0)
        in_specs += [pl.BlockSpec((1, dil, tm // dil, A_WIDTH), index)] * 2
    in_specs += [pl.BlockSpec((tm, A_WIDTH), lambda i: (jnp.maximum(i - npt, 0), 0))] * (2 * N_DIL)
    in_specs += [pl.BlockSpec((tm, RNN_WIDTH), lambda i: (i, 0)), pl.BlockSpec((tm, d), lambda i: (i, 0))]
    scratch = [pltpu.VMEM((A_WIDTH // LANES, tm, LANES), F32)] * 4 + [pltpu.VMEM((tm, A_WIDTH), F32)]
    return _mix_out_call(functools.partial(_even_out_body, n_prompt_tiles=npt), in_specs,
                         list(attn_p) + list(attn_s) + [y_b, x], consts, m, d, tm, scratch, "even_out")


def _odd_out(o_c, proj2, x, consts, tm):
    m, d = x.shape
    in_specs = [pl.BlockSpec((tm, C_V_W), lambda i: (i, 0)),
                pl.BlockSpec((tm, C_V_W), lambda i: (i, Z_COL // C_V_W)),
                pl.BlockSpec((tm, d), lambda i: (i, 0))]
    return _mix_out_call(_odd_out_body, in_specs, [o_c, proj2, x], consts, m, d, tm, [], "odd_out")


def _row_copy(src_ref, src_row, dst_ref, dst_row, sem):
    return pltpu.make_async_copy(src_ref.at[pl.ds(src_row, 1)], dst_ref.at[pl.ds(dst_row, 1)], sem)


def _moe_dispatch_body(s0_ref, s1_ref, fill_ref, used_ref, hn_ref, xs_ref, zero_ref, fill_sem, row_sem):
    i = pl.program_id(0)
    tm = hn_ref.shape[0]

    def fill_copy(e):
        return pltpu.make_async_copy(zero_ref, xs_ref.at[pl.ds(pl.multiple_of(fill_ref[e], MOE_TILE), MOE_TILE)], fill_sem)

    @pl.when(i == 0)
    def _():
        zero_ref[...] = jnp.zeros_like(zero_ref)
        for e in range(N_EXPERTS):
            @pl.when(used_ref[e] > 0)
            def _():
                fill_copy(e).start()
        for e in range(N_EXPERTS):
            @pl.when(used_ref[e] > 0)
            def _():
                fill_copy(e).wait()

    base = i * tm

    def scatter_row(j, carry):
        _row_copy(hn_ref, j, xs_ref, s0_ref[base + j], row_sem).start()
        _row_copy(hn_ref, j, xs_ref, s1_ref[base + j], row_sem).start(priority=1)
        return carry

    lax.fori_loop(0, tm, scatter_row, 0, unroll=8)
    for _ in range(2):
        pltpu.make_async_copy(hn_ref, xs_ref.at[pl.ds(0, tm)], row_sem).wait()


def _moe_ffn_body(te_ref, nu_ref, xs_ref, wg_ref, wu_ref, wd_ref, o_ref):
    del te_ref

    @pl.when(pl.program_id(0) < nu_ref[0])
    def _():
        xa, xb = _unpack_bf16_pair(xs_ref[...])
        half = xa.shape[1]
        hg = _dot(xa, wg_ref[0, :half]) + _dot(xb, wg_ref[0, half:])
        hu = _dot(xa, wu_ref[0, :half]) + _dot(xb, wu_ref[0, half:])
        o_ref[...] = _dot((_silu(hg) * hu).astype(BF16), wd_ref[0])

    @pl.when(pl.program_id(0) >= nu_ref[0])
    def _():
        o_ref[...] = jnp.zeros_like(o_ref)


def _moe_combine_body(s0_ref, s1_ref, route_ref, ys_ref, y_ref, buf_ref, sem):
    tm = y_ref.shape[0]
    base = pl.program_id(0) * tm

    def gather_row(j, carry):
        _row_copy(ys_ref, s0_ref[base + j], buf_ref.at[0], j, sem).start()
        _row_copy(ys_ref, s1_ref[base + j], buf_ref.at[1], j, sem).start(priority=1)
        return carry

    lax.fori_loop(0, tm, gather_row, 0, unroll=8)
    for k in range(2):
        pltpu.make_async_copy(ys_ref.at[pl.ds(0, tm)], buf_ref.at[k], sem).wait()
    y_ref[...] = route_ref[:, 2:3] * buf_ref[0] + route_ref[:, 3:4] * buf_ref[1]


def _moe(hn, route, counts, w_gate, w_up, w_down, tm):
    m = hn.shape[0]
    d = w_gate.shape[1]
    tile = MOE_TILE
    n_slots = 2 * m + N_EXPERTS * tile
    n_tiles = n_slots // tile
    cnt = counts[0, N_GROUPS:N_GROUPS + N_EXPERTS].astype(jnp.int32)
    padded = ((cnt + tile - 1) // tile) * tile
    pad_end = jnp.cumsum(padded)
    pad_start = pad_end - padded
    experts = jnp.arange(N_EXPERTS, dtype=jnp.int32)

    def slots(e, pos):
        start = jnp.sum(jnp.where(e.astype(jnp.int32)[:, None] == experts[None, :], pad_start[None, :], 0), axis=1)
        return (start + pos.astype(jnp.int32)).astype(jnp.int32)

    slot0 = slots(route[:, 0], route[:, 4])
    slot1 = slots(route[:, 1], route[:, 5])
    tile_start = jnp.arange(n_tiles, dtype=jnp.int32) * tile
    tile_expert = jnp.sum((tile_start[:, None] >= pad_end[None, :]).astype(jnp.int32), axis=1)
    tile_expert = jnp.minimum(tile_expert, N_EXPERTS - 1).astype(jnp.int32)
    n_used = (pad_end[-1:] // tile).astype(jnp.int32)
    fill_start = jnp.maximum(pad_end - tile, 0).astype(jnp.int32)

    xs = pl.pallas_call(
        _moe_dispatch_body,
        grid_spec=pltpu.PrefetchScalarGridSpec(
            num_scalar_prefetch=4, grid=(m // tm,),
            in_specs=[pl.BlockSpec((tm, d // 2), lambda i, *_: (i, 0))],
            out_specs=pl.BlockSpec(memory_space=pl.ANY),
            scratch_shapes=[pltpu.VMEM((tile, d // 2), jnp.uint32), pltpu.SemaphoreType.DMA(()),
                            pltpu.SemaphoreType.DMA(())]),
        out_shape=jax.ShapeDtypeStruct((n_slots, d // 2), jnp.uint32),
        compiler_params=_cparams(1), name="moe_dispatch",
    )(slot0, slot1, fill_start, cnt, hn)

    def used(i, nu):
        return jnp.minimum(i, jnp.maximum(nu[0] - 1, 0))

    ys = pl.pallas_call(
        _moe_ffn_body,
        grid_spec=pltpu.PrefetchScalarGridSpec(
            num_scalar_prefetch=2, grid=(n_tiles,),
            in_specs=[pl.BlockSpec((tile, d // 2), lambda i, te, nu: (used(i, nu), 0)),
                      pl.BlockSpec((1, d, EXPERT_FF), lambda i, te, nu: (te[used(i, nu)], 0, 0)),
                      pl.BlockSpec((1, d, EXPERT_FF), lambda i, te, nu: (te[used(i, nu)], 0, 0)),
                      pl.BlockSpec((1, EXPERT_FF, d), lambda i, te, nu: (te[used(i, nu)], 0, 0))],
            out_specs=pl.BlockSpec((tile, d), lambda i, te, nu: (i, 0))),
        out_shape=jax.ShapeDtypeStruct((n_slots, d), F32),
        compiler_params=_cparams(1), name="moe_ffn",
    )(tile_expert, n_used, xs, w_gate, w_up, w_down)

    return pl.pallas_call(
        _moe_combine_body,
        grid_spec=pltpu.PrefetchScalarGridSpec(
            num_scalar_prefetch=2, grid=(m // tm,),
            in_specs=[pl.BlockSpec((tm, ROUTE_LANES), lambda i, *_: (i, 0)),
                      pl.BlockSpec(memory_space=pl.ANY)],
            out_specs=pl.BlockSpec((tm, d), lambda i, *_: (i, 0)),
            scratch_shapes=[pltpu.VMEM((2, tm, d), F32), pltpu.SemaphoreType.DMA(())]),
        out_shape=jax.ShapeDtypeStruct((m, d), F32),
        compiler_params=_cparams(1), name="moe_combine",
    )(slot0, slot1, route, ys)


def _gdn_prep_math(xe, tt, cw_ref, bg, av_ref):
    acc = xe[8:8 + tt] * cw_ref[CONV_W - 1:CONV_W]
    for j in range(1, CONV_W):
        acc = acc + pltpu.roll(xe, j, 0)[8:8 + tt] * cw_ref[CONV_W - 1 - j:CONV_W - j]
    qkv = _silu(acc)
    outs = []
    for h in range(2 * C_QK_HEADS):
        xh = qkv[:, h * C_DK:(h + 1) * C_DK]
        xh = xh * lax.rsqrt(jnp.sum(xh * xh, axis=-1, keepdims=True) + EPS)
        if h < C_QK_HEADS:
            xh = xh * (C_DK ** -0.5)
        outs.append(xh)
    qk = jnp.concatenate(outs, axis=-1)
    v = qkv[:, 2 * C_QK_W:]
    lane = lax.broadcasted_iota(jnp.int32, bg.shape, 1)
    gdec = -jnp.exp(av_ref[0:1]) * _softplus(bg + av_ref[1:2])
    bgo = jnp.where(lane < C_V_HEADS, jax.nn.sigmoid(bg), gdec)
    return qk, v, bgo


def _gdn_prep_prompt_body(x_ref, bg_ref, cw_ref, av_ref, qk_ref, v_ref, bgo_ref, tail_ref):
    @pl.when(pl.program_id(1) == 0)
    def _():
        tail_ref[...] = jnp.zeros_like(tail_ref)

    x = x_ref[...]
    tt = x.shape[0]
    xe = jnp.concatenate([tail_ref[...], x], axis=0)
    tail_ref[...] = x[tt - 8:tt]
    qk, v, bgo = _gdn_prep_math(xe, tt, cw_ref, bg_ref[...], av_ref)
    qk_ref[...] = qk
    v_ref[...] = v
    bgo_ref[...] = bgo


def _gdn_prep_sample_body(x_ref, bg_ref, c0_ref, cw_ref, av_ref, p0, p1, p2, qk_ref, v_ref, bgo_ref, *, t_new):
    del p0, p1, p2
    xe = jnp.concatenate([c0_ref[0], x_ref[...]], axis=0)
    qk, v, bgo = _gdn_prep_math(xe, t_new, cw_ref, bg_ref[...], av_ref)
    qk_ref[...] = qk
    v_ref[...] = v
    bgo_ref[...] = bgo


def _gdn_prep(proj, conv0, conv_w, a_log, dt_bias, n_batch, seq, n_seq, t_new):
    m_total = proj.shape[0]
    mp = n_batch * seq
    av = jnp.zeros((2, 128), F32)
    av = av.at[0, C_V_HEADS:2 * C_V_HEADS].set(a_log).at[1, C_V_HEADS:2 * C_V_HEADS].set(dt_bias)
    bg_col = BG_COL // 128
    tt = min(_row_tile(seq), 128)
    nt = seq // tt
    out_shape = [jax.ShapeDtypeStruct((m_total, 2 * C_QK_W), F32),
                 jax.ShapeDtypeStruct((m_total, C_V_W), F32),
                 jax.ShapeDtypeStruct((m_total, 128), F32)]
    outs = pl.pallas_call(
        _gdn_prep_prompt_body,
        grid=(n_batch, nt),
        in_specs=[pl.BlockSpec((tt, C_CONV_DIM), lambda b, i: (b * nt + i, 0)),
                  pl.BlockSpec((tt, 128), lambda b, i: (b * nt + i, bg_col)),
                  pl.BlockSpec(conv_w.shape, lambda b, i: (0, 0)),
                  pl.BlockSpec(av.shape, lambda b, i: (0, 0))],
        out_specs=[pl.BlockSpec((tt, 2 * C_QK_W), lambda b, i: (b * nt + i, 0)),
                   pl.BlockSpec((tt, C_V_W), lambda b, i: (b * nt + i, 0)),
                   pl.BlockSpec((tt, 128), lambda b, i: (b * nt + i, 0))],
        out_shape=out_shape,
        scratch_shapes=[pltpu.VMEM((8, C_CONV_DIM), F32)],
        compiler_params=_cparams(2), name="gdn_prep_prompt",
    )(proj, proj, conv_w, av)
    blk0 = mp // t_new
    c0p = jnp.pad(conv0, ((0, 0), (8 - (CONV_W - 1), 0), (0, 0)))
    return pl.pallas_call(
        functools.partial(_gdn_prep_sample_body, t_new=t_new),
        grid=(n_seq,),
        in_specs=[pl.BlockSpec((t_new, C_CONV_DIM), lambda i: (blk0 + i, 0)),
                  pl.BlockSpec((t_new, 128), lambda i: (blk0 + i, bg_col)),
                  pl.BlockSpec((1, 8, C_CONV_DIM), lambda i: (i, 0, 0)),
                  pl.BlockSpec(conv_w.shape, lambda i: (0, 0)),
                  pl.BlockSpec(av.shape, lambda i: (0, 0))] + [pl.BlockSpec(memory_space=pl.ANY)] * 3,
        out_specs=[pl.BlockSpec((t_new, 2 * C_QK_W), lambda i: (blk0 + i, 0)),
                   pl.BlockSpec((t_new, C_V_W), lambda i: (blk0 + i, 0)),
                   pl.BlockSpec((t_new, 128), lambda i: (blk0 + i, 0))],
        out_shape=out_shape,
        input_output_aliases={5: 0, 6: 1, 7: 2},
        compiler_params=_cparams(1), name="gdn_prep_sample",
    )(proj, proj, c0p, conv_w, av, *outs)


def _unit_lower_inverses(l_bds, size):
    n = l_bds[0].shape[0]
    ri = lax.broadcasted_iota(jnp.int32, (n, n), 0)
    ci = lax.broadcasted_iota(jnp.int32, (n, n), 1)
    eye = jnp.where(ri == ci, 1.0, 0.0).astype(F32)
    base = ri // GDN_BASE == ci // GDN_BASE
    ps = [jnp.where(base, -l, 0.0) for l in l_bds]
    ts = [eye + p for p in ps]
    s = 2
    while s < GDN_BASE:
        ps = [_dot(p.astype(BF16), p.astype(BF16)) for p in ps]
        ts = [t + _dot(t.astype(BF16), p.astype(BF16)) for t, p in zip(ts, ps)]
        s *= 2
    s = GDN_BASE
    while s < size:
        lower_left = ((ri // s) % 2 == 1) & (ci // s == ri // s - 1)
        tbs = [t.astype(BF16) for t in ts]
        mids = [_dot(jnp.where(lower_left, l, 0.0).astype(BF16), tb).astype(BF16) for l, tb in zip(l_bds, tbs)]
        ts = [t - _dot(tb, mid) for t, tb, mid in zip(ts, tbs, mids)]
        s *= 2
    return ts


def _gdn_chunk_body(qk_ref, v_ref, bg_ref, nw_ref, *rest, chunk, pack, has_state):
    if has_state:
        s0_ref, _prev, o_ref, sout_ref, s_ref = rest
    else:
        o_ref, sout_ref, s_ref = rest
    c = chunk

    @pl.when(pl.program_id(1) == 0)
    def _():
        if has_state:
            s_ref[...] = s0_ref[0]
        else:
            s_ref[...] = jnp.zeros_like(s_ref)

    rep = C_V_HEADS // C_QK_HEADS
    beta = bg_ref[:, 0:C_V_HEADS]
    g = bg_ref[:, C_V_HEADS:2 * C_V_HEADS]
    ti = lax.broadcasted_iota(jnp.int32, (c, c), 0)
    tj = lax.broadcasted_iota(jnp.int32, (c, c), 1)
    tril = jnp.where(ti >= tj, 1.0, 0.0).astype(F32)
    gc = _dot_f32(tril, g)
    g_last = gc[c - 1:c]
    gam = jnp.exp(gc)
    kdec = jnp.exp(g_last - gc)
    g_end = jnp.exp(g_last)
    n = pack * c
    ri = lax.broadcasted_iota(jnp.int32, (n, n), 0)
    ci = lax.broadcasted_iota(jnp.int32, (n, n), 1)
    same = ri // c == ci // c
    incl = same & (ri >= ci)
    strict = same & (ri > ci)
    packs = [range(p0, p0 + pack) for p0 in range(0, C_V_HEADS, pack)]

    def col(a, heads):
        return jnp.concatenate([a[:, h:h + 1] for h in heads], axis=0)

    l_bds, qk_bds, rhss, qgs, kends = [], [], [], [], []
    for heads in packs:
        g_col = col(gc, heads)
        g_row = jnp.sum(jnp.where(ri == ci, g_col, 0.0), axis=0, keepdims=True)
        b_col = col(beta, heads)
        gam_col = col(gam, heads)
        k_st = jnp.concatenate([qk_ref[:, C_QK_W + (h // rep) * C_DK:C_QK_W + (h // rep + 1) * C_DK] for h in heads], axis=0)
        q_st = jnp.concatenate([qk_ref[:, (h // rep) * C_DK:(h // rep + 1) * C_DK] for h in heads], axis=0)
        v_st = jnp.concatenate([v_ref[:, h * C_DV:(h + 1) * C_DV] for h in heads], axis=0)
        kb = k_st.astype(BF16)
        decay = jnp.exp(jnp.where(incl, g_col - g_row, -jnp.inf))
        l_bds.append(jnp.where(strict, b_col * _dot_nt(kb, kb) * decay, 0.0))
        qk_bds.append((_dot_nt(q_st.astype(BF16), kb) * decay).astype(BF16))
        rhss.append(jnp.concatenate([b_col * v_st, (b_col * gam_col) * k_st], axis=1).astype(BF16))
        qgs.append(q_st * gam_col)
        kends.append((k_st * col(kdec, heads)).astype(BF16))
    t_invs = _unit_lower_inverses(l_bds, c)
    uws = [_dot(t.astype(BF16), rhs) for t, rhs in zip(t_invs, rhss)]
    u_sts, q_sts = [], []
    for heads, uw, qg in zip(packs, uws, qgs):
        us, qs = [], []
        for i, h in enumerate(heads):
            rows = slice(i * c, (i + 1) * c)
            lhs = jnp.concatenate([uw[rows, C_DV:], qg[rows]], axis=0).astype(BF16)
            ws = _dot(lhs, s_ref[h].astype(BF16))
            us.append(uw[rows, 0:C_DV] - ws[0:c])
            qs.append(ws[c:])
        u_sts.append(jnp.concatenate(us, axis=0).astype(BF16))
        q_sts.append(jnp.concatenate(qs, axis=0))
    o_sts = [q + _dot(qk_bd, ub) for q, qk_bd, ub in zip(q_sts, qk_bds, u_sts)]
    for heads, o_st, ub, kendb in zip(packs, o_sts, u_sts, kends):
        for i, h in enumerate(heads):
            rows = slice(i * c, (i + 1) * c)
            s_ref[h] = s_ref[h] * g_end[:, h:h + 1] + _dot_tn(kendb[rows], ub[rows])
            o_ref[:, h * C_DV:(h + 1) * C_DV] = _rms(o_st[rows], nw_ref[...])

    @pl.when(pl.program_id(1) == pl.num_programs(1) - 1)
    def _():
        sout_ref[0] = s_ref[...]


def _gdn_chunks(qk, v, bg, onorm_w, s0, n_batch, seq, n_seq, t_new, chunk):
    m_total = qk.shape[0]
    mp = n_batch * seq
    nc = seq // chunk
    nw = onorm_w.reshape(1, C_DV).astype(F32)
    state = (C_V_HEADS, C_DK, C_DV)
    o, s_p = pl.pallas_call(
        functools.partial(_gdn_chunk_body, chunk=chunk, pack=256 // chunk, has_state=False),
        grid=(n_batch, nc),
        in_specs=[pl.BlockSpec((chunk, 2 * C_QK_W), lambda b, i: (b * nc + i, 0)),
                  pl.BlockSpec((chunk, C_V_W), lambda b, i: (b * nc + i, 0)),
                  pl.BlockSpec((chunk, 128), lambda b, i: (b * nc + i, 0)),
                  pl.BlockSpec((1, C_DV), lambda b, i: (0, 0))],
        out_specs=[pl.BlockSpec((chunk, C_V_W), lambda b, i: (b * nc + i, 0)),
                   pl.BlockSpec((1,) + state, lambda b, i: (b, 0, 0, 0))],
        out_shape=[jax.ShapeDtypeStruct((m_total, C_V_W), F32),
                   jax.ShapeDtypeStruct((n_batch,) + state, F32)],
        scratch_shapes=[pltpu.VMEM(state, F32)],
        compiler_params=_cparams(2), name="gdn_prompt",
    )(qk, v, bg, nw)
    blk0 = mp // t_new
    o, s_s = pl.pallas_call(
        functools.partial(_gdn_chunk_body, chunk=t_new, pack=C_V_HEADS, has_state=True),
        grid=(n_seq, 1),
        in_specs=[pl.BlockSpec((t_new, 2 * C_QK_W), lambda b, i: (blk0 + b, 0)),
                  pl.BlockSpec((t_new, C_V_W), lambda b, i: (blk0 + b, 0)),
                  pl.BlockSpec((t_new, 128), lambda b, i: (blk0 + b, 0)),
                  pl.BlockSpec((1, C_DV), lambda b, i: (0, 0)),
                  pl.BlockSpec((1,) + state, lambda b, i: (b, 0, 0, 0)),
                  pl.BlockSpec(memory_space=pl.ANY)],
        out_specs=[pl.BlockSpec((t_new, C_V_W), lambda b, i: (blk0 + b, 0)),
                   pl.BlockSpec((1,) + state, lambda b, i: (b, 0, 0, 0))],
        out_shape=[jax.ShapeDtypeStruct((m_total, C_V_W), F32),
                   jax.ShapeDtypeStruct((n_seq,) + state, F32)],
        scratch_shapes=[pltpu.VMEM(state, F32)],
        input_output_aliases={5: 0},
        compiler_params=_cparams(2), name="gdn_sample",
    )(qk, v, bg, nw, s0, o)
    return o, s_p, s_s


def _odd_out_body(o_ref, z_ref, x_ref, wo_ref, gf_ref, wr_ref, br_ref, x1_ref, hn_ref, route_ref, counts_ref, run_ref):
    y = (o_ref[...] * _silu(z_ref[...])).astype(BF16)
    x1 = x_ref[...] + _dot(y, wo_ref[...])
    x1_ref[...] = x1
    _route(x1, gf_ref, wr_ref, br_ref, hn_ref, route_ref, counts_ref, run_ref)


def _final_body(x_ref, a_ref, g_ref, y_ref):
    y_ref[...] = _rms(x_ref[...] + a_ref[...], g_ref[...])


def _final_norm(x, add, g, tm, row0, n_rows):
    d = x.shape[1]
    blk0 = row0 // tm
    row = pl.BlockSpec((tm, d), lambda i: (blk0 + i, 0))
    return pl.pallas_call(
        _final_body, grid=(n_rows // tm,),
        in_specs=[row, row, pl.BlockSpec((1, d), lambda i: (0, 0))],
        out_specs=pl.BlockSpec((tm, d), lambda i: (i, 0)),
        out_shape=jax.ShapeDtypeStruct((n_rows, d), F32),
        compiler_params=_cparams(1), name="final_norm",
    )(x, add, g.reshape(1, d))


def kernel(x_prompt, x_sample, cache_a_g0_kv, cache_a_g1_kv, cache_a_g2_kv, state_b_h, state_b_conv, state_c_S, state_c_conv, t5_bias, norm_mix, norm_ffn, norm_final, e_w_in, e_conv_w, e_conv_b, e_rg_wa, e_rg_ba, e_rg_wx, e_rg_bx, e_rg_lambda, e_w_out, o_w_in, o_conv_w, o_a_log, o_dt_bias, o_onorm_w, o_w_out, moe_rg_w, moe_rg_b, moe_re_w, moe_re_b, moe_w_gate, moe_w_up, moe_w_down):
    n_batch, seq, d = x_prompt.shape
    n_seq, t_new, _ = x_sample.shape
    mp = n_batch * seq
    ms = n_seq * t_new
    m = mp + ms
    assert t_new == 8 and seq % (DIL_PAIRS[2][1] * A_BLOCK) == 0
    assert e_w_in.shape[0] == 1 and o_w_in.shape[0] == 1
    tm = _row_tile(mp, ms)
    x = jnp.concatenate([x_prompt.reshape(mp, d), x_sample.reshape(ms, d)], axis=0)

    def moe_weights(layer):
        shp = (N_EXPERTS, d, EXPERT_FF)
        return (moe_w_gate[layer].reshape(shp).astype(BF16), moe_w_up[layer].reshape(shp).astype(BF16),
                moe_w_down[layer].reshape(N_EXPERTS, EXPERT_FF, d).astype(BF16))

    def prompt_tail(a, keep, c0, c1):
        return jnp.stack([lax.slice(a, ((b + 1) * seq - keep, c0), ((b + 1) * seq, c1)) for b in range(n_batch)])

    def sample_rows(a, keep, c0, c1):
        return lax.slice(a, (mp, c0), (m, c1)).reshape(n_seq, t_new, c1 - c0)[:, t_new - keep:]

    proj, *qkv_rm = _norm_proj_even(x, norm_mix[0], e_w_in[0].astype(BF16), tm, n_batch, seq)
    attn_p = []
    for g, (_, dil) in enumerate(DIL_PAIRS):
        attn_p.extend(_attn_prompt(qkv_rm[g], t5_bias[:, g], g, dil, n_batch, seq))
    new_a = []
    for g, (win, _) in enumerate(DIL_PAIRS):
        c0 = g * 3 * A_WIDTH + A_WIDTH
        keep = min(win, seq)
        new_a.append(prompt_tail(proj, keep, c0, c0 + 2 * A_WIDTH).reshape(1, n_batch, keep, 2, A_HEADS, A_HEAD_DIM))
        new_a.append(sample_rows(proj, t_new, c0, c0 + 2 * A_WIDTH).reshape(1, n_seq, t_new, 2, A_HEADS, A_HEAD_DIM))
    q_new = jnp.stack([sample_rows(proj, t_new, g * 3 * A_WIDTH, g * 3 * A_WIDTH + A_WIDTH) for g in range(N_DIL)], axis=1)
    q_new = q_new.reshape(n_seq, N_DIL, t_new, A_HEADS, A_HEAD_DIM)
    attn_s = _attn_sample(q_new, new_a[1::2], (cache_a_g0_kv, cache_a_g1_kv, cache_a_g2_kv), t5_bias, n_seq, t_new)
    y_b, bh_p, bh_s = _rglru(proj, state_b_conv[0], state_b_h[0], e_conv_w[0], e_conv_b[0], e_rg_wa[0], e_rg_ba[0],
                             e_rg_wx[0], e_rg_bx[0], e_rg_lambda[0], n_batch, seq, n_seq, t_new)
    wr, br = _router_weights(moe_rg_w[0], moe_rg_b[0], moe_re_w[0], moe_re_b[0])
    x1, hn, route, counts = _even_out(attn_p, attn_s, y_b, x, [e_w_out[0].astype(BF16), norm_ffn[0].reshape(1, d), wr, br],
                                      tm, n_batch, seq)
    y_moe = _moe(hn, route, counts, *moe_weights(0), tm)

    w_in1 = jnp.pad(o_w_in[0], ((0, 0), (0, ODD_IN_PAD - ODD_IN))).astype(BF16)
    x2, proj2 = _norm_proj(x1, (y_moe,), norm_mix[1], w_in1, tm, 896)
    qk, v, bg = _gdn_prep(proj2, state_c_conv[0], o_conv_w[0], o_a_log[0], o_dt_bias[0], n_batch, seq, n_seq, t_new)
    o_c, cs_p, cs_s = _gdn_chunks(qk, v, bg, o_onorm_w[0], state_c_S[0], n_batch, seq, n_seq, t_new, 64)
    wr, br = _router_weights(moe_rg_w[1], moe_rg_b[1], moe_re_w[1], moe_re_b[1])
    x3, hn, route, counts = _odd_out(o_c, proj2, x2, [o_w_out[0].astype(BF16), norm_ffn[1].reshape(1, d), wr, br], tm)
    y_moe = _moe(hn, route, counts, *moe_weights(1), tm)
    y_p = _final_norm(x3, y_moe, norm_final, tm, 0, mp).reshape(n_batch, seq, d)
    y_s = _final_norm(x3, y_moe, norm_final, tm, mp, ms).reshape(n_seq, t_new, d)

    keep = CONV_W - 1
    bconv_p = prompt_tail(proj, keep, XB_COL, XB_COL + RNN_WIDTH)[None]
    bconv_s = sample_rows(proj, keep, XB_COL, XB_COL + RNN_WIDTH)[None]
    cconv_p = prompt_tail(proj2, keep, 0, C_CONV_DIM)[None]
    cconv_s = sample_rows(proj2, keep, 0, C_CONV_DIM)[None]
    return (y_p, y_s, *new_a, bh_p[None], bh_s[None], bconv_p, bconv_s, cs_p[None], cs_s[None], cconv_p, cconv_s)
```

```python
import functools
import math

import jax
import jax.numpy as jnp
import numpy as np
from jax import lax
from jax.experimental import pallas as pl
from jax.experimental.pallas import tpu as pltpu

F32 = jnp.float32
BF16 = jnp.bfloat16
EPS = 1e-6
NEG_INF = -1e30

D_MODEL = 1024
DIL_PAIRS = ((128, 1), (512, 4), (2048, 16))
N_DIL = 3
A_HEADS = 8
A_HEAD_DIM = 64
A_WIDTH = A_HEADS * A_HEAD_DIM
A_BLOCK = 128
SPAN = 128
NUM_BUCKETS = 32
MAX_DISTANCE = 2048
RNN_WIDTH = 512
RNN_BLOCKS = 8
CONV_W = 4
RG_C = 8.0
EVEN_IN = N_DIL * 3 * A_WIDTH + 2 * RNN_WIDTH
XB_COL = N_DIL * 3 * A_WIDTH
GB_COL = XB_COL + RNN_WIDTH
C_QK_HEADS = 8
C_V_HEADS = 16
C_DK = 128
C_DV = 128
C_QK_W = C_QK_HEADS * C_DK
C_V_W = C_V_HEADS * C_DV
C_CONV_DIM = 2 * C_QK_W + C_V_W
ODD_IN = C_CONV_DIM + C_V_W + 2 * C_V_HEADS
ODD_IN_PAD = 6272
Z_COL = C_CONV_DIM
BG_COL = C_CONV_DIM + C_V_W
GDN_BASE = 8
N_GROUPS = 4
EXPERTS_PER_GROUP = 8
N_EXPERTS = N_GROUPS * EXPERTS_PER_GROUP
EXPERT_FF = 256
ROUTE_LANES = 128
MOE_TILE = 256

LANES = 128
VMEM_LIMIT = 56 * 1024 * 1024


def _cparams(n_grid):
    return pltpu.CompilerParams(dimension_semantics=("arbitrary",) * n_grid,
                                vmem_limit_bytes=VMEM_LIMIT)


def _rms(x, g):
    return x * lax.rsqrt(jnp.mean(x * x, axis=-1, keepdims=True) + EPS) * g


def _silu(x):
    return x * jax.nn.sigmoid(x)


def _softplus(x):
    return jnp.maximum(x, 0.0) + jnp.log1p(jnp.exp(-jnp.abs(x)))


def _dot(a, b):
    return jnp.dot(a, b, preferred_element_type=F32)


def _dot_nt(a, b):
    return lax.dot_general(a, b, (((1,), (1,)), ((), ())), preferred_element_type=F32)


def _dot_tn(a, b):
    return lax.dot_general(a, b, (((0,), (0,)), ((), ())), preferred_element_type=F32)


def _dot_f32(a, b):
    return jnp.dot(a, b, preferred_element_type=F32, precision=lax.Precision.HIGHEST)


def _row_tile(*counts):
    for t in (256, 128, 64, 32, 16, 8):
        if all(c % t == 0 for c in counts):
            return t
    raise ValueError("token counts must be multiples of 8")


def _norm_proj_body(*refs, col_chunk, n_add):
    x_ref = refs[0]
    add_refs = refs[1:1 + n_add]
    g_ref, w_ref = refs[1 + n_add:3 + n_add]
    outs = refs[3 + n_add:]
    x = x_ref[...]
    for a in add_refs:
        x = x + a[...]
    if n_add:
        outs[0][...] = x
    o_ref = outs[-1]
    hb = _rms(x, g_ref[...]).astype(BF16)
    for c0 in range(0, o_ref.shape[1], col_chunk):
        o_ref[:, c0:c0 + col_chunk] = _dot(hb, w_ref[:, c0:c0 + col_chunk])


def _norm_proj(x, adds, g, w, tm, col_chunk):
    m, d = x.shape
    n = w.shape[1]
    row = pl.BlockSpec((tm, d), lambda i: (i, 0))
    out_shape = [jax.ShapeDtypeStruct((m, n), F32)]
    out_specs = [pl.BlockSpec((tm, n), lambda i: (i, 0))]
    if adds:
        out_shape = [jax.ShapeDtypeStruct((m, d), F32)] + out_shape
        out_specs = [row] + out_specs
    return pl.pallas_call(
        functools.partial(_norm_proj_body, col_chunk=col_chunk, n_add=len(adds)),
        grid=(m // tm,),
        in_specs=[row] * (1 + len(adds)) + [pl.BlockSpec((1, d), lambda i: (0, 0)),
                                            pl.BlockSpec((d, n), lambda i: (0, 0))],
        out_specs=out_specs, out_shape=out_shape,
        compiler_params=_cparams(1), name="norm_proj",
    )(x, *adds, g.reshape(1, d), w)


def _norm_proj_even_body(x_ref, g_ref, w_ref, o_ref, rm0_ref, rm1_ref, rm2_ref, lane_ref):
    tm = x_ref.shape[0]
    hb = _rms(x_ref[...], g_ref[...]).astype(BF16)
    rm_refs = (rm0_ref, rm1_ref, rm2_ref)
    for c in range(EVEN_IN // A_WIDTH):
        cols = slice(c * A_WIDTH, (c + 1) * A_WIDTH)
        res = _dot(hb, w_ref[:, cols])
        o_ref[:, cols] = res
        if c < 3 * N_DIL:
            g, j = divmod(c, 3)
            dil = DIL_PAIRS[g][1]
            if dil == 1:
                rm_refs[g][0, 0, :, j * A_WIDTH:(j + 1) * A_WIDTH] = res.astype(BF16)
            else:
                for t in range(A_WIDTH // LANES):
                    lane_ref[t] = res[:, t * LANES:(t + 1) * LANES]
                for r in range(dil):
                    part = [lane_ref[t, pl.ds(r, tm // dil, stride=dil), :] for t in range(A_WIDTH // LANES)]
                    rm_refs[g][0, r, :, j * A_WIDTH:(j + 1) * A_WIDTH] = jnp.concatenate(part, axis=1).astype(BF16)


def _norm_proj_even(x, g, w, tm, n_batch, seq):
    m, d = x.shape
    n = w.shape[1]
    tps = seq // tm
    npt = n_batch * tps

    def rm_index(i):
        return (jnp.where(i < npt, i // tps, n_batch), 0, jnp.where(i < npt, i % tps, i - npt), 0)

    rm_shapes = [jax.ShapeDtypeStruct((n_batch + 1, dil, seq // dil, 3 * A_WIDTH), BF16) for _, dil in DIL_PAIRS]
    rm_specs = [pl.BlockSpec((1, dil, tm // dil, 3 * A_WIDTH), rm_index) for _, dil in DIL_PAIRS]
    return pl.pallas_call(
        _norm_proj_even_body,
        grid=(m // tm,),
        in_specs=[pl.BlockSpec((tm, d), lambda i: (i, 0)), pl.BlockSpec((1, d), lambda i: (0, 0)),
                  pl.BlockSpec((d, n), lambda i: (0, 0))],
        out_specs=[pl.BlockSpec((tm, n), lambda i: (i, 0))] + rm_specs,
        out_shape=[jax.ShapeDtypeStruct((m, n), F32)] + rm_shapes,
        scratch_shapes=[pltpu.VMEM((A_WIDTH // LANES, tm, LANES), F32)],
        compiler_params=_cparams(1), name="norm_proj_even",
    )(x, g.reshape(1, d), w)


def _t5_bucket(dist):
    max_exact = NUM_BUCKETS // 2
    d = np.maximum(dist, 1).astype(np.float32)
    large = max_exact + (np.log(d / max_exact) / np.log(MAX_DISTANCE / max_exact)
                         * (NUM_BUCKETS - max_exact)).astype(np.int32)
    large = np.minimum(large, NUM_BUCKETS - 1)
    return np.where(dist < max_exact, dist, large).astype(np.int32)


def _bucket_lookup(tab, buckets):
    onehot = jnp.asarray(buckets[..., None, None] == np.arange(NUM_BUCKETS)[:, None])
    return jnp.sum(jnp.where(onehot, tab.astype(F32), 0.0), axis=-2)


def _prompt_bias(tab, dil):
    qi = np.arange(A_BLOCK)[:, None]
    km = np.arange(2 * A_BLOCK)[None, :]
    delta = A_BLOCK + qi - km
    valid = (delta >= 0) & (delta <= SPAN)
    bias = _bucket_lookup(tab, _t5_bucket(np.clip(delta, 0, SPAN) * dil))
    bias = jnp.where(valid[..., None], bias, NEG_INF)
    return jnp.transpose(bias, (2, 0, 1))


def _attn_prompt_body(q_ref, kp_ref, ko_ref, vp_ref, vo_ref, bias_ref, o_ref, lse_ref):
    first = pl.program_id(2) == 0
    scale = A_HEAD_DIM ** -0.5
    q = q_ref[0, 0]
    k = jnp.concatenate([kp_ref[0, 0], ko_ref[0, 0]], axis=0)
    v = jnp.concatenate([vp_ref[0, 0], vo_ref[0, 0]], axis=0)
    o_ref = o_ref.at[0, 0]
    lse_ref = lse_ref.at[0, 0]
    km = lax.broadcasted_iota(jnp.int32, (1, 2 * A_BLOCK), 1)
    no_prev = jnp.logical_and(first, km < A_BLOCK)
    heads_per_tile = LANES // A_HEAD_DIM
    head_of_lane = lax.broadcasted_iota(jnp.int32, (1, LANES), 1) // A_HEAD_DIM
    for t in range(A_WIDTH // LANES):
        sl = slice(t * LANES, (t + 1) * LANES)
        q_t, k_t, v_t = q[:, sl], k[:, sl], v[:, sl]
        o_t = jnp.zeros((A_BLOCK, LANES), F32)
        lse_t = jnp.zeros((A_BLOCK, LANES), F32)
        for j in range(heads_per_tile):
            mine = head_of_lane == j
            s = _dot_nt(jnp.where(mine, q_t, jnp.zeros_like(q_t)), k_t) * scale + bias_ref[t * heads_per_tile + j]
            s = jnp.where(no_prev, NEG_INF, s)
            m = jnp.max(s, axis=-1, keepdims=True)
            p = jnp.exp(s - m)
            den = jnp.sum(p, axis=-1, keepdims=True)
            o_t = jnp.where(mine, _dot(p.astype(BF16), v_t) / den, o_t)
            lse_t = jnp.where(mine, m + jnp.log(den), lse_t)
        o_ref[:, sl] = o_t
        lse_ref[:, sl] = lse_t


def _attn_prompt(qkv_rm, tab, g, dil, n_batch, seq):
    sub_len = seq // dil
    nb = sub_len // A_BLOCK

    def spec(j, prev):
        def index(b, r, i):
            return (b, r, jnp.maximum(i - 1, 0) if prev else i, j)
        return pl.BlockSpec((1, 1, A_BLOCK, A_WIDTH), index)

    out_spec = pl.BlockSpec((1, 1, A_BLOCK, A_WIDTH), lambda b, r, i: (b, r, i, 0))
    out_sds = jax.ShapeDtypeStruct((n_batch, dil, sub_len, A_WIDTH), F32)
    return pl.pallas_call(
        _attn_prompt_body,
        grid=(n_batch, dil, nb),
        in_specs=[spec(0, False), spec(1, True), spec(1, False), spec(2, True), spec(2, False),
                  pl.BlockSpec((A_HEADS, A_BLOCK, 2 * A_BLOCK), lambda b, r, i: (0, 0, 0))],
        out_specs=[out_spec, out_spec], out_shape=[out_sds, out_sds],
        compiler_params=_cparams(3), name="attn_prompt_g%d" % g,
    )(qkv_rm, qkv_rm, qkv_rm, qkv_rm, qkv_rm, _prompt_bias(tab, dil))


def _sample_bias(tab, dil, cache_len, key_index, t_new):
    t = np.arange(t_new)[:, None]
    dist = cache_len + t - key_index[None, :]
    valid = (dist >= 0) & (dist % dil == 0) & (dist <= SPAN * dil)
    bias = _bucket_lookup(tab, _t5_bucket(np.clip(dist, 0, SPAN * dil)))
    bias = jnp.where(valid[..., None], bias, NEG_INF)
    return jnp.transpose(bias, (2, 0, 1)).reshape(A_HEADS * t_new, key_index.shape[0])


def _attn_sample_body(new_ref, c0_ref, c1_ref, c2_ref, b0_ref, b1_ref, b2_ref, bn_ref, *out_refs, t_new):
    scale = A_HEAD_DIM ** -0.5
    rows = A_HEADS * t_new
    head_of_row = lax.broadcasted_iota(jnp.int32, (rows, A_WIDTH), 0) // t_new
    head_of_lane = lax.broadcasted_iota(jnp.int32, (rows, A_WIDTH), 1) // A_HEAD_DIM
    own = head_of_row == head_of_lane
    caches = (c0_ref, c1_ref, c2_ref)
    biases = (b0_ref, b1_ref, b2_ref)
    for g in range(N_DIL):
        base = g * 3 * A_WIDTH
        cache_len = caches[g].shape[-1]
        q = new_ref[:, base:base + A_WIDTH]
        k_new = new_ref[:, base + A_WIDTH:base + 2 * A_WIDTH].astype(BF16)
        v_new = new_ref[:, base + 2 * A_WIDTH:base + 3 * A_WIDTH].astype(BF16)
        q_bd = jnp.where(own, jnp.concatenate([q] * A_HEADS, axis=0), 0.0).astype(BF16)
        k_t = caches[g][0, 0, 0].reshape(A_WIDTH, cache_len).astype(BF16)
        v_t = caches[g][0, 0, 1].reshape(A_WIDTH, cache_len).astype(BF16)
        s_c = _dot(q_bd, k_t) * scale + biases[g][...]
        s_n = _dot_nt(q_bd, k_new) * scale + bn_ref[g]
        m = jnp.maximum(s_c.max(axis=-1, keepdims=True), s_n.max(axis=-1, keepdims=True))
        p_c = jnp.exp(s_c - m)
        p_n = jnp.exp(s_n - m)
        den = jnp.sum(p_c, axis=-1, keepdims=True) + jnp.sum(p_n, axis=-1, keepdims=True)
        acc = _dot_nt(p_c.astype(BF16), v_t) + _dot(p_n.astype(BF16), v_new)
        acc = jnp.where(own, acc / den, 0.0)
        lse = jnp.where(own, m + jnp.log(den), 0.0)
        o = acc[0:t_new]
        l = lse[0:t_new]
        for h in range(1, A_HEADS):
            o = o + acc[h * t_new:(h + 1) * t_new]
            l = l + lse[h * t_new:(h + 1) * t_new]
        out_refs[2 * g][...] = o
        out_refs[2 * g + 1][...] = l


def _attn_sample(proj, caches, t5_bias, n_seq, t_new, row0):
    cache_lens = [c.shape[2] for c in caches]
    caches_t = [jnp.transpose(c, (0, 1, 3, 4, 5, 2)) for c in caches]
    biases = [_sample_bias(t5_bias[:, g], DIL_PAIRS[g][1], cache_lens[g], np.arange(cache_lens[g]), t_new)
              for g in range(N_DIL)]
    bias_new = jnp.stack([_sample_bias(t5_bias[:, g], DIL_PAIRS[g][1], cache_lens[g],
                                       cache_lens[g] + np.arange(t_new), t_new) for g in range(N_DIL)])
    blk0 = row0 // t_new
    full = lambda a: pl.BlockSpec(a.shape, lambda b: (0,) * a.ndim)
    out_spec = pl.BlockSpec((t_new, A_WIDTH), lambda b: (b, 0))
    cache_specs = [pl.BlockSpec((1, 1, 2, A_HEADS, A_HEAD_DIM, n), lambda b: (0, b, 0, 0, 0, 0)) for n in cache_lens]
    return pl.pallas_call(
        functools.partial(_attn_sample_body, t_new=t_new),
        grid=(n_seq,),
        in_specs=[pl.BlockSpec((t_new, N_DIL * 3 * A_WIDTH), lambda b: (blk0 + b, 0))] + cache_specs
                 + [full(biases[0]), full(biases[1]), full(biases[2]), full(bias_new)],
        out_specs=[out_spec] * (2 * N_DIL),
        out_shape=[jax.ShapeDtypeStruct((n_seq * t_new, A_WIDTH), F32)] * (2 * N_DIL),
        compiler_params=_cparams(1), name="attn_sample",
    )(proj, *caches_t, *biases, bias_new)


def _shift_rows(x, s, fill, axis):
    t = lax.broadcasted_iota(jnp.int32, x.shape, axis)
    return jnp.where(t >= s, pltpu.roll(x, s, axis), fill)


def _linear_scan(a, b, axis):
    n = a.shape[axis]
    s = 1
    while s < n:
        b = b + a * _shift_rows(b, s, 0.0, axis)
        a = a * _shift_rows(a, s, 1.0, axis)
        s *= 2
    return a, b


def _rglru_gates(xc, wa_ref, wx_ref, vec_ref):
    xcb = xc.astype(BF16)
    r = jax.nn.sigmoid(_dot(xcb, wa_ref[...]) + vec_ref[1:2])
    ig = jax.nn.sigmoid(_dot(xcb, wx_ref[...]) + vec_ref[2:3])
    log_a = -RG_C * r * _softplus(-vec_ref[3:4])
    a = jnp.exp(log_a)
    b = jnp.sqrt(1.0 - jnp.exp(2.0 * log_a)) * (ig * xc)
    return a, b


def _gelu(x):
    return 0.5 * x * (1.0 + jnp.tanh(math.sqrt(2.0 / math.pi) * (x + 0.044715 * (x * x * x))))


def _rglru_prompt_body(xb_ref, gb_ref, cw_ref, wa_ref, wx_ref, vec_ref, y_ref, hl_ref, tail_ref, h_ref):
    @pl.when(pl.program_id(1) == 0)
    def _():
        tail_ref[...] = jnp.zeros_like(tail_ref)
        h_ref[...] = jnp.zeros_like(h_ref)

    x = xb_ref[...]
    tt = x.shape[0]
    xe = jnp.concatenate([tail_ref[...], x], axis=0)
    xc = vec_ref[0:1] + x * cw_ref[CONV_W - 1:CONV_W]
    for j in range(1, CONV_W):
        xc = xc + pltpu.roll(xe, j, 0)[8:8 + tt] * cw_ref[CONV_W - 1 - j:CONV_W - j]
    tail_ref[...] = x[tt - 8:tt]
    a, b = _rglru_gates(xc, wa_ref, wx_ref, vec_ref)
    a_cum, h = _linear_scan(a, b, 0)
    h = h + a_cum * h_ref[...]
    h_ref[...] = h[tt - 1:tt]
    hl_ref[0] = h[tt - 1:tt]
    y_ref[...] = h * _gelu(gb_ref[...])


def _rglru_sample_body(xb_ref, gb_ref, c0_ref, h0_ref, cw_ref, wa_ref, wx_ref, vec_ref, prev_ref, y_ref, hl_ref, *, t_new):
    del prev_ref
    x = xb_ref[...]
    rows = x.shape[0]
    ns = rows // t_new
    x3 = x.reshape(ns, t_new, RNN_WIDTH)
    xe = jnp.concatenate([c0_ref[...], x3], axis=1)
    xc = vec_ref[0:1] + x3 * cw_ref[CONV_W - 1:CONV_W]
    for j in range(1, CONV_W):
        xc = xc + pltpu.roll(xe, j, 1)[:, 8:8 + t_new] * cw_ref[CONV_W - 1 - j:CONV_W - j]
    a, b = _rglru_gates(xc.reshape(rows, RNN_WIDTH), wa_ref, wx_ref, vec_ref)
    a_cum, h = _linear_scan(a.reshape(ns, t_new, RNN_WIDTH), b.reshape(ns, t_new, RNN_WIDTH), 1)
    h = h + a_cum * h0_ref[...][:, None, :]
    hl_ref[...] = h[:, t_new - 1, :]
    y_ref[...] = h.reshape(rows, RNN_WIDTH) * _gelu(gb_ref[...])


def _block_diag(w):
    nb, bi, bo = w.shape
    eye = jnp.eye(nb, dtype=w.dtype)
    return (w[:, :, None, :] * eye[:, None, :, None]).reshape(nb * bi, nb * bo)


def _rglru(proj, conv0, h0, conv_w, conv_b, wa, ba, wx, bx, lam, n_batch, seq, n_seq, t_new):
    m_total = proj.shape[0]
    mp = n_batch * seq
    wa_bd = _block_diag(wa).astype(BF16)
    wx_bd = _block_diag(wx).astype(BF16)
    vec = jnp.stack([conv_b, ba, bx, lam]).astype(F32)
    xcol = XB_COL // RNN_WIDTH
    gcol = GB_COL // RNN_WIDTH
    full2 = lambda a, nd: pl.BlockSpec(a.shape, lambda *_: (0,) * a.ndim)
    tt = _row_tile(seq)
    nt = seq // tt
    w_specs2 = [pl.BlockSpec(a.shape, lambda b, i: (0, 0)) for a in (conv_w, wa_bd, wx_bd, vec)]
    y, hl_p = pl.pallas_call(
        _rglru_prompt_body,
        grid=(n_batch, nt),
        in_specs=[pl.BlockSpec((tt, RNN_WIDTH), lambda b, i: (b * nt + i, xcol)),
                  pl.BlockSpec((tt, RNN_WIDTH), lambda b, i: (b * nt + i, gcol))] + w_specs2,
        out_specs=[pl.BlockSpec((tt, RNN_WIDTH), lambda b, i: (b * nt + i, 0)),
                   pl.BlockSpec((1, 1, RNN_WIDTH), lambda b, i: (b, 0, 0))],
        out_shape=[jax.ShapeDtypeStruct((m_total, RNN_WIDTH), F32),
                   jax.ShapeDtypeStruct((n_batch, 1, RNN_WIDTH), F32)],
        scratch_shapes=[pltpu.VMEM((8, RNN_WIDTH), F32), pltpu.VMEM((1, RNN_WIDTH), F32)],
        compiler_params=_cparams(2), name="rglru_prompt",
    )(proj, proj, conv_w, wa_bd, wx_bd, vec)
    del full2
    ts = 32 if n_seq % 32 == 0 else 8
    rows = ts * t_new
    blk0 = mp // rows
    c0p = jnp.pad(conv0, ((0, 0), (8 - (CONV_W - 1), 0), (0, 0)))
    w_specs1 = [pl.BlockSpec(a.shape, lambda i: (0, 0)) for a in (conv_w, wa_bd, wx_bd, vec)]
    y, hl_s = pl.pallas_call(
        functools.partial(_rglru_sample_body, t_new=t_new),
        grid=(n_seq // ts,),
        in_specs=[pl.BlockSpec((rows, RNN_WIDTH), lambda i: (blk0 + i, xcol)),
                  pl.BlockSpec((rows, RNN_WIDTH), lambda i: (blk0 + i, gcol)),
                  pl.BlockSpec((ts, 8, RNN_WIDTH), lambda i: (i, 0, 0)),
                  pl.BlockSpec((ts, RNN_WIDTH), lambda i: (i, 0))] + w_specs1
                 + [pl.BlockSpec(memory_space=pl.ANY)],
        out_specs=[pl.BlockSpec((rows, RNN_WIDTH), lambda i: (blk0 + i, 0)),
                   pl.BlockSpec((ts, RNN_WIDTH), lambda i: (i, 0))],
        out_shape=[jax.ShapeDtypeStruct((m_total, RNN_WIDTH), F32),
                   jax.ShapeDtypeStruct((n_seq, RNN_WIDTH), F32)],
        input_output_aliases={8: 0},
        compiler_params=_cparams(1), name="rglru_sample",
    )(proj, proj, c0p, h0, conv_w, wa_bd, wx_bd, vec, y)
    return y, hl_p.reshape(n_batch, RNN_WIDTH), hl_s


def _pack_bf16_pair(x):
    w = x.shape[1] // 2
    hi = lax.bitcast_convert_type(x[:, :w].astype(BF16).astype(F32), jnp.uint32)
    lo = lax.bitcast_convert_type(x[:, w:].astype(BF16).astype(F32), jnp.uint32)
    return hi | (lo >> 16)


def _unpack_bf16_pair(p):
    hi = lax.bitcast_convert_type(p & jnp.uint32(0xFFFF0000), F32).astype(BF16)
    lo = lax.bitcast_convert_type(p << 16, F32).astype(BF16)
    return hi, lo


def _route(x1, gf_ref, wr_ref, br_ref, hn_ref, route_ref, counts_ref, run_ref):
    @pl.when(pl.program_id(0) == 0)
    def _():
        run_ref[...] = jnp.zeros_like(run_ref)

    hn = _rms(x1, gf_ref[...])
    hn_ref[...] = _pack_bf16_pair(hn)
    hn_hi = hn.astype(BF16)
    hn_lo = (hn - hn_hi.astype(F32)).astype(BF16)
    logits = _dot(hn_hi, wr_ref[0]) + (_dot(hn_hi, wr_ref[1]) + _dot(hn_lo, wr_ref[0])) + br_ref[...]
    lane = lax.broadcasted_iota(jnp.int32, logits.shape, 1)
    is_coarse = lane < N_GROUPS
    coarse = jnp.where(is_coarse, logits, -jnp.inf)
    cmax = jnp.max(coarse, axis=-1, keepdims=True)
    grp = jnp.min(jnp.where(coarse == cmax, lane, ROUTE_LANES), axis=-1, keepdims=True)
    p_grp = 1.0 / jnp.sum(jnp.where(is_coarse, jnp.exp(logits - cmax), 0.0), axis=-1, keepdims=True)
    expert = lane - N_GROUPS
    in_grp = (lane >= N_GROUPS) & (expert < N_EXPERTS) & (expert // EXPERTS_PER_GROUP == grp)
    fine = jnp.where(in_grp, logits, -jnp.inf)
    v1 = jnp.max(fine, axis=-1, keepdims=True)
    i1 = jnp.min(jnp.where(fine == v1, lane, ROUTE_LANES), axis=-1, keepdims=True)
    fine2 = jnp.where(lane == i1, -jnp.inf, fine)
    v2 = jnp.max(fine2, axis=-1, keepdims=True)
    i2 = jnp.min(jnp.where(fine2 == v2, lane, ROUTE_LANES), axis=-1, keepdims=True)
    e2 = jnp.exp(v2 - v1)
    w1 = p_grp / (1.0 + e2)
    w2 = p_grp * e2 / (1.0 + e2)
    tm = logits.shape[0]
    sel = jnp.where(lane == i1, 1.0, jnp.where(lane == i2, 1.0, 0.0))
    ri = lax.broadcasted_iota(jnp.int32, (tm, tm), 0)
    ci = lax.broadcasted_iota(jnp.int32, (tm, tm), 1)
    earlier = jnp.where(ri > ci, 1.0, 0.0).astype(BF16)
    before = _dot(earlier, sel.astype(BF16)) + run_ref[...]
    r1 = jnp.sum(jnp.where(lane == i1, before, 0.0), axis=-1, keepdims=True)
    r2 = jnp.sum(jnp.where(lane == i2, before, 0.0), axis=-1, keepdims=True)
    run_ref[...] = run_ref[...] + jnp.sum(sel, axis=0, keepdims=True)
    counts_ref[...] = run_ref[...]
    route = jnp.where(lane == 0, (i1 - N_GROUPS).astype(F32), 0.0)
    route = jnp.where(lane == 1, (i2 - N_GROUPS).astype(F32), route)
    route = jnp.where(lane == 2, w1, route)
    route = jnp.where(lane == 3, w2, route)
    route = jnp.where(lane == 4, r1, route)
    route = jnp.where(lane == 5, r2, route)
    route_ref[...] = route


def _merge_groups(os, ls):
    mx = jnp.maximum(jnp.maximum(ls[0], ls[1]), ls[2])
    es = [jnp.exp(l - mx) for l in ls]
    return (es[0] * os[0] + es[1] * os[1] + es[2] * os[2]) / (es[0] + es[1] + es[2])


def _even_out_body(o0, l0, o1, l1, o2, l2, so0, sl0, so1, sl1, so2, sl2, yb_ref, x_ref, wo_ref, gf_ref, wr_ref, br_ref,
                   x1_ref, hn_ref, route_ref, counts_ref, n1o, n1l, n2o, n2l, oa_ref, run_ref, *, n_prompt_tiles):
    i = pl.program_id(0)
    tm = x_ref.shape[0]

    @pl.when(i < n_prompt_tiles)
    def _():
        nat = []
        for src, dst, dil in ((o1, n1o, DIL_PAIRS[1][1]), (l1, n1l, DIL_PAIRS[1][1]),
                              (o2, n2o, DIL_PAIRS[2][1]), (l2, n2l, DIL_PAIRS[2][1])):
            for r in range(dil):
                blk = src[0, r]
                for t in range(A_WIDTH // LANES):
                    dst[t, pl.ds(r, tm // dil, stride=dil), :] = blk[:, t * LANES:(t + 1) * LANES]
            nat.append(jnp.concatenate([dst[t] for t in range(A_WIDTH // LANES)], axis=1))
        oa_ref[...] = _merge_groups((o0[0, 0], nat[0], nat[2]), (l0[0, 0], nat[1], nat[3]))

    @pl.when(i >= n_prompt_tiles)
    def _():
        oa_ref[...] = _merge_groups((so0[...], so1[...], so2[...]), (sl0[...], sl1[...], sl2[...]))

    cat = jnp.concatenate([oa_ref[...], yb_ref[...]], axis=-1).astype(BF16)
    x1 = x_ref[...] + _dot(cat, wo_ref[...])
    x1_ref[...] = x1
    _route(x1, gf_ref, wr_ref, br_ref, hn_ref, route_ref, counts_ref, run_ref)


def _router_weights(rg_w, rg_b, re_w, re_b):
    d = rg_w.shape[0]
    wr = jnp.concatenate([rg_w, re_w.reshape(d, N_EXPERTS)], axis=1)
    br = jnp.concatenate([rg_b, re_b.reshape(N_EXPERTS)])
    pad = ROUTE_LANES - wr.shape[1]
    wr = jnp.pad(wr, ((0, 0), (0, pad))).astype(F32)
    wr_hi = wr.astype(BF16)
    wr_lo = (wr - wr_hi.astype(F32)).astype(BF16)
    return jnp.stack([wr_hi, wr_lo]), jnp.pad(br, (0, pad)).reshape(1, ROUTE_LANES).astype(F32)


def _mix_out_call(body, in_specs, args, consts, m, d, tm, scratch, name):
    const_spec = lambda a: pl.BlockSpec(a.shape, lambda i: (0,) * a.ndim)
    return pl.pallas_call(
        body, grid=(m // tm,), in_specs=list(in_specs) + [const_spec(c) for c in consts],
        out_specs=[pl.BlockSpec((tm, d), lambda i: (i, 0)), pl.BlockSpec((tm, d // 2), lambda i: (i, 0)),
                   pl.BlockSpec((tm, ROUTE_LANES), lambda i: (i, 0)), pl.BlockSpec((1, ROUTE_LANES), lambda i: (0, 0))],
        out_shape=[jax.ShapeDtypeStruct((m, d), F32), jax.ShapeDtypeStruct((m, d // 2), jnp.uint32),
                   jax.ShapeDtypeStruct((m, ROUTE_LANES), F32), jax.ShapeDtypeStruct((1, ROUTE_LANES), F32)],
        scratch_shapes=list(scratch) + [pltpu.VMEM((1, ROUTE_LANES), F32)],
        compiler_params=_cparams(1), name=name,
    )(*args, *consts)


def _even_out(attn_p, attn_s, y_b, x, consts, tm, n_batch, seq):
    m, d = x.shape
    tps = seq // tm
    npt = n_batch * tps
    in_specs = []
    for g, (_, dil) in enumerate(DIL_PAIRS):
        def index(i):
            return (jnp.minimum(i // tps, n_batch - 1), 0, jnp.where(i < npt, i % tps, 0), 0)
        in_specs += [pl.BlockSpec((1, dil, tm // dil, A_WIDTH), index)] * 2
    in_specs += [pl.BlockSpec((tm, A_WIDTH), lambda i: (jnp.maximum(i - npt, 0), 0))] * (2 * N_DIL)
    in_specs += [pl.BlockSpec((tm, RNN_WIDTH), lambda i: (i, 0)), pl.BlockSpec((tm, d), lambda i: (i, 0))]
    scratch = [pltpu.VMEM((A_WIDTH // LANES, tm, LANES), F32)] * 4 + [pltpu.VMEM((tm, A_WIDTH), F32)]
    return _mix_out_call(functools.partial(_even_out_body, n_prompt_tiles=npt), in_specs,
                         list(attn_p) + list(attn_s) + [y_b, x], consts, m, d, tm, scratch, "even_out")


def _odd_out(o_c, proj2, x, consts, tm):
    m, d = x.shape
    in_specs = [pl.BlockSpec((tm, C_V_W), lambda i: (i, 0)),
                pl.BlockSpec((tm, C_V_W), lambda i: (i, Z_COL // C_V_W)),
                pl.BlockSpec((tm, d), lambda i: (i, 0))]
    return _mix_out_call(_odd_out_body, in_specs, [o_c, proj2, x], consts, m, d, tm, [], "odd_out")


def _row_copy(src_ref, src_row, dst_ref, dst_row, sem):
    return pltpu.make_async_copy(src_ref.at[pl.ds(src_row, 1)], dst_ref.at[pl.ds(dst_row, 1)], sem)


def _moe_dispatch_body(s0_ref, s1_ref, fill_ref, used_ref, hn_ref, xs_ref, zero_ref, fill_sem, row_sem):
    i = pl.program_id(0)
    tm = hn_ref.shape[0]

    def fill_copy(e):
        return pltpu.make_async_copy(zero_ref, xs_ref.at[pl.ds(pl.multiple_of(fill_ref[e], MOE_TILE), MOE_TILE)], fill_sem)

    @pl.when(i == 0)
    def _():
        zero_ref[...] = jnp.zeros_like(zero_ref)
        for e in range(N_EXPERTS):
            @pl.when(used_ref[e] > 0)
            def _():
                fill_copy(e).start()
        for e in range(N_EXPERTS):
            @pl.when(used_ref[e] > 0)
            def _():
                fill_copy(e).wait()

    base = i * tm

    def scatter_row(j, carry):
        _row_copy(hn_ref, j, xs_ref, s0_ref[base + j], row_sem).start()
        _row_copy(hn_ref, j, xs_ref, s1_ref[base + j], row_sem).start(priority=1)
        return carry

    lax.fori_loop(0, tm, scatter_row, 0, unroll=8)
    for _ in range(2):
        pltpu.make_async_copy(hn_ref, xs_ref.at[pl.ds(0, tm)], row_sem).wait()


def _moe_ffn_body(te_ref, nu_ref, xs_ref, wg_ref, wu_ref, wd_ref, o_ref):
    del te_ref

    @pl.when(pl.program_id(0) < nu_ref[0])
    def _():
        xa, xb = _unpack_bf16_pair(xs_ref[...])
        half = xa.shape[1]
        hg = _dot(xa, wg_ref[0, :half]) + _dot(xb, wg_ref[0, half:])
        hu = _dot(xa, wu_ref[0, :half]) + _dot(xb, wu_ref[0, half:])
        o_ref[...] = _dot((_silu(hg) * hu).astype(BF16), wd_ref[0])

    @pl.when(pl.program_id(0) >= nu_ref[0])
    def _():
        o_ref[...] = jnp.zeros_like(o_ref)


def _moe_combine_body(s0_ref, s1_ref, route_ref, ys_ref, y_ref, buf_ref, sem):
    tm = y_ref.shape[0]
    base = pl.program_id(0) * tm

    def gather_row(j, carry):
        _row_copy(ys_ref, s0_ref[base + j], buf_ref.at[0], j, sem).start()
        _row_copy(ys_ref, s1_ref[base + j], buf_ref.at[1], j, sem).start(priority=1)
        return carry

    lax.fori_loop(0, tm, gather_row, 0, unroll=8)
    for k in range(2):
        pltpu.make_async_copy(ys_ref.at[pl.ds(0, tm)], buf_ref.at[k], sem).wait()
    y_ref[...] = route_ref[:, 2:3] * buf_ref[0] + route_ref[:, 3:4] * buf_ref[1]


def _moe(hn, route, counts, w_gate, w_up, w_down, tm):
    m = hn.shape[0]
    d = w_gate.shape[1]
    tile = MOE_TILE
    n_slots = 2 * m + N_EXPERTS * tile
    n_tiles = n_slots // tile
    cnt = counts[0, N_GROUPS:N_GROUPS + N_EXPERTS].astype(jnp.int32)
    padded = ((cnt + tile - 1) // tile) * tile
    pad_end = jnp.cumsum(padded)
    pad_start = pad_end - padded
    experts = jnp.arange(N_EXPERTS, dtype=jnp.int32)

    def slots(e, pos):
        start = jnp.sum(jnp.where(e.astype(jnp.int32)[:, None] == experts[None, :], pad_start[None, :], 0), axis=1)
        return (start + pos.astype(jnp.int32)).astype(jnp.int32)

    slot0 = slots(route[:, 0], route[:, 4])
    slot1 = slots(route[:, 1], route[:, 5])
    tile_start = jnp.arange(n_tiles, dtype=jnp.int32) * tile
    tile_expert = jnp.sum((tile_start[:, None] >= pad_end[None, :]).astype(jnp.int32), axis=1)
    tile_expert = jnp.minimum(tile_expert, N_EXPERTS - 1).astype(jnp.int32)
    n_used = (pad_end[-1:] // tile).astype(jnp.int32)
    fill_start = jnp.maximum(pad_end - tile, 0).astype(jnp.int32)

    xs = pl.pallas_call(
        _moe_dispatch_body,
        grid_spec=pltpu.PrefetchScalarGridSpec(
            num_scalar_prefetch=4, grid=(m // tm,),
            in_specs=[pl.BlockSpec((tm, d // 2), lambda i, *_: (i, 0))],
            out_specs=pl.BlockSpec(memory_space=pl.ANY),
            scratch_shapes=[pltpu.VMEM((tile, d // 2), jnp.uint32), pltpu.SemaphoreType.DMA(()),
                            pltpu.SemaphoreType.DMA(())]),
        out_shape=jax.ShapeDtypeStruct((n_slots, d // 2), jnp.uint32),
        compiler_params=_cparams(1), name="moe_dispatch",
    )(slot0, slot1, fill_start, cnt, hn)

    def used(i, nu):
        return jnp.minimum(i, jnp.maximum(nu[0] - 1, 0))

    ys = pl.pallas_call(
        _moe_ffn_body,
        grid_spec=pltpu.PrefetchScalarGridSpec(
            num_scalar_prefetch=2, grid=(n_tiles,),
            in_specs=[pl.BlockSpec((tile, d // 2), lambda i, te, nu: (used(i, nu), 0)),
                      pl.BlockSpec((1, d, EXPERT_FF), lambda i, te, nu: (te[used(i, nu)], 0, 0)),
                      pl.BlockSpec((1, d, EXPERT_FF), lambda i, te, nu: (te[used(i, nu)], 0, 0)),
                      pl.BlockSpec((1, EXPERT_FF, d), lambda i, te, nu: (te[used(i, nu)], 0, 0))],
            out_specs=pl.BlockSpec((tile, d), lambda i, te, nu: (i, 0))),
        out_shape=jax.ShapeDtypeStruct((n_slots, d), F32),
        compiler_params=_cparams(1), name="moe_ffn",
    )(tile_expert, n_used, xs, w_gate, w_up, w_down)

    return pl.pallas_call(
        _moe_combine_body,
        grid_spec=pltpu.PrefetchScalarGridSpec(
            num_scalar_prefetch=2, grid=(m // tm,),
            in_specs=[pl.BlockSpec((tm, ROUTE_LANES), lambda i, *_: (i, 0)),
                      pl.BlockSpec(memory_space=pl.ANY)],
            out_specs=pl.BlockSpec((tm, d), lambda i, *_: (i, 0)),
            scratch_shapes=[pltpu.VMEM((2, tm, d), F32), pltpu.SemaphoreType.DMA(())]),
        out_shape=jax.ShapeDtypeStruct((m, d), F32),
        compiler_params=_cparams(1), name="moe_combine",
    )(slot0, slot1, route, ys)


def _gdn_prep_math(xe, tt, cw_ref, bg, av_ref):
    acc = xe[8:8 + tt] * cw_ref[CONV_W - 1:CONV_W]
    for j in range(1, CONV_W):
        acc = acc + pltpu.roll(xe, j, 0)[8:8 + tt] * cw_ref[CONV_W - 1 - j:CONV_W - j]
    qkv = _silu(acc)
    outs = []
    for h in range(2 * C_QK_HEADS):
        xh = qkv[:, h * C_DK:(h + 1) * C_DK]
        xh = xh * lax.rsqrt(jnp.sum(xh * xh, axis=-1, keepdims=True) + EPS)
        if h < C_QK_HEADS:
            xh = xh * (C_DK ** -0.5)
        outs.append(xh)
    qk = jnp.concatenate(outs, axis=-1)
    v = qkv[:, 2 * C_QK_W:]
    lane = lax.broadcasted_iota(jnp.int32, bg.shape, 1)
    gdec = -jnp.exp(av_ref[0:1]) * _softplus(bg + av_ref[1:2])
    bgo = jnp.where(lane < C_V_HEADS, jax.nn.sigmoid(bg), gdec)
    return qk, v, bgo


def _gdn_prep_prompt_body(x_ref, bg_ref, cw_ref, av_ref, qk_ref, v_ref, bgo_ref, tail_ref):
    @pl.when(pl.program_id(1) == 0)
    def _():
        tail_ref[...] = jnp.zeros_like(tail_ref)

    x = x_ref[...]
    tt = x.shape[0]
    xe = jnp.concatenate([tail_ref[...], x], axis=0)
    tail_ref[...] = x[tt - 8:tt]
    qk, v, bgo = _gdn_prep_math(xe, tt, cw_ref, bg_ref[...], av_ref)
    qk_ref[...] = qk
    v_ref[...] = v
    bgo_ref[...] = bgo


def _gdn_prep_sample_body(x_ref, bg_ref, c0_ref, cw_ref, av_ref, p0, p1, p2, qk_ref, v_ref, bgo_ref, *, t_new):
    del p0, p1, p2
    xe = jnp.concatenate([c0_ref[0], x_ref[...]], axis=0)
    qk, v, bgo = _gdn_prep_math(xe, t_new, cw_ref, bg_ref[...], av_ref)
    qk_ref[...] = qk
    v_ref[...] = v
    bgo_ref[...] = bgo


def _gdn_prep(proj, conv0, conv_w, a_log, dt_bias, n_batch, seq, n_seq, t_new):
    m_total = proj.shape[0]
    mp = n_batch * seq
    av = jnp.zeros((2, 128), F32)
    av = av.at[0, C_V_HEADS:2 * C_V_HEADS].set(a_log).at[1, C_V_HEADS:2 * C_V_HEADS].set(dt_bias)
    bg_col = BG_COL // 128
    tt = min(_row_tile(seq), 128)
    nt = seq // tt
    out_shape = [jax.ShapeDtypeStruct((m_total, 2 * C_QK_W), F32),
                 jax.ShapeDtypeStruct((m_total, C_V_W), F32),
                 jax.ShapeDtypeStruct((m_total, 128), F32)]
    outs = pl.pallas_call(
        _gdn_prep_prompt_body,
        grid=(n_batch, nt),
        in_specs=[pl.BlockSpec((tt, C_CONV_DIM), lambda b, i: (b * nt + i, 0)),
                  pl.BlockSpec((tt, 128), lambda b, i: (b * nt + i, bg_col)),
                  pl.BlockSpec(conv_w.shape, lambda b, i: (0, 0)),
                  pl.BlockSpec(av.shape, lambda b, i: (0, 0))],
        out_specs=[pl.BlockSpec((tt, 2 * C_QK_W), lambda b, i: (b * nt + i, 0)),
                   pl.BlockSpec((tt, C_V_W), lambda b, i: (b * nt + i, 0)),
                   pl.BlockSpec((tt, 128), lambda b, i: (b * nt + i, 0))],
        out_shape=out_shape,
        scratch_shapes=[pltpu.VMEM((8, C_CONV_DIM), F32)],
        compiler_params=_cparams(2), name="gdn_prep_prompt",
    )(proj, proj, conv_w, av)
    blk0 = mp // t_new
    c0p = jnp.pad(conv0, ((0, 0), (8 - (CONV_W - 1), 0), (0, 0)))
    return pl.pallas_call(
        functools.partial(_gdn_prep_sample_body, t_new=t_new),
        grid=(n_seq,),
        in_specs=[pl.BlockSpec((t_new, C_CONV_DIM), lambda i: (blk0 + i, 0)),
                  pl.BlockSpec((t_new, 128), lambda i: (blk0 + i, bg_col)),
                  pl.BlockSpec((1, 8, C_CONV_DIM), lambda i: (i, 0, 0)),
                  pl.BlockSpec(conv_w.shape, lambda i: (0, 0)),
                  pl.BlockSpec(av.shape, lambda i: (0, 0))] + [pl.BlockSpec(memory_space=pl.ANY)] * 3,
        out_specs=[pl.BlockSpec((t_new, 2 * C_QK_W), lambda i: (blk0 + i, 0)),
                   pl.BlockSpec((t_new, C_V_W), lambda i: (blk0 + i, 0)),
                   pl.BlockSpec((t_new, 128), lambda i: (blk0 + i, 0))],
        out_shape=out_shape,
        input_output_aliases={5: 0, 6: 1, 7: 2},
        compiler_params=_cparams(1), name="gdn_prep_sample",
    )(proj, proj, c0p, conv_w, av, *outs)


def _unit_lower_inverses(l_bds, size):
    n = l_bds[0].shape[0]
    ri = lax.broadcasted_iota(jnp.int32, (n, n), 0)
    ci = lax.broadcasted_iota(jnp.int32, (n, n), 1)
    eye = jnp.where(ri == ci, 1.0, 0.0).astype(F32)
    base = ri // GDN_BASE == ci // GDN_BASE
    ps = [jnp.where(base, -l, 0.0) for l in l_bds]
    ts = [eye + p for p in ps]
    s = 2
    while s < GDN_BASE:
        ps = [_dot(p.astype(BF16), p.astype(BF16)) for p in ps]
        ts = [t + _dot(t.astype(BF16), p.astype(BF16)) for t, p in zip(ts, ps)]
        s *= 2
    s = GDN_BASE
    while s < size:
        lower_left = ((ri // s) % 2 == 1) & (ci // s == ri // s - 1)
        tbs = [t.astype(BF16) for t in ts]
        mids = [_dot(jnp.where(lower_left, l, 0.0).astype(BF16), tb).astype(BF16) for l, tb in zip(l_bds, tbs)]
        ts = [t - _dot(tb, mid) for t, tb, mid in zip(ts, tbs, mids)]
        s *= 2
    return ts


def _gdn_chunk_body(qk_ref, v_ref, bg_ref, nw_ref, *rest, chunk, pack, has_state):
    if has_state:
        s0_ref, _prev, o_ref, sout_ref, s_ref = rest
    else:
        o_ref, sout_ref, s_ref = rest
    c = chunk

    @pl.when(pl.program_id(1) == 0)
    def _():
        if has_state:
            s_ref[...] = s0_ref[0]
        else:
            s_ref[...] = jnp.zeros_like(s_ref)

    rep = C_V_HEADS // C_QK_HEADS
    beta = bg_ref[:, 0:C_V_HEADS]
    g = bg_ref[:, C_V_HEADS:2 * C_V_HEADS]
    ti = lax.broadcasted_iota(jnp.int32, (c, c), 0)
    tj = lax.broadcasted_iota(jnp.int32, (c, c), 1)
    tril = jnp.where(ti >= tj, 1.0, 0.0).astype(F32)
    gc = _dot_f32(tril, g)
    g_last = gc[c - 1:c]
    gam = jnp.exp(gc)
    kdec = jnp.exp(g_last - gc)
    g_end = jnp.exp(g_last)
    n = pack * c
    ri = lax.broadcasted_iota(jnp.int32, (n, n), 0)
    ci = lax.broadcasted_iota(jnp.int32, (n, n), 1)
    same = ri // c == ci // c
    incl = same & (ri >= ci)
    strict = same & (ri > ci)
    packs = [range(p0, p0 + pack) for p0 in range(0, C_V_HEADS, pack)]

    def col(a, heads):
        return jnp.concatenate([a[:, h:h + 1] for h in heads], axis=0)

    l_bds, qk_bds, rhss, qgs, kends = [], [], [], [], []
    for heads in packs:
        g_col = col(gc, heads)
        g_row = jnp.sum(jnp.where(ri == ci, g_col, 0.0), axis=0, keepdims=True)
        b_col = col(beta, heads)
        gam_col = col(gam, heads)
        k_st = jnp.concatenate([qk_ref[:, C_QK_W + (h // rep) * C_DK:C_QK_W + (h // rep + 1) * C_DK] for h in heads], axis=0)
        q_st = jnp.concatenate([qk_ref[:, (h // rep) * C_DK:(h // rep + 1) * C_DK] for h in heads], axis=0)
        v_st = jnp.concatenate([v_ref[:, h * C_DV:(h + 1) * C_DV] for h in heads], axis=0)
        kb = k_st.astype(BF16)
        decay = jnp.exp(jnp.where(incl, g_col - g_row, -jnp.inf))
        l_bds.append(jnp.where(strict, b_col * _dot_nt(kb, kb) * decay, 0.0))
        qk_bds.append((_dot_nt(q_st.astype(BF16), kb) * decay).astype(BF16))
        rhss.append(jnp.concatenate([b_col * v_st, (b_col * gam_col) * k_st], axis=1).astype(BF16))
        qgs.append(q_st * gam_col)
        kends.append((k_st * col(kdec, heads)).astype(BF16))
    t_invs = _unit_lower_inverses(l_bds, c)
    uws = [_dot(t.astype(BF16), rhs) for t, rhs in zip(t_invs, rhss)]
    u_sts, q_sts = [], []
    for heads, uw, qg in zip(packs, uws, qgs):
        us, qs = [], []
        for i, h in enumerate(heads):
            rows = slice(i * c, (i + 1) * c)
            lhs = jnp.concatenate([uw[rows, C_DV:], qg[rows]], axis=0).astype(BF16)
            ws = _dot(lhs, s_ref[h].astype(BF16))
            us.append(uw[rows, 0:C_DV] - ws[0:c])
            qs.append(ws[c:])
        u_sts.append(jnp.concatenate(us, axis=0).astype(BF16))
        q_sts.append(jnp.concatenate(qs, axis=0))
    o_sts = [q + _dot(qk_bd, ub) for q, qk_bd, ub in zip(q_sts, qk_bds, u_sts)]
    for heads, o_st, ub, kendb in zip(packs, o_sts, u_sts, kends):
        for i, h in enumerate(heads):
            rows = slice(i * c, (i + 1) * c)
            s_ref[h] = s_ref[h] * g_end[:, h:h + 1] + _dot_tn(kendb[rows], ub[rows])
            o_ref[:, h * C_DV:(h + 1) * C_DV] = _rms(o_st[rows], nw_ref[...])

    @pl.when(pl.program_id(1) == pl.num_programs(1) - 1)
    def _():
        sout_ref[0] = s_ref[...]


def _gdn_chunks(qk, v, bg, onorm_w, s0, n_batch, seq, n_seq, t_new, chunk):
    m_total = qk.shape[0]
    mp = n_batch * seq
    nc = seq // chunk
    nw = onorm_w.reshape(1, C_DV).astype(F32)
    state = (C_V_HEADS, C_DK, C_DV)
    o, s_p = pl.pallas_call(
        functools.partial(_gdn_chunk_body, chunk=chunk, pack=256 // chunk, has_state=False),
        grid=(n_batch, nc),
        in_specs=[pl.BlockSpec((chunk, 2 * C_QK_W), lambda b, i: (b * nc + i, 0)),
                  pl.BlockSpec((chunk, C_V_W), lambda b, i: (b * nc + i, 0)),
                  pl.BlockSpec((chunk, 128), lambda b, i: (b * nc + i, 0)),
                  pl.BlockSpec((1, C_DV), lambda b, i: (0, 0))],
        out_specs=[pl.BlockSpec((chunk, C_V_W), lambda b, i: (b * nc + i, 0)),
                   pl.BlockSpec((1,) + state, lambda b, i: (b, 0, 0, 0))],
        out_shape=[jax.ShapeDtypeStruct((m_total, C_V_W), F32),
                   jax.ShapeDtypeStruct((n_batch,) + state, F32)],
        scratch_shapes=[pltpu.VMEM(state, F32)],
        compiler_params=_cparams(2), name="gdn_prompt",
    )(qk, v, bg, nw)
    blk0 = mp // t_new
    o, s_s = pl.pallas_call(
        functools.partial(_gdn_chunk_body, chunk=t_new, pack=C_V_HEADS, has_state=True),
        grid=(n_seq, 1),
        in_specs=[pl.BlockSpec((t_new, 2 * C_QK_W), lambda b, i: (blk0 + b, 0)),
                  pl.BlockSpec((t_new, C_V_W), lambda b, i: (blk0 + b, 0)),
                  pl.BlockSpec((t_new, 128), lambda b, i: (blk0 + b, 0)),
                  pl.BlockSpec((1, C_DV), lambda b, i: (0, 0)),
                  pl.BlockSpec((1,) + state, lambda b, i: (b, 0, 0, 0)),
                  pl.BlockSpec(memory_space=pl.ANY)],
        out_specs=[pl.BlockSpec((t_new, C_V_W), lambda b, i: (blk0 + b, 0)),
                   pl.BlockSpec((1,) + state, lambda b, i: (b, 0, 0, 0))],
        out_shape=[jax.ShapeDtypeStruct((m_total, C_V_W), F32),
                   jax.ShapeDtypeStruct((n_seq,) + state, F32)],
        scratch_shapes=[pltpu.VMEM(state, F32)],
        input_output_aliases={5: 0},
        compiler_params=_cparams(2), name="gdn_sample",
    )(qk, v, bg, nw, s0, o)
    return o, s_p, s_s


def _odd_out_body(o_ref, z_ref, x_ref, wo_ref, gf_ref, wr_ref, br_ref, x1_ref, hn_ref, route_ref, counts_ref, run_ref):
    y = (o_ref[...] * _silu(z_ref[...])).astype(BF16)
    x1 = x_ref[...] + _dot(y, wo_ref[...])
    x1_ref[...] = x1
    _route(x1, gf_ref, wr_ref, br_ref, hn_ref, route_ref, counts_ref, run_ref)


def _final_body(x_ref, a_ref, g_ref, y_ref):
    y_ref[...] = _rms(x_ref[...] + a_ref[...], g_ref[...])


def _final_norm(x, add, g, tm, row0, n_rows):
    d = x.shape[1]
    blk0 = row0 // tm
    row = pl.BlockSpec((tm, d), lambda i: (blk0 + i, 0))
    return pl.pallas_call(
        _final_body, grid=(n_rows // tm,),
        in_specs=[row, row, pl.BlockSpec((1, d), lambda i: (0, 0))],
        out_specs=pl.BlockSpec((tm, d), lambda i: (i, 0)),
        out_shape=jax.ShapeDtypeStruct((n_rows, d), F32),
        compiler_params=_cparams(1), name="final_norm",
    )(x, add, g.reshape(1, d))


def kernel(x_prompt, x_sample, cache_a_g0_kv, cache_a_g1_kv, cache_a_g2_kv, state_b_h, state_b_conv, state_c_S, state_c_conv, t5_bias, norm_mix, norm_ffn, norm_final, e_w_in, e_conv_w, e_conv_b, e_rg_wa, e_rg_ba, e_rg_wx, e_rg_bx, e_rg_lambda, e_w_out, o_w_in, o_conv_w, o_a_log, o_dt_bias, o_onorm_w, o_w_out, moe_rg_w, moe_rg_b, moe_re_w, moe_re_b, moe_w_gate, moe_w_up, moe_w_down):
    n_batch, seq, d = x_prompt.shape
    n_seq, t_new, _ = x_sample.shape
    mp = n_batch * seq
    ms = n_seq * t_new
    m = mp + ms
    assert t_new == 8 and seq % (DIL_PAIRS[2][1] * A_BLOCK) == 0
    assert e_w_in.shape[0] == 1 and o_w_in.shape[0] == 1
    tm = _row_tile(mp, ms)
    x = jnp.concatenate([x_prompt.reshape(mp, d), x_sample.reshape(ms, d)], axis=0)

    def moe_weights(layer):
        shp = (N_EXPERTS, d, EXPERT_FF)
        return (moe_w_gate[layer].reshape(shp).astype(BF16), moe_w_up[layer].reshape(shp).astype(BF16),
                moe_w_down[layer].reshape(N_EXPERTS, EXPERT_FF, d).astype(BF16))

    def prompt_tail(a, keep, c0, c1):
        return jnp.stack([lax.slice(a, ((b + 1) * seq - keep, c0), ((b + 1) * seq, c1)) for b in range(n_batch)])

    def sample_rows(a, keep, c0, c1):
        return lax.slice(a, (mp, c0), (m, c1)).reshape(n_seq, t_new, c1 - c0)[:, t_new - keep:]

    proj, *qkv_rm = _norm_proj_even(x, norm_mix[0], e_w_in[0].astype(BF16), tm, n_batch, seq)
    attn_p = []
    for g, (_, dil) in enumerate(DIL_PAIRS):
        attn_p.extend(_attn_prompt(qkv_rm[g], t5_bias[:, g], g, dil, n_batch, seq))
    new_a = []
    for g, (win, _) in enumerate(DIL_PAIRS):
        c0 = g * 3 * A_WIDTH + A_WIDTH
        keep = min(win, seq)
        new_a.append(prompt_tail(proj, keep, c0, c0 + 2 * A_WIDTH).reshape(1, n_batch, keep, 2, A_HEADS, A_HEAD_DIM))
        new_a.append(sample_rows(proj, t_new, c0, c0 + 2 * A_WIDTH).reshape(1, n_seq, t_new, 2, A_HEADS, A_HEAD_DIM))
    attn_s = _attn_sample(proj, (cache_a_g0_kv, cache_a_g1_kv, cache_a_g2_kv), t5_bias, n_seq, t_new, mp)
    y_b, bh_p, bh_s = _rglru(proj, state_b_conv[0], state_b_h[0], e_conv_w[0], e_conv_b[0], e_rg_wa[0], e_rg_ba[0],
                             e_rg_wx[0], e_rg_bx[0], e_rg_lambda[0], n_batch, seq, n_seq, t_new)
    wr, br = _router_weights(moe_rg_w[0], moe_rg_b[0], moe_re_w[0], moe_re_b[0])
    x1, hn, route, counts = _even_out(attn_p, attn_s, y_b, x, [e_w_out[0].astype(BF16), norm_ffn[0].reshape(1, d), wr, br],
                                      tm, n_batch, seq)
    y_moe = _moe(hn, route, counts, *moe_weights(0), tm)

    w_in1 = jnp.pad(o_w_in[0], ((0, 0), (0, ODD_IN_PAD - ODD_IN))).astype(BF16)
    x2, proj2 = _norm_proj(x1, (y_moe,), norm_mix[1], w_in1, tm, 896)
    qk, v, bg = _gdn_prep(proj2, state_c_conv[0], o_conv_w[0], o_a_log[0], o_dt_bias[0], n_batch, seq, n_seq, t_new)
    o_c, cs_p, cs_s = _gdn_chunks(qk, v, bg, o_onorm_w[0], state_c_S[0], n_batch, seq, n_seq, t_new, 64)
    wr, br = _router_weights(moe_rg_w[1], moe_rg_b[1], moe_re_w[1], moe_re_b[1])
    x3, hn, route, counts = _odd_out(o_c, proj2, x2, [o_w_out[0].astype(BF16), norm_ffn[1].reshape(1, d), wr, br], tm)
    y_moe = _moe(hn, route, counts, *moe_weights(1), tm)
    y_p = _final_norm(x3, y_moe, norm_final, tm, 0, mp).reshape(n_batch, seq, d)
    y_s = _final_norm(x3, y_moe, norm_final, tm, mp, ms).reshape(n_seq, t_new, d)

    keep = CONV_W - 1
    bconv_p = prompt_tail(proj, keep, XB_COL, XB_COL + RNN_WIDTH)[None]
    bconv_s = sample_rows(proj, keep, XB_COL, XB_COL + RNN_WIDTH)[None]
    cconv_p = prompt_tail(proj2, keep, 0, C_CONV_DIM)[None]
    cconv_s = sample_rows(proj2, keep, 0, C_CONV_DIM)[None]
    return (y_p, y_s, *new_a, bh_p[None], bh_s[None], bconv_p, bconv_s, cs_p[None], cs_s[None], cconv_p, cconv_s)
```

```python
import functools
import math

import jax
import jax.numpy as jnp
import numpy as np
from jax import lax
from jax.experimental import pallas as pl
from jax.experimental.pallas import tpu as pltpu

F32 = jnp.float32
BF16 = jnp.bfloat16
EPS = 1e-6
NEG_INF = -1e30

D_MODEL = 1024
DIL_PAIRS = ((128, 1), (512, 4), (2048, 16))
N_DIL = 3
A_HEADS = 8
A_HEAD_DIM = 64
A_WIDTH = A_HEADS * A_HEAD_DIM
A_BLOCK = 128
SPAN = 128
NUM_BUCKETS = 32
MAX_DISTANCE = 2048
RNN_WIDTH = 512
RNN_BLOCKS = 8
CONV_W = 4
RG_C = 8.0
EVEN_IN = N_DIL * 3 * A_WIDTH + 2 * RNN_WIDTH
XB_COL = N_DIL * 3 * A_WIDTH
GB_COL = XB_COL + RNN_WIDTH
C_QK_HEADS = 8
C_V_HEADS = 16
C_DK = 128
C_DV = 128
C_QK_W = C_QK_HEADS * C_DK
C_V_W = C_V_HEADS * C_DV
C_CONV_DIM = 2 * C_QK_W + C_V_W
ODD_IN = C_CONV_DIM + C_V_W + 2 * C_V_HEADS
ODD_IN_PAD = 6272
Z_COL = C_CONV_DIM
BG_COL = C_CONV_DIM + C_V_W
GDN_BASE = 8
N_GROUPS = 4
EXPERTS_PER_GROUP = 8
N_EXPERTS = N_GROUPS * EXPERTS_PER_GROUP
EXPERT_FF = 256
ROUTE_LANES = 128
MOE_TILE = 256

LANES = 128
VMEM_LIMIT = 56 * 1024 * 1024


def _cparams(n_grid):
    return pltpu.CompilerParams(dimension_semantics=("arbitrary",) * n_grid,
                                vmem_limit_bytes=VMEM_LIMIT)


def _rms(x, g):
    return x * lax.rsqrt(jnp.mean(x * x, axis=-1, keepdims=True) + EPS) * g


def _silu(x):
    return x * jax.nn.sigmoid(x)


def _softplus(x):
    return jnp.maximum(x, 0.0) + jnp.log1p(jnp.exp(-jnp.abs(x)))


def _dot(a, b):
    return jnp.dot(a, b, preferred_element_type=F32)


def _dot_nt(a, b):
    return lax.dot_general(a, b, (((1,), (1,)), ((), ())), preferred_element_type=F32)


def _dot_tn(a, b):
    return lax.dot_general(a, b, (((0,), (0,)), ((), ())), preferred_element_type=F32)


def _dot_f32(a, b):
    return jnp.dot(a, b, preferred_element_type=F32, precision=lax.Precision.HIGHEST)


def _row_tile(*counts):
    for t in (256, 128, 64, 32, 16, 8):
        if all(c % t == 0 for c in counts):
            return t
    raise ValueError("token counts must be multiples of 8")


def _norm_proj_body(*refs, col_chunk, n_add):
    x_ref = refs[0]
    add_refs = refs[1:1 + n_add]
    g_ref, w_ref = refs[1 + n_add:3 + n_add]
    outs = refs[3 + n_add:]
    x = x_ref[...]
    for a in add_refs:
        x = x + a[...]
    if n_add:
        outs[0][...] = x
    o_ref = outs[-1]
    hb = _rms(x, g_ref[...]).astype(BF16)
    for c0 in range(0, o_ref.shape[1], col_chunk):
        o_ref[:, c0:c0 + col_chunk] = _dot(hb, w_ref[:, c0:c0 + col_chunk])


def _norm_proj(x, adds, g, w, tm, col_chunk):
    m, d = x.shape
    n = w.shape[1]
    row = pl.BlockSpec((tm, d), lambda i: (i, 0))
    out_shape = [jax.ShapeDtypeStruct((m, n), F32)]
    out_specs = [pl.BlockSpec((tm, n), lambda i: (i, 0))]
    if adds:
        out_shape = [jax.ShapeDtypeStruct((m, d), F32)] + out_shape
        out_specs = [row] + out_specs
    return pl.pallas_call(
        functools.partial(_norm_proj_body, col_chunk=col_chunk, n_add=len(adds)),
        grid=(m // tm,),
        in_specs=[row] * (1 + len(adds)) + [pl.BlockSpec((1, d), lambda i: (0, 0)),
                                            pl.BlockSpec((d, n), lambda i: (0, 0))],
        out_specs=out_specs, out_shape=out_shape,
        compiler_params=_cparams(1), name="norm_proj",
    )(x, *adds, g.reshape(1, d), w)


def _norm_proj_even_body(x_ref, g_ref, w_ref, o_ref, rm0_ref, rm1_ref, rm2_ref, lane_ref):
    tm = x_ref.shape[0]
    hb = _rms(x_ref[...], g_ref[...]).astype(BF16)
    rm_refs = (rm0_ref, rm1_ref, rm2_ref)
    for c in range(EVEN_IN // A_WIDTH):
        cols = slice(c * A_WIDTH, (c + 1) * A_WIDTH)
        res = _dot(hb, w_ref[:, cols])
        o_ref[:, cols] = res
        if c < 3 * N_DIL:
            g, j = divmod(c, 3)
            dil = DIL_PAIRS[g][1]
            if dil == 1:
                rm_refs[g][0, 0, :, j * A_WIDTH:(j + 1) * A_WIDTH] = res.astype(BF16)
            else:
                for t in range(A_WIDTH // LANES):
                    lane_ref[t] = res[:, t * LANES:(t + 1) * LANES]
                for r in range(dil):
                    part = [lane_ref[t, pl.ds(r, tm // dil, stride=dil), :] for t in range(A_WIDTH // LANES)]
                    rm_refs[g][0, r, :, j * A_WIDTH:(j + 1) * A_WIDTH] = jnp.concatenate(part, axis=1).astype(BF16)


def _norm_proj_even(x, g, w, tm, n_batch, seq):
    m, d = x.shape
    n = w.shape[1]
    tps = seq // tm
    npt = n_batch * tps

    def rm_index(i):
        return (jnp.where(i < npt, i // tps, n_batch), 0, jnp.where(i < npt, i % tps, i - npt), 0)

    rm_shapes = [jax.ShapeDtypeStruct((n_batch + 1, dil, seq // dil, 3 * A_WIDTH), BF16) for _, dil in DIL_PAIRS]
    rm_specs = [pl.BlockSpec((1, dil, tm // dil, 3 * A_WIDTH), rm_index) for _, dil in DIL_PAIRS]
    return pl.pallas_call(
        _norm_proj_even_body,
        grid=(m // tm,),
        in_specs=[pl.BlockSpec((tm, d), lambda i: (i, 0)), pl.BlockSpec((1, d), lambda i: (0, 0)),
                  pl.BlockSpec((d, n), lambda i: (0, 0))],
        out_specs=[pl.BlockSpec((tm, n), lambda i: (i, 0))] + rm_specs,
        out_shape=[jax.ShapeDtypeStruct((m, n), F32)] + rm_shapes,
        scratch_shapes=[pltpu.VMEM((A_WIDTH // LANES, tm, LANES), F32)],
        compiler_params=_cparams(1), name="norm_proj_even",
    )(x, g.reshape(1, d), w)


def _t5_bucket(dist):
    max_exact = NUM_BUCKETS // 2
    d = np.maximum(dist, 1).astype(np.float32)
    large = max_exact + (np.log(d / max_exact) / np.log(MAX_DISTANCE / max_exact)
                         * (NUM_BUCKETS - max_exact)).astype(np.int32)
    large = np.minimum(large, NUM_BUCKETS - 1)
    return np.where(dist < max_exact, dist, large).astype(np.int32)


def _bucket_lookup(tab, buckets):
    onehot = jnp.asarray(buckets[..., None, None] == np.arange(NUM_BUCKETS)[:, None])
    return jnp.sum(jnp.where(onehot, tab.astype(F32), 0.0), axis=-2)


def _prompt_bias(tab, dil):
    qi = np.arange(A_BLOCK)[:, None]
    km = np.arange(2 * A_BLOCK)[None, :]
    delta = A_BLOCK + qi - km
    valid = (delta >= 0) & (delta <= SPAN)
    bias = _bucket_lookup(tab, _t5_bucket(np.clip(delta, 0, SPAN) * dil))
    bias = jnp.where(valid[..., None], bias, NEG_INF)
    return jnp.transpose(bias, (2, 0, 1))


def _attn_prompt_body(q_ref, kp_ref, ko_ref, vp_ref, vo_ref, bias_ref, o_ref, lse_ref):
    first = pl.program_id(2) == 0
    scale = A_HEAD_DIM ** -0.5
    q = q_ref[0, 0]
    k = jnp.concatenate([kp_ref[0, 0], ko_ref[0, 0]], axis=0)
    v = jnp.concatenate([vp_ref[0, 0], vo_ref[0, 0]], axis=0)
    o_ref = o_ref.at[0, 0]
    lse_ref = lse_ref.at[0, 0]
    km = lax.broadcasted_iota(jnp.int32, (1, 2 * A_BLOCK), 1)
    no_prev = jnp.logical_and(first, km < A_BLOCK)
    heads_per_tile = LANES // A_HEAD_DIM
    head_of_lane = lax.broadcasted_iota(jnp.int32, (1, LANES), 1) // A_HEAD_DIM
    for t in range(A_WIDTH // LANES):
        sl = slice(t * LANES, (t + 1) * LANES)
        q_t, k_t, v_t = q[:, sl], k[:, sl], v[:, sl]
        o_t = jnp.zeros((A_BLOCK, LANES), F32)
        lse_t = jnp.zeros((A_BLOCK, LANES), F32)
        for j in range(heads_per_tile):
            mine = head_of_lane == j
            s = _dot_nt(jnp.where(mine, q_t, jnp.zeros_like(q_t)), k_t) * scale + bias_ref[t * heads_per_tile + j]
            s = jnp.where(no_prev, NEG_INF, s)
            m = jnp.max(s, axis=-1, keepdims=True)
            p = jnp.exp(s - m)
            den = jnp.sum(p, axis=-1, keepdims=True)
            o_t = jnp.where(mine, _dot(p.astype(BF16), v_t) / den, o_t)
            lse_t = jnp.where(mine, m + jnp.log(den), lse_t)
        o_ref[:, sl] = o_t
        lse_ref[:, sl] = lse_t


def _attn_prompt(qkv_rm, tab, g, dil, n_batch, seq):
    sub_len = seq // dil
    nb = sub_len // A_BLOCK

    def spec(j, prev):
        def index(b, r, i):
            return (b, r, jnp.maximum(i - 1, 0) if prev else i, j)
        return pl.BlockSpec((1, 1, A_BLOCK, A_WIDTH), index)

    out_spec = pl.BlockSpec((1, 1, A_BLOCK, A_WIDTH), lambda b, r, i: (b, r, i, 0))
    out_sds = jax.ShapeDtypeStruct((n_batch, dil, sub_len, A_WIDTH), F32)
    return pl.pallas_call(
        _attn_prompt_body,
        grid=(n_batch, dil, nb),
        in_specs=[spec(0, False), spec(1, True), spec(1, False), spec(2, True), spec(2, False),
                  pl.BlockSpec((A_HEADS, A_BLOCK, 2 * A_BLOCK), lambda b, r, i: (0, 0, 0))],
        out_specs=[out_spec, out_spec], out_shape=[out_sds, out_sds],
        compiler_params=_cparams(3), name="attn_prompt_g%d" % g,
    )(qkv_rm, qkv_rm, qkv_rm, qkv_rm, qkv_rm, _prompt_bias(tab, dil))


def _sample_bias(tab, dil, cache_len, key_index, t_new):
    t = np.arange(t_new)[:, None]
    dist = cache_len + t - key_index[None, :]
    valid = (dist >= 0) & (dist % dil == 0) & (dist <= SPAN * dil)
    bias = _bucket_lookup(tab, _t5_bucket(np.clip(dist, 0, SPAN * dil)))
    bias = jnp.where(valid[..., None], bias, NEG_INF)
    return jnp.transpose(bias, (2, 0, 1)).reshape(A_HEADS * t_new, key_index.shape[0])


def _attn_sample_body(new_ref, c0_ref, c1_ref, c2_ref, b0_ref, b1_ref, b2_ref, bn_ref, *out_refs, t_new):
    scale = A_HEAD_DIM ** -0.5
    rows = A_HEADS * t_new
    head_of_row = lax.broadcasted_iota(jnp.int32, (rows, A_WIDTH), 0) // t_new
    head_of_lane = lax.broadcasted_iota(jnp.int32, (rows, A_WIDTH), 1) // A_HEAD_DIM
    own = head_of_row == head_of_lane
    caches = (c0_ref, c1_ref, c2_ref)
    biases = (b0_ref, b1_ref, b2_ref)
    for g in range(N_DIL):
        base = g * 3 * A_WIDTH
        cache_len = caches[g].shape[-1]
        q = new_ref[:, base:base + A_WIDTH]
        k_new = new_ref[:, base + A_WIDTH:base + 2 * A_WIDTH].astype(BF16)
        v_new = new_ref[:, base + 2 * A_WIDTH:base + 3 * A_WIDTH].astype(BF16)
        q_bd = jnp.where(own, jnp.concatenate([q] * A_HEADS, axis=0), 0.0).astype(BF16)
        k_t = caches[g][0, 0, 0].reshape(A_WIDTH, cache_len).astype(BF16)
        v_t = caches[g][0, 0, 1].reshape(A_WIDTH, cache_len).astype(BF16)
        s_c = _dot(q_bd, k_t) * scale + biases[g][...]
        s_n = _dot_nt(q_bd, k_new) * scale + bn_ref[g]
        m = jnp.maximum(s_c.max(axis=-1, keepdims=True), s_n.max(axis=-1, keepdims=True))
        p_c = jnp.exp(s_c - m)
        p_n = jnp.exp(s_n - m)
        den = jnp.sum(p_c, axis=-1, keepdims=True) + jnp.sum(p_n, axis=-1, keepdims=True)
        acc = _dot_nt(p_c.astype(BF16), v_t) + _dot(p_n.astype(BF16), v_new)
        acc = jnp.where(own, acc / den, 0.0)
        lse = jnp.where(own, m + jnp.log(den), 0.0)
        o = acc[0:t_new]
        l = lse[0:t_new]
        for h in range(1, A_HEADS):
            o = o + acc[h * t_new:(h + 1) * t_new]
            l = l + lse[h * t_new:(h + 1) * t_new]
        out_refs[2 * g][...] = o
        out_refs[2 * g + 1][...] = l


def _attn_sample(proj, caches, t5_bias, n_seq, t_new, row0):
    cache_lens = [c.shape[2] for c in caches]
    caches_t = [jnp.transpose(c, (0, 1, 3, 4, 5, 2)) for c in caches]
    biases = [_sample_bias(t5_bias[:, g], DIL_PAIRS[g][1], cache_lens[g], np.arange(cache_lens[g]), t_new)
              for g in range(N_DIL)]
    bias_new = jnp.stack([_sample_bias(t5_bias[:, g], DIL_PAIRS[g][1], cache_lens[g],
                                       cache_lens[g] + np.arange(t_new), t_new) for g in range(N_DIL)])
    blk0 = row0 // t_new
    full = lambda a: pl.BlockSpec(a.shape, lambda b: (0,) * a.ndim)
    out_spec = pl.BlockSpec((t_new, A_WIDTH), lambda b: (b, 0))
    cache_specs = [pl.BlockSpec((1, 1, 2, A_HEADS, A_HEAD_DIM, n), lambda b: (0, b, 0, 0, 0, 0)) for n in cache_lens]
    return pl.pallas_call(
        functools.partial(_attn_sample_body, t_new=t_new),
        grid=(n_seq,),
        in_specs=[pl.BlockSpec((t_new, N_DIL * 3 * A_WIDTH), lambda b: (blk0 + b, 0))] + cache_specs
                 + [full(biases[0]), full(biases[1]), full(biases[2]), full(bias_new)],
        out_specs=[out_spec] * (2 * N_DIL),
        out_shape=[jax.ShapeDtypeStruct((n_seq * t_new, A_WIDTH), F32)] * (2 * N_DIL),
        compiler_params=_cparams(1), name="attn_sample",
    )(proj, *caches_t, *biases, bias_new)


def _shift_rows(x, s, fill, axis):
    t = lax.broadcasted_iota(jnp.int32, x.shape, axis)
    return jnp.where(t >= s, pltpu.roll(x, s, axis), fill)


def _linear_scan(a, b, axis):
    n = a.shape[axis]
    s = 1
    while s < n:
        b = b + a * _shift_rows(b, s, 0.0, axis)
        a = a * _shift_rows(a, s, 1.0, axis)
        s *= 2
    return a, b


def _rglru_gates(xc, wa_ref, wx_ref, vec_ref):
    xcb = xc.astype(BF16)
    r = jax.nn.sigmoid(_dot(xcb, wa_ref[...]) + vec_ref[1:2])
    ig = jax.nn.sigmoid(_dot(xcb, wx_ref[...]) + vec_ref[2:3])
    log_a = -RG_C * r * _softplus(-vec_ref[3:4])
    a = jnp.exp(log_a)
    b = jnp.sqrt(1.0 - jnp.exp(2.0 * log_a)) * (ig * xc)
    return a, b


def _gelu(x):
    return 0.5 * x * (1.0 + jnp.tanh(math.sqrt(2.0 / math.pi) * (x + 0.044715 * (x * x * x))))


def _rglru_prompt_body(xb_ref, gb_ref, cw_ref, wa_ref, wx_ref, vec_ref, y_ref, hl_ref, tail_ref, h_ref):
    @pl.when(pl.program_id(1) == 0)
    def _():
        tail_ref[...] = jnp.zeros_like(tail_ref)
        h_ref[...] = jnp.zeros_like(h_ref)

    x = xb_ref[...]
    tt = x.shape[0]
    xe = jnp.concatenate([tail_ref[...], x], axis=0)
    xc = vec_ref[0:1] + x * cw_ref[CONV_W - 1:CONV_W]
    for j in range(1, CONV_W):
        xc = xc + pltpu.roll(xe, j, 0)[8:8 + tt] * cw_ref[CONV_W - 1 - j:CONV_W - j]
    tail_ref[...] = x[tt - 8:tt]
    a, b = _rglru_gates(xc, wa_ref, wx_ref, vec_ref)
    a_cum, h = _linear_scan(a, b, 0)
    h = h + a_cum * h_ref[...]
    h_ref[...] = h[tt - 1:tt]
    hl_ref[0] = h[tt - 1:tt]
    y_ref[...] = h * _gelu(gb_ref[...])


def _rglru_sample_body(xb_ref, gb_ref, c0_ref, h0_ref, cw_ref, wa_ref, wx_ref, vec_ref, prev_ref, y_ref, hl_ref, *, t_new):
    del prev_ref
    x = xb_ref[...]
    rows = x.shape[0]
    ns = rows // t_new
    x3 = x.reshape(ns, t_new, RNN_WIDTH)
    xe = jnp.concatenate([c0_ref[...], x3], axis=1)
    xc = vec_ref[0:1] + x3 * cw_ref[CONV_W - 1:CONV_W]
    for j in range(1, CONV_W):
        xc = xc + pltpu.roll(xe, j, 1)[:, 8:8 + t_new] * cw_ref[CONV_W - 1 - j:CONV_W - j]
    a, b = _rglru_gates(xc.reshape(rows, RNN_WIDTH), wa_ref, wx_ref, vec_ref)
    a_cum, h = _linear_scan(a.reshape(ns, t_new, RNN_WIDTH), b.reshape(ns, t_new, RNN_WIDTH), 1)
    h = h + a_cum * h0_ref[...][:, None, :]
    hl_ref[...] = h[:, t_new - 1, :]
    y_ref[...] = h.reshape(rows, RNN_WIDTH) * _gelu(gb_ref[...])


def _block_diag(w):
    nb, bi, bo = w.shape
    eye = jnp.eye(nb, dtype=w.dtype)
    return (w[:, :, None, :] * eye[:, None, :, None]).reshape(nb * bi, nb * bo)


def _rglru(proj, conv0, h0, conv_w, conv_b, wa, ba, wx, bx, lam, n_batch, seq, n_seq, t_new):
    m_total = proj.shape[0]
    mp = n_batch * seq
    wa_bd = _block_diag(wa).astype(BF16)
    wx_bd = _block_diag(wx).astype(BF16)
    vec = jnp.stack([conv_b, ba, bx, lam]).astype(F32)
    xcol = XB_COL // RNN_WIDTH
    gcol = GB_COL // RNN_WIDTH
    full2 = lambda a, nd: pl.BlockSpec(a.shape, lambda *_: (0,) * a.ndim)
    tt = _row_tile(seq)
    nt = seq // tt
    w_specs2 = [pl.BlockSpec(a.shape, lambda b, i: (0, 0)) for a in (conv_w, wa_bd, wx_bd, vec)]
    y, hl_p = pl.pallas_call(
        _rglru_prompt_body,
        grid=(n_batch, nt),
        in_specs=[pl.BlockSpec((tt, RNN_WIDTH), lambda b, i: (b * nt + i, xcol)),
                  pl.BlockSpec((tt, RNN_WIDTH), lambda b, i: (b * nt + i, gcol))] + w_specs2,
        out_specs=[pl.BlockSpec((tt, RNN_WIDTH), lambda b, i: (b * nt + i, 0)),
                   pl.BlockSpec((1, 1, RNN_WIDTH), lambda b, i: (b, 0, 0))],
        out_shape=[jax.ShapeDtypeStruct((m_total, RNN_WIDTH), F32),
                   jax.ShapeDtypeStruct((n_batch, 1, RNN_WIDTH), F32)],
        scratch_shapes=[pltpu.VMEM((8, RNN_WIDTH), F32), pltpu.VMEM((1, RNN_WIDTH), F32)],
        compiler_params=_cparams(2), name="rglru_prompt",
    )(proj, proj, conv_w, wa_bd, wx_bd, vec)
    del full2
    ts = 32 if n_seq % 32 == 0 else 8
    rows = ts * t_new
    blk0 = mp // rows
    c0p = jnp.pad(conv0, ((0, 0), (8 - (CONV_W - 1), 0), (0, 0)))
    w_specs1 = [pl.BlockSpec(a.shape, lambda i: (0, 0)) for a in (conv_w, wa_bd, wx_bd, vec)]
    y, hl_s = pl.pallas_call(
        functools.partial(_rglru_sample_body, t_new=t_new),
        grid=(n_seq // ts,),
        in_specs=[pl.BlockSpec((rows, RNN_WIDTH), lambda i: (blk0 + i, xcol)),
                  pl.BlockSpec((rows, RNN_WIDTH), lambda i: (blk0 + i, gcol)),
                  pl.BlockSpec((ts, 8, RNN_WIDTH), lambda i: (i, 0, 0)),
                  pl.BlockSpec((ts, RNN_WIDTH), lambda i: (i, 0))] + w_specs1
                 + [pl.BlockSpec(memory_space=pl.ANY)],
        out_specs=[pl.BlockSpec((rows, RNN_WIDTH), lambda i: (blk0 + i, 0)),
                   pl.BlockSpec((ts, RNN_WIDTH), lambda i: (i, 0))],
        out_shape=[jax.ShapeDtypeStruct((m_total, RNN_WIDTH), F32),
                   jax.ShapeDtypeStruct((n_seq, RNN_WIDTH), F32)],
        input_output_aliases={8: 0},
        compiler_params=_cparams(1), name="rglru_sample",
    )(proj, proj, c0p, h0, conv_w, wa_bd, wx_bd, vec, y)
    return y, hl_p.reshape(n_batch, RNN_WIDTH), hl_s


def _pack_bf16_pair(x):
    w = x.shape[1] // 2
    hi = lax.bitcast_convert_type(x[:, :w].astype(BF16).astype(F32), jnp.uint32)
    lo = lax.bitcast_convert_type(x[:, w:].astype(BF16).astype(F32), jnp.uint32)
    return hi | (lo >> 16)


def _unpack_bf16_pair(p):
    hi = lax.bitcast_convert_type(p & jnp.uint32(0xFFFF0000), F32).astype(BF16)
    lo = lax.bitcast_convert_type(p << 16, F32).astype(BF16)
    return hi, lo


def _route(x1, gf_ref, wr_ref, br_ref, hn_ref, route_ref, counts_ref, run_ref):
    @pl.when(pl.program_id(0) == 0)
    def _():
        run_ref[...] = jnp.zeros_like(run_ref)

    hn = _rms(x1, gf_ref[...])
    hn_ref[...] = _pack_bf16_pair(hn)
    hn_hi = hn.astype(BF16)
    hn_lo = (hn - hn_hi.astype(F32)).astype(BF16)
    logits = _dot(hn_hi, wr_ref[0]) + (_dot(hn_hi, wr_ref[1]) + _dot(hn_lo, wr_ref[0])) + br_ref[...]
    lane = lax.broadcasted_iota(jnp.int32, logits.shape, 1)
    is_coarse = lane < N_GROUPS
    coarse = jnp.where(is_coarse, logits, -jnp.inf)
    cmax = jnp.max(coarse, axis=-1, keepdims=True)
    grp = jnp.min(jnp.where(coarse == cmax, lane, ROUTE_LANES), axis=-1, keepdims=True)
    p_grp = 1.0 / jnp.sum(jnp.where(is_coarse, jnp.exp(logits - cmax), 0.0), axis=-1, keepdims=True)
    expert = lane - N_GROUPS
    in_grp = (lane >= N_GROUPS) & (expert < N_EXPERTS) & (expert // EXPERTS_PER_GROUP == grp)
    fine = jnp.where(in_grp, logits, -jnp.inf)
    v1 = jnp.max(fine, axis=-1, keepdims=True)
    i1 = jnp.min(jnp.where(fine == v1, lane, ROUTE_LANES), axis=-1, keepdims=True)
    fine2 = jnp.where(lane == i1, -jnp.inf, fine)
    v2 = jnp.max(fine2, axis=-1, keepdims=True)
    i2 = jnp.min(jnp.where(fine2 == v2, lane, ROUTE_LANES), axis=-1, keepdims=True)
    e2 = jnp.exp(v2 - v1)
    w1 = p_grp / (1.0 + e2)
    w2 = p_grp * e2 / (1.0 + e2)
    tm = logits.shape[0]
    sel = jnp.where(lane == i1, 1.0, jnp.where(lane == i2, 1.0, 0.0))
    ri = lax.broadcasted_iota(jnp.int32, (tm, tm), 0)
    ci = lax.broadcasted_iota(jnp.int32, (tm, tm), 1)
    earlier = jnp.where(ri > ci, 1.0, 0.0).astype(BF16)
    before = _dot(earlier, sel.astype(BF16)) + run_ref[...]
    r1 = jnp.sum(jnp.where(lane == i1, before, 0.0), axis=-1, keepdims=True)
    r2 = jnp.sum(jnp.where(lane == i2, before, 0.0), axis=-1, keepdims=True)
    run_ref[...] = run_ref[...] + jnp.sum(sel, axis=0, keepdims=True)
    counts_ref[...] = run_ref[...]
    route = jnp.where(lane == 0, (i1 - N_GROUPS).astype(F32), 0.0)
    route = jnp.where(lane == 1, (i2 - N_GROUPS).astype(F32), route)
    route = jnp.where(lane == 2, w1, route)
    route = jnp.where(lane == 3, w2, route)
    route = jnp.where(lane == 4, r1, route)
    route = jnp.where(lane == 5, r2, route)
    route_ref[...] = route


def _merge_groups(os, ls):
    mx = jnp.maximum(jnp.maximum(ls[0], ls[1]), ls[2])
    es = [jnp.exp(l - mx) for l in ls]
    return (es[0] * os[0] + es[1] * os[1] + es[2] * os[2]) / (es[0] + es[1] + es[2])


def _even_out_body(o0, l0, o1, l1, o2, l2, so0, sl0, so1, sl1, so2, sl2, yb_ref, x_ref, wo_ref, gf_ref, wr_ref, br_ref,
                   x1_ref, hn_ref, route_ref, counts_ref, n1o, n1l, n2o, n2l, oa_ref, run_ref, *, n_prompt_tiles):
    i = pl.program_id(0)
    tm = x_ref.shape[0]

    @pl.when(i < n_prompt_tiles)
    def _():
        nat = []
        for src, dst, dil in ((o1, n1o, DIL_PAIRS[1][1]), (l1, n1l, DIL_PAIRS[1][1]),
                              (o2, n2o, DIL_PAIRS[2][1]), (l2, n2l, DIL_PAIRS[2][1])):
            for r in range(dil):
                blk = src[0, r]
                for t in range(A_WIDTH // LANES):
                    dst[t, pl.ds(r, tm // dil, stride=dil), :] = blk[:, t * LANES:(t + 1) * LANES]
            nat.append(jnp.concatenate([dst[t] for t in range(A_WIDTH // LANES)], axis=1))
        oa_ref[...] = _merge_groups((o0[0, 0], nat[0], nat[2]), (l0[0, 0], nat[1], nat[3]))

    @pl.when(i >= n_prompt_tiles)
    def _():
        oa_ref[...] = _merge_groups((so0[...], so1[...], so2[...]), (sl0[...], sl1[...], sl2[...]))

    cat = jnp.concatenate([oa_ref[...], yb_ref[...]], axis=-1).astype(BF16)
    x1 = x_ref[...] + _dot(cat, wo_ref[...])
    x1_ref[...] = x1
    _route(x1, gf_ref, wr_ref, br_ref, hn_ref, route_ref, counts_ref, run_ref)


def _router_weights(rg_w, rg_b, re_w, re_b):
    d = rg_w.shape[0]
    wr = jnp.concatenate([rg_w, re_w.reshape(d, N_EXPERTS)], axis=1)
    br = jnp.concatenate([rg_b, re_b.reshape(N_EXPERTS)])
    pad = ROUTE_LANES - wr.shape[1]
    wr = jnp.pad(wr, ((0, 0), (0, pad))).astype(F32)
    wr_hi = wr.astype(BF16)
    wr_lo = (wr - wr_hi.astype(F32)).astype(BF16)
    return jnp.stack([wr_hi, wr_lo]), jnp.pad(br, (0, pad)).reshape(1, ROUTE_LANES).astype(F32)


def _mix_out_call(body, in_specs, args, consts, m, d, tm, scratch, name):
    const_spec = lambda a: pl.BlockSpec(a.shape, lambda i: (0,) * a.ndim)
    return pl.pallas_call(
        body, grid=(m // tm,), in_specs=list(in_specs) + [const_spec(c) for c in consts],
        out_specs=[pl.BlockSpec((tm, d), lambda i: (i, 0)), pl.BlockSpec((tm, d // 2), lambda i: (i, 0)),
                   pl.BlockSpec((tm, ROUTE_LANES), lambda i: (i, 0)), pl.BlockSpec((1, ROUTE_LANES), lambda i: (0, 0))],
        out_shape=[jax.ShapeDtypeStruct((m, d), F32), jax.ShapeDtypeStruct((m, d // 2), jnp.uint32),
                   jax.ShapeDtypeStruct((m, ROUTE_LANES), F32), jax.ShapeDtypeStruct((1, ROUTE_LANES), F32)],
        scratch_shapes=list(scratch) + [pltpu.VMEM((1, ROUTE_LANES), F32)],
        compiler_params=_cparams(1), name=name,
    )(*args, *consts)


def _even_out(attn_p, attn_s, y_b, x, consts, tm, n_batch, seq):
    m, d = x.shape
    tps = seq // tm
    npt = n_batch * tps
    in_specs = []
    for g, (_, dil) in enumerate(DIL_PAIRS):
        def index(i):
            return (jnp.minimum(i // tps, n_batch - 1), 0, jnp.where(i < npt, i % tps, 0), 0)
        in_specs += [pl.BlockSpec((1, dil, tm // dil, A_WIDTH), index)] * 2
    in_specs += [pl.BlockSpec((tm, A_WIDTH), lambda i: (jnp.maximum(i - npt, 0), 0))] * (2 * N_DIL)
    in_specs += [pl.BlockSpec((tm, RNN_WIDTH), lambda i: (i, 0)), pl.BlockSpec((tm, d), lambda i: (i, 0))]
    scratch = [pltpu.VMEM((A_WIDTH // LANES, tm, LANES), F32)] * 4 + [pltpu.VMEM((tm, A_WIDTH), F32)]
    return _mix_out_call(functools.partial(_even_out_body, n_prompt_tiles=npt), in_specs,
                         list(attn_p) + list(attn_s) + [y_b, x], consts, m, d, tm, scratch, "even_out")


def _odd_out(o_c, proj2, x, consts, tm):
    m, d = x.shape
    in_specs = [pl.BlockSpec((tm, C_V_W), lambda i: (i, 0)),
                pl.BlockSpec((tm, C_V_W), lambda i: (i, Z_COL // C_V_W)),
                pl.BlockSpec((tm, d), lambda i: (i, 0))]
    return _mix_out_call(_odd_out_body, in_specs, [o_c, proj2, x], consts, m, d, tm, [], "odd_out")


def _row_copy(src_ref, src_row, dst_ref, dst_row, sem):
    return pltpu.make_async_copy(src_ref.at[pl.ds(src_row, 1)], dst_ref.at[pl.ds(dst_row, 1)], sem)


def _moe_dispatch_body(s0_ref, s1_ref, fill_ref, used_ref, hn_ref, xs_ref, zero_ref, fill_sem, row_sem):
    i = pl.program_id(0)
    tm = hn_ref.shape[0]

    def fill_copy(e):
        return pltpu.make_async_copy(zero_ref, xs_ref.at[pl.ds(pl.multiple_of(fill_ref[e], MOE_TILE), MOE_TILE)], fill_sem)

    @pl.when(i == 0)
    def _():
        zero_ref[...] = jnp.zeros_like(zero_ref)
        for e in range(N_EXPERTS):
            @pl.when(used_ref[e] > 0)
            def _():
                fill_copy(e).start()
        for e in range(N_EXPERTS):
            @pl.when(used_ref[e] > 0)
            def _():
                fill_copy(e).wait()

    base = i * tm

    def scatter_row(j, carry):
        _row_copy(hn_ref, j, xs_ref, s0_ref[base + j], row_sem).start()
        _row_copy(hn_ref, j, xs_ref, s1_ref[base + j], row_sem).start(priority=1)
        return carry

    lax.fori_loop(0, tm, scatter_row, 0, unroll=8)
    for _ in range(2):
        pltpu.make_async_copy(hn_ref, xs_ref.at[pl.ds(0, tm)], row_sem).wait()


def _moe_ffn_body(te_ref, nu_ref, xs_ref, wg_ref, wu_ref, wd_ref, o_ref):
    del te_ref

    @pl.when(pl.program_id(0) < nu_ref[0])
    def _():
        xa, xb = _unpack_bf16_pair(xs_ref[...])
        half = xa.shape[1]
        hg = _dot(xa, wg_ref[0, :half]) + _dot(xb, wg_ref[0, half:])
        hu = _dot(xa, wu_ref[0, :half]) + _dot(xb, wu_ref[0, half:])
        o_ref[...] = _dot((_silu(hg) * hu).astype(BF16), wd_ref[0])

    @pl.when(pl.program_id(0) >= nu_ref[0])
    def _():
        o_ref[...] = jnp.zeros_like(o_ref)


def _moe_combine_body(s0_ref, s1_ref, route_ref, ys_ref, y_ref, buf_ref, sem):
    tm = y_ref.shape[0]
    base = pl.program_id(0) * tm

    def gather_row(j, carry):
        _row_copy(ys_ref, s0_ref[base + j], buf_ref.at[0], j, sem).start()
        _row_copy(ys_ref, s1_ref[base + j], buf_ref.at[1], j, sem).start(priority=1)
        return carry

    lax.fori_loop(0, tm, gather_row, 0, unroll=8)
    for k in range(2):
        pltpu.make_async_copy(ys_ref.at[pl.ds(0, tm)], buf_ref.at[k], sem).wait()
    y_ref[...] = route_ref[:, 2:3] * buf_ref[0] + route_ref[:, 3:4] * buf_ref[1]


def _moe(hn, route, counts, w_gate, w_up, w_down, tm):
    m = hn.shape[0]
    d = w_gate.shape[1]
    tile = MOE_TILE
    n_slots = 2 * m + N_EXPERTS * tile
    n_tiles = n_slots // tile
    cnt = counts[0, N_GROUPS:N_GROUPS + N_EXPERTS].astype(jnp.int32)
    padded = ((cnt + tile - 1) // tile) * tile
    pad_end = jnp.cumsum(padded)
    pad_start = pad_end - padded
    experts = jnp.arange(N_EXPERTS, dtype=jnp.int32)

    def slots(e, pos):
        start = jnp.sum(jnp.where(e.astype(jnp.int32)[:, None] == experts[None, :], pad_start[None, :], 0), axis=1)
        return (start + pos.astype(jnp.int32)).astype(jnp.int32)

    slot0 = slots(route[:, 0], route[:, 4])
    slot1 = slots(route[:, 1], route[:, 5])
    tile_start = jnp.arange(n_tiles, dtype=jnp.int32) * tile
    tile_expert = jnp.sum((tile_start[:, None] >= pad_end[None, :]).astype(jnp.int32), axis=1)
    tile_expert = jnp.minimum(tile_expert, N_EXPERTS - 1).astype(jnp.int32)
    n_used = (pad_end[-1:] // tile).astype(jnp.int32)
    fill_start = jnp.maximum(pad_end - tile, 0).astype(jnp.int32)

    xs = pl.pallas_call(
        _moe_dispatch_body,
        grid_spec=pltpu.PrefetchScalarGridSpec(
            num_scalar_prefetch=4, grid=(m // tm,),
            in_specs=[pl.BlockSpec((tm, d // 2), lambda i, *_: (i, 0))],
            out_specs=pl.BlockSpec(memory_space=pl.ANY),
            scratch_shapes=[pltpu.VMEM((tile, d // 2), jnp.uint32), pltpu.SemaphoreType.DMA(()),
                            pltpu.SemaphoreType.DMA(())]),
        out_shape=jax.ShapeDtypeStruct((n_slots, d // 2), jnp.uint32),
        compiler_params=_cparams(1), name="moe_dispatch",
    )(slot0, slot1, fill_start, cnt, hn)

    def used(i, nu):
        return jnp.minimum(i, jnp.maximum(nu[0] - 1, 0))

    ys = pl.pallas_call(
        _moe_ffn_body,
        grid_spec=pltpu.PrefetchScalarGridSpec(
            num_scalar_prefetch=2, grid=(n_tiles,),
            in_specs=[pl.BlockSpec((tile, d // 2), lambda i, te, nu: (used(i, nu), 0)),
                      pl.BlockSpec((1, d, EXPERT_FF), lambda i, te, nu: (te[used(i, nu)], 0, 0)),
                      pl.BlockSpec((1, d, EXPERT_FF), lambda i, te, nu: (te[used(i, nu)], 0, 0)),
                      pl.BlockSpec((1, EXPERT_FF, d), lambda i, te, nu: (te[used(i, nu)], 0, 0))],
            out_specs=pl.BlockSpec((tile, d), lambda i, te, nu: (i, 0))),
        out_shape=jax.ShapeDtypeStruct((n_slots, d), F32),
        compiler_params=_cparams(1), name="moe_ffn",
    )(tile_expert, n_used, xs, w_gate, w_up, w_down)

    return pl.pallas_call(
        _moe_combine_body,
        grid_spec=pltpu.PrefetchScalarGridSpec(
            num_scalar_prefetch=2, grid=(m // tm,),
            in_specs=[pl.BlockSpec((tm, ROUTE_LANES), lambda i, *_: (i, 0)),
                      pl.BlockSpec(memory_space=pl.ANY)],
            out_specs=pl.BlockSpec((tm, d), lambda i, *_: (i, 0)),
            scratch_shapes=[pltpu.VMEM((2, tm, d), F32), pltpu.SemaphoreType.DMA(())]),
        out_shape=jax.ShapeDtypeStruct((m, d), F32),
        compiler_params=_cparams(1), name="moe_combine",
    )(slot0, slot1, route, ys)


def _gdn_prep_math(xe, tt, cw_ref, bg, av_ref):
    acc = xe[8:8 + tt] * cw_ref[CONV_W - 1:CONV_W]
    for j in range(1, CONV_W):
        acc = acc + pltpu.roll(xe, j, 0)[8:8 + tt] * cw_ref[CONV_W - 1 - j:CONV_W - j]
    qkv = _silu(acc)
    outs = []
    for h in range(2 * C_QK_HEADS):
        xh = qkv[:, h * C_DK:(h + 1) * C_DK]
        xh = xh * lax.rsqrt(jnp.sum(xh * xh, axis=-1, keepdims=True) + EPS)
        if h < C_QK_HEADS:
            xh = xh * (C_DK ** -0.5)
        outs.append(xh)
    qk = jnp.concatenate(outs, axis=-1)
    v = qkv[:, 2 * C_QK_W:]
    lane = lax.broadcasted_iota(jnp.int32, bg.shape, 1)
    gdec = -jnp.exp(av_ref[0:1]) * _softplus(bg + av_ref[1:2])
    bgo = jnp.where(lane < C_V_HEADS, jax.nn.sigmoid(bg), gdec)
    return qk, v, bgo


def _unit_lower_inverses(l_bds, size):
    n = l_bds[0].shape[0]
    ri = lax.broadcasted_iota(jnp.int32, (n, n), 0)
    ci = lax.broadcasted_iota(jnp.int32, (n, n), 1)
    eye = jnp.where(ri == ci, 1.0, 0.0).astype(F32)
    base = ri // GDN_BASE == ci // GDN_BASE
    ps = [jnp.where(base, -l, 0.0) for l in l_bds]
    ts = [eye + p for p in ps]
    s = 2
    while s < GDN_BASE:
        ps = [_dot(p.astype(BF16), p.astype(BF16)) for p in ps]
        ts = [t + _dot(t.astype(BF16), p.astype(BF16)) for t, p in zip(ts, ps)]
        s *= 2
    s = GDN_BASE
    while s < size:
        lower_left = ((ri // s) % 2 == 1) & (ci // s == ri // s - 1)
        tbs = [t.astype(BF16) for t in ts]
        mids = [_dot(jnp.where(lower_left, l, 0.0).astype(BF16), tb).astype(BF16) for l, tb in zip(l_bds, tbs)]
        ts = [t - _dot(tb, mid) for t, tb, mid in zip(ts, tbs, mids)]
        s *= 2
    return ts


def _gdn_body(*refs, chunk, pack, n_par, has_state):
    c = chunk
    x_refs = refs[0:2 * n_par:2]
    bg_refs = refs[1:2 * n_par:2]
    pos = 2 * n_par
    if has_state:
        c0_ref = refs[pos]
        pos += 1
    cw_ref, av_ref, nw_ref = refs[pos:pos + 3]
    pos += 3
    if has_state:
        s0_ref = refs[pos]
        pos += 2
    o_ref, sout_ref, s_ref = refs[pos:pos + 3]
    tail_ref = None if has_state else refs[pos + 3]

    @pl.when(pl.program_id(1) == 0)
    def _():
        if has_state:
            s_ref[...] = s0_ref[...]
        else:
            s_ref[...] = jnp.zeros_like(s_ref)
            tail_ref[...] = jnp.zeros_like(tail_ref)

    rep = C_V_HEADS // C_QK_HEADS
    ti = lax.broadcasted_iota(jnp.int32, (c, c), 0)
    tj = lax.broadcasted_iota(jnp.int32, (c, c), 1)
    tril = jnp.where(ti >= tj, 1.0, 0.0).astype(F32)
    n = pack * c
    ri = lax.broadcasted_iota(jnp.int32, (n, n), 0)
    ci = lax.broadcasted_iota(jnp.int32, (n, n), 1)
    same = ri // c == ci // c
    incl = same & (ri >= ci)
    strict = same & (ri > ci)

    def col(a, heads):
        return jnp.concatenate([a[:, h:h + 1] for h in heads], axis=0)

    items, g_ends = [], []
    l_bds, qk_bds, rhss, qgs, kends = [], [], [], [], []
    for k in range(n_par):
        x = x_refs[k][...]
        prev = c0_ref[k] if has_state else tail_ref[k]
        if not has_state:
            tail_ref[k] = x[c - 8:c]
        qk, v, bgo = _gdn_prep_math(jnp.concatenate([prev, x], axis=0), c, cw_ref, bg_refs[k][...], av_ref)
        beta = bgo[:, 0:C_V_HEADS]
        gc = _dot_f32(tril, bgo[:, C_V_HEADS:2 * C_V_HEADS])
        g_last = gc[c - 1:c]
        gam = jnp.exp(gc)
        kdec = jnp.exp(g_last - gc)
        g_ends.append(jnp.exp(g_last))
        for p0 in range(0, C_V_HEADS, pack):
            heads = range(p0, p0 + pack)
            items.append((k, heads))
            g_col = col(gc, heads)
            g_row = jnp.sum(jnp.where(ri == ci, g_col, 0.0), axis=0, keepdims=True)
            b_col = col(beta, heads)
            gam_col = col(gam, heads)
            k_st = jnp.concatenate([qk[:, C_QK_W + (h // rep) * C_DK:C_QK_W + (h // rep + 1) * C_DK] for h in heads], axis=0)
            q_st = jnp.concatenate([qk[:, (h // rep) * C_DK:(h // rep + 1) * C_DK] for h in heads], axis=0)
            v_st = jnp.concatenate([v[:, h * C_DV:(h + 1) * C_DV] for h in heads], axis=0)
            kb = k_st.astype(BF16)
            decay = jnp.exp(jnp.where(incl, g_col - g_row, -jnp.inf))
            l_bds.append(jnp.where(strict, b_col * _dot_nt(kb, kb) * decay, 0.0))
            qk_bds.append((_dot_nt(q_st.astype(BF16), kb) * decay).astype(BF16))
            rhss.append(jnp.concatenate([b_col * v_st, (b_col * gam_col) * k_st], axis=1).astype(BF16))
            qgs.append(q_st * gam_col)
            kends.append((k_st * col(kdec, heads)).astype(BF16))
    t_invs = _unit_lower_inverses(l_bds, c)
    uws = [_dot(t.astype(BF16), rhs) for t, rhs in zip(t_invs, rhss)]
    u_sts, q_sts = [], []
    for (k, heads), uw, qg in zip(items, uws, qgs):
        us, qs = [], []
        for i, h in enumerate(heads):
            rows = slice(i * c, (i + 1) * c)
            lhs = jnp.concatenate([uw[rows, C_DV:], qg[rows]], axis=0).astype(BF16)
            ws = _dot(lhs, s_ref[k, h].astype(BF16))
            us.append(uw[rows, 0:C_DV] - ws[0:c])
            qs.append(ws[c:])
        u_sts.append(jnp.concatenate(us, axis=0).astype(BF16))
        q_sts.append(jnp.concatenate(qs, axis=0))
    o_sts = [q + _dot(qk_bd, ub) for q, qk_bd, ub in zip(q_sts, qk_bds, u_sts)]
    for (k, heads), o_st, ub, kendb in zip(items, o_sts, u_sts, kends):
        for i, h in enumerate(heads):
            rows = slice(i * c, (i + 1) * c)
            s_ref[k, h] = s_ref[k, h] * g_ends[k][:, h:h + 1] + _dot_tn(kendb[rows], ub[rows])
            o_h = _rms(o_st[rows], nw_ref[...])
            if has_state:
                o_ref[k * c:(k + 1) * c, h * C_DV:(h + 1) * C_DV] = o_h
            else:
                o_ref[k, 0, :, h * C_DV:(h + 1) * C_DV] = o_h

    @pl.when(pl.program_id(1) == pl.num_programs(1) - 1)
    def _():
        sout_ref[...] = s_ref[...]


def _largest_divisor(n, candidates):
    return next(c for c in candidates if n % c == 0)


def _gdn(proj, conv0, conv_w, a_log, dt_bias, onorm_w, s0, n_batch, seq, n_seq, t_new, chunk):
    mp = n_batch * seq
    nc = seq // chunk
    av = jnp.zeros((2, 128), F32)
    av = av.at[0, C_V_HEADS:2 * C_V_HEADS].set(a_log).at[1, C_V_HEADS:2 * C_V_HEADS].set(dt_bias)
    nw = onorm_w.reshape(1, C_DV).astype(F32)
    bg_col = BG_COL // 128
    state = (C_V_HEADS, C_DK, C_DV)
    consts = [conv_w, av, nw]
    const_specs = [pl.BlockSpec(a.shape, lambda b, i: (0, 0)) for a in consts]

    par_p = _largest_divisor(n_batch, (2, 1))
    seq_specs = []
    for k in range(par_p):
        seq_specs += [pl.BlockSpec((chunk, C_CONV_DIM), lambda b, i, k=k: ((b * par_p + k) * nc + i, 0)),
                      pl.BlockSpec((chunk, 128), lambda b, i, k=k: ((b * par_p + k) * nc + i, bg_col))]
    extra = -(-(n_seq * t_new) // seq)
    o4, s_p = pl.pallas_call(
        functools.partial(_gdn_body, chunk=chunk, pack=256 // chunk, n_par=par_p, has_state=False),
        grid=(n_batch // par_p, nc),
        in_specs=seq_specs + const_specs,
        out_specs=[pl.BlockSpec((par_p, 1, chunk, C_V_W), lambda b, i: (b, i, 0, 0)),
                   pl.BlockSpec((par_p,) + state, lambda b, i: (b, 0, 0, 0))],
        out_shape=[jax.ShapeDtypeStruct((n_batch + extra, nc, chunk, C_V_W), F32),
                   jax.ShapeDtypeStruct((n_batch,) + state, F32)],
        scratch_shapes=[pltpu.VMEM((par_p,) + state, F32), pltpu.VMEM((par_p, 8, C_CONV_DIM), F32)],
        compiler_params=_cparams(2), name="gdn_prompt",
    )(*([proj, proj] * par_p), *consts)
    o = o4.reshape((n_batch + extra) * seq, C_V_W)

    par_s = _largest_divisor(n_seq, (4, 2, 1))
    blk0 = mp // t_new
    c0p = jnp.pad(conv0, ((0, 0), (8 - (CONV_W - 1), 0), (0, 0)))
    seq_specs = []
    for k in range(par_s):
        seq_specs += [pl.BlockSpec((t_new, C_CONV_DIM), lambda b, i, k=k: (blk0 + b * par_s + k, 0)),
                      pl.BlockSpec((t_new, 128), lambda b, i, k=k: (blk0 + b * par_s + k, bg_col))]
    n_in = 2 * par_s + 1 + len(consts) + 1
    o, s_s = pl.pallas_call(
        functools.partial(_gdn_body, chunk=t_new, pack=C_V_HEADS, n_par=par_s, has_state=True),
        grid=(n_seq // par_s, 1),
        in_specs=seq_specs + [pl.BlockSpec((par_s, 8, C_CONV_DIM), lambda b, i: (b, 0, 0))] + const_specs
                 + [pl.BlockSpec((par_s,) + state, lambda b, i: (b, 0, 0, 0)), pl.BlockSpec(memory_space=pl.ANY)],
        out_specs=[pl.BlockSpec((par_s * t_new, C_V_W), lambda b, i: (blk0 // par_s + b, 0)),
                   pl.BlockSpec((par_s,) + state, lambda b, i: (b, 0, 0, 0))],
        out_shape=[jax.ShapeDtypeStruct(o.shape, F32), jax.ShapeDtypeStruct((n_seq,) + state, F32)],
        scratch_shapes=[pltpu.VMEM((par_s,) + state, F32)],
        input_output_aliases={n_in: 0},
        compiler_params=_cparams(2), name="gdn_sample",
    )(*([proj, proj] * par_s), c0p, *consts, s0, o)
    return o, s_p, s_s


def _odd_out_body(o_ref, z_ref, x_ref, wo_ref, gf_ref, wr_ref, br_ref, x1_ref, hn_ref, route_ref, counts_ref, run_ref):
    y = (o_ref[...] * _silu(z_ref[...])).astype(BF16)
    x1 = x_ref[...] + _dot(y, wo_ref[...])
    x1_ref[...] = x1
    _route(x1, gf_ref, wr_ref, br_ref, hn_ref, route_ref, counts_ref, run_ref)


def _final_body(x_ref, a_ref, g_ref, y_ref):
    y_ref[...] = _rms(x_ref[...] + a_ref[...], g_ref[...])


def _final_norm(x, add, g, tm, row0, n_rows):
    d = x.shape[1]
    blk0 = row0 // tm
    row = pl.BlockSpec((tm, d), lambda i: (blk0 + i, 0))
    return pl.pallas_call(
        _final_body, grid=(n_rows // tm,),
        in_specs=[row, row, pl.BlockSpec((1, d), lambda i: (0, 0))],
        out_specs=pl.BlockSpec((tm, d), lambda i: (i, 0)),
        out_shape=jax.ShapeDtypeStruct((n_rows, d), F32),
        compiler_params=_cparams(1), name="final_norm",
    )(x, add, g.reshape(1, d))


def kernel(x_prompt, x_sample, cache_a_g0_kv, cache_a_g1_kv, cache_a_g2_kv, state_b_h, state_b_conv, state_c_S, state_c_conv, t5_bias, norm_mix, norm_ffn, norm_final, e_w_in, e_conv_w, e_conv_b, e_rg_wa, e_rg_ba, e_rg_wx, e_rg_bx, e_rg_lambda, e_w_out, o_w_in, o_conv_w, o_a_log, o_dt_bias, o_onorm_w, o_w_out, moe_rg_w, moe_rg_b, moe_re_w, moe_re_b, moe_w_gate, moe_w_up, moe_w_down):
    n_batch, seq, d = x_prompt.shape
    n_seq, t_new, _ = x_sample.shape
    mp = n_batch * seq
    ms = n_seq * t_new
    m = mp + ms
    assert t_new == 8 and seq % (DIL_PAIRS[2][1] * A_BLOCK) == 0
    assert e_w_in.shape[0] == 1 and o_w_in.shape[0] == 1
    tm = _row_tile(mp, ms)
    x = jnp.concatenate([x_prompt.reshape(mp, d), x_sample.reshape(ms, d)], axis=0)

    def moe_weights(layer):
        shp = (N_EXPERTS, d, EXPERT_FF)
        return (moe_w_gate[layer].reshape(shp).astype(BF16), moe_w_up[layer].reshape(shp).astype(BF16),
                moe_w_down[layer].reshape(N_EXPERTS, EXPERT_FF, d).astype(BF16))

    def prompt_tail(a, keep, c0, c1):
        return jnp.stack([lax.slice(a, ((b + 1) * seq - keep, c0), ((b + 1) * seq, c1)) for b in range(n_batch)])

    def sample_rows(a, keep, c0, c1):
        return lax.slice(a, (mp, c0), (m, c1)).reshape(n_seq, t_new, c1 - c0)[:, t_new - keep:]

    proj, *qkv_rm = _norm_proj_even(x, norm_mix[0], e_w_in[0].astype(BF16), tm, n_batch, seq)
    attn_p = []
    for g, (_, dil) in enumerate(DIL_PAIRS):
        attn_p.extend(_attn_prompt(qkv_rm[g], t5_bias[:, g], g, dil, n_batch, seq))
    new_a = []
    for g, (win, _) in enumerate(DIL_PAIRS):
        c0 = g * 3 * A_WIDTH + A_WIDTH
        keep = min(win, seq)
        new_a.append(prompt_tail(proj, keep, c0, c0 + 2 * A_WIDTH).reshape(1, n_batch, keep, 2, A_HEADS, A_HEAD_DIM))
        new_a.append(sample_rows(proj, t_new, c0, c0 + 2 * A_WIDTH).reshape(1, n_seq, t_new, 2, A_HEADS, A_HEAD_DIM))
    attn_s = _attn_sample(proj, (cache_a_g0_kv, cache_a_g1_kv, cache_a_g2_kv), t5_bias, n_seq, t_new, mp)
    y_b, bh_p, bh_s = _rglru(proj, state_b_conv[0], state_b_h[0], e_conv_w[0], e_conv_b[0], e_rg_wa[0], e_rg_ba[0],
                             e_rg_wx[0], e_rg_bx[0], e_rg_lambda[0], n_batch, seq, n_seq, t_new)
    wr, br = _router_weights(moe_rg_w[0], moe_rg_b[0], moe_re_w[0], moe_re_b[0])
    x1, hn, route, counts = _even_out(attn_p, attn_s, y_b, x, [e_w_out[0].astype(BF16), norm_ffn[0].reshape(1, d), wr, br],
                                      tm, n_batch, seq)
    y_moe = _moe(hn, route, counts, *moe_weights(0), tm)

    w_in1 = jnp.pad(o_w_in[0], ((0, 0), (0, ODD_IN_PAD - ODD_IN))).astype(BF16)
    x2, proj2 = _norm_proj(x1, (y_moe,), norm_mix[1], w_in1, tm, 896)
    o_c, cs_p, cs_s = _gdn(proj2, state_c_conv[0], o_conv_w[0], o_a_log[0], o_dt_bias[0], o_onorm_w[0], state_c_S[0],
                           n_batch, seq, n_seq, t_new, 64)
    wr, br = _router_weights(moe_rg_w[1], moe_rg_b[1], moe_re_w[1], moe_re_b[1])
    x3, hn, route, counts = _odd_out(o_c, proj2, x2, [o_w_out[0].astype(BF16), norm_ffn[1].reshape(1, d), wr, br], tm)
    y_moe = _moe(hn, route, counts, *moe_weights(1), tm)
    y_p = _final_norm(x3, y_moe, norm_final, tm, 0, mp).reshape(n_batch, seq, d)
    y_s = _final_norm(x3, y_moe, norm_final, tm, mp, ms).reshape(n_seq, t_new, d)

    keep = CONV_W - 1
    bconv_p = prompt_tail(proj, keep, XB_COL, XB_COL + RNN_WIDTH)[None]
    bconv_s = sample_rows(proj, keep, XB_COL, XB_COL + RNN_WIDTH)[None]
    cconv_p = prompt_tail(proj2, keep, 0, C_CONV_DIM)[None]
    cconv_s = sample_rows(proj2, keep, 0, C_CONV_DIM)[None]
    return (y_p, y_s, *new_a, bh_p[None], bh_s[None], bconv_p, bconv_s, cs_p[None], cs_s[None], cconv_p, cconv_s)
```

```python
import functools
import math

import jax
import jax.numpy as jnp
import numpy as np
from jax import lax
from jax.experimental import pallas as pl
from jax.experimental.pallas import tpu as pltpu

F32 = jnp.float32
BF16 = jnp.bfloat16
EPS = 1e-6
NEG_INF = -1e30

D_MODEL = 1024
DIL_PAIRS = ((128, 1), (512, 4), (2048, 16))
N_DIL = 3
A_HEADS = 8
A_HEAD_DIM = 64
A_WIDTH = A_HEADS * A_HEAD_DIM
A_BLOCK = 128
SPAN = 128
NUM_BUCKETS = 32
MAX_DISTANCE = 2048
RNN_WIDTH = 512
RNN_BLOCKS = 8
CONV_W = 4
RG_C = 8.0
EVEN_IN = N_DIL * 3 * A_WIDTH + 2 * RNN_WIDTH
XB_COL = N_DIL * 3 * A_WIDTH
GB_COL = XB_COL + RNN_WIDTH
C_QK_HEADS = 8
C_V_HEADS = 16
C_DK = 128
C_DV = 128
C_QK_W = C_QK_HEADS * C_DK
C_V_W = C_V_HEADS * C_DV
C_CONV_DIM = 2 * C_QK_W + C_V_W
ODD_IN = C_CONV_DIM + C_V_W + 2 * C_V_HEADS
ODD_IN_PAD = 6272
Z_COL = C_CONV_DIM
BG_COL = C_CONV_DIM + C_V_W
GDN_BASE = 8
N_GROUPS = 4
EXPERTS_PER_GROUP = 8
N_EXPERTS = N_GROUPS * EXPERTS_PER_GROUP
EXPERT_FF = 256
ROUTE_LANES = 128
MOE_TILE = 256

LANES = 128
VMEM_LIMIT = 56 * 1024 * 1024


def _cparams(n_grid):
    return pltpu.CompilerParams(dimension_semantics=("arbitrary",) * n_grid,
                                vmem_limit_bytes=VMEM_LIMIT)


def _rms(x, g):
    return x * lax.rsqrt(jnp.mean(x * x, axis=-1, keepdims=True) + EPS) * g


def _silu(x):
    return x * jax.nn.sigmoid(x)


def _softplus(x):
    return jnp.maximum(x, 0.0) + jnp.log1p(jnp.exp(-jnp.abs(x)))


def _dot(a, b):
    return jnp.dot(a, b, preferred_element_type=F32)


def _dot_nt(a, b):
    return lax.dot_general(a, b, (((1,), (1,)), ((), ())), preferred_element_type=F32)


def _dot_tn(a, b):
    return lax.dot_general(a, b, (((0,), (0,)), ((), ())), preferred_element_type=F32)


def _dot_f32(a, b):
    return jnp.dot(a, b, preferred_element_type=F32, precision=lax.Precision.HIGHEST)


def _row_tile(*counts):
    for t in (256, 128, 64, 32, 16, 8):
        if all(c % t == 0 for c in counts):
            return t
    raise ValueError("token counts must be multiples of 8")


def _row_copy(src_ref, src_row, dst_ref, dst_row, sem):
    return pltpu.make_async_copy(src_ref.at[pl.ds(src_row, 1)], dst_ref.at[pl.ds(dst_row, 1)], sem)


def _moe_residual(x_ref, route_ref, ys_ref, s0_ref, s1_ref, buf_ref, sems, tile0):
    i = pl.program_id(0)
    tm = x_ref.shape[0]

    def start(step, slot):
        base = (tile0 + step) * tm

        def gather_row(j, carry):
            _row_copy(ys_ref, s0_ref[base + j], buf_ref.at[slot, 0], j, sems.at[slot]).start()
            _row_copy(ys_ref, s1_ref[base + j], buf_ref.at[slot, 1], j, sems.at[slot]).start()
            return carry

        lax.fori_loop(0, tm, gather_row, 0, unroll=8)

    @pl.when(i == 0)
    def _():
        start(0, 0)

    @pl.when(i + 1 < pl.num_programs(0))
    def _():
        start(i + 1, (i + 1) % 2)

    slot = i % 2
    for k in range(2):
        pltpu.make_async_copy(ys_ref.at[pl.ds(0, tm)], buf_ref.at[slot, k], sems.at[slot]).wait()
    return x_ref[...] + route_ref[:, 2:3] * buf_ref[slot, 0] + route_ref[:, 3:4] * buf_ref[slot, 1]


def _moe_scratch(tm, d):
    return [pltpu.VMEM((2, 2, tm, d), F32), pltpu.SemaphoreType.DMA((2,))]


def _norm_proj_body(s0_ref, s1_ref, x_ref, route_ref, ys_ref, g_ref, w_ref, x_out_ref, o_ref, buf_ref, sems, *, col_chunk):
    x = _moe_residual(x_ref, route_ref, ys_ref, s0_ref, s1_ref, buf_ref, sems, 0)
    x_out_ref[...] = x
    hb = _rms(x, g_ref[...]).astype(BF16)
    for c0 in range(0, o_ref.shape[1], col_chunk):
        o_ref[:, c0:c0 + col_chunk] = _dot(hb, w_ref[:, c0:c0 + col_chunk])


def _norm_proj(x, moe, g, w, tm, col_chunk):
    route, ys, slot0, slot1 = moe
    m, d = x.shape
    n = w.shape[1]
    row = pl.BlockSpec((tm, d), lambda i, *_: (i, 0))
    return pl.pallas_call(
        functools.partial(_norm_proj_body, col_chunk=col_chunk),
        grid_spec=pltpu.PrefetchScalarGridSpec(
            num_scalar_prefetch=2, grid=(m // tm,),
            in_specs=[row, pl.BlockSpec((tm, ROUTE_LANES), lambda i, *_: (i, 0)), pl.BlockSpec(memory_space=pl.ANY),
                      pl.BlockSpec((1, d), lambda i, *_: (0, 0)), pl.BlockSpec((d, n), lambda i, *_: (0, 0))],
            out_specs=[row, pl.BlockSpec((tm, n), lambda i, *_: (i, 0))],
            scratch_shapes=_moe_scratch(tm, d)),
        out_shape=[jax.ShapeDtypeStruct((m, d), F32), jax.ShapeDtypeStruct((m, n), F32)],
        compiler_params=_cparams(1), name="norm_proj",
    )(slot0, slot1, x, route, ys, g.reshape(1, d), w)


def _norm_proj_even_body(x_ref, g_ref, w_ref, o_ref, rm0_ref, rm1_ref, rm2_ref, lane_ref):
    tm = x_ref.shape[0]
    hb = _rms(x_ref[...], g_ref[...]).astype(BF16)
    rm_refs = (rm0_ref, rm1_ref, rm2_ref)
    for c in range(EVEN_IN // A_WIDTH):
        cols = slice(c * A_WIDTH, (c + 1) * A_WIDTH)
        res = _dot(hb, w_ref[:, cols])
        o_ref[:, cols] = res
        if c < 3 * N_DIL:
            g, j = divmod(c, 3)
            dil = DIL_PAIRS[g][1]
            if dil == 1:
                rm_refs[g][0, 0, :, j * A_WIDTH:(j + 1) * A_WIDTH] = res.astype(BF16)
            else:
                for t in range(A_WIDTH // LANES):
                    lane_ref[t] = res[:, t * LANES:(t + 1) * LANES]
                for r in range(dil):
                    part = [lane_ref[t, pl.ds(r, tm // dil, stride=dil), :] for t in range(A_WIDTH // LANES)]
                    rm_refs[g][0, r, :, j * A_WIDTH:(j + 1) * A_WIDTH] = jnp.concatenate(part, axis=1).astype(BF16)


def _norm_proj_even(x, g, w, tm, n_batch, seq):
    m, d = x.shape
    n = w.shape[1]
    tps = seq // tm
    npt = n_batch * tps

    def rm_index(i):
        return (jnp.where(i < npt, i // tps, n_batch), 0, jnp.where(i < npt, i % tps, i - npt), 0)

    rm_shapes = [jax.ShapeDtypeStruct((n_batch + 1, dil, seq // dil, 3 * A_WIDTH), BF16) for _, dil in DIL_PAIRS]
    rm_specs = [pl.BlockSpec((1, dil, tm // dil, 3 * A_WIDTH), rm_index) for _, dil in DIL_PAIRS]
    return pl.pallas_call(
        _norm_proj_even_body,
        grid=(m // tm,),
        in_specs=[pl.BlockSpec((tm, d), lambda i: (i, 0)), pl.BlockSpec((1, d), lambda i: (0, 0)),
                  pl.BlockSpec((d, n), lambda i: (0, 0))],
        out_specs=[pl.BlockSpec((tm, n), lambda i: (i, 0))] + rm_specs,
        out_shape=[jax.ShapeDtypeStruct((m, n), F32)] + rm_shapes,
        scratch_shapes=[pltpu.VMEM((A_WIDTH // LANES, tm, LANES), F32)],
        compiler_params=_cparams(1), name="norm_proj_even",
    )(x, g.reshape(1, d), w)


def _t5_bucket(dist):
    max_exact = NUM_BUCKETS // 2
    d = np.maximum(dist, 1).astype(np.float32)
    large = max_exact + (np.log(d / max_exact) / np.log(MAX_DISTANCE / max_exact)
                         * (NUM_BUCKETS - max_exact)).astype(np.int32)
    large = np.minimum(large, NUM_BUCKETS - 1)
    return np.where(dist < max_exact, dist, large).astype(np.int32)


def _bucket_lookup(tab, buckets):
    onehot = jnp.asarray(buckets[..., None, None] == np.arange(NUM_BUCKETS)[:, None])
    return jnp.sum(jnp.where(onehot, tab.astype(F32), 0.0), axis=-2)


def _prompt_bias(tab, dil):
    qi = np.arange(A_BLOCK)[:, None]
    km = np.arange(2 * A_BLOCK)[None, :]
    delta = A_BLOCK + qi - km
    valid = (delta >= 0) & (delta <= SPAN)
    bias = _bucket_lookup(tab, _t5_bucket(np.clip(delta, 0, SPAN) * dil))
    bias = jnp.where(valid[..., None], bias, NEG_INF)
    return jnp.transpose(bias, (2, 0, 1))


def _attn_prompt_body(q_ref, kp_ref, ko_ref, vp_ref, vo_ref, bias_ref, o_ref, lse_ref):
    first = pl.program_id(2) == 0
    scale = A_HEAD_DIM ** -0.5
    q = q_ref[0, 0]
    k = jnp.concatenate([kp_ref[0, 0], ko_ref[0, 0]], axis=0)
    v = jnp.concatenate([vp_ref[0, 0], vo_ref[0, 0]], axis=0)
    o_ref = o_ref.at[0, 0]
    lse_ref = lse_ref.at[0, 0]
    km = lax.broadcasted_iota(jnp.int32, (1, 2 * A_BLOCK), 1)
    no_prev = jnp.logical_and(first, km < A_BLOCK)
    heads_per_tile = LANES // A_HEAD_DIM
    head_of_lane = lax.broadcasted_iota(jnp.int32, (1, LANES), 1) // A_HEAD_DIM
    for t in range(A_WIDTH // LANES):
        sl = slice(t * LANES, (t + 1) * LANES)
        q_t, k_t, v_t = q[:, sl], k[:, sl], v[:, sl]
        o_t = jnp.zeros((A_BLOCK, LANES), F32)
        lse_t = jnp.zeros((A_BLOCK, LANES), F32)
        for j in range(heads_per_tile):
            mine = head_of_lane == j
            s = _dot_nt(jnp.where(mine, q_t, jnp.zeros_like(q_t)), k_t) * scale + bias_ref[t * heads_per_tile + j]
            s = jnp.where(no_prev, NEG_INF, s)
            m = jnp.max(s, axis=-1, keepdims=True)
            p = jnp.exp(s - m)
            den = jnp.sum(p, axis=-1, keepdims=True)
            o_t = jnp.where(mine, _dot(p.astype(BF16), v_t) / den, o_t)
            lse_t = jnp.where(mine, m + jnp.log(den), lse_t)
        o_ref[:, sl] = o_t
        lse_ref[:, sl] = lse_t


def _attn_prompt(qkv_rm, tab, g, dil, n_batch, seq):
    sub_len = seq // dil
    nb = sub_len // A_BLOCK

    def spec(j, prev):
        def index(b, r, i):
            return (b, r, jnp.maximum(i - 1, 0) if prev else i, j)
        return pl.BlockSpec((1, 1, A_BLOCK, A_WIDTH), index)

    out_spec = pl.BlockSpec((1, 1, A_BLOCK, A_WIDTH), lambda b, r, i: (b, r, i, 0))
    out_sds = jax.ShapeDtypeStruct((n_batch, dil, sub_len, A_WIDTH), F32)
    return pl.pallas_call(
        _attn_prompt_body,
        grid=(n_batch, dil, nb),
        in_specs=[spec(0, False), spec(1, True), spec(1, False), spec(2, True), spec(2, False),
                  pl.BlockSpec((A_HEADS, A_BLOCK, 2 * A_BLOCK), lambda b, r, i: (0, 0, 0))],
        out_specs=[out_spec, out_spec], out_shape=[out_sds, out_sds],
        compiler_params=_cparams(3), name="attn_prompt_g%d" % g,
    )(qkv_rm, qkv_rm, qkv_rm, qkv_rm, qkv_rm, _prompt_bias(tab, dil))


def _sample_bias(tab, dil, cache_len, key_index, t_new):
    t = np.arange(t_new)[:, None]
    dist = cache_len + t - key_index[None, :]
    valid = (dist >= 0) & (dist % dil == 0) & (dist <= SPAN * dil)
    bias = _bucket_lookup(tab, _t5_bucket(np.clip(dist, 0, SPAN * dil)))
    bias = jnp.where(valid[..., None], bias, NEG_INF)
    return jnp.transpose(bias, (2, 0, 1)).reshape(A_HEADS * t_new, key_index.shape[0])


def _attn_sample_body(new_ref, c0_ref, c1_ref, c2_ref, b0_ref, b1_ref, b2_ref, bn_ref, *out_refs, t_new):
    scale = A_HEAD_DIM ** -0.5
    rows = A_HEADS * t_new
    head_of_row = lax.broadcasted_iota(jnp.int32, (rows, A_WIDTH), 0) // t_new
    head_of_lane = lax.broadcasted_iota(jnp.int32, (rows, A_WIDTH), 1) // A_HEAD_DIM
    own = head_of_row == head_of_lane
    caches = (c0_ref, c1_ref, c2_ref)
    biases = (b0_ref, b1_ref, b2_ref)
    for g in range(N_DIL):
        base = g * 3 * A_WIDTH
        cache_len = caches[g].shape[-1]
        q = new_ref[:, base:base + A_WIDTH]
        k_new = new_ref[:, base + A_WIDTH:base + 2 * A_WIDTH].astype(BF16)
        v_new = new_ref[:, base + 2 * A_WIDTH:base + 3 * A_WIDTH].astype(BF16)
        q_bd = jnp.where(own, jnp.concatenate([q] * A_HEADS, axis=0), 0.0).astype(BF16)
        k_t = caches[g][0, 0, 0].reshape(A_WIDTH, cache_len).astype(BF16)
        v_t = caches[g][0, 0, 1].reshape(A_WIDTH, cache_len).astype(BF16)
        s_c = _dot(q_bd, k_t) * scale + biases[g][...]
        s_n = _dot_nt(q_bd, k_new) * scale + bn_ref[g]
        m = jnp.maximum(s_c.max(axis=-1, keepdims=True), s_n.max(axis=-1, keepdims=True))
        p_c = jnp.exp(s_c - m)
        p_n = jnp.exp(s_n - m)
        den = jnp.sum(p_c, axis=-1, keepdims=True) + jnp.sum(p_n, axis=-1, keepdims=True)
        acc = _dot_nt(p_c.astype(BF16), v_t) + _dot(p_n.astype(BF16), v_new)
        acc = jnp.where(own, acc / den, 0.0)
        lse = jnp.where(own, m + jnp.log(den), 0.0)
        o = acc[0:t_new]
        l = lse[0:t_new]
        for h in range(1, A_HEADS):
            o = o + acc[h * t_new:(h + 1) * t_new]
            l = l + lse[h * t_new:(h + 1) * t_new]
        out_refs[2 * g][...] = o
        out_refs[2 * g + 1][...] = l


def _attn_sample(proj, caches, t5_bias, n_seq, t_new, row0):
    cache_lens = [c.shape[2] for c in caches]
    caches_t = [jnp.transpose(c, (0, 1, 3, 4, 5, 2)) for c in caches]
    biases = [_sample_bias(t5_bias[:, g], DIL_PAIRS[g][1], cache_lens[g], np.arange(cache_lens[g]), t_new)
              for g in range(N_DIL)]
    bias_new = jnp.stack([_sample_bias(t5_bias[:, g], DIL_PAIRS[g][1], cache_lens[g],
                                       cache_lens[g] + np.arange(t_new), t_new) for g in range(N_DIL)])
    blk0 = row0 // t_new
    full = lambda a: pl.BlockSpec(a.shape, lambda b: (0,) * a.ndim)
    out_spec = pl.BlockSpec((t_new, A_WIDTH), lambda b: (b, 0))
    cache_specs = [pl.BlockSpec((1, 1, 2, A_HEADS, A_HEAD_DIM, n), lambda b: (0, b, 0, 0, 0, 0)) for n in cache_lens]
    return pl.pallas_call(
        functools.partial(_attn_sample_body, t_new=t_new),
        grid=(n_seq,),
        in_specs=[pl.BlockSpec((t_new, N_DIL * 3 * A_WIDTH), lambda b: (blk0 + b, 0))] + cache_specs
                 + [full(biases[0]), full(biases[1]), full(biases[2]), full(bias_new)],
        out_specs=[out_spec] * (2 * N_DIL),
        out_shape=[jax.ShapeDtypeStruct((n_seq * t_new, A_WIDTH), F32)] * (2 * N_DIL),
        compiler_params=_cparams(1), name="attn_sample",
    )(proj, *caches_t, *biases, bias_new)


def _shift_rows(x, s, fill, axis):
    t = lax.broadcasted_iota(jnp.int32, x.shape, axis)
    return jnp.where(t >= s, pltpu.roll(x, s, axis), fill)


def _linear_scan(a, b, axis):
    n = a.shape[axis]
    s = 1
    while s < n:
        b = b + a * _shift_rows(b, s, 0.0, axis)
        a = a * _shift_rows(a, s, 1.0, axis)
        s *= 2
    return a, b


def _rglru_gates(xc, wa_ref, wx_ref, vec_ref):
    xcb = xc.astype(BF16)
    r = jax.nn.sigmoid(_dot(xcb, wa_ref[...]) + vec_ref[1:2])
    ig = jax.nn.sigmoid(_dot(xcb, wx_ref[...]) + vec_ref[2:3])
    log_a = -RG_C * r * _softplus(-vec_ref[3:4])
    a = jnp.exp(log_a)
    b = jnp.sqrt(1.0 - jnp.exp(2.0 * log_a)) * (ig * xc)
    return a, b


def _gelu(x):
    return 0.5 * x * (1.0 + jnp.tanh(math.sqrt(2.0 / math.pi) * (x + 0.044715 * (x * x * x))))


def _rglru_prompt_body(xb_ref, gb_ref, cw_ref, wa_ref, wx_ref, vec_ref, y_ref, hl_ref, tail_ref, h_ref):
    @pl.when(pl.program_id(1) == 0)
    def _():
        tail_ref[...] = jnp.zeros_like(tail_ref)
        h_ref[...] = jnp.zeros_like(h_ref)

    x = xb_ref[...]
    tt = x.shape[0]
    xe = jnp.concatenate([tail_ref[...], x], axis=0)
    xc = vec_ref[0:1] + x * cw_ref[CONV_W - 1:CONV_W]
    for j in range(1, CONV_W):
        xc = xc + pltpu.roll(xe, j, 0)[8:8 + tt] * cw_ref[CONV_W - 1 - j:CONV_W - j]
    tail_ref[...] = x[tt - 8:tt]
    a, b = _rglru_gates(xc, wa_ref, wx_ref, vec_ref)
    a_cum, h = _linear_scan(a, b, 0)
    h = h + a_cum * h_ref[...]
    h_ref[...] = h[tt - 1:tt]
    hl_ref[0] = h[tt - 1:tt]
    y_ref[...] = h * _gelu(gb_ref[...])


def _rglru_sample_body(xb_ref, gb_ref, c0_ref, h0_ref, cw_ref, wa_ref, wx_ref, vec_ref, prev_ref, y_ref, hl_ref, *, t_new):
    del prev_ref
    x = xb_ref[...]
    rows = x.shape[0]
    ns = rows // t_new
    x3 = x.reshape(ns, t_new, RNN_WIDTH)
    xe = jnp.concatenate([c0_ref[...], x3], axis=1)
    xc = vec_ref[0:1] + x3 * cw_ref[CONV_W - 1:CONV_W]
    for j in range(1, CONV_W):
        xc = xc + pltpu.roll(xe, j, 1)[:, 8:8 + t_new] * cw_ref[CONV_W - 1 - j:CONV_W - j]
    a, b = _rglru_gates(xc.reshape(rows, RNN_WIDTH), wa_ref, wx_ref, vec_ref)
    a_cum, h = _linear_scan(a.reshape(ns, t_new, RNN_WIDTH), b.reshape(ns, t_new, RNN_WIDTH), 1)
    h = h + a_cum * h0_ref[...][:, None, :]
    hl_ref[...] = h[:, t_new - 1, :]
    y_ref[...] = h.reshape(rows, RNN_WIDTH) * _gelu(gb_ref[...])


def _block_diag(w):
    nb, bi, bo = w.shape
    eye = jnp.eye(nb, dtype=w.dtype)
    return (w[:, :, None, :] * eye[:, None, :, None]).reshape(nb * bi, nb * bo)


def _rglru(proj, conv0, h0, conv_w, conv_b, wa, ba, wx, bx, lam, n_batch, seq, n_seq, t_new):
    m_total = proj.shape[0]
    mp = n_batch * seq
    wa_bd = _block_diag(wa).astype(BF16)
    wx_bd = _block_diag(wx).astype(BF16)
    vec = jnp.stack([conv_b, ba, bx, lam]).astype(F32)
    xcol = XB_COL // RNN_WIDTH
    gcol = GB_COL // RNN_WIDTH
    full2 = lambda a, nd: pl.BlockSpec(a.shape, lambda *_: (0,) * a.ndim)
    tt = _row_tile(seq)
    nt = seq // tt
    w_specs2 = [pl.BlockSpec(a.shape, lambda b, i: (0, 0)) for a in (conv_w, wa_bd, wx_bd, vec)]
    y, hl_p = pl.pallas_call(
        _rglru_prompt_body,
        grid=(n_batch, nt),
        in_specs=[pl.BlockSpec((tt, RNN_WIDTH), lambda b, i: (b * nt + i, xcol)),
                  pl.BlockSpec((tt, RNN_WIDTH), lambda b, i: (b * nt + i, gcol))] + w_specs2,
        out_specs=[pl.BlockSpec((tt, RNN_WIDTH), lambda b, i: (b * nt + i, 0)),
                   pl.BlockSpec((1, 1, RNN_WIDTH), lambda b, i: (b, 0, 0))],
        out_shape=[jax.ShapeDtypeStruct((m_total, RNN_WIDTH), F32),
                   jax.ShapeDtypeStruct((n_batch, 1, RNN_WIDTH), F32)],
        scratch_shapes=[pltpu.VMEM((8, RNN_WIDTH), F32), pltpu.VMEM((1, RNN_WIDTH), F32)],
        compiler_params=_cparams(2), name="rglru_prompt",
    )(proj, proj, conv_w, wa_bd, wx_bd, vec)
    del full2
    ts = 32 if n_seq % 32 == 0 else 8
    rows = ts * t_new
    blk0 = mp // rows
    c0p = jnp.pad(conv0, ((0, 0), (8 - (CONV_W - 1), 0), (0, 0)))
    w_specs1 = [pl.BlockSpec(a.shape, lambda i: (0, 0)) for a in (conv_w, wa_bd, wx_bd, vec)]
    y, hl_s = pl.pallas_call(
        functools.partial(_rglru_sample_body, t_new=t_new),
        grid=(n_seq // ts,),
        in_specs=[pl.BlockSpec((rows, RNN_WIDTH), lambda i: (blk0 + i, xcol)),
                  pl.BlockSpec((rows, RNN_WIDTH), lambda i: (blk0 + i, gcol)),
                  pl.BlockSpec((ts, 8, RNN_WIDTH), lambda i: (i, 0, 0)),
                  pl.BlockSpec((ts, RNN_WIDTH), lambda i: (i, 0))] + w_specs1
                 + [pl.BlockSpec(memory_space=pl.ANY)],
        out_specs=[pl.BlockSpec((rows, RNN_WIDTH), lambda i: (blk0 + i, 0)),
                   pl.BlockSpec((ts, RNN_WIDTH), lambda i: (i, 0))],
        out_shape=[jax.ShapeDtypeStruct((m_total, RNN_WIDTH), F32),
                   jax.ShapeDtypeStruct((n_seq, RNN_WIDTH), F32)],
        input_output_aliases={8: 0},
        compiler_params=_cparams(1), name="rglru_sample",
    )(proj, proj, c0p, h0, conv_w, wa_bd, wx_bd, vec, y)
    return y, hl_p.reshape(n_batch, RNN_WIDTH), hl_s


def _pack_bf16_pair(x):
    w = x.shape[1] // 2
    hi = lax.bitcast_convert_type(x[:, :w].astype(BF16).astype(F32), jnp.uint32)
    lo = lax.bitcast_convert_type(x[:, w:].astype(BF16).astype(F32), jnp.uint32)
    return hi | (lo >> 16)


def _unpack_bf16_pair(p):
    hi = lax.bitcast_convert_type(p & jnp.uint32(0xFFFF0000), F32).astype(BF16)
    lo = lax.bitcast_convert_type(p << 16, F32).astype(BF16)
    return hi, lo


def _route(x1, gf_ref, wr_ref, br_ref, hn_ref, route_ref, counts_ref, run_ref):
    @pl.when(pl.program_id(0) == 0)
    def _():
        run_ref[...] = jnp.zeros_like(run_ref)

    hn = _rms(x1, gf_ref[...])
    hn_ref[...] = _pack_bf16_pair(hn)
    hn_hi = hn.astype(BF16)
    hn_lo = (hn - hn_hi.astype(F32)).astype(BF16)
    logits = _dot(hn_hi, wr_ref[0]) + (_dot(hn_hi, wr_ref[1]) + _dot(hn_lo, wr_ref[0])) + br_ref[...]
    lane = lax.broadcasted_iota(jnp.int32, logits.shape, 1)
    is_coarse = lane < N_GROUPS
    coarse = jnp.where(is_coarse, logits, -jnp.inf)
    cmax = jnp.max(coarse, axis=-1, keepdims=True)
    grp = jnp.min(jnp.where(coarse == cmax, lane, ROUTE_LANES), axis=-1, keepdims=True)
    p_grp = 1.0 / jnp.sum(jnp.where(is_coarse, jnp.exp(logits - cmax), 0.0), axis=-1, keepdims=True)
    expert = lane - N_GROUPS
    in_grp = (lane >= N_GROUPS) & (expert < N_EXPERTS) & (expert // EXPERTS_PER_GROUP == grp)
    fine = jnp.where(in_grp, logits, -jnp.inf)
    v1 = jnp.max(fine, axis=-1, keepdims=True)
    i1 = jnp.min(jnp.where(fine == v1, lane, ROUTE_LANES), axis=-1, keepdims=True)
    fine2 = jnp.where(lane == i1, -jnp.inf, fine)
    v2 = jnp.max(fine2, axis=-1, keepdims=True)
    i2 = jnp.min(jnp.where(fine2 == v2, lane, ROUTE_LANES), axis=-1, keepdims=True)
    e2 = jnp.exp(v2 - v1)
    w1 = p_grp / (1.0 + e2)
    w2 = p_grp * e2 / (1.0 + e2)
    tm = logits.shape[0]
    sel = jnp.where(lane == i1, 1.0, jnp.where(lane == i2, 1.0, 0.0))
    ri = lax.broadcasted_iota(jnp.int32, (tm, tm), 0)
    ci = lax.broadcasted_iota(jnp.int32, (tm, tm), 1)
    earlier = jnp.where(ri > ci, 1.0, 0.0).astype(BF16)
    before = _dot(earlier, sel.astype(BF16)) + run_ref[...]
    r1 = jnp.sum(jnp.where(lane == i1, before, 0.0), axis=-1, keepdims=True)
    r2 = jnp.sum(jnp.where(lane == i2, before, 0.0), axis=-1, keepdims=True)
    run_ref[...] = run_ref[...] + jnp.sum(sel, axis=0, keepdims=True)
    counts_ref[...] = run_ref[...]
    route = jnp.where(lane == 0, (i1 - N_GROUPS).astype(F32), 0.0)
    route = jnp.where(lane == 1, (i2 - N_GROUPS).astype(F32), route)
    route = jnp.where(lane == 2, w1, route)
    route = jnp.where(lane == 3, w2, route)
    route = jnp.where(lane == 4, r1, route)
    route = jnp.where(lane == 5, r2, route)
    route_ref[...] = route


def _merge_groups(os, ls):
    mx = jnp.maximum(jnp.maximum(ls[0], ls[1]), ls[2])
    es = [jnp.exp(l - mx) for l in ls]
    return (es[0] * os[0] + es[1] * os[1] + es[2] * os[2]) / (es[0] + es[1] + es[2])


def _even_out_body(o0, l0, o1, l1, o2, l2, so0, sl0, so1, sl1, so2, sl2, yb_ref, x_ref, wo_ref, gf_ref, wr_ref, br_ref,
                   x1_ref, hn_ref, route_ref, counts_ref, n1o, n1l, n2o, n2l, oa_ref, run_ref, *, n_prompt_tiles):
    i = pl.program_id(0)
    tm = x_ref.shape[0]

    @pl.when(i < n_prompt_tiles)
    def _():
        nat = []
        for src, dst, dil in ((o1, n1o, DIL_PAIRS[1][1]), (l1, n1l, DIL_PAIRS[1][1]),
                              (o2, n2o, DIL_PAIRS[2][1]), (l2, n2l, DIL_PAIRS[2][1])):
            for r in range(dil):
                blk = src[0, r]
                for t in range(A_WIDTH // LANES):
                    dst[t, pl.ds(r, tm // dil, stride=dil), :] = blk[:, t * LANES:(t + 1) * LANES]
            nat.append(jnp.concatenate([dst[t] for t in range(A_WIDTH // LANES)], axis=1))
        oa_ref[...] = _merge_groups((o0[0, 0], nat[0], nat[2]), (l0[0, 0], nat[1], nat[3]))

    @pl.when(i >= n_prompt_tiles)
    def _():
        oa_ref[...] = _merge_groups((so0[...], so1[...], so2[...]), (sl0[...], sl1[...], sl2[...]))

    cat = jnp.concatenate([oa_ref[...], yb_ref[...]], axis=-1).astype(BF16)
    x1 = x_ref[...] + _dot(cat, wo_ref[...])
    x1_ref[...] = x1
    _route(x1, gf_ref, wr_ref, br_ref, hn_ref, route_ref, counts_ref, run_ref)


def _router_weights(rg_w, rg_b, re_w, re_b):
    d = rg_w.shape[0]
    wr = jnp.concatenate([rg_w, re_w.reshape(d, N_EXPERTS)], axis=1)
    br = jnp.concatenate([rg_b, re_b.reshape(N_EXPERTS)])
    pad = ROUTE_LANES - wr.shape[1]
    wr = jnp.pad(wr, ((0, 0), (0, pad))).astype(F32)
    wr_hi = wr.astype(BF16)
    wr_lo = (wr - wr_hi.astype(F32)).astype(BF16)
    return jnp.stack([wr_hi, wr_lo]), jnp.pad(br, (0, pad)).reshape(1, ROUTE_LANES).astype(F32)


def _mix_out_call(body, in_specs, args, consts, m, d, tm, scratch, name):
    const_spec = lambda a: pl.BlockSpec(a.shape, lambda i: (0,) * a.ndim)
    return pl.pallas_call(
        body, grid=(m // tm,), in_specs=list(in_specs) + [const_spec(c) for c in consts],
        out_specs=[pl.BlockSpec((tm, d), lambda i: (i, 0)), pl.BlockSpec((tm, d // 2), lambda i: (i, 0)),
                   pl.BlockSpec((tm, ROUTE_LANES), lambda i: (i, 0)), pl.BlockSpec((1, ROUTE_LANES), lambda i: (0, 0))],
        out_shape=[jax.ShapeDtypeStruct((m, d), F32), jax.ShapeDtypeStruct((m, d // 2), jnp.uint32),
                   jax.ShapeDtypeStruct((m, ROUTE_LANES), F32), jax.ShapeDtypeStruct((1, ROUTE_LANES), F32)],
        scratch_shapes=list(scratch) + [pltpu.VMEM((1, ROUTE_LANES), F32)],
        compiler_params=_cparams(1), name=name,
    )(*args, *consts)


def _even_out(attn_p, attn_s, y_b, x, consts, tm, n_batch, seq):
    m, d = x.shape
    tps = seq // tm
    npt = n_batch * tps
    in_specs = []
    for g, (_, dil) in enumerate(DIL_PAIRS):
        def index(i):
            return (jnp.minimum(i // tps, n_batch - 1), 0, jnp.where(i < npt, i % tps, 0), 0)
        in_specs += [pl.BlockSpec((1, dil, tm // dil, A_WIDTH), index)] * 2
    in_specs += [pl.BlockSpec((tm, A_WIDTH), lambda i: (jnp.maximum(i - npt, 0), 0))] * (2 * N_DIL)
    in_specs += [pl.BlockSpec((tm, RNN_WIDTH), lambda i: (i, 0)), pl.BlockSpec((tm, d), lambda i: (i, 0))]
    scratch = [pltpu.VMEM((A_WIDTH // LANES, tm, LANES), F32)] * 4 + [pltpu.VMEM((tm, A_WIDTH), F32)]
    return _mix_out_call(functools.partial(_even_out_body, n_prompt_tiles=npt), in_specs,
                         list(attn_p) + list(attn_s) + [y_b, x], consts, m, d, tm, scratch, "even_out")


def _odd_out(o_c, proj2, x, consts, tm):
    m, d = x.shape
    in_specs = [pl.BlockSpec((tm, C_V_W), lambda i: (i, 0)),
                pl.BlockSpec((tm, C_V_W), lambda i: (i, Z_COL // C_V_W)),
                pl.BlockSpec((tm, d), lambda i: (i, 0))]
    return _mix_out_call(_odd_out_body, in_specs, [o_c, proj2, x], consts, m, d, tm, [], "odd_out")


def _moe_dispatch_body(s0_ref, s1_ref, fill_ref, used_ref, hn_ref, xs_ref, zero_ref, fill_sem, row_sem):
    i = pl.program_id(0)
    tm = hn_ref.shape[0]

    def fill_copy(e):
        return pltpu.make_async_copy(zero_ref, xs_ref.at[pl.ds(pl.multiple_of(fill_ref[e], MOE_TILE), MOE_TILE)], fill_sem)

    @pl.when(i == 0)
    def _():
        zero_ref[...] = jnp.zeros_like(zero_ref)
        for e in range(N_EXPERTS):
            @pl.when(used_ref[e] > 0)
            def _():
                fill_copy(e).start()
        for e in range(N_EXPERTS):
            @pl.when(used_ref[e] > 0)
            def _():
                fill_copy(e).wait()

    base = i * tm

    def scatter_row(j, carry):
        _row_copy(hn_ref, j, xs_ref, s0_ref[base + j], row_sem).start()
        _row_copy(hn_ref, j, xs_ref, s1_ref[base + j], row_sem).start(priority=1)
        return carry

    lax.fori_loop(0, tm, scatter_row, 0, unroll=8)
    for _ in range(2):
        pltpu.make_async_copy(hn_ref, xs_ref.at[pl.ds(0, tm)], row_sem).wait()


def _moe_ffn_body(te_ref, nu_ref, xs_ref, wg_ref, wu_ref, wd_ref, o_ref):
    del te_ref

    @pl.when(pl.program_id(0) < nu_ref[0])
    def _():
        xa, xb = _unpack_bf16_pair(xs_ref[...])
        half = xa.shape[1]
        hg = _dot(xa, wg_ref[0, :half]) + _dot(xb, wg_ref[0, half:])
        hu = _dot(xa, wu_ref[0, :half]) + _dot(xb, wu_ref[0, half:])
        o_ref[...] = _dot((_silu(hg) * hu).astype(BF16), wd_ref[0])

    @pl.when(pl.program_id(0) >= nu_ref[0])
    def _():
        o_ref[...] = jnp.zeros_like(o_ref)


def _moe(hn, route, counts, w_gate, w_up, w_down, tm):
    m = hn.shape[0]
    d = w_gate.shape[1]
    tile = MOE_TILE
    n_slots = 2 * m + N_EXPERTS * tile
    n_tiles = n_slots // tile
    cnt = counts[0, N_GROUPS:N_GROUPS + N_EXPERTS].astype(jnp.int32)
    padded = ((cnt + tile - 1) // tile) * tile
    pad_end = jnp.cumsum(padded)
    pad_start = pad_end - padded
    experts = jnp.arange(N_EXPERTS, dtype=jnp.int32)

    def slots(e, pos):
        start = jnp.sum(jnp.where(e.astype(jnp.int32)[:, None] == experts[None, :], pad_start[None, :], 0), axis=1)
        return (start + pos.astype(jnp.int32)).astype(jnp.int32)

    slot0 = slots(route[:, 0], route[:, 4])
    slot1 = slots(route[:, 1], route[:, 5])
    tile_start = jnp.arange(n_tiles, dtype=jnp.int32) * tile
    tile_expert = jnp.sum((tile_start[:, None] >= pad_end[None, :]).astype(jnp.int32), axis=1)
    tile_expert = jnp.minimum(tile_expert, N_EXPERTS - 1).astype(jnp.int32)
    n_used = (pad_end[-1:] // tile).astype(jnp.int32)
    fill_start = jnp.maximum(pad_end - tile, 0).astype(jnp.int32)

    xs = pl.pallas_call(
        _moe_dispatch_body,
        grid_spec=pltpu.PrefetchScalarGridSpec(
            num_scalar_prefetch=4, grid=(m // tm,),
            in_specs=[pl.BlockSpec((tm, d // 2), lambda i, *_: (i, 0))],
            out_specs=pl.BlockSpec(memory_space=pl.ANY),
            scratch_shapes=[pltpu.VMEM((tile, d // 2), jnp.uint32), pltpu.SemaphoreType.DMA(()),
                            pltpu.SemaphoreType.DMA(())]),
        out_shape=jax.ShapeDtypeStruct((n_slots, d // 2), jnp.uint32),
        compiler_params=_cparams(1), name="moe_dispatch",
    )(slot0, slot1, fill_start, cnt, hn)

    def used(i, nu):
        return jnp.minimum(i, jnp.maximum(nu[0] - 1, 0))

    ys = pl.pallas_call(
        _moe_ffn_body,
        grid_spec=pltpu.PrefetchScalarGridSpec(
            num_scalar_prefetch=2, grid=(n_tiles,),
            in_specs=[pl.BlockSpec((tile, d // 2), lambda i, te, nu: (used(i, nu), 0)),
                      pl.BlockSpec((1, d, EXPERT_FF), lambda i, te, nu: (te[used(i, nu)], 0, 0)),
                      pl.BlockSpec((1, d, EXPERT_FF), lambda i, te, nu: (te[used(i, nu)], 0, 0)),
                      pl.BlockSpec((1, EXPERT_FF, d), lambda i, te, nu: (te[used(i, nu)], 0, 0))],
            out_specs=pl.BlockSpec((tile, d), lambda i, te, nu: (i, 0))),
        out_shape=jax.ShapeDtypeStruct((n_slots, d), F32),
        compiler_params=_cparams(1), name="moe_ffn",
    )(tile_expert, n_used, xs, w_gate, w_up, w_down)

    return route, ys, slot0, slot1


def _gdn_prep_math(xe, tt, cw_ref, bg, av_ref):
    acc = xe[8:8 + tt] * cw_ref[CONV_W - 1:CONV_W]
    for j in range(1, CONV_W):
        acc = acc + pltpu.roll(xe, j, 0)[8:8 + tt] * cw_ref[CONV_W - 1 - j:CONV_W - j]
    qkv = _silu(acc)
    outs = []
    for h in range(2 * C_QK_HEADS):
        xh = qkv[:, h * C_DK:(h + 1) * C_DK]
        xh = xh * lax.rsqrt(jnp.sum(xh * xh, axis=-1, keepdims=True) + EPS)
        if h < C_QK_HEADS:
            xh = xh * (C_DK ** -0.5)
        outs.append(xh)
    qk = jnp.concatenate(outs, axis=-1)
    v = qkv[:, 2 * C_QK_W:]
    lane = lax.broadcasted_iota(jnp.int32, bg.shape, 1)
    gdec = -jnp.exp(av_ref[0:1]) * _softplus(bg + av_ref[1:2])
    bgo = jnp.where(lane < C_V_HEADS, jax.nn.sigmoid(bg), gdec)
    return qk, v, bgo


def _unit_lower_inverses(l_bds, size):
    n = l_bds[0].shape[0]
    ri = lax.broadcasted_iota(jnp.int32, (n, n), 0)
    ci = lax.broadcasted_iota(jnp.int32, (n, n), 1)
    eye = jnp.where(ri == ci, 1.0, 0.0).astype(F32)
    base = ri // GDN_BASE == ci // GDN_BASE
    ps = [jnp.where(base, -l, 0.0) for l in l_bds]
    ts = [eye + p for p in ps]
    s = 2
    while s < GDN_BASE:
        ps = [_dot(p.astype(BF16), p.astype(BF16)) for p in ps]
        ts = [t + _dot(t.astype(BF16), p.astype(BF16)) for t, p in zip(ts, ps)]
        s *= 2
    s = GDN_BASE
    while s < size:
        lower_left = ((ri // s) % 2 == 1) & (ci // s == ri // s - 1)
        tbs = [t.astype(BF16) for t in ts]
        mids = [_dot(jnp.where(lower_left, l, 0.0).astype(BF16), tb).astype(BF16) for l, tb in zip(l_bds, tbs)]
        ts = [t - _dot(tb, mid) for t, tb, mid in zip(ts, tbs, mids)]
        s *= 2
    return ts


def _gdn_body(*refs, chunk, pack, n_par, has_state):
    c = chunk
    x_refs = refs[0:2 * n_par:2]
    bg_refs = refs[1:2 * n_par:2]
    pos = 2 * n_par
    if has_state:
        c0_ref = refs[pos]
        pos += 1
    cw_ref, av_ref, nw_ref = refs[pos:pos + 3]
    pos += 3
    if has_state:
        s0_ref = refs[pos]
        pos += 2
    o_ref, sout_ref, s_ref = refs[pos:pos + 3]
    tail_ref = None if has_state else refs[pos + 3]

    @pl.when(pl.program_id(1) == 0)
    def _():
        if has_state:
            s_ref[...] = s0_ref[...]
        else:
            s_ref[...] = jnp.zeros_like(s_ref)
            tail_ref[...] = jnp.zeros_like(tail_ref)

    rep = C_V_HEADS // C_QK_HEADS
    ti = lax.broadcasted_iota(jnp.int32, (c, c), 0)
    tj = lax.broadcasted_iota(jnp.int32, (c, c), 1)
    tril = jnp.where(ti >= tj, 1.0, 0.0).astype(F32)
    n = pack * c
    ri = lax.broadcasted_iota(jnp.int32, (n, n), 0)
    ci = lax.broadcasted_iota(jnp.int32, (n, n), 1)
    same = ri // c == ci // c
    incl = same & (ri >= ci)
    strict = same & (ri > ci)

    def col(a, heads):
        return jnp.concatenate([a[:, h:h + 1] for h in heads], axis=0)

    items, g_ends = [], []
    l_bds, qk_bds, rhss, qgs, kends = [], [], [], [], []
    for k in range(n_par):
        x = x_refs[k][...]
        prev = c0_ref[k] if has_state else tail_ref[k]
        if not has_state:
            tail_ref[k] = x[c - 8:c]
        qk, v, bgo = _gdn_prep_math(jnp.concatenate([prev, x], axis=0), c, cw_ref, bg_refs[k][...], av_ref)
        beta = bgo[:, 0:C_V_HEADS]
        gc = _dot_f32(tril, bgo[:, C_V_HEADS:2 * C_V_HEADS])
        g_last = gc[c - 1:c]
        gam = jnp.exp(gc)
        kdec = jnp.exp(g_last - gc)
        g_ends.append(jnp.exp(g_last))
        for p0 in range(0, C_V_HEADS, pack):
            heads = range(p0, p0 + pack)
            items.append((k, heads))
            g_col = col(gc, heads)
            g_row = jnp.sum(jnp.where(ri == ci, g_col, 0.0), axis=0, keepdims=True)
            b_col = col(beta, heads)
            gam_col = col(gam, heads)
            k_st = jnp.concatenate([qk[:, C_QK_W + (h // rep) * C_DK:C_QK_W + (h // rep + 1) * C_DK] for h in heads], axis=0)
            q_st = jnp.concatenate([qk[:, (h // rep) * C_DK:(h // rep + 1) * C_DK] for h in heads], axis=0)
            v_st = jnp.concatenate([v[:, h * C_DV:(h + 1) * C_DV] for h in heads], axis=0)
            kb = k_st.astype(BF16)
            decay = jnp.exp(jnp.where(incl, g_col - g_row, -jnp.inf))
            l_bds.append(jnp.where(strict, b_col * _dot_nt(kb, kb) * decay, 0.0))
            qk_bds.append((_dot_nt(q_st.astype(BF16), kb) * decay).astype(BF16))
            rhss.append(jnp.concatenate([b_col * v_st, (b_col * gam_col) * k_st], axis=1).astype(BF16))
            qgs.append(q_st * gam_col)
            kends.append((k_st * col(kdec, heads)).astype(BF16))
    t_invs = _unit_lower_inverses(l_bds, c)
    uws = [_dot(t.astype(BF16), rhs) for t, rhs in zip(t_invs, rhss)]
    u_sts, q_sts = [], []
    for (k, heads), uw, qg in zip(items, uws, qgs):
        us, qs = [], []
        for i, h in enumerate(heads):
            rows = slice(i * c, (i + 1) * c)
            lhs = jnp.concatenate([uw[rows, C_DV:], qg[rows]], axis=0).astype(BF16)
            ws = _dot(lhs, s_ref[k, h].astype(BF16))
            us.append(uw[rows, 0:C_DV] - ws[0:c])
            qs.append(ws[c:])
        u_sts.append(jnp.concatenate(us, axis=0).astype(BF16))
        q_sts.append(jnp.concatenate(qs, axis=0))
    o_sts = [q + _dot(qk_bd, ub) for q, qk_bd, ub in zip(q_sts, qk_bds, u_sts)]
    for (k, heads), o_st, ub, kendb in zip(items, o_sts, u_sts, kends):
        for i, h in enumerate(heads):
            rows = slice(i * c, (i + 1) * c)
            s_ref[k, h] = s_ref[k, h] * g_ends[k][:, h:h + 1] + _dot_tn(kendb[rows], ub[rows])
            o_h = _rms(o_st[rows], nw_ref[...])
            if has_state:
                o_ref[k * c:(k + 1) * c, h * C_DV:(h + 1) * C_DV] = o_h
            else:
                o_ref[k, 0, :, h * C_DV:(h + 1) * C_DV] = o_h

    @pl.when(pl.program_id(1) == pl.num_programs(1) - 1)
    def _():
        sout_ref[...] = s_ref[...]


def _largest_divisor(n, candidates):
    return next(c for c in candidates if n % c == 0)


def _gdn(proj, conv0, conv_w, a_log, dt_bias, onorm_w, s0, n_batch, seq, n_seq, t_new, chunk):
    mp = n_batch * seq
    nc = seq // chunk
    av = jnp.zeros((2, 128), F32)
    av = av.at[0, C_V_HEADS:2 * C_V_HEADS].set(a_log).at[1, C_V_HEADS:2 * C_V_HEADS].set(dt_bias)
    nw = onorm_w.reshape(1, C_DV).astype(F32)
    bg_col = BG_COL // 128
    state = (C_V_HEADS, C_DK, C_DV)
    consts = [conv_w, av, nw]
    const_specs = [pl.BlockSpec(a.shape, lambda b, i: (0, 0)) for a in consts]

    par_p = _largest_divisor(n_batch, (2, 1))
    seq_specs = []
    for k in range(par_p):
        seq_specs += [pl.BlockSpec((chunk, C_CONV_DIM), lambda b, i, k=k: ((b * par_p + k) * nc + i, 0)),
                      pl.BlockSpec((chunk, 128), lambda b, i, k=k: ((b * par_p + k) * nc + i, bg_col))]
    extra = -(-(n_seq * t_new) // seq)
    o4, s_p = pl.pallas_call(
        functools.partial(_gdn_body, chunk=chunk, pack=256 // chunk, n_par=par_p, has_state=False),
        grid=(n_batch // par_p, nc),
        in_specs=seq_specs + const_specs,
        out_specs=[pl.BlockSpec((par_p, 1, chunk, C_V_W), lambda b, i: (b, i, 0, 0)),
                   pl.BlockSpec((par_p,) + state, lambda b, i: (b, 0, 0, 0))],
        out_shape=[jax.ShapeDtypeStruct((n_batch + extra, nc, chunk, C_V_W), F32),
                   jax.ShapeDtypeStruct((n_batch,) + state, F32)],
        scratch_shapes=[pltpu.VMEM((par_p,) + state, F32), pltpu.VMEM((par_p, 8, C_CONV_DIM), F32)],
        compiler_params=_cparams(2), name="gdn_prompt",
    )(*([proj, proj] * par_p), *consts)
    o = o4.reshape((n_batch + extra) * seq, C_V_W)

    par_s = _largest_divisor(n_seq, (4, 2, 1))
    blk0 = mp // t_new
    c0p = jnp.pad(conv0, ((0, 0), (8 - (CONV_W - 1), 0), (0, 0)))
    seq_specs = []
    for k in range(par_s):
        seq_specs += [pl.BlockSpec((t_new, C_CONV_DIM), lambda b, i, k=k: (blk0 + b * par_s + k, 0)),
                      pl.BlockSpec((t_new, 128), lambda b, i, k=k: (blk0 + b * par_s + k, bg_col))]
    n_in = 2 * par_s + 1 + len(consts) + 1
    o, s_s = pl.pallas_call(
        functools.partial(_gdn_body, chunk=t_new, pack=C_V_HEADS, n_par=par_s, has_state=True),
        grid=(n_seq // par_s, 1),
        in_specs=seq_specs + [pl.BlockSpec((par_s, 8, C_CONV_DIM), lambda b, i: (b, 0, 0))] + const_specs
                 + [pl.BlockSpec((par_s,) + state, lambda b, i: (b, 0, 0, 0)), pl.BlockSpec(memory_space=pl.ANY)],
        out_specs=[pl.BlockSpec((par_s * t_new, C_V_W), lambda b, i: (blk0 // par_s + b, 0)),
                   pl.BlockSpec((par_s,) + state, lambda b, i: (b, 0, 0, 0))],
        out_shape=[jax.ShapeDtypeStruct(o.shape, F32), jax.ShapeDtypeStruct((n_seq,) + state, F32)],
        scratch_shapes=[pltpu.VMEM((par_s,) + state, F32)],
        input_output_aliases={n_in: 0},
        compiler_params=_cparams(2), name="gdn_sample",
    )(*([proj, proj] * par_s), c0p, *consts, s0, o)
    return o, s_p, s_s


def _odd_out_body(o_ref, z_ref, x_ref, wo_ref, gf_ref, wr_ref, br_ref, x1_ref, hn_ref, route_ref, counts_ref, run_ref):
    y = (o_ref[...] * _silu(z_ref[...])).astype(BF16)
    x1 = x_ref[...] + _dot(y, wo_ref[...])
    x1_ref[...] = x1
    _route(x1, gf_ref, wr_ref, br_ref, hn_ref, route_ref, counts_ref, run_ref)


def _final_body(s0_ref, s1_ref, x_ref, route_ref, ys_ref, g_ref, y_ref, buf_ref, sems, *, tile0):
    y_ref[...] = _rms(_moe_residual(x_ref, route_ref, ys_ref, s0_ref, s1_ref, buf_ref, sems, tile0), g_ref[...])


def _final_norm(x, moe, g, tm, row0, n_rows):
    route, ys, slot0, slot1 = moe
    d = x.shape[1]
    blk0 = row0 // tm
    return pl.pallas_call(
        functools.partial(_final_body, tile0=blk0),
        grid_spec=pltpu.PrefetchScalarGridSpec(
            num_scalar_prefetch=2, grid=(n_rows // tm,),
            in_specs=[pl.BlockSpec((tm, d), lambda i, *_: (blk0 + i, 0)),
                      pl.BlockSpec((tm, ROUTE_LANES), lambda i, *_: (blk0 + i, 0)),
                      pl.BlockSpec(memory_space=pl.ANY), pl.BlockSpec((1, d), lambda i, *_: (0, 0))],
            out_specs=pl.BlockSpec((tm, d), lambda i, *_: (i, 0)),
            scratch_shapes=_moe_scratch(tm, d)),
        out_shape=jax.ShapeDtypeStruct((n_rows, d), F32),
        compiler_params=_cparams(1), name="final_norm",
    )(slot0, slot1, x, route, ys, g.reshape(1, d))


def kernel(x_prompt, x_sample, cache_a_g0_kv, cache_a_g1_kv, cache_a_g2_kv, state_b_h, state_b_conv, state_c_S, state_c_conv, t5_bias, norm_mix, norm_ffn, norm_final, e_w_in, e_conv_w, e_conv_b, e_rg_wa, e_rg_ba, e_rg_wx, e_rg_bx, e_rg_lambda, e_w_out, o_w_in, o_conv_w, o_a_log, o_dt_bias, o_onorm_w, o_w_out, moe_rg_w, moe_rg_b, moe_re_w, moe_re_b, moe_w_gate, moe_w_up, moe_w_down):
    n_batch, seq, d = x_prompt.shape
    n_seq, t_new, _ = x_sample.shape
    mp = n_batch * seq
    ms = n_seq * t_new
    m = mp + ms
    assert t_new == 8 and seq % (DIL_PAIRS[2][1] * A_BLOCK) == 0
    assert e_w_in.shape[0] == 1 and o_w_in.shape[0] == 1
    tm = _row_tile(mp, ms)
    x = jnp.concatenate([x_prompt.reshape(mp, d), x_sample.reshape(ms, d)], axis=0)

    def moe_weights(layer):
        shp = (N_EXPERTS, d, EXPERT_FF)
        return (moe_w_gate[layer].reshape(shp).astype(BF16), moe_w_up[layer].reshape(shp).astype(BF16),
                moe_w_down[layer].reshape(N_EXPERTS, EXPERT_FF, d).astype(BF16))

    def prompt_tail(a, keep, c0, c1):
        return jnp.stack([lax.slice(a, ((b + 1) * seq - keep, c0), ((b + 1) * seq, c1)) for b in range(n_batch)])

    def sample_rows(a, keep, c0, c1):
        return lax.slice(a, (mp, c0), (m, c1)).reshape(n_seq, t_new, c1 - c0)[:, t_new - keep:]

    proj, *qkv_rm = _norm_proj_even(x, norm_mix[0], e_w_in[0].astype(BF16), tm, n_batch, seq)
    attn_p = []
    for g, (_, dil) in enumerate(DIL_PAIRS):
        attn_p.extend(_attn_prompt(qkv_rm[g], t5_bias[:, g], g, dil, n_batch, seq))
    new_a = []
    for g, (win, _) in enumerate(DIL_PAIRS):
        c0 = g * 3 * A_WIDTH + A_WIDTH
        keep = min(win, seq)
        new_a.append(prompt_tail(proj, keep, c0, c0 + 2 * A_WIDTH).reshape(1, n_batch, keep, 2, A_HEADS, A_HEAD_DIM))
        new_a.append(sample_rows(proj, t_new, c0, c0 + 2 * A_WIDTH).reshape(1, n_seq, t_new, 2, A_HEADS, A_HEAD_DIM))
    attn_s = _attn_sample(proj, (cache_a_g0_kv, cache_a_g1_kv, cache_a_g2_kv), t5_bias, n_seq, t_new, mp)
    y_b, bh_p, bh_s = _rglru(proj, state_b_conv[0], state_b_h[0], e_conv_w[0], e_conv_b[0], e_rg_wa[0], e_rg_ba[0],
                             e_rg_wx[0], e_rg_bx[0], e_rg_lambda[0], n_batch, seq, n_seq, t_new)
    wr, br = _router_weights(moe_rg_w[0], moe_rg_b[0], moe_re_w[0], moe_re_b[0])
    x1, hn, route, counts = _even_out(attn_p, attn_s, y_b, x, [e_w_out[0].astype(BF16), norm_ffn[0].reshape(1, d), wr, br],
                                      tm, n_batch, seq)
    y_moe = _moe(hn, route, counts, *moe_weights(0), tm)

    w_in1 = jnp.pad(o_w_in[0], ((0, 0), (0, ODD_IN_PAD - ODD_IN))).astype(BF16)
    x2, proj2 = _norm_proj(x1, y_moe, norm_mix[1], w_in1, tm, 896)
    o_c, cs_p, cs_s = _gdn(proj2, state_c_conv[0], o_conv_w[0], o_a_log[0], o_dt_bias[0], o_onorm_w[0], state_c_S[0],
                           n_batch, seq, n_seq, t_new, 64)
    wr, br = _router_weights(moe_rg_w[1], moe_rg_b[1], moe_re_w[1], moe_re_b[1])
    x3, hn, route, counts = _odd_out(o_c, proj2, x2, [o_w_out[0].astype(BF16), norm_ffn[1].reshape(1, d), wr, br], tm)
    y_moe = _moe(hn, route, counts, *moe_weights(1), tm)
    y_p = _final_norm(x3, y_moe, norm_final, tm, 0, mp).reshape(n_batch, seq, d)
    y_s = _final_norm(x3, y_moe, norm_final, tm, mp, ms).reshape(n_seq, t_new, d)

    keep = CONV_W - 1
    bconv_p = prompt_tail(proj, keep, XB_COL, XB_COL + RNN_WIDTH)[None]
    bconv_s = sample_rows(proj, keep, XB_COL, XB_COL + RNN_WIDTH)[None]
    cconv_p = prompt_tail(proj2, keep, 0, C_CONV_DIM)[None]
    cconv_s = sample_rows(proj2, keep, 0, C_CONV_DIM)[None]
    return (y_p, y_s, *new_a, bh_p[None], bh_s[None], bconv_p, bconv_s, cs_p[None], cs_s[None], cconv_p, cconv_s)
```

```python
import functools
import math

import jax
import jax.numpy as jnp
import numpy as np
from jax import lax
from jax.experimental import pallas as pl
from jax.experimental.pallas import tpu as pltpu

F32 = jnp.float32
BF16 = jnp.bfloat16
EPS = 1e-6
NEG_INF = -1e30

D_MODEL = 1024
DIL_PAIRS = ((128, 1), (512, 4), (2048, 16))
N_DIL = 3
A_HEADS = 8
A_HEAD_DIM = 64
A_WIDTH = A_HEADS * A_HEAD_DIM
A_BLOCK = 128
SPAN = 128
NUM_BUCKETS = 32
MAX_DISTANCE = 2048
RNN_WIDTH = 512
RNN_BLOCKS = 8
CONV_W = 4
RG_C = 8.0
EVEN_IN = N_DIL * 3 * A_WIDTH + 2 * RNN_WIDTH
XB_COL = N_DIL * 3 * A_WIDTH
GB_COL = XB_COL + RNN_WIDTH
C_QK_HEADS = 8
C_V_HEADS = 16
C_DK = 128
C_DV = 128
C_QK_W = C_QK_HEADS * C_DK
C_V_W = C_V_HEADS * C_DV
C_CONV_DIM = 2 * C_QK_W + C_V_W
ODD_IN = C_CONV_DIM + C_V_W + 2 * C_V_HEADS
ODD_IN_PAD = 6272
Z_COL = C_CONV_DIM
BG_COL = C_CONV_DIM + C_V_W
GDN_BASE = 8
N_GROUPS = 4
EXPERTS_PER_GROUP = 8
N_EXPERTS = N_GROUPS * EXPERTS_PER_GROUP
EXPERT_FF = 256
ROUTE_LANES = 128
MOE_TILE = 256
SLOT_CODE = 1 << 18

LANES = 128
VMEM_LIMIT = 56 * 1024 * 1024


def _cparams(n_grid):
    return pltpu.CompilerParams(dimension_semantics=("arbitrary",) * n_grid,
                                vmem_limit_bytes=VMEM_LIMIT)


def _rms(x, g):
    return x * lax.rsqrt(jnp.mean(x * x, axis=-1, keepdims=True) + EPS) * g


def _silu(x):
    return x * jax.nn.sigmoid(x)


def _softplus(x):
    return jnp.maximum(x, 0.0) + jnp.log1p(jnp.exp(-jnp.abs(x)))


def _dot(a, b):
    return jnp.dot(a, b, preferred_element_type=F32)


def _dot_nt(a, b):
    return lax.dot_general(a, b, (((1,), (1,)), ((), ())), preferred_element_type=F32)


def _dot_tn(a, b):
    return lax.dot_general(a, b, (((0,), (0,)), ((), ())), preferred_element_type=F32)


def _dot_f32(a, b):
    return jnp.dot(a, b, preferred_element_type=F32, precision=lax.Precision.HIGHEST)


def _row_tile(*counts):
    for t in (256, 128, 64, 32, 16, 8):
        if all(c % t == 0 for c in counts):
            return t
    raise ValueError("token counts must be multiples of 8")


def _row_copy(src_ref, src_row, dst_ref, dst_row, sem):
    return pltpu.make_async_copy(src_ref.at[pl.ds(src_row, 1)], dst_ref.at[pl.ds(dst_row, 1)], sem)


def _slot(code, start_ref):
    return start_ref[code // SLOT_CODE] + code % SLOT_CODE


def _moe_residual(x_ref, route_ref, ys_ref, c0_ref, c1_ref, start_ref, buf_ref, sems, tile0):
    i = pl.program_id(0)
    tm = x_ref.shape[0]

    def start(step, slot):
        base = (tile0 + step) * tm

        def gather_row(j, carry):
            _row_copy(ys_ref, _slot(c0_ref[base + j], start_ref), buf_ref.at[slot, 0], j, sems.at[slot]).start(priority=1)
            _row_copy(ys_ref, _slot(c1_ref[base + j], start_ref), buf_ref.at[slot, 1], j, sems.at[slot]).start(priority=1)
            return carry

        lax.fori_loop(0, tm, gather_row, 0, unroll=8)

    @pl.when(i == 0)
    def _():
        start(0, 0)

    @pl.when(i + 1 < pl.num_programs(0))
    def _():
        start(i + 1, (i + 1) % 2)

    slot = i % 2
    for k in range(2):
        pltpu.make_async_copy(ys_ref.at[pl.ds(0, tm)], buf_ref.at[slot, k], sems.at[slot]).wait()
    return x_ref[...] + route_ref[:, 2:3] * buf_ref[slot, 0] + route_ref[:, 3:4] * buf_ref[slot, 1]


def _moe_scratch(tm, d):
    return [pltpu.VMEM((2, 2, tm, d), F32), pltpu.SemaphoreType.DMA((2,))]


def _norm_proj_body(c0_ref, c1_ref, start_ref, x_ref, route_ref, ys_ref, g_ref, w_ref, x_out_ref, o_ref, buf_ref, sems, *, col_chunk):
    x = _moe_residual(x_ref, route_ref, ys_ref, c0_ref, c1_ref, start_ref, buf_ref, sems, 0)
    x_out_ref[...] = x
    hb = _rms(x, g_ref[...]).astype(BF16)
    for c0 in range(0, o_ref.shape[1], col_chunk):
        o_ref[:, c0:c0 + col_chunk] = _dot(hb, w_ref[:, c0:c0 + col_chunk])


def _norm_proj(x, moe, g, w, tm, col_chunk):
    route, ys, code0, code1, starts = moe
    m, d = x.shape
    n = w.shape[1]
    row = pl.BlockSpec((tm, d), lambda i, *_: (i, 0))
    return pl.pallas_call(
        functools.partial(_norm_proj_body, col_chunk=col_chunk),
        grid_spec=pltpu.PrefetchScalarGridSpec(
            num_scalar_prefetch=3, grid=(m // tm,),
            in_specs=[row, pl.BlockSpec((tm, ROUTE_LANES), lambda i, *_: (i, 0)), pl.BlockSpec(memory_space=pl.ANY),
                      pl.BlockSpec((1, d), lambda i, *_: (0, 0)), pl.BlockSpec((d, n), lambda i, *_: (0, 0))],
            out_specs=[row, pl.BlockSpec((tm, n), lambda i, *_: (i, 0))],
            scratch_shapes=_moe_scratch(tm, d)),
        out_shape=[jax.ShapeDtypeStruct((m, d), F32), jax.ShapeDtypeStruct((m, n), F32)],
        compiler_params=_cparams(1), name="norm_proj",
    )(code0, code1, starts, x, route, ys, g.reshape(1, d), w)


def _norm_proj_even_body(x_ref, g_ref, w_ref, o_ref, rm0_ref, rm1_ref, rm2_ref, lane_ref):
    tm = x_ref.shape[0]
    hb = _rms(x_ref[...], g_ref[...]).astype(BF16)
    rm_refs = (rm0_ref, rm1_ref, rm2_ref)
    for c in range(EVEN_IN // A_WIDTH):
        cols = slice(c * A_WIDTH, (c + 1) * A_WIDTH)
        res = _dot(hb, w_ref[:, cols])
        o_ref[:, cols] = res
        if c < 3 * N_DIL:
            g, j = divmod(c, 3)
            dil = DIL_PAIRS[g][1]
            if dil == 1:
                rm_refs[g][0, 0, :, j * A_WIDTH:(j + 1) * A_WIDTH] = res.astype(BF16)
            else:
                for t in range(A_WIDTH // LANES):
                    lane_ref[t] = res[:, t * LANES:(t + 1) * LANES]
                for r in range(dil):
                    part = [lane_ref[t, pl.ds(r, tm // dil, stride=dil), :] for t in range(A_WIDTH // LANES)]
                    rm_refs[g][0, r, :, j * A_WIDTH:(j + 1) * A_WIDTH] = jnp.concatenate(part, axis=1).astype(BF16)


def _norm_proj_even(x, g, w, tm, n_batch, seq):
    m, d = x.shape
    n = w.shape[1]
    tps = seq // tm
    npt = n_batch * tps

    def rm_index(i):
        return (jnp.where(i < npt, i // tps, n_batch), 0, jnp.where(i < npt, i % tps, i - npt), 0)

    rm_shapes = [jax.ShapeDtypeStruct((n_batch + 1, dil, seq // dil, 3 * A_WIDTH), BF16) for _, dil in DIL_PAIRS]
    rm_specs = [pl.BlockSpec((1, dil, tm // dil, 3 * A_WIDTH), rm_index) for _, dil in DIL_PAIRS]
    return pl.pallas_call(
        _norm_proj_even_body,
        grid=(m // tm,),
        in_specs=[pl.BlockSpec((tm, d), lambda i: (i, 0)), pl.BlockSpec((1, d), lambda i: (0, 0)),
                  pl.BlockSpec((d, n), lambda i: (0, 0))],
        out_specs=[pl.BlockSpec((tm, n), lambda i: (i, 0))] + rm_specs,
        out_shape=[jax.ShapeDtypeStruct((m, n), F32)] + rm_shapes,
        scratch_shapes=[pltpu.VMEM((A_WIDTH // LANES, tm, LANES), F32)],
        compiler_params=_cparams(1), name="norm_proj_even",
    )(x, g.reshape(1, d), w)


def _t5_bucket(dist):
    max_exact = NUM_BUCKETS // 2
    d = np.maximum(dist, 1).astype(np.float32)
    large = max_exact + (np.log(d / max_exact) / np.log(MAX_DISTANCE / max_exact)
                         * (NUM_BUCKETS - max_exact)).astype(np.int32)
    large = np.minimum(large, NUM_BUCKETS - 1)
    return np.where(dist < max_exact, dist, large).astype(np.int32)


def _bucket_lookup(tab, buckets):
    onehot = jnp.asarray(buckets[..., None, None] == np.arange(NUM_BUCKETS)[:, None])
    return jnp.sum(jnp.where(onehot, tab.astype(F32), 0.0), axis=-2)


def _prompt_bias(tab, dil):
    qi = np.arange(A_BLOCK)[:, None]
    km = np.arange(2 * A_BLOCK)[None, :]
    delta = A_BLOCK + qi - km
    valid = (delta >= 0) & (delta <= SPAN)
    bias = _bucket_lookup(tab, _t5_bucket(np.clip(delta, 0, SPAN) * dil))
    bias = jnp.where(valid[..., None], bias, NEG_INF)
    return jnp.transpose(bias, (2, 0, 1))


def _attn_prompt_body(q_ref, kp_ref, ko_ref, vp_ref, vo_ref, bias_ref, o_ref, lse_ref):
    first = pl.program_id(2) == 0
    scale = A_HEAD_DIM ** -0.5
    n_sub = q_ref.shape[2] // A_BLOCK
    k_all = jnp.concatenate([kp_ref[0, 0], ko_ref[0, 0]], axis=0)
    v_all = jnp.concatenate([vp_ref[0, 0], vo_ref[0, 0]], axis=0)
    km = lax.broadcasted_iota(jnp.int32, (1, 2 * A_BLOCK), 1)
    no_prev = jnp.logical_and(first, km < A_BLOCK)
    heads_per_tile = LANES // A_HEAD_DIM
    head_of_lane = lax.broadcasted_iota(jnp.int32, (1, LANES), 1) // A_HEAD_DIM
    for sub in range(n_sub):
        rows = slice(sub * A_BLOCK, (sub + 1) * A_BLOCK)
        keys = slice(sub * A_BLOCK, (sub + 2) * A_BLOCK)
        for t in range(A_WIDTH // LANES):
            sl = slice(t * LANES, (t + 1) * LANES)
            q_t, k_t, v_t = q_ref[0, 0, rows, sl], k_all[keys, sl], v_all[keys, sl]
            o_t = jnp.zeros((A_BLOCK, LANES), F32)
            lse_t = jnp.zeros((A_BLOCK, LANES), F32)
            for j in range(heads_per_tile):
                mine = head_of_lane == j
                s = _dot_nt(jnp.where(mine, q_t, jnp.zeros_like(q_t)), k_t) * scale + bias_ref[t * heads_per_tile + j]
                if sub == 0:
                    s = jnp.where(no_prev, NEG_INF, s)
                m = jnp.max(s, axis=-1, keepdims=True)
                p = jnp.exp(s - m)
                den = jnp.sum(p, axis=-1, keepdims=True)
                o_t = jnp.where(mine, _dot(p.astype(BF16), v_t) / den, o_t)
                lse_t = jnp.where(mine, m + jnp.log(den), lse_t)
            o_ref[0, 0, rows, sl] = o_t
            lse_ref[0, 0, rows, sl] = lse_t


def _attn_prompt(qkv_rm, tab, g, dil, n_batch, seq):
    sub_len = seq // dil
    n_sub = _largest_divisor(sub_len // A_BLOCK, (2, 1))
    rows = n_sub * A_BLOCK
    nb = sub_len // rows

    def spec(j, prev):
        if prev:
            return pl.BlockSpec((1, 1, A_BLOCK, A_WIDTH), lambda b, r, i: (b, r, jnp.maximum(i * n_sub - 1, 0), j))
        return pl.BlockSpec((1, 1, rows, A_WIDTH), lambda b, r, i: (b, r, i, j))

    out_spec = pl.BlockSpec((1, 1, rows, A_WIDTH), lambda b, r, i: (b, r, i, 0))
    out_sds = jax.ShapeDtypeStruct((n_batch, dil, sub_len, A_WIDTH), F32)
    return pl.pallas_call(
        _attn_prompt_body,
        grid=(n_batch, dil, nb),
        in_specs=[spec(0, False), spec(1, True), spec(1, False), spec(2, True), spec(2, False),
                  pl.BlockSpec((A_HEADS, A_BLOCK, 2 * A_BLOCK), lambda b, r, i: (0, 0, 0))],
        out_specs=[out_spec, out_spec], out_shape=[out_sds, out_sds],
        compiler_params=_cparams(3), name="attn_prompt_g%d" % g,
    )(qkv_rm, qkv_rm, qkv_rm, qkv_rm, qkv_rm, _prompt_bias(tab, dil))


def _sample_bias(tab, dil, cache_len, key_index, t_new):
    t = np.arange(t_new)[:, None]
    dist = cache_len + t - key_index[None, :]
    valid = (dist >= 0) & (dist % dil == 0) & (dist <= SPAN * dil)
    bias = _bucket_lookup(tab, _t5_bucket(np.clip(dist, 0, SPAN * dil)))
    bias = jnp.where(valid[..., None], bias, NEG_INF)
    return jnp.transpose(bias, (2, 0, 1)).reshape(A_HEADS * t_new, key_index.shape[0])


def _attn_sample_body(new_ref, c0_ref, c1_ref, c2_ref, b0_ref, b1_ref, b2_ref, bn_ref, *out_refs, t_new):
    scale = A_HEAD_DIM ** -0.5
    rows = A_HEADS * t_new
    head_of_row = lax.broadcasted_iota(jnp.int32, (rows, A_WIDTH), 0) // t_new
    head_of_lane = lax.broadcasted_iota(jnp.int32, (rows, A_WIDTH), 1) // A_HEAD_DIM
    own = head_of_row == head_of_lane
    caches = (c0_ref, c1_ref, c2_ref)
    biases = (b0_ref, b1_ref, b2_ref)
    for g in range(N_DIL):
        base = g * 3 * A_WIDTH
        cache_len = caches[g].shape[-1]
        q = new_ref[:, base:base + A_WIDTH]
        k_new = new_ref[:, base + A_WIDTH:base + 2 * A_WIDTH].astype(BF16)
        v_new = new_ref[:, base + 2 * A_WIDTH:base + 3 * A_WIDTH].astype(BF16)
        q_bd = jnp.where(own, jnp.concatenate([q] * A_HEADS, axis=0), 0.0).astype(BF16)
        k_t = caches[g][0, 0, 0].reshape(A_WIDTH, cache_len).astype(BF16)
        v_t = caches[g][0, 0, 1].reshape(A_WIDTH, cache_len).astype(BF16)
        s_c = _dot(q_bd, k_t) * scale + biases[g][...]
        s_n = _dot_nt(q_bd, k_new) * scale + bn_ref[g]
        m = jnp.maximum(s_c.max(axis=-1, keepdims=True), s_n.max(axis=-1, keepdims=True))
        p_c = jnp.exp(s_c - m)
        p_n = jnp.exp(s_n - m)
        den = jnp.sum(p_c, axis=-1, keepdims=True) + jnp.sum(p_n, axis=-1, keepdims=True)
        acc = _dot_nt(p_c.astype(BF16), v_t) + _dot(p_n.astype(BF16), v_new)
        acc = jnp.where(own, acc / den, 0.0)
        lse = jnp.where(own, m + jnp.log(den), 0.0)
        o = acc[0:t_new]
        l = lse[0:t_new]
        for h in range(1, A_HEADS):
            o = o + acc[h * t_new:(h + 1) * t_new]
            l = l + lse[h * t_new:(h + 1) * t_new]
        out_refs[2 * g][...] = o
        out_refs[2 * g + 1][...] = l


def _attn_sample(proj, caches, t5_bias, n_seq, t_new, row0):
    cache_lens = [c.shape[2] for c in caches]
    caches_t = [jnp.transpose(c, (0, 1, 3, 4, 5, 2)) for c in caches]
    biases = [_sample_bias(t5_bias[:, g], DIL_PAIRS[g][1], cache_lens[g], np.arange(cache_lens[g]), t_new)
              for g in range(N_DIL)]
    bias_new = jnp.stack([_sample_bias(t5_bias[:, g], DIL_PAIRS[g][1], cache_lens[g],
                                       cache_lens[g] + np.arange(t_new), t_new) for g in range(N_DIL)])
    blk0 = row0 // t_new
    full = lambda a: pl.BlockSpec(a.shape, lambda b: (0,) * a.ndim)
    out_spec = pl.BlockSpec((t_new, A_WIDTH), lambda b: (b, 0))
    cache_specs = [pl.BlockSpec((1, 1, 2, A_HEADS, A_HEAD_DIM, n), lambda b: (0, b, 0, 0, 0, 0)) for n in cache_lens]
    return pl.pallas_call(
        functools.partial(_attn_sample_body, t_new=t_new),
        grid=(n_seq,),
        in_specs=[pl.BlockSpec((t_new, N_DIL * 3 * A_WIDTH), lambda b: (blk0 + b, 0))] + cache_specs
                 + [full(biases[0]), full(biases[1]), full(biases[2]), full(bias_new)],
        out_specs=[out_spec] * (2 * N_DIL),
        out_shape=[jax.ShapeDtypeStruct((n_seq * t_new, A_WIDTH), F32)] * (2 * N_DIL),
        compiler_params=_cparams(1), name="attn_sample",
    )(proj, *caches_t, *biases, bias_new)


def _shift_rows(x, s, fill, axis):
    t = lax.broadcasted_iota(jnp.int32, x.shape, axis)
    return jnp.where(t >= s, pltpu.roll(x, s, axis), fill)


def _linear_scan(a, b, axis):
    n = a.shape[axis]
    s = 1
    while s < n:
        b = b + a * _shift_rows(b, s, 0.0, axis)
        a = a * _shift_rows(a, s, 1.0, axis)
        s *= 2
    return a, b


def _rglru_gates(xc, wa_ref, wx_ref, vec_ref):
    xcb = xc.astype(BF16)
    r = jax.nn.sigmoid(_dot(xcb, wa_ref[...]) + vec_ref[1:2])
    ig = jax.nn.sigmoid(_dot(xcb, wx_ref[...]) + vec_ref[2:3])
    log_a = -RG_C * r * _softplus(-vec_ref[3:4])
    a = jnp.exp(log_a)
    b = jnp.sqrt(1.0 - jnp.exp(2.0 * log_a)) * (ig * xc)
    return a, b


def _gelu(x):
    return 0.5 * x * (1.0 + jnp.tanh(math.sqrt(2.0 / math.pi) * (x + 0.044715 * (x * x * x))))


def _rglru_prompt_body(xb_ref, gb_ref, cw_ref, wa_ref, wx_ref, vec_ref, y_ref, hl_ref, tail_ref, h_ref):
    @pl.when(pl.program_id(1) == 0)
    def _():
        tail_ref[...] = jnp.zeros_like(tail_ref)
        h_ref[...] = jnp.zeros_like(h_ref)

    x = xb_ref[...]
    tt = x.shape[0]
    xe = jnp.concatenate([tail_ref[...], x], axis=0)
    xc = vec_ref[0:1] + x * cw_ref[CONV_W - 1:CONV_W]
    for j in range(1, CONV_W):
        xc = xc + pltpu.roll(xe, j, 0)[8:8 + tt] * cw_ref[CONV_W - 1 - j:CONV_W - j]
    tail_ref[...] = x[tt - 8:tt]
    a, b = _rglru_gates(xc, wa_ref, wx_ref, vec_ref)
    a_cum, h = _linear_scan(a, b, 0)
    h = h + a_cum * h_ref[...]
    h_ref[...] = h[tt - 1:tt]
    hl_ref[0] = h[tt - 1:tt]
    y_ref[...] = h * _gelu(gb_ref[...])


def _rglru_sample_body(xb_ref, gb_ref, c0_ref, h0_ref, cw_ref, wa_ref, wx_ref, vec_ref, prev_ref, y_ref, hl_ref, *, t_new):
    del prev_ref
    x = xb_ref[...]
    rows = x.shape[0]
    ns = rows // t_new
    x3 = x.reshape(ns, t_new, RNN_WIDTH)
    xe = jnp.concatenate([c0_ref[...], x3], axis=1)
    xc = vec_ref[0:1] + x3 * cw_ref[CONV_W - 1:CONV_W]
    for j in range(1, CONV_W):
        xc = xc + pltpu.roll(xe, j, 1)[:, 8:8 + t_new] * cw_ref[CONV_W - 1 - j:CONV_W - j]
    a, b = _rglru_gates(xc.reshape(rows, RNN_WIDTH), wa_ref, wx_ref, vec_ref)
    a_cum, h = _linear_scan(a.reshape(ns, t_new, RNN_WIDTH), b.reshape(ns, t_new, RNN_WIDTH), 1)
    h = h + a_cum * h0_ref[...][:, None, :]
    hl_ref[...] = h[:, t_new - 1, :]
    y_ref[...] = h.reshape(rows, RNN_WIDTH) * _gelu(gb_ref[...])


def _block_diag(w):
    nb, bi, bo = w.shape
    eye = jnp.eye(nb, dtype=w.dtype)
    return (w[:, :, None, :] * eye[:, None, :, None]).reshape(nb * bi, nb * bo)


def _rglru(proj, conv0, h0, conv_w, conv_b, wa, ba, wx, bx, lam, n_batch, seq, n_seq, t_new):
    m_total = proj.shape[0]
    mp = n_batch * seq
    wa_bd = _block_diag(wa).astype(BF16)
    wx_bd = _block_diag(wx).astype(BF16)
    vec = jnp.stack([conv_b, ba, bx, lam]).astype(F32)
    xcol = XB_COL // RNN_WIDTH
    gcol = GB_COL // RNN_WIDTH
    full2 = lambda a, nd: pl.BlockSpec(a.shape, lambda *_: (0,) * a.ndim)
    tt = _row_tile(seq)
    nt = seq // tt
    w_specs2 = [pl.BlockSpec(a.shape, lambda b, i: (0, 0)) for a in (conv_w, wa_bd, wx_bd, vec)]
    y, hl_p = pl.pallas_call(
        _rglru_prompt_body,
        grid=(n_batch, nt),
        in_specs=[pl.BlockSpec((tt, RNN_WIDTH), lambda b, i: (b * nt + i, xcol)),
                  pl.BlockSpec((tt, RNN_WIDTH), lambda b, i: (b * nt + i, gcol))] + w_specs2,
        out_specs=[pl.BlockSpec((tt, RNN_WIDTH), lambda b, i: (b * nt + i, 0)),
                   pl.BlockSpec((1, 1, RNN_WIDTH), lambda b, i: (b, 0, 0))],
        out_shape=[jax.ShapeDtypeStruct((m_total, RNN_WIDTH), F32),
                   jax.ShapeDtypeStruct((n_batch, 1, RNN_WIDTH), F32)],
        scratch_shapes=[pltpu.VMEM((8, RNN_WIDTH), F32), pltpu.VMEM((1, RNN_WIDTH), F32)],
        compiler_params=_cparams(2), name="rglru_prompt",
    )(proj, proj, conv_w, wa_bd, wx_bd, vec)
    del full2
    ts = 32 if n_seq % 32 == 0 else 8
    rows = ts * t_new
    blk0 = mp // rows
    c0p = jnp.pad(conv0, ((0, 0), (8 - (CONV_W - 1), 0), (0, 0)))
    w_specs1 = [pl.BlockSpec(a.shape, lambda i: (0, 0)) for a in (conv_w, wa_bd, wx_bd, vec)]
    y, hl_s = pl.pallas_call(
        functools.partial(_rglru_sample_body, t_new=t_new),
        grid=(n_seq // ts,),
        in_specs=[pl.BlockSpec((rows, RNN_WIDTH), lambda i: (blk0 + i, xcol)),
                  pl.BlockSpec((rows, RNN_WIDTH), lambda i: (blk0 + i, gcol)),
                  pl.BlockSpec((ts, 8, RNN_WIDTH), lambda i: (i, 0, 0)),
                  pl.BlockSpec((ts, RNN_WIDTH), lambda i: (i, 0))] + w_specs1
                 + [pl.BlockSpec(memory_space=pl.ANY)],
        out_specs=[pl.BlockSpec((rows, RNN_WIDTH), lambda i: (blk0 + i, 0)),
                   pl.BlockSpec((ts, RNN_WIDTH), lambda i: (i, 0))],
        out_shape=[jax.ShapeDtypeStruct((m_total, RNN_WIDTH), F32),
                   jax.ShapeDtypeStruct((n_seq, RNN_WIDTH), F32)],
        input_output_aliases={8: 0},
        compiler_params=_cparams(1), name="rglru_sample",
    )(proj, proj, c0p, h0, conv_w, wa_bd, wx_bd, vec, y)
    return y, hl_p.reshape(n_batch, RNN_WIDTH), hl_s


def _pack_bf16_pair(x):
    w = x.shape[1] // 2
    hi = lax.bitcast_convert_type(x[:, :w].astype(BF16).astype(F32), jnp.uint32)
    lo = lax.bitcast_convert_type(x[:, w:].astype(BF16).astype(F32), jnp.uint32)
    return hi | (lo >> 16)


def _unpack_bf16_pair(p):
    hi = lax.bitcast_convert_type(p & jnp.uint32(0xFFFF0000), F32).astype(BF16)
    lo = lax.bitcast_convert_type(p << 16, F32).astype(BF16)
    return hi, lo


def _route(x1, gf_ref, wr_ref, br_ref, hn_ref, route_ref, counts_ref, run_ref):
    @pl.when(pl.program_id(0) == 0)
    def _():
        run_ref[...] = jnp.zeros_like(run_ref)

    hn = _rms(x1, gf_ref[...])
    hn_ref[...] = _pack_bf16_pair(hn)
    hn_hi = hn.astype(BF16)
    hn_lo = (hn - hn_hi.astype(F32)).astype(BF16)
    logits = _dot(hn_hi, wr_ref[0]) + (_dot(hn_hi, wr_ref[1]) + _dot(hn_lo, wr_ref[0])) + br_ref[...]
    lane = lax.broadcasted_iota(jnp.int32, logits.shape, 1)
    is_coarse = lane < N_GROUPS
    coarse = jnp.where(is_coarse, logits, -jnp.inf)
    cmax = jnp.max(coarse, axis=-1, keepdims=True)
    grp = jnp.min(jnp.where(coarse == cmax, lane, ROUTE_LANES), axis=-1, keepdims=True)
    p_grp = 1.0 / jnp.sum(jnp.where(is_coarse, jnp.exp(logits - cmax), 0.0), axis=-1, keepdims=True)
    expert = lane - N_GROUPS
    in_grp = (lane >= N_GROUPS) & (expert < N_EXPERTS) & (expert // EXPERTS_PER_GROUP == grp)
    fine = jnp.where(in_grp, logits, -jnp.inf)
    v1 = jnp.max(fine, axis=-1, keepdims=True)
    i1 = jnp.min(jnp.where(fine == v1, lane, ROUTE_LANES), axis=-1, keepdims=True)
    fine2 = jnp.where(lane == i1, -jnp.inf, fine)
    v2 = jnp.max(fine2, axis=-1, keepdims=True)
    i2 = jnp.min(jnp.where(fine2 == v2, lane, ROUTE_LANES), axis=-1, keepdims=True)
    e2 = jnp.exp(v2 - v1)
    w1 = p_grp / (1.0 + e2)
    w2 = p_grp * e2 / (1.0 + e2)
    tm = logits.shape[0]
    sel = jnp.where(lane == i1, 1.0, jnp.where(lane == i2, 1.0, 0.0))
    ri = lax.broadcasted_iota(jnp.int32, (tm, tm), 0)
    ci = lax.broadcasted_iota(jnp.int32, (tm, tm), 1)
    earlier = jnp.where(ri > ci, 1.0, 0.0).astype(BF16)
    before = _dot(earlier, sel.astype(BF16)) + run_ref[...]
    r1 = jnp.sum(jnp.where(lane == i1, before, 0.0), axis=-1, keepdims=True)
    r2 = jnp.sum(jnp.where(lane == i2, before, 0.0), axis=-1, keepdims=True)
    run_ref[...] = run_ref[...] + jnp.sum(sel, axis=0, keepdims=True)
    counts_ref[...] = run_ref[...]
    route = jnp.where(lane == 0, (i1 - N_GROUPS).astype(F32), 0.0)
    route = jnp.where(lane == 1, (i2 - N_GROUPS).astype(F32), route)
    route = jnp.where(lane == 2, w1, route)
    route = jnp.where(lane == 3, w2, route)
    route = jnp.where(lane == 4, r1, route)
    route = jnp.where(lane == 5, r2, route)
    route = jnp.where(lane == 6, (i1 - N_GROUPS).astype(F32) * SLOT_CODE + r1, route)
    route = jnp.where(lane == 7, (i2 - N_GROUPS).astype(F32) * SLOT_CODE + r2, route)
    route_ref[...] = route


def _merge_groups(os, ls):
    mx = jnp.maximum(jnp.maximum(ls[0], ls[1]), ls[2])
    es = [jnp.exp(l - mx) for l in ls]
    return (es[0] * os[0] + es[1] * os[1] + es[2] * os[2]) / (es[0] + es[1] + es[2])


def _even_out_body(o0, l0, o1, l1, o2, l2, so0, sl0, so1, sl1, so2, sl2, yb_ref, x_ref, wo_ref, gf_ref, wr_ref, br_ref,
                   x1_ref, hn_ref, route_ref, counts_ref, n1o, n1l, n2o, n2l, oa_ref, run_ref, *, n_prompt_tiles):
    i = pl.program_id(0)
    tm = x_ref.shape[0]

    @pl.when(i < n_prompt_tiles)
    def _():
        nat = []
        for src, dst, dil in ((o1, n1o, DIL_PAIRS[1][1]), (l1, n1l, DIL_PAIRS[1][1]),
                              (o2, n2o, DIL_PAIRS[2][1]), (l2, n2l, DIL_PAIRS[2][1])):
            for r in range(dil):
                blk = src[0, r]
                for t in range(A_WIDTH // LANES):
                    dst[t, pl.ds(r, tm // dil, stride=dil), :] = blk[:, t * LANES:(t + 1) * LANES]
            nat.append(jnp.concatenate([dst[t] for t in range(A_WIDTH // LANES)], axis=1))
        oa_ref[...] = _merge_groups((o0[0, 0], nat[0], nat[2]), (l0[0, 0], nat[1], nat[3]))

    @pl.when(i >= n_prompt_tiles)
    def _():
        oa_ref[...] = _merge_groups((so0[...], so1[...], so2[...]), (sl0[...], sl1[...], sl2[...]))

    cat = jnp.concatenate([oa_ref[...], yb_ref[...]], axis=-1).astype(BF16)
    x1 = x_ref[...] + _dot(cat, wo_ref[...])
    x1_ref[...] = x1
    _route(x1, gf_ref, wr_ref, br_ref, hn_ref, route_ref, counts_ref, run_ref)


def _router_weights(rg_w, rg_b, re_w, re_b):
    d = rg_w.shape[0]
    wr = jnp.concatenate([rg_w, re_w.reshape(d, N_EXPERTS)], axis=1)
    br = jnp.concatenate([rg_b, re_b.reshape(N_EXPERTS)])
    pad = ROUTE_LANES - wr.shape[1]
    wr = jnp.pad(wr, ((0, 0), (0, pad))).astype(F32)
    wr_hi = wr.astype(BF16)
    wr_lo = (wr - wr_hi.astype(F32)).astype(BF16)
    return jnp.stack([wr_hi, wr_lo]), jnp.pad(br, (0, pad)).reshape(1, ROUTE_LANES).astype(F32)


def _mix_out_call(body, in_specs, args, consts, m, d, tm, scratch, name):
    const_spec = lambda a: pl.BlockSpec(a.shape, lambda i: (0,) * a.ndim)
    return pl.pallas_call(
        body, grid=(m // tm,), in_specs=list(in_specs) + [const_spec(c) for c in consts],
        out_specs=[pl.BlockSpec((tm, d), lambda i: (i, 0)), pl.BlockSpec((tm, d // 2), lambda i: (i, 0)),
                   pl.BlockSpec((tm, ROUTE_LANES), lambda i: (i, 0)), pl.BlockSpec((1, ROUTE_LANES), lambda i: (0, 0))],
        out_shape=[jax.ShapeDtypeStruct((m, d), F32), jax.ShapeDtypeStruct((m, d // 2), jnp.uint32),
                   jax.ShapeDtypeStruct((m, ROUTE_LANES), F32), jax.ShapeDtypeStruct((1, ROUTE_LANES), F32)],
        scratch_shapes=list(scratch) + [pltpu.VMEM((1, ROUTE_LANES), F32)],
        compiler_params=_cparams(1), name=name,
    )(*args, *consts)


def _even_out(attn_p, attn_s, y_b, x, consts, tm, n_batch, seq):
    m, d = x.shape
    tps = seq // tm
    npt = n_batch * tps
    in_specs = []
    for g, (_, dil) in enumerate(DIL_PAIRS):
        def index(i):
            return (jnp.minimum(i // tps, n_batch - 1), 0, jnp.where(i < npt, i % tps, 0), 0)
        in_specs += [pl.BlockSpec((1, dil, tm // dil, A_WIDTH), index)] * 2
    in_specs += [pl.BlockSpec((tm, A_WIDTH), lambda i: (jnp.maximum(i - npt, 0), 0))] * (2 * N_DIL)
    in_specs += [pl.BlockSpec((tm, RNN_WIDTH), lambda i: (i, 0)), pl.BlockSpec((tm, d), lambda i: (i, 0))]
    scratch = [pltpu.VMEM((A_WIDTH // LANES, tm, LANES), F32)] * 4 + [pltpu.VMEM((tm, A_WIDTH), F32)]
    return _mix_out_call(functools.partial(_even_out_body, n_prompt_tiles=npt), in_specs,
                         list(attn_p) + list(attn_s) + [y_b, x], consts, m, d, tm, scratch, "even_out")


def _odd_out(o_c, proj2, x, consts, tm):
    m, d = x.shape
    in_specs = [pl.BlockSpec((tm, C_V_W), lambda i: (i, 0)),
                pl.BlockSpec((tm, C_V_W), lambda i: (i, Z_COL // C_V_W)),
                pl.BlockSpec((tm, d), lambda i: (i, 0))]
    return _mix_out_call(_odd_out_body, in_specs, [o_c, proj2, x], consts, m, d, tm, [], "odd_out")


def _moe_dispatch_body(c0_ref, c1_ref, start_ref, fill_ref, used_ref, hn_ref, xs_ref, zero_ref, fill_sem, row_sem):
    i = pl.program_id(0)
    tm = hn_ref.shape[0]

    def fill_copy(e):
        return pltpu.make_async_copy(zero_ref, xs_ref.at[pl.ds(pl.multiple_of(fill_ref[e], MOE_TILE), MOE_TILE)], fill_sem)

    @pl.when(i == 0)
    def _():
        zero_ref[...] = jnp.zeros_like(zero_ref)
        for e in range(N_EXPERTS):
            @pl.when(used_ref[e] > 0)
            def _():
                fill_copy(e).start()
        for e in range(N_EXPERTS):
            @pl.when(used_ref[e] > 0)
            def _():
                fill_copy(e).wait()

    base = i * tm

    def scatter_row(j, carry):
        _row_copy(hn_ref, j, xs_ref, _slot(c0_ref[base + j], start_ref), row_sem).start()
        _row_copy(hn_ref, j, xs_ref, _slot(c1_ref[base + j], start_ref), row_sem).start(priority=1)
        return carry

    lax.fori_loop(0, tm, scatter_row, 0, unroll=8)
    for _ in range(2):
        pltpu.make_async_copy(hn_ref, xs_ref.at[pl.ds(0, tm)], row_sem).wait()


def _moe_ffn_body(te_ref, nu_ref, xs_ref, wg_ref, wu_ref, wd_ref, o_ref):
    del te_ref

    @pl.when(pl.program_id(0) < nu_ref[0])
    def _():
        xa, xb = _unpack_bf16_pair(xs_ref[...])
        half = xa.shape[1]
        hg = _dot(xa, wg_ref[0, :half]) + _dot(xb, wg_ref[0, half:])
        hu = _dot(xa, wu_ref[0, :half]) + _dot(xb, wu_ref[0, half:])
        o_ref[...] = _dot((_silu(hg) * hu).astype(BF16), wd_ref[0])

    @pl.when(pl.program_id(0) >= nu_ref[0])
    def _():
        o_ref[...] = jnp.zeros_like(o_ref)


def _moe(hn, route, counts, w_gate, w_up, w_down, tm):
    m = hn.shape[0]
    d = w_gate.shape[1]
    tile = MOE_TILE
    n_slots = 2 * m + N_EXPERTS * tile
    n_tiles = n_slots // tile
    assert 2 * m < SLOT_CODE and N_EXPERTS * SLOT_CODE <= 1 << 24, "slot codes must stay exact in f32"
    cnt = counts[0, N_GROUPS:N_GROUPS + N_EXPERTS].astype(jnp.int32)
    padded = ((cnt + tile - 1) // tile) * tile
    pad_end = jnp.cumsum(padded)
    pad_start = (pad_end - padded).astype(jnp.int32)
    code0 = route[:, 6].astype(jnp.int32)
    code1 = route[:, 7].astype(jnp.int32)
    tile_start = jnp.arange(n_tiles, dtype=jnp.int32) * tile
    tile_expert = jnp.sum((tile_start[:, None] >= pad_end[None, :]).astype(jnp.int32), axis=1)
    tile_expert = jnp.minimum(tile_expert, N_EXPERTS - 1).astype(jnp.int32)
    n_used = (pad_end[-1:] // tile).astype(jnp.int32)
    fill_start = jnp.maximum(pad_end - tile, 0).astype(jnp.int32)

    xs = pl.pallas_call(
        _moe_dispatch_body,
        grid_spec=pltpu.PrefetchScalarGridSpec(
            num_scalar_prefetch=5, grid=(m // tm,),
            in_specs=[pl.BlockSpec((tm, d // 2), lambda i, *_: (i, 0))],
            out_specs=pl.BlockSpec(memory_space=pl.ANY),
            scratch_shapes=[pltpu.VMEM((tile, d // 2), jnp.uint32), pltpu.SemaphoreType.DMA(()),
                            pltpu.SemaphoreType.DMA(())]),
        out_shape=jax.ShapeDtypeStruct((n_slots, d // 2), jnp.uint32),
        compiler_params=_cparams(1), name="moe_dispatch",
    )(code0, code1, pad_start, fill_start, cnt, hn)

    def used(i, nu):
        return jnp.minimum(i, jnp.maximum(nu[0] - 1, 0))

    ys = pl.pallas_call(
        _moe_ffn_body,
        grid_spec=pltpu.PrefetchScalarGridSpec(
            num_scalar_prefetch=2, grid=(n_tiles,),
            in_specs=[pl.BlockSpec((tile, d // 2), lambda i, te, nu: (used(i, nu), 0)),
                      pl.BlockSpec((1, d, EXPERT_FF), lambda i, te, nu: (te[used(i, nu)], 0, 0)),
                      pl.BlockSpec((1, d, EXPERT_FF), lambda i, te, nu: (te[used(i, nu)], 0, 0)),
                      pl.BlockSpec((1, EXPERT_FF, d), lambda i, te, nu: (te[used(i, nu)], 0, 0))],
            out_specs=pl.BlockSpec((tile, d), lambda i, te, nu: (i, 0))),
        out_shape=jax.ShapeDtypeStruct((n_slots, d), F32),
        compiler_params=_cparams(1), name="moe_ffn",
    )(tile_expert, n_used, xs, w_gate, w_up, w_down)

    return route, ys, code0, code1, pad_start


def _gdn_prep_math(xe, tt, cw_ref, bg, av_ref):
    acc = xe[8:8 + tt] * cw_ref[CONV_W - 1:CONV_W]
    for j in range(1, CONV_W):
        acc = acc + pltpu.roll(xe, j, 0)[8:8 + tt] * cw_ref[CONV_W - 1 - j:CONV_W - j]
    qkv = _silu(acc)
    outs = []
    for h in range(2 * C_QK_HEADS):
        xh = qkv[:, h * C_DK:(h + 1) * C_DK]
        xh = xh * lax.rsqrt(jnp.sum(xh * xh, axis=-1, keepdims=True) + EPS)
        if h < C_QK_HEADS:
            xh = xh * (C_DK ** -0.5)
        outs.append(xh)
    qk = jnp.concatenate(outs, axis=-1)
    v = qkv[:, 2 * C_QK_W:]
    lane = lax.broadcasted_iota(jnp.int32, bg.shape, 1)
    gdec = -jnp.exp(av_ref[0:1]) * _softplus(bg + av_ref[1:2])
    bgo = jnp.where(lane < C_V_HEADS, jax.nn.sigmoid(bg), gdec)
    return qk, v, bgo


def _unit_lower_inverses(l_bds, size):
    n = l_bds[0].shape[0]
    ri = lax.broadcasted_iota(jnp.int32, (n, n), 0)
    ci = lax.broadcasted_iota(jnp.int32, (n, n), 1)
    eye = jnp.where(ri == ci, 1.0, 0.0).astype(F32)
    base = ri // GDN_BASE == ci // GDN_BASE
    ps = [jnp.where(base, -l, 0.0) for l in l_bds]
    ts = [eye + p for p in ps]
    s = 2
    while s < GDN_BASE:
        ps = [_dot(p.astype(BF16), p.astype(BF16)) for p in ps]
        ts = [t + _dot(t.astype(BF16), p.astype(BF16)) for t, p in zip(ts, ps)]
        s *= 2
    s = GDN_BASE
    while s < size:
        lower_left = ((ri // s) % 2 == 1) & (ci // s == ri // s - 1)
        tbs = [t.astype(BF16) for t in ts]
        mids = [_dot(jnp.where(lower_left, l, 0.0).astype(BF16), tb).astype(BF16) for l, tb in zip(l_bds, tbs)]
        ts = [t - _dot(tb, mid) for t, tb, mid in zip(ts, tbs, mids)]
        s *= 2
    return ts


def _gdn_body(*refs, chunk, pack, n_par, has_state):
    c = chunk
    x_refs = refs[0:2 * n_par:2]
    bg_refs = refs[1:2 * n_par:2]
    pos = 2 * n_par
    if has_state:
        c0_ref = refs[pos]
        pos += 1
    cw_ref, av_ref, nw_ref = refs[pos:pos + 3]
    pos += 3
    if has_state:
        s0_ref = refs[pos]
        pos += 2
    o_ref, sout_ref, s_ref = refs[pos:pos + 3]
    tail_ref = None if has_state else refs[pos + 3]

    @pl.when(pl.program_id(1) == 0)
    def _():
        if has_state:
            s_ref[...] = s0_ref[...]
        else:
            s_ref[...] = jnp.zeros_like(s_ref)
            tail_ref[...] = jnp.zeros_like(tail_ref)

    rep = C_V_HEADS // C_QK_HEADS
    ti = lax.broadcasted_iota(jnp.int32, (c, c), 0)
    tj = lax.broadcasted_iota(jnp.int32, (c, c), 1)
    tril = jnp.where(ti >= tj, 1.0, 0.0).astype(F32)
    n = pack * c
    ri = lax.broadcasted_iota(jnp.int32, (n, n), 0)
    ci = lax.broadcasted_iota(jnp.int32, (n, n), 1)
    same = ri // c == ci // c
    incl = same & (ri >= ci)
    strict = same & (ri > ci)

    def col(a, heads):
        return jnp.concatenate([a[:, h:h + 1] for h in heads], axis=0)

    items, g_ends = [], []
    l_bds, qk_bds, rhss, qgs, kends = [], [], [], [], []
    for k in range(n_par):
        x = x_refs[k][...]
        prev = c0_ref[k] if has_state else tail_ref[k]
        if not has_state:
            tail_ref[k] = x[c - 8:c]
        qk, v, bgo = _gdn_prep_math(jnp.concatenate([prev, x], axis=0), c, cw_ref, bg_refs[k][...], av_ref)
        beta = bgo[:, 0:C_V_HEADS]
        gc = _dot_f32(tril, bgo[:, C_V_HEADS:2 * C_V_HEADS])
        g_last = gc[c - 1:c]
        gam = jnp.exp(gc)
        kdec = jnp.exp(g_last - gc)
        g_ends.append(jnp.exp(g_last))
        for p0 in range(0, C_V_HEADS, pack):
            heads = range(p0, p0 + pack)
            items.append((k, heads))
            g_col = col(gc, heads)
            g_row = jnp.sum(jnp.where(ri == ci, g_col, 0.0), axis=0, keepdims=True)
            b_col = col(beta, heads)
            gam_col = col(gam, heads)
            k_st = jnp.concatenate([qk[:, C_QK_W + (h // rep) * C_DK:C_QK_W + (h // rep + 1) * C_DK] for h in heads], axis=0)
            q_st = jnp.concatenate([qk[:, (h // rep) * C_DK:(h // rep + 1) * C_DK] for h in heads], axis=0)
            v_st = jnp.concatenate([v[:, h * C_DV:(h + 1) * C_DV] for h in heads], axis=0)
            kb = k_st.astype(BF16)
            decay = jnp.exp(jnp.where(incl, g_col - g_row, -jnp.inf))
            l_bds.append(jnp.where(strict, b_col * _dot_nt(kb, kb) * decay, 0.0))
            qk_bds.append((_dot_nt(q_st.astype(BF16), kb) * decay).astype(BF16))
            rhss.append(jnp.concatenate([b_col * v_st, (b_col * gam_col) * k_st], axis=1).astype(BF16))
            qgs.append(q_st * gam_col)
            kends.append((k_st * col(kdec, heads)).astype(BF16))
    t_invs = _unit_lower_inverses(l_bds, c)
    uws = [_dot(t.astype(BF16), rhs) for t, rhs in zip(t_invs, rhss)]
    u_sts, q_sts = [], []
    for (k, heads), uw, qg in zip(items, uws, qgs):
        us, qs = [], []
        for i, h in enumerate(heads):
            rows = slice(i * c, (i + 1) * c)
            lhs = jnp.concatenate([uw[rows, C_DV:], qg[rows]], axis=0).astype(BF16)
            ws = _dot(lhs, s_ref[k, h].astype(BF16))
            us.append(uw[rows, 0:C_DV] - ws[0:c])
            qs.append(ws[c:])
        u_sts.append(jnp.concatenate(us, axis=0).astype(BF16))
        q_sts.append(jnp.concatenate(qs, axis=0))
    o_sts = [q + _dot(qk_bd, ub) for q, qk_bd, ub in zip(q_sts, qk_bds, u_sts)]
    for (k, heads), o_st, ub, kendb in zip(items, o_sts, u_sts, kends):
        for i, h in enumerate(heads):
            rows = slice(i * c, (i + 1) * c)
            s_ref[k, h] = s_ref[k, h] * g_ends[k][:, h:h + 1] + _dot_tn(kendb[rows], ub[rows])
            o_h = _rms(o_st[rows], nw_ref[...])
            if has_state:
                o_ref[k * c:(k + 1) * c, h * C_DV:(h + 1) * C_DV] = o_h
            else:
                o_ref[k, 0, :, h * C_DV:(h + 1) * C_DV] = o_h

    @pl.when(pl.program_id(1) == pl.num_programs(1) - 1)
    def _():
        sout_ref[...] = s_ref[...]


def _largest_divisor(n, candidates):
    return next(c for c in candidates if n % c == 0)


def _gdn(proj, conv0, conv_w, a_log, dt_bias, onorm_w, s0, n_batch, seq, n_seq, t_new, chunk):
    mp = n_batch * seq
    nc = seq // chunk
    av = jnp.zeros((2, 128), F32)
    av = av.at[0, C_V_HEADS:2 * C_V_HEADS].set(a_log).at[1, C_V_HEADS:2 * C_V_HEADS].set(dt_bias)
    nw = onorm_w.reshape(1, C_DV).astype(F32)
    bg_col = BG_COL // 128
    state = (C_V_HEADS, C_DK, C_DV)
    consts = [conv_w, av, nw]
    const_specs = [pl.BlockSpec(a.shape, lambda b, i: (0, 0)) for a in consts]

    par_p = _largest_divisor(n_batch, (2, 1))
    seq_specs = []
    for k in range(par_p):
        seq_specs += [pl.BlockSpec((chunk, C_CONV_DIM), lambda b, i, k=k: ((b * par_p + k) * nc + i, 0)),
                      pl.BlockSpec((chunk, 128), lambda b, i, k=k: ((b * par_p + k) * nc + i, bg_col))]
    extra = -(-(n_seq * t_new) // seq)
    o4, s_p = pl.pallas_call(
        functools.partial(_gdn_body, chunk=chunk, pack=256 // chunk, n_par=par_p, has_state=False),
        grid=(n_batch // par_p, nc),
        in_specs=seq_specs + const_specs,
        out_specs=[pl.BlockSpec((par_p, 1, chunk, C_V_W), lambda b, i: (b, i, 0, 0)),
                   pl.BlockSpec((par_p,) + state, lambda b, i: (b, 0, 0, 0))],
        out_shape=[jax.ShapeDtypeStruct((n_batch + extra, nc, chunk, C_V_W), F32),
                   jax.ShapeDtypeStruct((n_batch,) + state, F32)],
        scratch_shapes=[pltpu.VMEM((par_p,) + state, F32), pltpu.VMEM((par_p, 8, C_CONV_DIM), F32)],
        compiler_params=_cparams(2), name="gdn_prompt",
    )(*([proj, proj] * par_p), *consts)
    o = o4.reshape((n_batch + extra) * seq, C_V_W)

    par_s = _largest_divisor(n_seq, (4, 2, 1))
    blk0 = mp // t_new
    c0p = jnp.pad(conv0, ((0, 0), (8 - (CONV_W - 1), 0), (0, 0)))
    seq_specs = []
    for k in range(par_s):
        seq_specs += [pl.BlockSpec((t_new, C_CONV_DIM), lambda b, i, k=k: (blk0 + b * par_s + k, 0)),
                      pl.BlockSpec((t_new, 128), lambda b, i, k=k: (blk0 + b * par_s + k, bg_col))]
    n_in = 2 * par_s + 1 + len(consts) + 1
    o, s_s = pl.pallas_call(
        functools.partial(_gdn_body, chunk=t_new, pack=C_V_HEADS, n_par=par_s, has_state=True),
        grid=(n_seq // par_s, 1),
        in_specs=seq_specs + [pl.BlockSpec((par_s, 8, C_CONV_DIM), lambda b, i: (b, 0, 0))] + const_specs
                 + [pl.BlockSpec((par_s,) + state, lambda b, i: (b, 0, 0, 0)), pl.BlockSpec(memory_space=pl.ANY)],
        out_specs=[pl.BlockSpec((par_s * t_new, C_V_W), lambda b, i: (blk0 // par_s + b, 0)),
                   pl.BlockSpec((par_s,) + state, lambda b, i: (b, 0, 0, 0))],
        out_shape=[jax.ShapeDtypeStruct(o.shape, F32), jax.ShapeDtypeStruct((n_seq,) + state, F32)],
        scratch_shapes=[pltpu.VMEM((par_s,) + state, F32)],
        input_output_aliases={n_in: 0},
        compiler_params=_cparams(2), name="gdn_sample",
    )(*([proj, proj] * par_s), c0p, *consts, s0, o)
    return o, s_p, s_s


def _odd_out_body(o_ref, z_ref, x_ref, wo_ref, gf_ref, wr_ref, br_ref, x1_ref, hn_ref, route_ref, counts_ref, run_ref):
    y = (o_ref[...] * _silu(z_ref[...])).astype(BF16)
    x1 = x_ref[...] + _dot(y, wo_ref[...])
    x1_ref[...] = x1
    _route(x1, gf_ref, wr_ref, br_ref, hn_ref, route_ref, counts_ref, run_ref)


def _final_body(c0_ref, c1_ref, start_ref, x_ref, route_ref, ys_ref, g_ref, y_ref, buf_ref, sems, *, tile0):
    x = _moe_residual(x_ref, route_ref, ys_ref, c0_ref, c1_ref, start_ref, buf_ref, sems, tile0)
    y_ref[...] = _rms(x, g_ref[...])


def _final_norm(x, moe, g, tm, row0, n_rows):
    route, ys, code0, code1, starts = moe
    d = x.shape[1]
    blk0 = row0 // tm
    return pl.pallas_call(
        functools.partial(_final_body, tile0=blk0),
        grid_spec=pltpu.PrefetchScalarGridSpec(
            num_scalar_prefetch=3, grid=(n_rows // tm,),
            in_specs=[pl.BlockSpec((tm, d), lambda i, *_: (blk0 + i, 0)),
                      pl.BlockSpec((tm, ROUTE_LANES), lambda i, *_: (blk0 + i, 0)),
                      pl.BlockSpec(memory_space=pl.ANY), pl.BlockSpec((1, d), lambda i, *_: (0, 0))],
            out_specs=pl.BlockSpec((tm, d), lambda i, *_: (i, 0)),
            scratch_shapes=_moe_scratch(tm, d)),
        out_shape=jax.ShapeDtypeStruct((n_rows, d), F32),
        compiler_params=_cparams(1), name="final_norm",
    )(code0, code1, starts, x, route, ys, g.reshape(1, d))


def kernel(x_prompt, x_sample, cache_a_g0_kv, cache_a_g1_kv, cache_a_g2_kv, state_b_h, state_b_conv, state_c_S, state_c_conv, t5_bias, norm_mix, norm_ffn, norm_final, e_w_in, e_conv_w, e_conv_b, e_rg_wa, e_rg_ba, e_rg_wx, e_rg_bx, e_rg_lambda, e_w_out, o_w_in, o_conv_w, o_a_log, o_dt_bias, o_onorm_w, o_w_out, moe_rg_w, moe_rg_b, moe_re_w, moe_re_b, moe_w_gate, moe_w_up, moe_w_down):
    n_batch, seq, d = x_prompt.shape
    n_seq, t_new, _ = x_sample.shape
    mp = n_batch * seq
    ms = n_seq * t_new
    m = mp + ms
    assert t_new == 8 and seq % (DIL_PAIRS[2][1] * A_BLOCK) == 0
    assert e_w_in.shape[0] == 1 and o_w_in.shape[0] == 1
    tm = _row_tile(mp, ms)
    x = jnp.concatenate([x_prompt.reshape(mp, d), x_sample.reshape(ms, d)], axis=0)

    def moe_weights(layer):
        shp = (N_EXPERTS, d, EXPERT_FF)
        return (moe_w_gate[layer].reshape(shp).astype(BF16), moe_w_up[layer].reshape(shp).astype(BF16),
                moe_w_down[layer].reshape(N_EXPERTS, EXPERT_FF, d).astype(BF16))

    def prompt_tail(a, keep, c0, c1):
        return jnp.stack([lax.slice(a, ((b + 1) * seq - keep, c0), ((b + 1) * seq, c1)) for b in range(n_batch)])

    def sample_rows(a, keep, c0, c1):
        return lax.slice(a, (mp, c0), (m, c1)).reshape(n_seq, t_new, c1 - c0)[:, t_new - keep:]

    proj, *qkv_rm = _norm_proj_even(x, norm_mix[0], e_w_in[0].astype(BF16), tm, n_batch, seq)
    attn_p = []
    for g, (_, dil) in enumerate(DIL_PAIRS):
        attn_p.extend(_attn_prompt(qkv_rm[g], t5_bias[:, g], g, dil, n_batch, seq))
    new_a = []
    for g, (win, _) in enumerate(DIL_PAIRS):
        c0 = g * 3 * A_WIDTH + A_WIDTH
        keep = min(win, seq)
        new_a.append(prompt_tail(proj, keep, c0, c0 + 2 * A_WIDTH).reshape(1, n_batch, keep, 2, A_HEADS, A_HEAD_DIM))
        new_a.append(sample_rows(proj, t_new, c0, c0 + 2 * A_WIDTH).reshape(1, n_seq, t_new, 2, A_HEADS, A_HEAD_DIM))
    attn_s = _attn_sample(proj, (cache_a_g0_kv, cache_a_g1_kv, cache_a_g2_kv), t5_bias, n_seq, t_new, mp)
    y_b, bh_p, bh_s = _rglru(proj, state_b_conv[0], state_b_h[0], e_conv_w[0], e_conv_b[0], e_rg_wa[0], e_rg_ba[0],
                             e_rg_wx[0], e_rg_bx[0], e_rg_lambda[0], n_batch, seq, n_seq, t_new)
    wr, br = _router_weights(moe_rg_w[0], moe_rg_b[0], moe_re_w[0], moe_re_b[0])
    x1, hn, route, counts = _even_out(attn_p, attn_s, y_b, x, [e_w_out[0].astype(BF16), norm_ffn[0].reshape(1, d), wr, br],
                                      tm, n_batch, seq)
    y_moe = _moe(hn, route, counts, *moe_weights(0), tm)

    w_in1 = jnp.pad(o_w_in[0], ((0, 0), (0, ODD_IN_PAD - ODD_IN))).astype(BF16)
    x2, proj2 = _norm_proj(x1, y_moe, norm_mix[1], w_in1, tm, 896)
    o_c, cs_p, cs_s = _gdn(proj2, state_c_conv[0], o_conv_w[0], o_a_log[0], o_dt_bias[0], o_onorm_w[0], state_c_S[0],
                           n_batch, seq, n_seq, t_new, 64)
    wr, br = _router_weights(moe_rg_w[1], moe_rg_b[1], moe_re_w[1], moe_re_b[1])
    x3, hn, route, counts = _odd_out(o_c, proj2, x2, [o_w_out[0].astype(BF16), norm_ffn[1].reshape(1, d), wr, br], tm)
    y_moe = _moe(hn, route, counts, *moe_weights(1), tm)
    y_p = _final_norm(x3, y_moe, norm_final, tm, 0, mp).reshape(n_batch, seq, d)
    y_s = _final_norm(x3, y_moe, norm_final, tm, mp, ms).reshape(n_seq, t_new, d)

    keep = CONV_W - 1
    bconv_p = prompt_tail(proj, keep, XB_COL, XB_COL + RNN_WIDTH)[None]
    bconv_s = sample_rows(proj, keep, XB_COL, XB_COL + RNN_WIDTH)[None]
    cconv_p = prompt_tail(proj2, keep, 0, C_CONV_DIM)[None]
    cconv_s = sample_rows(proj2, keep, 0, C_CONV_DIM)[None]
    return (y_p, y_s, *new_a, bh_p[None], bh_s[None], bconv_p, bconv_s, cs_p[None], cs_s[None], cconv_p, cconv_s)
```

```python
import functools
import math

import jax
import jax.numpy as jnp
import numpy as np
from jax import lax
from jax.experimental import pallas as pl
from jax.experimental.pallas import tpu as pltpu

F32 = jnp.float32
BF16 = jnp.bfloat16
EPS = 1e-6
NEG_INF = -1e30

D_MODEL = 1024
DIL_PAIRS = ((128, 1), (512, 4), (2048, 16))
N_DIL = 3
A_HEADS = 8
A_HEAD_DIM = 64
A_WIDTH = A_HEADS * A_HEAD_DIM
A_BLOCK = 128
SPAN = 128
NUM_BUCKETS = 32
MAX_DISTANCE = 2048
RNN_WIDTH = 512
RNN_BLOCKS = 8
CONV_W = 4
RG_C = 8.0
EVEN_IN = N_DIL * 3 * A_WIDTH + 2 * RNN_WIDTH
XB_COL = N_DIL * 3 * A_WIDTH
GB_COL = XB_COL + RNN_WIDTH
C_QK_HEADS = 8
C_V_HEADS = 16
C_DK = 128
C_DV = 128
C_QK_W = C_QK_HEADS * C_DK
C_V_W = C_V_HEADS * C_DV
C_CONV_DIM = 2 * C_QK_W + C_V_W
ODD_IN = C_CONV_DIM + C_V_W + 2 * C_V_HEADS
ODD_IN_PAD = 6272
Z_COL = C_CONV_DIM
BG_COL = C_CONV_DIM + C_V_W
GDN_BASE = 8
N_GROUPS = 4
EXPERTS_PER_GROUP = 8
N_EXPERTS = N_GROUPS * EXPERTS_PER_GROUP
EXPERT_FF = 256
ROUTE_LANES = 128
MOE_TILE = 256
SLOT_CODE_BITS = 18
SLOT_CODE = 1 << SLOT_CODE_BITS

LANES = 128
VMEM_LIMIT = 56 * 1024 * 1024


def _cparams(n_grid):
    return pltpu.CompilerParams(dimension_semantics=("arbitrary",) * n_grid,
                                vmem_limit_bytes=VMEM_LIMIT)


def _rms(x, g):
    return x * lax.rsqrt(jnp.mean(x * x, axis=-1, keepdims=True) + EPS) * g


def _silu(x):
    return x * jax.nn.sigmoid(x)


def _softplus(x):
    return jnp.maximum(x, 0.0) + jnp.log1p(jnp.exp(-jnp.abs(x)))


def _dot(a, b):
    return jnp.dot(a, b, preferred_element_type=F32)


def _dot_nt(a, b):
    return lax.dot_general(a, b, (((1,), (1,)), ((), ())), preferred_element_type=F32)


def _dot_tn(a, b):
    return lax.dot_general(a, b, (((0,), (0,)), ((), ())), preferred_element_type=F32)


def _dot_f32(a, b):
    return jnp.dot(a, b, preferred_element_type=F32, precision=lax.Precision.HIGHEST)


def _row_tile(*counts):
    for t in (256, 128, 64, 32, 16, 8):
        if all(c % t == 0 for c in counts):
            return t
    raise ValueError("token counts must be multiples of 8")


def _row_copy(src_ref, src_row, dst_ref, dst_row, sem):
    return pltpu.make_async_copy(src_ref.at[pl.ds(src_row, 1)], dst_ref.at[pl.ds(dst_row, 1)], sem)


def _slot(code, start_ref):
    return start_ref[lax.shift_right_logical(code, SLOT_CODE_BITS)] + (code & (SLOT_CODE - 1))


def _moe_residual(x_ref, route_ref, ys_ref, c0_ref, c1_ref, start_ref, buf_ref, sems, tile0):
    i = pl.program_id(0)
    tm = x_ref.shape[0]

    def start(step, slot):
        base = (tile0 + step) * tm

        def gather_row(j, carry):
            _row_copy(ys_ref, _slot(c0_ref[base + j], start_ref), buf_ref.at[slot, 0], j, sems.at[slot]).start(priority=1)
            _row_copy(ys_ref, _slot(c1_ref[base + j], start_ref), buf_ref.at[slot, 1], j, sems.at[slot]).start(priority=1)
            return carry

        lax.fori_loop(0, tm, gather_row, 0, unroll=8)

    @pl.when(i == 0)
    def _():
        start(0, 0)

    @pl.when(i + 1 < pl.num_programs(0))
    def _():
        start(i + 1, (i + 1) % 2)

    slot = i % 2
    for k in range(2):
        pltpu.make_async_copy(ys_ref.at[pl.ds(0, tm)], buf_ref.at[slot, k], sems.at[slot]).wait()
    return x_ref[...] + route_ref[:, 2:3] * buf_ref[slot, 0] + route_ref[:, 3:4] * buf_ref[slot, 1]


def _moe_scratch(tm, d):
    return [pltpu.VMEM((2, 2, tm, d), F32), pltpu.SemaphoreType.DMA((2,))]


def _norm_proj_body(c0_ref, c1_ref, start_ref, x_ref, route_ref, ys_ref, g_ref, w_ref, x_out_ref, o_ref, buf_ref, sems, *, col_chunk):
    x = _moe_residual(x_ref, route_ref, ys_ref, c0_ref, c1_ref, start_ref, buf_ref, sems, 0)
    x_out_ref[...] = x
    hb = _rms(x, g_ref[...]).astype(BF16)
    for c0 in range(0, o_ref.shape[1], col_chunk):
        o_ref[:, c0:c0 + col_chunk] = _dot(hb, w_ref[:, c0:c0 + col_chunk])


def _norm_proj(x, moe, g, w, tm, col_chunk):
    route, ys, code0, code1, starts = moe
    m, d = x.shape
    n = w.shape[1]
    row = pl.BlockSpec((tm, d), lambda i, *_: (i, 0))
    return pl.pallas_call(
        functools.partial(_norm_proj_body, col_chunk=col_chunk),
        grid_spec=pltpu.PrefetchScalarGridSpec(
            num_scalar_prefetch=3, grid=(m // tm,),
            in_specs=[row, pl.BlockSpec((tm, ROUTE_LANES), lambda i, *_: (i, 0)), pl.BlockSpec(memory_space=pl.ANY),
                      pl.BlockSpec((1, d), lambda i, *_: (0, 0)), pl.BlockSpec((d, n), lambda i, *_: (0, 0))],
            out_specs=[row, pl.BlockSpec((tm, n), lambda i, *_: (i, 0))],
            scratch_shapes=_moe_scratch(tm, d)),
        out_shape=[jax.ShapeDtypeStruct((m, d), F32), jax.ShapeDtypeStruct((m, n), F32)],
        compiler_params=_cparams(1), name="norm_proj",
    )(code0, code1, starts, x, route, ys, g.reshape(1, d), w)


def _norm_proj_even_body(x_ref, g_ref, w_ref, o_ref, rm0_ref, rm1_ref, rm2_ref, lane_ref):
    tm = x_ref.shape[0]
    hb = _rms(x_ref[...], g_ref[...]).astype(BF16)
    rm_refs = (rm0_ref, rm1_ref, rm2_ref)
    for c in range(EVEN_IN // A_WIDTH):
        cols = slice(c * A_WIDTH, (c + 1) * A_WIDTH)
        res = _dot(hb, w_ref[:, cols])
        o_ref[:, cols] = res
        if c < 3 * N_DIL:
            g, j = divmod(c, 3)
            dil = DIL_PAIRS[g][1]
            if dil == 1:
                rm_refs[g][0, 0, :, j * A_WIDTH:(j + 1) * A_WIDTH] = res.astype(BF16)
            else:
                for t in range(A_WIDTH // LANES):
                    lane_ref[t] = res[:, t * LANES:(t + 1) * LANES]
                for r in range(dil):
                    part = [lane_ref[t, pl.ds(r, tm // dil, stride=dil), :] for t in range(A_WIDTH // LANES)]
                    rm_refs[g][0, r, :, j * A_WIDTH:(j + 1) * A_WIDTH] = jnp.concatenate(part, axis=1).astype(BF16)


def _norm_proj_even(x, g, w, tm, n_batch, seq):
    m, d = x.shape
    n = w.shape[1]
    tps = seq // tm
    npt = n_batch * tps

    def rm_index(i):
        return (jnp.where(i < npt, i // tps, n_batch), 0, jnp.where(i < npt, i % tps, i - npt), 0)

    rm_shapes = [jax.ShapeDtypeStruct((n_batch + 1, dil, seq // dil, 3 * A_WIDTH), BF16) for _, dil in DIL_PAIRS]
    rm_specs = [pl.BlockSpec((1, dil, tm // dil, 3 * A_WIDTH), rm_index) for _, dil in DIL_PAIRS]
    return pl.pallas_call(
        _norm_proj_even_body,
        grid=(m // tm,),
        in_specs=[pl.BlockSpec((tm, d), lambda i: (i, 0)), pl.BlockSpec((1, d), lambda i: (0, 0)),
                  pl.BlockSpec((d, n), lambda i: (0, 0))],
        out_specs=[pl.BlockSpec((tm, n), lambda i: (i, 0))] + rm_specs,
        out_shape=[jax.ShapeDtypeStruct((m, n), F32)] + rm_shapes,
        scratch_shapes=[pltpu.VMEM((A_WIDTH // LANES, tm, LANES), F32)],
        compiler_params=_cparams(1), name="norm_proj_even",
    )(x, g.reshape(1, d), w)


def _t5_bucket(dist):
    max_exact = NUM_BUCKETS // 2
    d = np.maximum(dist, 1).astype(np.float32)
    large = max_exact + (np.log(d / max_exact) / np.log(MAX_DISTANCE / max_exact)
                         * (NUM_BUCKETS - max_exact)).astype(np.int32)
    large = np.minimum(large, NUM_BUCKETS - 1)
    return np.where(dist < max_exact, dist, large).astype(np.int32)


def _bucket_lookup(tab, buckets):
    onehot = jnp.asarray(buckets[..., None, None] == np.arange(NUM_BUCKETS)[:, None])
    return jnp.sum(jnp.where(onehot, tab.astype(F32), 0.0), axis=-2)


def _prompt_bias(tab, dil):
    qi = np.arange(A_BLOCK)[:, None]
    km = np.arange(2 * A_BLOCK)[None, :]
    delta = A_BLOCK + qi - km
    valid = (delta >= 0) & (delta <= SPAN)
    bias = _bucket_lookup(tab, _t5_bucket(np.clip(delta, 0, SPAN) * dil))
    bias = jnp.where(valid[..., None], bias, NEG_INF)
    return jnp.transpose(bias, (2, 0, 1))


def _attn_prompt_body(q_ref, kp_ref, ko_ref, vp_ref, vo_ref, bias_ref, o_ref, lse_ref):
    first = pl.program_id(2) == 0
    scale = A_HEAD_DIM ** -0.5
    n_sub = q_ref.shape[2] // A_BLOCK
    k_all = jnp.concatenate([kp_ref[0, 0], ko_ref[0, 0]], axis=0)
    v_all = jnp.concatenate([vp_ref[0, 0], vo_ref[0, 0]], axis=0)
    km = lax.broadcasted_iota(jnp.int32, (1, 2 * A_BLOCK), 1)
    no_prev = jnp.logical_and(first, km < A_BLOCK)
    heads_per_tile = LANES // A_HEAD_DIM
    head_of_lane = lax.broadcasted_iota(jnp.int32, (1, LANES), 1) // A_HEAD_DIM
    for sub in range(n_sub):
        rows = slice(sub * A_BLOCK, (sub + 1) * A_BLOCK)
        keys = slice(sub * A_BLOCK, (sub + 2) * A_BLOCK)
        for t in range(A_WIDTH // LANES):
            sl = slice(t * LANES, (t + 1) * LANES)
            q_t, k_t, v_t = q_ref[0, 0, rows, sl], k_all[keys, sl], v_all[keys, sl]
            o_t = jnp.zeros((A_BLOCK, LANES), F32)
            lse_t = jnp.zeros((A_BLOCK, LANES), F32)
            for j in range(heads_per_tile):
                mine = head_of_lane == j
                s = _dot_nt(jnp.where(mine, q_t, jnp.zeros_like(q_t)), k_t) * scale + bias_ref[t * heads_per_tile + j]
                if sub == 0:
                    s = jnp.where(no_prev, NEG_INF, s)
                m = jnp.max(s, axis=-1, keepdims=True)
                p = jnp.exp(s - m)
                den = jnp.sum(p, axis=-1, keepdims=True)
                o_t = jnp.where(mine, _dot(p.astype(BF16), v_t) / den, o_t)
                lse_t = jnp.where(mine, m + jnp.log(den), lse_t)
            o_ref[0, 0, rows, sl] = o_t
            lse_ref[0, 0, rows, sl] = lse_t


def _attn_prompt(qkv_rm, tab, g, dil, n_batch, seq):
    sub_len = seq // dil
    n_sub = _largest_divisor(sub_len // A_BLOCK, (2, 1))
    rows = n_sub * A_BLOCK
    nb = sub_len // rows

    def spec(j, prev):
        if prev:
            return pl.BlockSpec((1, 1, A_BLOCK, A_WIDTH), lambda b, r, i: (b, r, jnp.maximum(i * n_sub - 1, 0), j))
        return pl.BlockSpec((1, 1, rows, A_WIDTH), lambda b, r, i: (b, r, i, j))

    out_spec = pl.BlockSpec((1, 1, rows, A_WIDTH), lambda b, r, i: (b, r, i, 0))
    out_sds = jax.ShapeDtypeStruct((n_batch, dil, sub_len, A_WIDTH), F32)
    return pl.pallas_call(
        _attn_prompt_body,
        grid=(n_batch, dil, nb),
        in_specs=[spec(0, False), spec(1, True), spec(1, False), spec(2, True), spec(2, False),
                  pl.BlockSpec((A_HEADS, A_BLOCK, 2 * A_BLOCK), lambda b, r, i: (0, 0, 0))],
        out_specs=[out_spec, out_spec], out_shape=[out_sds, out_sds],
        compiler_params=_cparams(3), name="attn_prompt_g%d" % g,
    )(qkv_rm, qkv_rm, qkv_rm, qkv_rm, qkv_rm, _prompt_bias(tab, dil))


def _sample_bias(tab, dil, cache_len, key_index, t_new):
    t = np.arange(t_new)[:, None]
    dist = cache_len + t - key_index[None, :]
    valid = (dist >= 0) & (dist % dil == 0) & (dist <= SPAN * dil)
    bias = _bucket_lookup(tab, _t5_bucket(np.clip(dist, 0, SPAN * dil)))
    bias = jnp.where(valid[..., None], bias, NEG_INF)
    return jnp.transpose(bias, (2, 0, 1)).reshape(A_HEADS * t_new, key_index.shape[0])


def _attn_sample_body(new_ref, c0_ref, c1_ref, c2_ref, b0_ref, b1_ref, b2_ref, bn_ref, *out_refs, t_new):
    scale = A_HEAD_DIM ** -0.5
    rows = A_HEADS * t_new
    head_of_row = lax.broadcasted_iota(jnp.int32, (rows, A_WIDTH), 0) // t_new
    head_of_lane = lax.broadcasted_iota(jnp.int32, (rows, A_WIDTH), 1) // A_HEAD_DIM
    own = head_of_row == head_of_lane
    caches = (c0_ref, c1_ref, c2_ref)
    biases = (b0_ref, b1_ref, b2_ref)
    for g in range(N_DIL):
        base = g * 3 * A_WIDTH
        cache_len = caches[g].shape[-1]
        q = new_ref[:, base:base + A_WIDTH]
        k_new = new_ref[:, base + A_WIDTH:base + 2 * A_WIDTH].astype(BF16)
        v_new = new_ref[:, base + 2 * A_WIDTH:base + 3 * A_WIDTH].astype(BF16)
        q_bd = jnp.where(own, jnp.concatenate([q] * A_HEADS, axis=0), 0.0).astype(BF16)
        k_t = caches[g][0, 0, 0].reshape(A_WIDTH, cache_len).astype(BF16)
        v_t = caches[g][0, 0, 1].reshape(A_WIDTH, cache_len).astype(BF16)
        s_c = _dot(q_bd, k_t) * scale + biases[g][...]
        s_n = _dot_nt(q_bd, k_new) * scale + bn_ref[g]
        m = jnp.maximum(s_c.max(axis=-1, keepdims=True), s_n.max(axis=-1, keepdims=True))
        p_c = jnp.exp(s_c - m)
        p_n = jnp.exp(s_n - m)
        den = jnp.sum(p_c, axis=-1, keepdims=True) + jnp.sum(p_n, axis=-1, keepdims=True)
        acc = _dot_nt(p_c.astype(BF16), v_t) + _dot(p_n.astype(BF16), v_new)
        acc = jnp.where(own, acc / den, 0.0)
        lse = jnp.where(own, m + jnp.log(den), 0.0)
        o = acc[0:t_new]
        l = lse[0:t_new]
        for h in range(1, A_HEADS):
            o = o + acc[h * t_new:(h + 1) * t_new]
            l = l + lse[h * t_new:(h + 1) * t_new]
        out_refs[2 * g][...] = o
        out_refs[2 * g + 1][...] = l


def _attn_sample(proj, caches, t5_bias, n_seq, t_new, row0):
    cache_lens = [c.shape[2] for c in caches]
    caches_t = [jnp.transpose(c, (0, 1, 3, 4, 5, 2)) for c in caches]
    biases = [_sample_bias(t5_bias[:, g], DIL_PAIRS[g][1], cache_lens[g], np.arange(cache_lens[g]), t_new)
              for g in range(N_DIL)]
    bias_new = jnp.stack([_sample_bias(t5_bias[:, g], DIL_PAIRS[g][1], cache_lens[g],
                                       cache_lens[g] + np.arange(t_new), t_new) for g in range(N_DIL)])
    blk0 = row0 // t_new
    full = lambda a: pl.BlockSpec(a.shape, lambda b: (0,) * a.ndim)
    out_spec = pl.BlockSpec((t_new, A_WIDTH), lambda b: (b, 0))
    cache_specs = [pl.BlockSpec((1, 1, 2, A_HEADS, A_HEAD_DIM, n), lambda b: (0, b, 0, 0, 0, 0)) for n in cache_lens]
    return pl.pallas_call(
        functools.partial(_attn_sample_body, t_new=t_new),
        grid=(n_seq,),
        in_specs=[pl.BlockSpec((t_new, N_DIL * 3 * A_WIDTH), lambda b: (blk0 + b, 0))] + cache_specs
                 + [full(biases[0]), full(biases[1]), full(biases[2]), full(bias_new)],
        out_specs=[out_spec] * (2 * N_DIL),
        out_shape=[jax.ShapeDtypeStruct((n_seq * t_new, A_WIDTH), F32)] * (2 * N_DIL),
        compiler_params=_cparams(1), name="attn_sample",
    )(proj, *caches_t, *biases, bias_new)


def _shift_rows(x, s, fill, axis):
    t = lax.broadcasted_iota(jnp.int32, x.shape, axis)
    return jnp.where(t >= s, pltpu.roll(x, s, axis), fill)


def _linear_scan(a, b, axis):
    n = a.shape[axis]
    s = 1
    while s < n:
        b = b + a * _shift_rows(b, s, 0.0, axis)
        a = a * _shift_rows(a, s, 1.0, axis)
        s *= 2
    return a, b


def _rglru_gates(xc, wa_ref, wx_ref, vec_ref):
    xcb = xc.astype(BF16)
    r = jax.nn.sigmoid(_dot(xcb, wa_ref[...]) + vec_ref[1:2])
    ig = jax.nn.sigmoid(_dot(xcb, wx_ref[...]) + vec_ref[2:3])
    log_a = -RG_C * r * _softplus(-vec_ref[3:4])
    a = jnp.exp(log_a)
    b = jnp.sqrt(1.0 - jnp.exp(2.0 * log_a)) * (ig * xc)
    return a, b


def _gelu(x):
    return 0.5 * x * (1.0 + jnp.tanh(math.sqrt(2.0 / math.pi) * (x + 0.044715 * (x * x * x))))


def _rglru_prompt_body(xb_ref, gb_ref, cw_ref, wa_ref, wx_ref, vec_ref, y_ref, hl_ref, tail_ref, h_ref):
    @pl.when(pl.program_id(1) == 0)
    def _():
        tail_ref[...] = jnp.zeros_like(tail_ref)
        h_ref[...] = jnp.zeros_like(h_ref)

    x = xb_ref[...]
    tt = x.shape[0]
    xe = jnp.concatenate([tail_ref[...], x], axis=0)
    xc = vec_ref[0:1] + x * cw_ref[CONV_W - 1:CONV_W]
    for j in range(1, CONV_W):
        xc = xc + pltpu.roll(xe, j, 0)[8:8 + tt] * cw_ref[CONV_W - 1 - j:CONV_W - j]
    tail_ref[...] = x[tt - 8:tt]
    a, b = _rglru_gates(xc, wa_ref, wx_ref, vec_ref)
    a_cum, h = _linear_scan(a, b, 0)
    h = h + a_cum * h_ref[...]
    h_ref[...] = h[tt - 1:tt]
    hl_ref[0] = h[tt - 1:tt]
    y_ref[...] = h * _gelu(gb_ref[...])


def _rglru_sample_body(xb_ref, gb_ref, c0_ref, h0_ref, cw_ref, wa_ref, wx_ref, vec_ref, prev_ref, y_ref, hl_ref, *, t_new):
    del prev_ref
    x = xb_ref[...]
    rows = x.shape[0]
    ns = rows // t_new
    x3 = x.reshape(ns, t_new, RNN_WIDTH)
    xe = jnp.concatenate([c0_ref[...], x3], axis=1)
    xc = vec_ref[0:1] + x3 * cw_ref[CONV_W - 1:CONV_W]
    for j in range(1, CONV_W):
        xc = xc + pltpu.roll(xe, j, 1)[:, 8:8 + t_new] * cw_ref[CONV_W - 1 - j:CONV_W - j]
    a, b = _rglru_gates(xc.reshape(rows, RNN_WIDTH), wa_ref, wx_ref, vec_ref)
    a_cum, h = _linear_scan(a.reshape(ns, t_new, RNN_WIDTH), b.reshape(ns, t_new, RNN_WIDTH), 1)
    h = h + a_cum * h0_ref[...][:, None, :]
    hl_ref[...] = h[:, t_new - 1, :]
    y_ref[...] = h.reshape(rows, RNN_WIDTH) * _gelu(gb_ref[...])


def _block_diag(w):
    nb, bi, bo = w.shape
    eye = jnp.eye(nb, dtype=w.dtype)
    return (w[:, :, None, :] * eye[:, None, :, None]).reshape(nb * bi, nb * bo)


def _rglru(proj, conv0, h0, conv_w, conv_b, wa, ba, wx, bx, lam, n_batch, seq, n_seq, t_new):
    m_total = proj.shape[0]
    mp = n_batch * seq
    wa_bd = _block_diag(wa).astype(BF16)
    wx_bd = _block_diag(wx).astype(BF16)
    vec = jnp.stack([conv_b, ba, bx, lam]).astype(F32)
    xcol = XB_COL // RNN_WIDTH
    gcol = GB_COL // RNN_WIDTH
    full2 = lambda a, nd: pl.BlockSpec(a.shape, lambda *_: (0,) * a.ndim)
    tt = _row_tile(seq)
    nt = seq // tt
    w_specs2 = [pl.BlockSpec(a.shape, lambda b, i: (0, 0)) for a in (conv_w, wa_bd, wx_bd, vec)]
    y, hl_p = pl.pallas_call(
        _rglru_prompt_body,
        grid=(n_batch, nt),
        in_specs=[pl.BlockSpec((tt, RNN_WIDTH), lambda b, i: (b * nt + i, xcol)),
                  pl.BlockSpec((tt, RNN_WIDTH), lambda b, i: (b * nt + i, gcol))] + w_specs2,
        out_specs=[pl.BlockSpec((tt, RNN_WIDTH), lambda b, i: (b * nt + i, 0)),
                   pl.BlockSpec((1, 1, RNN_WIDTH), lambda b, i: (b, 0, 0))],
        out_shape=[jax.ShapeDtypeStruct((m_total, RNN_WIDTH), F32),
                   jax.ShapeDtypeStruct((n_batch, 1, RNN_WIDTH), F32)],
        scratch_shapes=[pltpu.VMEM((8, RNN_WIDTH), F32), pltpu.VMEM((1, RNN_WIDTH), F32)],
        compiler_params=_cparams(2), name="rglru_prompt",
    )(proj, proj, conv_w, wa_bd, wx_bd, vec)
    del full2
    ts = 32 if n_seq % 32 == 0 else 8
    rows = ts * t_new
    blk0 = mp // rows
    c0p = jnp.pad(conv0, ((0, 0), (8 - (CONV_W - 1), 0), (0, 0)))
    w_specs1 = [pl.BlockSpec(a.shape, lambda i: (0, 0)) for a in (conv_w, wa_bd, wx_bd, vec)]
    y, hl_s = pl.pallas_call(
        functools.partial(_rglru_sample_body, t_new=t_new),
        grid=(n_seq // ts,),
        in_specs=[pl.BlockSpec((rows, RNN_WIDTH), lambda i: (blk0 + i, xcol)),
                  pl.BlockSpec((rows, RNN_WIDTH), lambda i: (blk0 + i, gcol)),
                  pl.BlockSpec((ts, 8, RNN_WIDTH), lambda i: (i, 0, 0)),
                  pl.BlockSpec((ts, RNN_WIDTH), lambda i: (i, 0))] + w_specs1
                 + [pl.BlockSpec(memory_space=pl.ANY)],
        out_specs=[pl.BlockSpec((rows, RNN_WIDTH), lambda i: (blk0 + i, 0)),
                   pl.BlockSpec((ts, RNN_WIDTH), lambda i: (i, 0))],
        out_shape=[jax.ShapeDtypeStruct((m_total, RNN_WIDTH), F32),
                   jax.ShapeDtypeStruct((n_seq, RNN_WIDTH), F32)],
        input_output_aliases={8: 0},
        compiler_params=_cparams(1), name="rglru_sample",
    )(proj, proj, c0p, h0, conv_w, wa_bd, wx_bd, vec, y)
    return y, hl_p.reshape(n_batch, RNN_WIDTH), hl_s


def _pack_bf16_pair(x):
    w = x.shape[1] // 2
    hi = lax.bitcast_convert_type(x[:, :w].astype(BF16).astype(F32), jnp.uint32)
    lo = lax.bitcast_convert_type(x[:, w:].astype(BF16).astype(F32), jnp.uint32)
    return hi | (lo >> 16)


def _unpack_bf16_pair(p):
    hi = lax.bitcast_convert_type(p & jnp.uint32(0xFFFF0000), F32).astype(BF16)
    lo = lax.bitcast_convert_type(p << 16, F32).astype(BF16)
    return hi, lo


def _route(x1, gf_ref, wr_ref, br_ref, hn_ref, route_ref, counts_ref, run_ref):
    @pl.when(pl.program_id(0) == 0)
    def _():
        run_ref[...] = jnp.zeros_like(run_ref)

    hn = _rms(x1, gf_ref[...])
    hn_ref[...] = _pack_bf16_pair(hn)
    hn_hi = hn.astype(BF16)
    hn_lo = (hn - hn_hi.astype(F32)).astype(BF16)
    logits = _dot(hn_hi, wr_ref[0]) + (_dot(hn_hi, wr_ref[1]) + _dot(hn_lo, wr_ref[0])) + br_ref[...]
    lane = lax.broadcasted_iota(jnp.int32, logits.shape, 1)
    is_coarse = lane < N_GROUPS
    coarse = jnp.where(is_coarse, logits, -jnp.inf)
    cmax = jnp.max(coarse, axis=-1, keepdims=True)
    grp = jnp.min(jnp.where(coarse == cmax, lane, ROUTE_LANES), axis=-1, keepdims=True)
    p_grp = 1.0 / jnp.sum(jnp.where(is_coarse, jnp.exp(logits - cmax), 0.0), axis=-1, keepdims=True)
    expert = lane - N_GROUPS
    in_grp = (lane >= N_GROUPS) & (expert < N_EXPERTS) & (expert // EXPERTS_PER_GROUP == grp)
    fine = jnp.where(in_grp, logits, -jnp.inf)
    v1 = jnp.max(fine, axis=-1, keepdims=True)
    i1 = jnp.min(jnp.where(fine == v1, lane, ROUTE_LANES), axis=-1, keepdims=True)
    fine2 = jnp.where(lane == i1, -jnp.inf, fine)
    v2 = jnp.max(fine2, axis=-1, keepdims=True)
    i2 = jnp.min(jnp.where(fine2 == v2, lane, ROUTE_LANES), axis=-1, keepdims=True)
    e2 = jnp.exp(v2 - v1)
    w1 = p_grp / (1.0 + e2)
    w2 = p_grp * e2 / (1.0 + e2)
    tm = logits.shape[0]
    sel = jnp.where(lane == i1, 1.0, jnp.where(lane == i2, 1.0, 0.0))
    ri = lax.broadcasted_iota(jnp.int32, (tm, tm), 0)
    ci = lax.broadcasted_iota(jnp.int32, (tm, tm), 1)
    earlier = jnp.where(ri > ci, 1.0, 0.0).astype(BF16)
    before = _dot(earlier, sel.astype(BF16)) + run_ref[...]
    r1 = jnp.sum(jnp.where(lane == i1, before, 0.0), axis=-1, keepdims=True)
    r2 = jnp.sum(jnp.where(lane == i2, before, 0.0), axis=-1, keepdims=True)
    run_ref[...] = run_ref[...] + jnp.sum(sel, axis=0, keepdims=True)
    counts_ref[...] = run_ref[...]
    route = jnp.where(lane == 0, (i1 - N_GROUPS).astype(F32), 0.0)
    route = jnp.where(lane == 1, (i2 - N_GROUPS).astype(F32), route)
    route = jnp.where(lane == 2, w1, route)
    route = jnp.where(lane == 3, w2, route)
    route = jnp.where(lane == 4, r1, route)
    route = jnp.where(lane == 5, r2, route)
    route = jnp.where(lane == 6, (i1 - N_GROUPS).astype(F32) * SLOT_CODE + r1, route)
    route = jnp.where(lane == 7, (i2 - N_GROUPS).astype(F32) * SLOT_CODE + r2, route)
    route_ref[...] = route


def _merge_groups(os, ls):
    mx = jnp.maximum(jnp.maximum(ls[0], ls[1]), ls[2])
    es = [jnp.exp(l - mx) for l in ls]
    return (es[0] * os[0] + es[1] * os[1] + es[2] * os[2]) / (es[0] + es[1] + es[2])


def _even_out_body(o0, l0, o1, l1, o2, l2, so0, sl0, so1, sl1, so2, sl2, yb_ref, x_ref, wo_ref, gf_ref, wr_ref, br_ref,
                   x1_ref, hn_ref, route_ref, counts_ref, n1o, n1l, n2o, n2l, oa_ref, run_ref, *, n_prompt_tiles):
    i = pl.program_id(0)
    tm = x_ref.shape[0]

    @pl.when(i < n_prompt_tiles)
    def _():
        nat = []
        for src, dst, dil in ((o1, n1o, DIL_PAIRS[1][1]), (l1, n1l, DIL_PAIRS[1][1]),
                              (o2, n2o, DIL_PAIRS[2][1]), (l2, n2l, DIL_PAIRS[2][1])):
            for r in range(dil):
                blk = src[0, r]
                for t in range(A_WIDTH // LANES):
                    dst[t, pl.ds(r, tm // dil, stride=dil), :] = blk[:, t * LANES:(t + 1) * LANES]
            nat.append(jnp.concatenate([dst[t] for t in range(A_WIDTH // LANES)], axis=1))
        oa_ref[...] = _merge_groups((o0[0, 0], nat[0], nat[2]), (l0[0, 0], nat[1], nat[3]))

    @pl.when(i >= n_prompt_tiles)
    def _():
        oa_ref[...] = _merge_groups((so0[...], so1[...], so2[...]), (sl0[...], sl1[...], sl2[...]))

    cat = jnp.concatenate([oa_ref[...], yb_ref[...]], axis=-1).astype(BF16)
    x1 = x_ref[...] + _dot(cat, wo_ref[...])
    x1_ref[...] = x1
    _route(x1, gf_ref, wr_ref, br_ref, hn_ref, route_ref, counts_ref, run_ref)


def _router_weights(rg_w, rg_b, re_w, re_b):
    d = rg_w.shape[0]
    wr = jnp.concatenate([rg_w, re_w.reshape(d, N_EXPERTS)], axis=1)
    br = jnp.concatenate([rg_b, re_b.reshape(N_EXPERTS)])
    pad = ROUTE_LANES - wr.shape[1]
    wr = jnp.pad(wr, ((0, 0), (0, pad))).astype(F32)
    wr_hi = wr.astype(BF16)
    wr_lo = (wr - wr_hi.astype(F32)).astype(BF16)
    return jnp.stack([wr_hi, wr_lo]), jnp.pad(br, (0, pad)).reshape(1, ROUTE_LANES).astype(F32)


def _mix_out_call(body, in_specs, args, consts, m, d, tm, scratch, name):
    const_spec = lambda a: pl.BlockSpec(a.shape, lambda i: (0,) * a.ndim)
    return pl.pallas_call(
        body, grid=(m // tm,), in_specs=list(in_specs) + [const_spec(c) for c in consts],
        out_specs=[pl.BlockSpec((tm, d), lambda i: (i, 0)), pl.BlockSpec((tm, d // 2), lambda i: (i, 0)),
                   pl.BlockSpec((tm, ROUTE_LANES), lambda i: (i, 0)), pl.BlockSpec((1, ROUTE_LANES), lambda i: (0, 0))],
        out_shape=[jax.ShapeDtypeStruct((m, d), F32), jax.ShapeDtypeStruct((m, d // 2), jnp.uint32),
                   jax.ShapeDtypeStruct((m, ROUTE_LANES), F32), jax.ShapeDtypeStruct((1, ROUTE_LANES), F32)],
        scratch_shapes=list(scratch) + [pltpu.VMEM((1, ROUTE_LANES), F32)],
        compiler_params=_cparams(1), name=name,
    )(*args, *consts)


def _even_out(attn_p, attn_s, y_b, x, consts, tm, n_batch, seq):
    m, d = x.shape
    tps = seq // tm
    npt = n_batch * tps
    in_specs = []
    for g, (_, dil) in enumerate(DIL_PAIRS):
        def index(i):
            return (jnp.minimum(i // tps, n_batch - 1), 0, jnp.where(i < npt, i % tps, 0), 0)
        in_specs += [pl.BlockSpec((1, dil, tm // dil, A_WIDTH), index)] * 2
    in_specs += [pl.BlockSpec((tm, A_WIDTH), lambda i: (jnp.maximum(i - npt, 0), 0))] * (2 * N_DIL)
    in_specs += [pl.BlockSpec((tm, RNN_WIDTH), lambda i: (i, 0)), pl.BlockSpec((tm, d), lambda i: (i, 0))]
    scratch = [pltpu.VMEM((A_WIDTH // LANES, tm, LANES), F32)] * 4 + [pltpu.VMEM((tm, A_WIDTH), F32)]
    return _mix_out_call(functools.partial(_even_out_body, n_prompt_tiles=npt), in_specs,
                         list(attn_p) + list(attn_s) + [y_b, x], consts, m, d, tm, scratch, "even_out")


def _odd_out(o_c, proj2, x, consts, tm):
    m, d = x.shape
    in_specs = [pl.BlockSpec((tm, C_V_W), lambda i: (i, 0)),
                pl.BlockSpec((tm, C_V_W), lambda i: (i, Z_COL // C_V_W)),
                pl.BlockSpec((tm, d), lambda i: (i, 0))]
    return _mix_out_call(_odd_out_body, in_specs, [o_c, proj2, x], consts, m, d, tm, [], "odd_out")


def _moe_dispatch_body(c0_ref, c1_ref, start_ref, fill_ref, used_ref, hn_ref, xs_ref, zero_ref, fill_sem, row_sem):
    i = pl.program_id(0)
    tm = hn_ref.shape[0]

    def fill_copy(e):
        return pltpu.make_async_copy(zero_ref, xs_ref.at[pl.ds(pl.multiple_of(fill_ref[e], MOE_TILE), MOE_TILE)], fill_sem)

    @pl.when(i == 0)
    def _():
        zero_ref[...] = jnp.zeros_like(zero_ref)
        for e in range(N_EXPERTS):
            @pl.when(used_ref[e] > 0)
            def _():
                fill_copy(e).start()
        for e in range(N_EXPERTS):
            @pl.when(used_ref[e] > 0)
            def _():
                fill_copy(e).wait()

    base = i * tm

    def scatter_row(j, carry):
        _row_copy(hn_ref, j, xs_ref, _slot(c0_ref[base + j], start_ref), row_sem).start()
        _row_copy(hn_ref, j, xs_ref, _slot(c1_ref[base + j], start_ref), row_sem).start(priority=1)
        return carry

    lax.fori_loop(0, tm, scatter_row, 0, unroll=8)
    for _ in range(2):
        pltpu.make_async_copy(hn_ref, xs_ref.at[pl.ds(0, tm)], row_sem).wait()


def _moe_ffn_body(te_ref, nu_ref, xs_ref, wg_ref, wu_ref, wd_ref, o_ref):
    del te_ref

    @pl.when(pl.program_id(0) < nu_ref[0])
    def _():
        xa, xb = _unpack_bf16_pair(xs_ref[...])
        half = xa.shape[1]
        hg = _dot(xa, wg_ref[0, :half]) + _dot(xb, wg_ref[0, half:])
        hu = _dot(xa, wu_ref[0, :half]) + _dot(xb, wu_ref[0, half:])
        o_ref[...] = _dot((_silu(hg) * hu).astype(BF16), wd_ref[0])

    @pl.when(pl.program_id(0) >= nu_ref[0])
    def _():
        o_ref[...] = jnp.zeros_like(o_ref)


def _moe(hn, route, counts, w_gate, w_up, w_down, tm):
    m = hn.shape[0]
    d = w_gate.shape[1]
    tile = MOE_TILE
    n_slots = 2 * m + N_EXPERTS * tile
    n_tiles = n_slots // tile
    assert 2 * m < SLOT_CODE and N_EXPERTS * SLOT_CODE <= 1 << 24, "slot codes must stay exact in f32"
    cnt = counts[0, N_GROUPS:N_GROUPS + N_EXPERTS].astype(jnp.int32)
    padded = ((cnt + tile - 1) // tile) * tile
    pad_end = jnp.cumsum(padded)
    pad_start = (pad_end - padded).astype(jnp.int32)
    code0 = route[:, 6].astype(jnp.int32)
    code1 = route[:, 7].astype(jnp.int32)
    tile_start = jnp.arange(n_tiles, dtype=jnp.int32) * tile
    tile_expert = jnp.sum((tile_start[:, None] >= pad_end[None, :]).astype(jnp.int32), axis=1)
    tile_expert = jnp.minimum(tile_expert, N_EXPERTS - 1).astype(jnp.int32)
    n_used = (pad_end[-1:] // tile).astype(jnp.int32)
    fill_start = jnp.maximum(pad_end - tile, 0).astype(jnp.int32)

    xs = pl.pallas_call(
        _moe_dispatch_body,
        grid_spec=pltpu.PrefetchScalarGridSpec(
            num_scalar_prefetch=5, grid=(m // tm,),
            in_specs=[pl.BlockSpec((tm, d // 2), lambda i, *_: (i, 0))],
            out_specs=pl.BlockSpec(memory_space=pl.ANY),
            scratch_shapes=[pltpu.VMEM((tile, d // 2), jnp.uint32), pltpu.SemaphoreType.DMA(()),
                            pltpu.SemaphoreType.DMA(())]),
        out_shape=jax.ShapeDtypeStruct((n_slots, d // 2), jnp.uint32),
        compiler_params=_cparams(1), name="moe_dispatch",
    )(code0, code1, pad_start, fill_start, cnt, hn)

    def used(i, nu):
        return jnp.minimum(i, jnp.maximum(nu[0] - 1, 0))

    ys = pl.pallas_call(
        _moe_ffn_body,
        grid_spec=pltpu.PrefetchScalarGridSpec(
            num_scalar_prefetch=2, grid=(n_tiles,),
            in_specs=[pl.BlockSpec((tile, d // 2), lambda i, te, nu: (used(i, nu), 0)),
                      pl.BlockSpec((1, d, EXPERT_FF), lambda i, te, nu: (te[used(i, nu)], 0, 0)),
                      pl.BlockSpec((1, d, EXPERT_FF), lambda i, te, nu: (te[used(i, nu)], 0, 0)),
                      pl.BlockSpec((1, EXPERT_FF, d), lambda i, te, nu: (te[used(i, nu)], 0, 0))],
            out_specs=pl.BlockSpec((tile, d), lambda i, te, nu: (i, 0))),
        out_shape=jax.ShapeDtypeStruct((n_slots, d), F32),
        compiler_params=_cparams(1), name="moe_ffn",
    )(tile_expert, n_used, xs, w_gate, w_up, w_down)

    return route, ys, code0, code1, pad_start


def _gdn_prep_math(xe, tt, cw_ref, bg, av_ref):
    acc = xe[8:8 + tt] * cw_ref[CONV_W - 1:CONV_W]
    for j in range(1, CONV_W):
        acc = acc + pltpu.roll(xe, j, 0)[8:8 + tt] * cw_ref[CONV_W - 1 - j:CONV_W - j]
    qkv = _silu(acc)
    outs = []
    for h in range(2 * C_QK_HEADS):
        xh = qkv[:, h * C_DK:(h + 1) * C_DK]
        xh = xh * lax.rsqrt(jnp.sum(xh * xh, axis=-1, keepdims=True) + EPS)
        if h < C_QK_HEADS:
            xh = xh * (C_DK ** -0.5)
        outs.append(xh)
    qk = jnp.concatenate(outs, axis=-1)
    v = qkv[:, 2 * C_QK_W:]
    lane = lax.broadcasted_iota(jnp.int32, bg.shape, 1)
    gdec = -jnp.exp(av_ref[0:1]) * _softplus(bg + av_ref[1:2])
    bgo = jnp.where(lane < C_V_HEADS, jax.nn.sigmoid(bg), gdec)
    return qk, v, bgo


def _unit_lower_inverses(l_bds, size):
    n = l_bds[0].shape[0]
    ri = lax.broadcasted_iota(jnp.int32, (n, n), 0)
    ci = lax.broadcasted_iota(jnp.int32, (n, n), 1)
    eye = jnp.where(ri == ci, 1.0, 0.0).astype(F32)
    base = ri // GDN_BASE == ci // GDN_BASE
    ps = [jnp.where(base, -l, 0.0) for l in l_bds]
    ts = [eye + p for p in ps]
    s = 2
    while s < GDN_BASE:
        ps = [_dot(p.astype(BF16), p.astype(BF16)) for p in ps]
        ts = [t + _dot(t.astype(BF16), p.astype(BF16)) for t, p in zip(ts, ps)]
        s *= 2
    s = GDN_BASE
    while s < size:
        lower_left = ((ri // s) % 2 == 1) & (ci // s == ri // s - 1)
        tbs = [t.astype(BF16) for t in ts]
        mids = [_dot(jnp.where(lower_left, l, 0.0).astype(BF16), tb).astype(BF16) for l, tb in zip(l_bds, tbs)]
        ts = [t - _dot(tb, mid) for t, tb, mid in zip(ts, tbs, mids)]
        s *= 2
    return ts


def _gdn_body(*refs, chunk, pack, n_par, has_state):
    c = chunk
    x_refs = refs[0:2 * n_par:2]
    bg_refs = refs[1:2 * n_par:2]
    pos = 2 * n_par
    if has_state:
        c0_ref = refs[pos]
        pos += 1
    cw_ref, av_ref, nw_ref = refs[pos:pos + 3]
    pos += 3
    if has_state:
        s0_ref = refs[pos]
        pos += 2
    o_ref, sout_ref, s_ref = refs[pos:pos + 3]
    tail_ref = None if has_state else refs[pos + 3]

    @pl.when(pl.program_id(1) == 0)
    def _():
        if has_state:
            s_ref[...] = s0_ref[...]
        else:
            s_ref[...] = jnp.zeros_like(s_ref)
            tail_ref[...] = jnp.zeros_like(tail_ref)

    rep = C_V_HEADS // C_QK_HEADS
    ti = lax.broadcasted_iota(jnp.int32, (c, c), 0)
    tj = lax.broadcasted_iota(jnp.int32, (c, c), 1)
    tril = jnp.where(ti >= tj, 1.0, 0.0).astype(F32)
    n = pack * c
    ri = lax.broadcasted_iota(jnp.int32, (n, n), 0)
    ci = lax.broadcasted_iota(jnp.int32, (n, n), 1)
    same = ri // c == ci // c
    incl = same & (ri >= ci)
    strict = same & (ri > ci)

    def col(a, heads):
        return jnp.concatenate([a[:, h:h + 1] for h in heads], axis=0)

    items, g_ends = [], []
    l_bds, qk_bds, rhss, qgs, kends = [], [], [], [], []
    for k in range(n_par):
        x = x_refs[k][...]
        prev = c0_ref[k] if has_state else tail_ref[k]
        if not has_state:
            tail_ref[k] = x[c - 8:c]
        qk, v, bgo = _gdn_prep_math(jnp.concatenate([prev, x], axis=0), c, cw_ref, bg_refs[k][...], av_ref)
        beta = bgo[:, 0:C_V_HEADS]
        gc = _dot_f32(tril, bgo[:, C_V_HEADS:2 * C_V_HEADS])
        g_last = gc[c - 1:c]
        gam = jnp.exp(gc)
        kdec = jnp.exp(g_last - gc)
        g_ends.append(jnp.exp(g_last))
        for p0 in range(0, C_V_HEADS, pack):
            heads = range(p0, p0 + pack)
            items.append((k, heads))
            g_col = col(gc, heads)
            g_row = jnp.sum(jnp.where(ri == ci, g_col, 0.0), axis=0, keepdims=True)
            b_col = col(beta, heads)
            gam_col = col(gam, heads)
            k_st = jnp.concatenate([qk[:, C_QK_W + (h // rep) * C_DK:C_QK_W + (h // rep + 1) * C_DK] for h in heads], axis=0)
            q_st = jnp.concatenate([qk[:, (h // rep) * C_DK:(h // rep + 1) * C_DK] for h in heads], axis=0)
            v_st = jnp.concatenate([v[:, h * C_DV:(h + 1) * C_DV] for h in heads], axis=0)
            kb = k_st.astype(BF16)
            decay = jnp.exp(jnp.where(incl, g_col - g_row, -jnp.inf))
            l_bds.append(jnp.where(strict, b_col * _dot_nt(kb, kb) * decay, 0.0))
            qk_bds.append((_dot_nt(q_st.astype(BF16), kb) * decay).astype(BF16))
            rhss.append(jnp.concatenate([b_col * v_st, (b_col * gam_col) * k_st], axis=1).astype(BF16))
            qgs.append(q_st * gam_col)
            kends.append((k_st * col(kdec, heads)).astype(BF16))
    t_invs = _unit_lower_inverses(l_bds, c)
    uws = [_dot(t.astype(BF16), rhs) for t, rhs in zip(t_invs, rhss)]
    u_sts, q_sts = [], []
    for (k, heads), uw, qg in zip(items, uws, qgs):
        us, qs = [], []
        for i, h in enumerate(heads):
            rows = slice(i * c, (i + 1) * c)
            lhs = jnp.concatenate([uw[rows, C_DV:], qg[rows]], axis=0).astype(BF16)
            ws = _dot(lhs, s_ref[k, h].astype(BF16))
            us.append(uw[rows, 0:C_DV] - ws[0:c])
            qs.append(ws[c:])
        u_sts.append(jnp.concatenate(us, axis=0).astype(BF16))
        q_sts.append(jnp.concatenate(qs, axis=0))
    o_sts = [q + _dot(qk_bd, ub) for q, qk_bd, ub in zip(q_sts, qk_bds, u_sts)]
    for (k, heads), o_st, ub, kendb in zip(items, o_sts, u_sts, kends):
        for i, h in enumerate(heads):
            rows = slice(i * c, (i + 1) * c)
            s_ref[k, h] = s_ref[k, h] * g_ends[k][:, h:h + 1] + _dot_tn(kendb[rows], ub[rows])
            o_h = _rms(o_st[rows], nw_ref[...])
            if has_state:
                o_ref[k * c:(k + 1) * c, h * C_DV:(h + 1) * C_DV] = o_h
            else:
                o_ref[k, 0, :, h * C_DV:(h + 1) * C_DV] = o_h

    @pl.when(pl.program_id(1) == pl.num_programs(1) - 1)
    def _():
        sout_ref[...] = s_ref[...]


def _largest_divisor(n, candidates):
    return next(c for c in candidates if n % c == 0)


def _gdn(proj, conv0, conv_w, a_log, dt_bias, onorm_w, s0, n_batch, seq, n_seq, t_new, chunk):
    mp = n_batch * seq
    nc = seq // chunk
    av = jnp.zeros((2, 128), F32)
    av = av.at[0, C_V_HEADS:2 * C_V_HEADS].set(a_log).at[1, C_V_HEADS:2 * C_V_HEADS].set(dt_bias)
    nw = onorm_w.reshape(1, C_DV).astype(F32)
    bg_col = BG_COL // 128
    state = (C_V_HEADS, C_DK, C_DV)
    consts = [conv_w, av, nw]
    const_specs = [pl.BlockSpec(a.shape, lambda b, i: (0, 0)) for a in consts]

    par_p = _largest_divisor(n_batch, (2, 1))
    seq_specs = []
    for k in range(par_p):
        seq_specs += [pl.BlockSpec((chunk, C_CONV_DIM), lambda b, i, k=k: ((b * par_p + k) * nc + i, 0)),
                      pl.BlockSpec((chunk, 128), lambda b, i, k=k: ((b * par_p + k) * nc + i, bg_col))]
    extra = -(-(n_seq * t_new) // seq)
    o4, s_p = pl.pallas_call(
        functools.partial(_gdn_body, chunk=chunk, pack=256 // chunk, n_par=par_p, has_state=False),
        grid=(n_batch // par_p, nc),
        in_specs=seq_specs + const_specs,
        out_specs=[pl.BlockSpec((par_p, 1, chunk, C_V_W), lambda b, i: (b, i, 0, 0)),
                   pl.BlockSpec((par_p,) + state, lambda b, i: (b, 0, 0, 0))],
        out_shape=[jax.ShapeDtypeStruct((n_batch + extra, nc, chunk, C_V_W), F32),
                   jax.ShapeDtypeStruct((n_batch,) + state, F32)],
        scratch_shapes=[pltpu.VMEM((par_p,) + state, F32), pltpu.VMEM((par_p, 8, C_CONV_DIM), F32)],
        compiler_params=_cparams(2), name="gdn_prompt",
    )(*([proj, proj] * par_p), *consts)
    o = o4.reshape((n_batch + extra) * seq, C_V_W)

    par_s = _largest_divisor(n_seq, (4, 2, 1))
    blk0 = mp // t_new
    c0p = jnp.pad(conv0, ((0, 0), (8 - (CONV_W - 1), 0), (0, 0)))
    seq_specs = []
    for k in range(par_s):
        seq_specs += [pl.BlockSpec((t_new, C_CONV_DIM), lambda b, i, k=k: (blk0 + b * par_s + k, 0)),
                      pl.BlockSpec((t_new, 128), lambda b, i, k=k: (blk0 + b * par_s + k, bg_col))]
    n_in = 2 * par_s + 1 + len(consts) + 1
    o, s_s = pl.pallas_call(
        functools.partial(_gdn_body, chunk=t_new, pack=C_V_HEADS, n_par=par_s, has_state=True),
        grid=(n_seq // par_s, 1),
        in_specs=seq_specs + [pl.BlockSpec((par_s, 8, C_CONV_DIM), lambda b, i: (b, 0, 0))] + const_specs
                 + [pl.BlockSpec((par_s,) + state, lambda b, i: (b, 0, 0, 0)), pl.BlockSpec(memory_space=pl.ANY)],
        out_specs=[pl.BlockSpec((par_s * t_new, C_V_W), lambda b, i: (blk0 // par_s + b, 0)),
                   pl.BlockSpec((par_s,) + state, lambda b, i: (b, 0, 0, 0))],
        out_shape=[jax.ShapeDtypeStruct(o.shape, F32), jax.ShapeDtypeStruct((n_seq,) + state, F32)],
        scratch_shapes=[pltpu.VMEM((par_s,) + state, F32)],
        input_output_aliases={n_in: 0},
        compiler_params=_cparams(2), name="gdn_sample",
    )(*([proj, proj] * par_s), c0p, *consts, s0, o)
    return o, s_p, s_s


def _odd_out_body(o_ref, z_ref, x_ref, wo_ref, gf_ref, wr_ref, br_ref, x1_ref, hn_ref, route_ref, counts_ref, run_ref):
    y = (o_ref[...] * _silu(z_ref[...])).astype(BF16)
    x1 = x_ref[...] + _dot(y, wo_ref[...])
    x1_ref[...] = x1
    _route(x1, gf_ref, wr_ref, br_ref, hn_ref, route_ref, counts_ref, run_ref)


def _final_body(c0_ref, c1_ref, start_ref, x_ref, route_ref, ys_ref, g_ref, y_ref, buf_ref, sems, *, tile0):
    x = _moe_residual(x_ref, route_ref, ys_ref, c0_ref, c1_ref, start_ref, buf_ref, sems, tile0)
    y_ref[...] = _rms(x, g_ref[...])


def _final_norm(x, moe, g, tm, row0, n_rows):
    route, ys, code0, code1, starts = moe
    d = x.shape[1]
    blk0 = row0 // tm
    return pl.pallas_call(
        functools.partial(_final_body, tile0=blk0),
        grid_spec=pltpu.PrefetchScalarGridSpec(
            num_scalar_prefetch=3, grid=(n_rows // tm,),
            in_specs=[pl.BlockSpec((tm, d), lambda i, *_: (blk0 + i, 0)),
                      pl.BlockSpec((tm, ROUTE_LANES), lambda i, *_: (blk0 + i, 0)),
                      pl.BlockSpec(memory_space=pl.ANY), pl.BlockSpec((1, d), lambda i, *_: (0, 0))],
            out_specs=pl.BlockSpec((tm, d), lambda i, *_: (i, 0)),
            scratch_shapes=_moe_scratch(tm, d)),
        out_shape=jax.ShapeDtypeStruct((n_rows, d), F32),
        compiler_params=_cparams(1), name="final_norm",
    )(code0, code1, starts, x, route, ys, g.reshape(1, d))


def kernel(x_prompt, x_sample, cache_a_g0_kv, cache_a_g1_kv, cache_a_g2_kv, state_b_h, state_b_conv, state_c_S, state_c_conv, t5_bias, norm_mix, norm_ffn, norm_final, e_w_in, e_conv_w, e_conv_b, e_rg_wa, e_rg_ba, e_rg_wx, e_rg_bx, e_rg_lambda, e_w_out, o_w_in, o_conv_w, o_a_log, o_dt_bias, o_onorm_w, o_w_out, moe_rg_w, moe_rg_b, moe_re_w, moe_re_b, moe_w_gate, moe_w_up, moe_w_down):
    n_batch, seq, d = x_prompt.shape
    n_seq, t_new, _ = x_sample.shape
    mp = n_batch * seq
    ms = n_seq * t_new
    m = mp + ms
    assert t_new == 8 and seq % (DIL_PAIRS[2][1] * A_BLOCK) == 0
    assert e_w_in.shape[0] == 1 and o_w_in.shape[0] == 1
    tm = _row_tile(mp, ms)
    x = jnp.concatenate([x_prompt.reshape(mp, d), x_sample.reshape(ms, d)], axis=0)

    def moe_weights(layer):
        shp = (N_EXPERTS, d, EXPERT_FF)
        return (moe_w_gate[layer].reshape(shp).astype(BF16), moe_w_up[layer].reshape(shp).astype(BF16),
                moe_w_down[layer].reshape(N_EXPERTS, EXPERT_FF, d).astype(BF16))

    def prompt_tail(a, keep, c0, c1):
        return jnp.stack([lax.slice(a, ((b + 1) * seq - keep, c0), ((b + 1) * seq, c1)) for b in range(n_batch)])

    def sample_rows(a, keep, c0, c1):
        return lax.slice(a, (mp, c0), (m, c1)).reshape(n_seq, t_new, c1 - c0)[:, t_new - keep:]

    proj, *qkv_rm = _norm_proj_even(x, norm_mix[0], e_w_in[0].astype(BF16), tm, n_batch, seq)
    attn_p = []
    for g, (_, dil) in enumerate(DIL_PAIRS):
        attn_p.extend(_attn_prompt(qkv_rm[g], t5_bias[:, g], g, dil, n_batch, seq))
    new_a = []
    for g, (win, _) in enumerate(DIL_PAIRS):
        c0 = g * 3 * A_WIDTH + A_WIDTH
        keep = min(win, seq)
        new_a.append(prompt_tail(proj, keep, c0, c0 + 2 * A_WIDTH).reshape(1, n_batch, keep, 2, A_HEADS, A_HEAD_DIM))
        new_a.append(sample_rows(proj, t_new, c0, c0 + 2 * A_WIDTH).reshape(1, n_seq, t_new, 2, A_HEADS, A_HEAD_DIM))
    attn_s = _attn_sample(proj, (cache_a_g0_kv, cache_a_g1_kv, cache_a_g2_kv), t5_bias, n_seq, t_new, mp)
    y_b, bh_p, bh_s = _rglru(proj, state_b_conv[0], state_b_h[0], e_conv_w[0], e_conv_b[0], e_rg_wa[0], e_rg_ba[0],
                             e_rg_wx[0], e_rg_bx[0], e_rg_lambda[0], n_batch, seq, n_seq, t_new)
    wr, br = _router_weights(moe_rg_w[0], moe_rg_b[0], moe_re_w[0], moe_re_b[0])
    x1, hn, route, counts = _even_out(attn_p, attn_s, y_b, x, [e_w_out[0].astype(BF16), norm_ffn[0].reshape(1, d), wr, br],
                                      tm, n_batch, seq)
    y_moe = _moe(hn, route, counts, *moe_weights(0), tm)

    w_in1 = jnp.pad(o_w_in[0], ((0, 0), (0, ODD_IN_PAD - ODD_IN))).astype(BF16)
    x2, proj2 = _norm_proj(x1, y_moe, norm_mix[1], w_in1, tm, 896)
    o_c, cs_p, cs_s = _gdn(proj2, state_c_conv[0], o_conv_w[0], o_a_log[0], o_dt_bias[0], o_onorm_w[0], state_c_S[0],
                           n_batch, seq, n_seq, t_new, 64)
    wr, br = _router_weights(moe_rg_w[1], moe_rg_b[1], moe_re_w[1], moe_re_b[1])
    x3, hn, route, counts = _odd_out(o_c, proj2, x2, [o_w_out[0].astype(BF16), norm_ffn[1].reshape(1, d), wr, br], tm)
    y_moe = _moe(hn, route, counts, *moe_weights(1), tm)
    y_p = _final_norm(x3, y_moe, norm_final, tm, 0, mp).reshape(n_batch, seq, d)
    y_s = _final_norm(x3, y_moe, norm_final, tm, mp, ms).reshape(n_seq, t_new, d)

    keep = CONV_W - 1
    bconv_p = prompt_tail(proj, keep, XB_COL, XB_COL + RNN_WIDTH)[None]
    bconv_s = sample_rows(proj, keep, XB_COL, XB_COL + RNN_WIDTH)[None]
    cconv_p = prompt_tail(proj2, keep, 0, C_CONV_DIM)[None]
    cconv_s = sample_rows(proj2, keep, 0, C_CONV_DIM)[None]
    return (y_p, y_s, *new_a, bh_p[None], bh_s[None], bconv_p, bconv_s, cs_p[None], cs_s[None], cconv_p, cconv_s)
```

```python
import functools
import math

import jax
import jax.numpy as jnp
import numpy as np
from jax import lax
from jax.experimental import pallas as pl
from jax.experimental.pallas import tpu as pltpu

F32 = jnp.float32
BF16 = jnp.bfloat16
EPS = 1e-6
NEG_INF = -1e30

D_MODEL = 1024
DIL_PAIRS = ((128, 1), (512, 4), (2048, 16))
N_DIL = 3
A_HEADS = 8
A_HEAD_DIM = 64
A_WIDTH = A_HEADS * A_HEAD_DIM
A_BLOCK = 128
SPAN = 128
NUM_BUCKETS = 32
MAX_DISTANCE = 2048
RNN_WIDTH = 512
RNN_BLOCKS = 8
CONV_W = 4
RG_C = 8.0
EVEN_IN = N_DIL * 3 * A_WIDTH + 2 * RNN_WIDTH
XB_COL = N_DIL * 3 * A_WIDTH
GB_COL = XB_COL + RNN_WIDTH
C_QK_HEADS = 8
C_V_HEADS = 16
C_DK = 128
C_DV = 128
C_QK_W = C_QK_HEADS * C_DK
C_V_W = C_V_HEADS * C_DV
C_CONV_DIM = 2 * C_QK_W + C_V_W
ODD_IN = C_CONV_DIM + C_V_W + 2 * C_V_HEADS
ODD_IN_PAD = 6272
Z_COL = C_CONV_DIM
BG_COL = C_CONV_DIM + C_V_W
GDN_BASE = 8
N_GROUPS = 4
EXPERTS_PER_GROUP = 8
N_EXPERTS = N_GROUPS * EXPERTS_PER_GROUP
EXPERT_FF = 256
ROUTE_LANES = 128
MOE_TILE = 256

LANES = 128
VMEM_LIMIT = 56 * 1024 * 1024


def _cparams(n_grid):
    return pltpu.CompilerParams(dimension_semantics=("arbitrary",) * n_grid,
                                vmem_limit_bytes=VMEM_LIMIT)


def _rms(x, g):
    return x * lax.rsqrt(jnp.mean(x * x, axis=-1, keepdims=True) + EPS) * g


def _silu(x):
    return x * jax.nn.sigmoid(x)


def _softplus(x):
    return jnp.maximum(x, 0.0) + jnp.log1p(jnp.exp(-jnp.abs(x)))


def _dot(a, b):
    return jnp.dot(a, b, preferred_element_type=F32)


def _dot_nt(a, b):
    return lax.dot_general(a, b, (((1,), (1,)), ((), ())), preferred_element_type=F32)


def _dot_tn(a, b):
    return lax.dot_general(a, b, (((0,), (0,)), ((), ())), preferred_element_type=F32)


def _dot_f32(a, b):
    return jnp.dot(a, b, preferred_element_type=F32, precision=lax.Precision.HIGHEST)


def _row_tile(*counts):
    for t in (256, 128, 64, 32, 16, 8):
        if all(c % t == 0 for c in counts):
            return t
    raise ValueError("token counts must be multiples of 8")


def _row_copy(src_ref, src_row, dst_ref, dst_row, sem):
    return pltpu.make_async_copy(src_ref.at[pl.ds(src_row, 1)], dst_ref.at[pl.ds(dst_row, 1)], sem)


def _moe_residual(x_ref, route_ref, ys_ref, s0_ref, s1_ref, buf_ref, sems, tile0):
    i = pl.program_id(0)
    tm = x_ref.shape[0]

    def start(step, slot):
        base = (tile0 + step) * tm

        def gather_row(j, carry):
            _row_copy(ys_ref, s0_ref[base + j], buf_ref.at[slot, 0], j, sems.at[slot]).start()
            _row_copy(ys_ref, s1_ref[base + j], buf_ref.at[slot, 1], j, sems.at[slot]).start()
            return carry

        lax.fori_loop(0, tm, gather_row, 0, unroll=8)

    @pl.when(i == 0)
    def _():
        start(0, 0)

    @pl.when(i + 1 < pl.num_programs(0))
    def _():
        start(i + 1, (i + 1) % 2)

    slot = i % 2
    for k in range(2):
        pltpu.make_async_copy(ys_ref.at[pl.ds(0, tm)], buf_ref.at[slot, k], sems.at[slot]).wait()
    return x_ref[...] + route_ref[:, 2:3] * buf_ref[slot, 0] + route_ref[:, 3:4] * buf_ref[slot, 1]


def _moe_scratch(tm, d):
    return [pltpu.VMEM((2, 2, tm, d), F32), pltpu.SemaphoreType.DMA((2,))]


def _norm_proj_body(s0_ref, s1_ref, x_ref, route_ref, ys_ref, g_ref, w_ref, x_out_ref, o_ref, buf_ref, sems, *, col_chunk):
    x = _moe_residual(x_ref, route_ref, ys_ref, s0_ref, s1_ref, buf_ref, sems, 0)
    x_out_ref[...] = x
    hb = _rms(x, g_ref[...]).astype(BF16)
    for c0 in range(0, o_ref.shape[1], col_chunk):
        o_ref[:, c0:c0 + col_chunk] = _dot(hb, w_ref[:, c0:c0 + col_chunk])


def _norm_proj(x, moe, g, w, tm, col_chunk):
    route, ys, slot0, slot1 = moe
    m, d = x.shape
    n = w.shape[1]
    row = pl.BlockSpec((tm, d), lambda i, *_: (i, 0))
    return pl.pallas_call(
        functools.partial(_norm_proj_body, col_chunk=col_chunk),
        grid_spec=pltpu.PrefetchScalarGridSpec(
            num_scalar_prefetch=2, grid=(m // tm,),
            in_specs=[row, pl.BlockSpec((tm, ROUTE_LANES), lambda i, *_: (i, 0)), pl.BlockSpec(memory_space=pl.ANY),
                      pl.BlockSpec((1, d), lambda i, *_: (0, 0)), pl.BlockSpec((d, n), lambda i, *_: (0, 0))],
            out_specs=[row, pl.BlockSpec((tm, n), lambda i, *_: (i, 0))],
            scratch_shapes=_moe_scratch(tm, d)),
        out_shape=[jax.ShapeDtypeStruct((m, d), F32), jax.ShapeDtypeStruct((m, n), F32)],
        compiler_params=_cparams(1), name="norm_proj",
    )(slot0, slot1, x, route, ys, g.reshape(1, d), w)


def _norm_proj_even_body(x_ref, g_ref, w_ref, o_ref, rm0_ref, rm1_ref, rm2_ref, lane_ref):
    tm = x_ref.shape[0]
    hb = _rms(x_ref[...], g_ref[...]).astype(BF16)
    rm_refs = (rm0_ref, rm1_ref, rm2_ref)
    for c in range(EVEN_IN // A_WIDTH):
        cols = slice(c * A_WIDTH, (c + 1) * A_WIDTH)
        res = _dot(hb, w_ref[:, cols])
        o_ref[:, cols] = res
        if c < 3 * N_DIL:
            g, j = divmod(c, 3)
            dil = DIL_PAIRS[g][1]
            if dil == 1:
                rm_refs[g][0, 0, :, j * A_WIDTH:(j + 1) * A_WIDTH] = res.astype(BF16)
            else:
                for t in range(A_WIDTH // LANES):
                    lane_ref[t] = res[:, t * LANES:(t + 1) * LANES]
                for r in range(dil):
                    part = [lane_ref[t, pl.ds(r, tm // dil, stride=dil), :] for t in range(A_WIDTH // LANES)]
                    rm_refs[g][0, r, :, j * A_WIDTH:(j + 1) * A_WIDTH] = jnp.concatenate(part, axis=1).astype(BF16)


def _norm_proj_even(x, g, w, tm, n_batch, seq):
    m, d = x.shape
    n = w.shape[1]
    tps = seq // tm
    npt = n_batch * tps

    def rm_index(i):
        return (jnp.where(i < npt, i // tps, n_batch), 0, jnp.where(i < npt, i % tps, i - npt), 0)

    rm_shapes = [jax.ShapeDtypeStruct((n_batch + 1, dil, seq // dil, 3 * A_WIDTH), BF16) for _, dil in DIL_PAIRS]
    rm_specs = [pl.BlockSpec((1, dil, tm // dil, 3 * A_WIDTH), rm_index) for _, dil in DIL_PAIRS]
    return pl.pallas_call(
        _norm_proj_even_body,
        grid=(m // tm,),
        in_specs=[pl.BlockSpec((tm, d), lambda i: (i, 0)), pl.BlockSpec((1, d), lambda i: (0, 0)),
                  pl.BlockSpec((d, n), lambda i: (0, 0))],
        out_specs=[pl.BlockSpec((tm, n), lambda i: (i, 0))] + rm_specs,
        out_shape=[jax.ShapeDtypeStruct((m, n), F32)] + rm_shapes,
        scratch_shapes=[pltpu.VMEM((A_WIDTH // LANES, tm, LANES), F32)],
        compiler_params=_cparams(1), name="norm_proj_even",
    )(x, g.reshape(1, d), w)


def _t5_bucket(dist):
    max_exact = NUM_BUCKETS // 2
    d = np.maximum(dist, 1).astype(np.float32)
    large = max_exact + (np.log(d / max_exact) / np.log(MAX_DISTANCE / max_exact)
                         * (NUM_BUCKETS - max_exact)).astype(np.int32)
    large = np.minimum(large, NUM_BUCKETS - 1)
    return np.where(dist < max_exact, dist, large).astype(np.int32)


def _bucket_lookup(tab, buckets):
    onehot = jnp.asarray(buckets[..., None, None] == np.arange(NUM_BUCKETS)[:, None])
    return jnp.sum(jnp.where(onehot, tab.astype(F32), 0.0), axis=-2)


def _prompt_bias(tab, dil):
    qi = np.arange(A_BLOCK)[:, None]
    km = np.arange(2 * A_BLOCK)[None, :]
    delta = A_BLOCK + qi - km
    valid = (delta >= 0) & (delta <= SPAN)
    bias = _bucket_lookup(tab, _t5_bucket(np.clip(delta, 0, SPAN) * dil))
    bias = jnp.where(valid[..., None], bias, NEG_INF)
    return jnp.transpose(bias, (2, 0, 1))


def _attn_prompt_body(q_ref, kp_ref, ko_ref, vp_ref, vo_ref, bias_ref, o_ref, lse_ref):
    first = pl.program_id(2) == 0
    scale = A_HEAD_DIM ** -0.5
    n_sub = q_ref.shape[2] // A_BLOCK
    k_all = jnp.concatenate([kp_ref[0, 0], ko_ref[0, 0]], axis=0)
    v_all = jnp.concatenate([vp_ref[0, 0], vo_ref[0, 0]], axis=0)
    km = lax.broadcasted_iota(jnp.int32, (1, 2 * A_BLOCK), 1)
    no_prev = jnp.logical_and(first, km < A_BLOCK)
    heads_per_tile = LANES // A_HEAD_DIM
    head_of_lane = lax.broadcasted_iota(jnp.int32, (1, LANES), 1) // A_HEAD_DIM
    for sub in range(n_sub):
        rows = slice(sub * A_BLOCK, (sub + 1) * A_BLOCK)
        keys = slice(sub * A_BLOCK, (sub + 2) * A_BLOCK)
        for t in range(A_WIDTH // LANES):
            sl = slice(t * LANES, (t + 1) * LANES)
            q_t, k_t, v_t = q_ref[0, 0, rows, sl], k_all[keys, sl], v_all[keys, sl]
            o_t = jnp.zeros((A_BLOCK, LANES), F32)
            lse_t = jnp.zeros((A_BLOCK, LANES), F32)
            for j in range(heads_per_tile):
                mine = head_of_lane == j
                s = _dot_nt(jnp.where(mine, q_t, jnp.zeros_like(q_t)), k_t) * scale + bias_ref[t * heads_per_tile + j]
                if sub == 0:
                    s = jnp.where(no_prev, NEG_INF, s)
                m = jnp.max(s, axis=-1, keepdims=True)
                p = jnp.exp(s - m)
                den = jnp.sum(p, axis=-1, keepdims=True)
                o_t = jnp.where(mine, _dot(p.astype(BF16), v_t) / den, o_t)
                lse_t = jnp.where(mine, m + jnp.log(den), lse_t)
            o_ref[0, 0, rows, sl] = o_t
            lse_ref[0, 0, rows, sl] = lse_t


def _attn_prompt(qkv_rm, tab, g, dil, n_batch, seq):
    sub_len = seq // dil
    n_sub = _largest_divisor(sub_len // A_BLOCK, (2, 1))
    rows = n_sub * A_BLOCK
    nb = sub_len // rows

    def spec(j, prev):
        if prev:
            return pl.BlockSpec((1, 1, A_BLOCK, A_WIDTH), lambda b, r, i: (b, r, jnp.maximum(i * n_sub - 1, 0), j))
        return pl.BlockSpec((1, 1, rows, A_WIDTH), lambda b, r, i: (b, r, i, j))

    out_spec = pl.BlockSpec((1, 1, rows, A_WIDTH), lambda b, r, i: (b, r, i, 0))
    out_sds = jax.ShapeDtypeStruct((n_batch, dil, sub_len, A_WIDTH), F32)
    return pl.pallas_call(
        _attn_prompt_body,
        grid=(n_batch, dil, nb),
        in_specs=[spec(0, False), spec(1, True), spec(1, False), spec(2, True), spec(2, False),
                  pl.BlockSpec((A_HEADS, A_BLOCK, 2 * A_BLOCK), lambda b, r, i: (0, 0, 0))],
        out_specs=[out_spec, out_spec], out_shape=[out_sds, out_sds],
        compiler_params=_cparams(3), name="attn_prompt_g%d" % g,
    )(qkv_rm, qkv_rm, qkv_rm, qkv_rm, qkv_rm, _prompt_bias(tab, dil))


def _sample_bias(tab, dil, cache_len, key_index, t_new):
    t = np.arange(t_new)[:, None]
    dist = cache_len + t - key_index[None, :]
    valid = (dist >= 0) & (dist % dil == 0) & (dist <= SPAN * dil)
    bias = _bucket_lookup(tab, _t5_bucket(np.clip(dist, 0, SPAN * dil)))
    bias = jnp.where(valid[..., None], bias, NEG_INF)
    return jnp.transpose(bias, (2, 0, 1)).reshape(A_HEADS * t_new, key_index.shape[0])


def _attn_sample_body(new_ref, c0_ref, c1_ref, c2_ref, b0_ref, b1_ref, b2_ref, bn_ref, *out_refs, t_new):
    scale = A_HEAD_DIM ** -0.5
    rows = A_HEADS * t_new
    head_of_row = lax.broadcasted_iota(jnp.int32, (rows, A_WIDTH), 0) // t_new
    head_of_lane = lax.broadcasted_iota(jnp.int32, (rows, A_WIDTH), 1) // A_HEAD_DIM
    own = head_of_row == head_of_lane
    caches = (c0_ref, c1_ref, c2_ref)
    biases = (b0_ref, b1_ref, b2_ref)
    for g in range(N_DIL):
        base = g * 3 * A_WIDTH
        cache_len = caches[g].shape[-1]
        q = new_ref[:, base:base + A_WIDTH]
        k_new = new_ref[:, base + A_WIDTH:base + 2 * A_WIDTH].astype(BF16)
        v_new = new_ref[:, base + 2 * A_WIDTH:base + 3 * A_WIDTH].astype(BF16)
        q_bd = jnp.where(own, jnp.concatenate([q] * A_HEADS, axis=0), 0.0).astype(BF16)
        k_t = caches[g][0, 0, 0].reshape(A_WIDTH, cache_len).astype(BF16)
        v_t = caches[g][0, 0, 1].reshape(A_WIDTH, cache_len).astype(BF16)
        s_c = _dot(q_bd, k_t) * scale + biases[g][...]
        s_n = _dot_nt(q_bd, k_new) * scale + bn_ref[g]
        m = jnp.maximum(s_c.max(axis=-1, keepdims=True), s_n.max(axis=-1, keepdims=True))
        p_c = jnp.exp(s_c - m)
        p_n = jnp.exp(s_n - m)
        den = jnp.sum(p_c, axis=-1, keepdims=True) + jnp.sum(p_n, axis=-1, keepdims=True)
        acc = _dot_nt(p_c.astype(BF16), v_t) + _dot(p_n.astype(BF16), v_new)
        acc = jnp.where(own, acc / den, 0.0)
        lse = jnp.where(own, m + jnp.log(den), 0.0)
        o = acc[0:t_new]
        l = lse[0:t_new]
        for h in range(1, A_HEADS):
            o = o + acc[h * t_new:(h + 1) * t_new]
            l = l + lse[h * t_new:(h + 1) * t_new]
        out_refs[2 * g][...] = o
        out_refs[2 * g + 1][...] = l


def _attn_sample(proj, caches, t5_bias, n_seq, t_new, row0):
    cache_lens = [c.shape[2] for c in caches]
    caches_t = [jnp.transpose(c, (0, 1, 3, 4, 5, 2)) for c in caches]
    biases = [_sample_bias(t5_bias[:, g], DIL_PAIRS[g][1], cache_lens[g], np.arange(cache_lens[g]), t_new)
              for g in range(N_DIL)]
    bias_new = jnp.stack([_sample_bias(t5_bias[:, g], DIL_PAIRS[g][1], cache_lens[g],
                                       cache_lens[g] + np.arange(t_new), t_new) for g in range(N_DIL)])
    blk0 = row0 // t_new
    full = lambda a: pl.BlockSpec(a.shape, lambda b: (0,) * a.ndim)
    out_spec = pl.BlockSpec((t_new, A_WIDTH), lambda b: (b, 0))
    cache_specs = [pl.BlockSpec((1, 1, 2, A_HEADS, A_HEAD_DIM, n), lambda b: (0, b, 0, 0, 0, 0)) for n in cache_lens]
    return pl.pallas_call(
        functools.partial(_attn_sample_body, t_new=t_new),
        grid=(n_seq,),
        in_specs=[pl.BlockSpec((t_new, N_DIL * 3 * A_WIDTH), lambda b: (blk0 + b, 0))] + cache_specs
                 + [full(biases[0]), full(biases[1]), full(biases[2]), full(bias_new)],
        out_specs=[out_spec] * (2 * N_DIL),
        out_shape=[jax.ShapeDtypeStruct((n_seq * t_new, A_WIDTH), F32)] * (2 * N_DIL),
        compiler_params=_cparams(1), name="attn_sample",
    )(proj, *caches_t, *biases, bias_new)


def _shift_rows(x, s, fill, axis):
    t = lax.broadcasted_iota(jnp.int32, x.shape, axis)
    return jnp.where(t >= s, pltpu.roll(x, s, axis), fill)


def _linear_scan(a, b, axis):
    n = a.shape[axis]
    s = 1
    while s < n:
        b = b + a * _shift_rows(b, s, 0.0, axis)
        a = a * _shift_rows(a, s, 1.0, axis)
        s *= 2
    return a, b


def _rglru_gates(xc, wa_ref, wx_ref, vec_ref):
    xcb = xc.astype(BF16)
    r = jax.nn.sigmoid(_dot(xcb, wa_ref[...]) + vec_ref[1:2])
    ig = jax.nn.sigmoid(_dot(xcb, wx_ref[...]) + vec_ref[2:3])
    log_a = -RG_C * r * _softplus(-vec_ref[3:4])
    a = jnp.exp(log_a)
    b = jnp.sqrt(1.0 - jnp.exp(2.0 * log_a)) * (ig * xc)
    return a, b


def _gelu(x):
    return 0.5 * x * (1.0 + jnp.tanh(math.sqrt(2.0 / math.pi) * (x + 0.044715 * (x * x * x))))


def _rglru_prompt_body(xb_ref, gb_ref, cw_ref, wa_ref, wx_ref, vec_ref, y_ref, hl_ref, tail_ref, h_ref):
    @pl.when(pl.program_id(1) == 0)
    def _():
        tail_ref[...] = jnp.zeros_like(tail_ref)
        h_ref[...] = jnp.zeros_like(h_ref)

    x = xb_ref[...]
    tt = x.shape[0]
    xe = jnp.concatenate([tail_ref[...], x], axis=0)
    xc = vec_ref[0:1] + x * cw_ref[CONV_W - 1:CONV_W]
    for j in range(1, CONV_W):
        xc = xc + pltpu.roll(xe, j, 0)[8:8 + tt] * cw_ref[CONV_W - 1 - j:CONV_W - j]
    tail_ref[...] = x[tt - 8:tt]
    a, b = _rglru_gates(xc, wa_ref, wx_ref, vec_ref)
    a_cum, h = _linear_scan(a, b, 0)
    h = h + a_cum * h_ref[...]
    h_ref[...] = h[tt - 1:tt]
    hl_ref[0] = h[tt - 1:tt]
    y_ref[...] = h * _gelu(gb_ref[...])


def _rglru_sample_body(xb_ref, gb_ref, c0_ref, h0_ref, cw_ref, wa_ref, wx_ref, vec_ref, prev_ref, y_ref, hl_ref, *, t_new):
    del prev_ref
    x = xb_ref[...]
    rows = x.shape[0]
    ns = rows // t_new
    x3 = x.reshape(ns, t_new, RNN_WIDTH)
    xe = jnp.concatenate([c0_ref[...], x3], axis=1)
    xc = vec_ref[0:1] + x3 * cw_ref[CONV_W - 1:CONV_W]
    for j in range(1, CONV_W):
        xc = xc + pltpu.roll(xe, j, 1)[:, 8:8 + t_new] * cw_ref[CONV_W - 1 - j:CONV_W - j]
    a, b = _rglru_gates(xc.reshape(rows, RNN_WIDTH), wa_ref, wx_ref, vec_ref)
    a_cum, h = _linear_scan(a.reshape(ns, t_new, RNN_WIDTH), b.reshape(ns, t_new, RNN_WIDTH), 1)
    h = h + a_cum * h0_ref[...][:, None, :]
    hl_ref[...] = h[:, t_new - 1, :]
    y_ref[...] = h.reshape(rows, RNN_WIDTH) * _gelu(gb_ref[...])


def _block_diag(w):
    nb, bi, bo = w.shape
    eye = jnp.eye(nb, dtype=w.dtype)
    return (w[:, :, None, :] * eye[:, None, :, None]).reshape(nb * bi, nb * bo)


def _rglru(proj, conv0, h0, conv_w, conv_b, wa, ba, wx, bx, lam, n_batch, seq, n_seq, t_new):
    m_total = proj.shape[0]
    mp = n_batch * seq
    wa_bd = _block_diag(wa).astype(BF16)
    wx_bd = _block_diag(wx).astype(BF16)
    vec = jnp.stack([conv_b, ba, bx, lam]).astype(F32)
    xcol = XB_COL // RNN_WIDTH
    gcol = GB_COL // RNN_WIDTH
    full2 = lambda a, nd: pl.BlockSpec(a.shape, lambda *_: (0,) * a.ndim)
    tt = _row_tile(seq)
    nt = seq // tt
    w_specs2 = [pl.BlockSpec(a.shape, lambda b, i: (0, 0)) for a in (conv_w, wa_bd, wx_bd, vec)]
    y, hl_p = pl.pallas_call(
        _rglru_prompt_body,
        grid=(n_batch, nt),
        in_specs=[pl.BlockSpec((tt, RNN_WIDTH), lambda b, i: (b * nt + i, xcol)),
                  pl.BlockSpec((tt, RNN_WIDTH), lambda b, i: (b * nt + i, gcol))] + w_specs2,
        out_specs=[pl.BlockSpec((tt, RNN_WIDTH), lambda b, i: (b * nt + i, 0)),
                   pl.BlockSpec((1, 1, RNN_WIDTH), lambda b, i: (b, 0, 0))],
        out_shape=[jax.ShapeDtypeStruct((m_total, RNN_WIDTH), F32),
                   jax.ShapeDtypeStruct((n_batch, 1, RNN_WIDTH), F32)],
        scratch_shapes=[pltpu.VMEM((8, RNN_WIDTH), F32), pltpu.VMEM((1, RNN_WIDTH), F32)],
        compiler_params=_cparams(2), name="rglru_prompt",
    )(proj, proj, conv_w, wa_bd, wx_bd, vec)
    del full2
    ts = 32 if n_seq % 32 == 0 else 8
    rows = ts * t_new
    blk0 = mp // rows
    c0p = jnp.pad(conv0, ((0, 0), (8 - (CONV_W - 1), 0), (0, 0)))
    w_specs1 = [pl.BlockSpec(a.shape, lambda i: (0, 0)) for a in (conv_w, wa_bd, wx_bd, vec)]
    y, hl_s = pl.pallas_call(
        functools.partial(_rglru_sample_body, t_new=t_new),
        grid=(n_seq // ts,),
        in_specs=[pl.BlockSpec((rows, RNN_WIDTH), lambda i: (blk0 + i, xcol)),
                  pl.BlockSpec((rows, RNN_WIDTH), lambda i: (blk0 + i, gcol)),
                  pl.BlockSpec((ts, 8, RNN_WIDTH), lambda i: (i, 0, 0)),
                  pl.BlockSpec((ts, RNN_WIDTH), lambda i: (i, 0))] + w_specs1
                 + [pl.BlockSpec(memory_space=pl.ANY)],
        out_specs=[pl.BlockSpec((rows, RNN_WIDTH), lambda i: (blk0 + i, 0)),
                   pl.BlockSpec((ts, RNN_WIDTH), lambda i: (i, 0))],
        out_shape=[jax.ShapeDtypeStruct((m_total, RNN_WIDTH), F32),
                   jax.ShapeDtypeStruct((n_seq, RNN_WIDTH), F32)],
        input_output_aliases={8: 0},
        compiler_params=_cparams(1), name="rglru_sample",
    )(proj, proj, c0p, h0, conv_w, wa_bd, wx_bd, vec, y)
    return y, hl_p.reshape(n_batch, RNN_WIDTH), hl_s


def _pack_bf16_pair(x):
    w = x.shape[1] // 2
    hi = lax.bitcast_convert_type(x[:, :w].astype(BF16).astype(F32), jnp.uint32)
    lo = lax.bitcast_convert_type(x[:, w:].astype(BF16).astype(F32), jnp.uint32)
    return hi | (lo >> 16)


def _unpack_bf16_pair(p):
    hi = lax.bitcast_convert_type(p & jnp.uint32(0xFFFF0000), F32).astype(BF16)
    lo = lax.bitcast_convert_type(p << 16, F32).astype(BF16)
    return hi, lo


def _route(x1, gf_ref, wr_ref, br_ref, hn_ref, route_ref, counts_ref, run_ref):
    @pl.when(pl.program_id(0) == 0)
    def _():
        run_ref[...] = jnp.zeros_like(run_ref)

    hn = _rms(x1, gf_ref[...])
    hn_ref[...] = _pack_bf16_pair(hn)
    hn_hi = hn.astype(BF16)
    hn_lo = (hn - hn_hi.astype(F32)).astype(BF16)
    logits = _dot(hn_hi, wr_ref[0]) + (_dot(hn_hi, wr_ref[1]) + _dot(hn_lo, wr_ref[0])) + br_ref[...]
    lane = lax.broadcasted_iota(jnp.int32, logits.shape, 1)
    is_coarse = lane < N_GROUPS
    coarse = jnp.where(is_coarse, logits, -jnp.inf)
    cmax = jnp.max(coarse, axis=-1, keepdims=True)
    grp = jnp.min(jnp.where(coarse == cmax, lane, ROUTE_LANES), axis=-1, keepdims=True)
    p_grp = 1.0 / jnp.sum(jnp.where(is_coarse, jnp.exp(logits - cmax), 0.0), axis=-1, keepdims=True)
    expert = lane - N_GROUPS
    in_grp = (lane >= N_GROUPS) & (expert < N_EXPERTS) & (expert // EXPERTS_PER_GROUP == grp)
    fine = jnp.where(in_grp, logits, -jnp.inf)
    v1 = jnp.max(fine, axis=-1, keepdims=True)
    i1 = jnp.min(jnp.where(fine == v1, lane, ROUTE_LANES), axis=-1, keepdims=True)
    fine2 = jnp.where(lane == i1, -jnp.inf, fine)
    v2 = jnp.max(fine2, axis=-1, keepdims=True)
    i2 = jnp.min(jnp.where(fine2 == v2, lane, ROUTE_LANES), axis=-1, keepdims=True)
    e2 = jnp.exp(v2 - v1)
    w1 = p_grp / (1.0 + e2)
    w2 = p_grp * e2 / (1.0 + e2)
    tm = logits.shape[0]
    sel = jnp.where(lane == i1, 1.0, jnp.where(lane == i2, 1.0, 0.0))
    ri = lax.broadcasted_iota(jnp.int32, (tm, tm), 0)
    ci = lax.broadcasted_iota(jnp.int32, (tm, tm), 1)
    earlier = jnp.where(ri > ci, 1.0, 0.0).astype(BF16)
    before = _dot(earlier, sel.astype(BF16)) + run_ref[...]
    r1 = jnp.sum(jnp.where(lane == i1, before, 0.0), axis=-1, keepdims=True)
    r2 = jnp.sum(jnp.where(lane == i2, before, 0.0), axis=-1, keepdims=True)
    run_ref[...] = run_ref[...] + jnp.sum(sel, axis=0, keepdims=True)
    counts_ref[...] = run_ref[...]
    route = jnp.where(lane == 0, (i1 - N_GROUPS).astype(F32), 0.0)
    route = jnp.where(lane == 1, (i2 - N_GROUPS).astype(F32), route)
    route = jnp.where(lane == 2, w1, route)
    route = jnp.where(lane == 3, w2, route)
    route = jnp.where(lane == 4, r1, route)
    route = jnp.where(lane == 5, r2, route)
    route_ref[...] = route


def _merge_groups(os, ls):
    mx = jnp.maximum(jnp.maximum(ls[0], ls[1]), ls[2])
    es = [jnp.exp(l - mx) for l in ls]
    return (es[0] * os[0] + es[1] * os[1] + es[2] * os[2]) / (es[0] + es[1] + es[2])


def _even_out_body(o0, l0, o1, l1, o2, l2, so0, sl0, so1, sl1, so2, sl2, yb_ref, x_ref, wo_ref, gf_ref, wr_ref, br_ref,
                   x1_ref, hn_ref, route_ref, counts_ref, n1o, n1l, n2o, n2l, oa_ref, run_ref, *, n_prompt_tiles):
    i = pl.program_id(0)
    tm = x_ref.shape[0]

    @pl.when(i < n_prompt_tiles)
    def _():
        nat = []
        for src, dst, dil in ((o1, n1o, DIL_PAIRS[1][1]), (l1, n1l, DIL_PAIRS[1][1]),
                              (o2, n2o, DIL_PAIRS[2][1]), (l2, n2l, DIL_PAIRS[2][1])):
            for r in range(dil):
                blk = src[0, r]
                for t in range(A_WIDTH // LANES):
                    dst[t, pl.ds(r, tm // dil, stride=dil), :] = blk[:, t * LANES:(t + 1) * LANES]
            nat.append(jnp.concatenate([dst[t] for t in range(A_WIDTH // LANES)], axis=1))
        oa_ref[...] = _merge_groups((o0[0, 0], nat[0], nat[2]), (l0[0, 0], nat[1], nat[3]))

    @pl.when(i >= n_prompt_tiles)
    def _():
        oa_ref[...] = _merge_groups((so0[...], so1[...], so2[...]), (sl0[...], sl1[...], sl2[...]))

    cat = jnp.concatenate([oa_ref[...], yb_ref[...]], axis=-1).astype(BF16)
    x1 = x_ref[...] + _dot(cat, wo_ref[...])
    x1_ref[...] = x1
    _route(x1, gf_ref, wr_ref, br_ref, hn_ref, route_ref, counts_ref, run_ref)


def _router_weights(rg_w, rg_b, re_w, re_b):
    d = rg_w.shape[0]
    wr = jnp.concatenate([rg_w, re_w.reshape(d, N_EXPERTS)], axis=1)
    br = jnp.concatenate([rg_b, re_b.reshape(N_EXPERTS)])
    pad = ROUTE_LANES - wr.shape[1]
    wr = jnp.pad(wr, ((0, 0), (0, pad))).astype(F32)
    wr_hi = wr.astype(BF16)
    wr_lo = (wr - wr_hi.astype(F32)).astype(BF16)
    return jnp.stack([wr_hi, wr_lo]), jnp.pad(br, (0, pad)).reshape(1, ROUTE_LANES).astype(F32)


def _mix_out_call(body, in_specs, args, consts, m, d, tm, scratch, name):
    const_spec = lambda a: pl.BlockSpec(a.shape, lambda i: (0,) * a.ndim)
    return pl.pallas_call(
        body, grid=(m // tm,), in_specs=list(in_specs) + [const_spec(c) for c in consts],
        out_specs=[pl.BlockSpec((tm, d), lambda i: (i, 0)), pl.BlockSpec((tm, d // 2), lambda i: (i, 0)),
                   pl.BlockSpec((tm, ROUTE_LANES), lambda i: (i, 0)), pl.BlockSpec((1, ROUTE_LANES), lambda i: (0, 0))],
        out_shape=[jax.ShapeDtypeStruct((m, d), F32), jax.ShapeDtypeStruct((m, d // 2), jnp.uint32),
                   jax.ShapeDtypeStruct((m, ROUTE_LANES), F32), jax.ShapeDtypeStruct((1, ROUTE_LANES), F32)],
        scratch_shapes=list(scratch) + [pltpu.VMEM((1, ROUTE_LANES), F32)],
        compiler_params=_cparams(1), name=name,
    )(*args, *consts)


def _even_out(attn_p, attn_s, y_b, x, consts, tm, n_batch, seq):
    m, d = x.shape
    tps = seq // tm
    npt = n_batch * tps
    in_specs = []
    for g, (_, dil) in enumerate(DIL_PAIRS):
        def index(i):
            return (jnp.minimum(i // tps, n_batch - 1), 0, jnp.where(i < npt, i % tps, 0), 0)
        in_specs += [pl.BlockSpec((1, dil, tm // dil, A_WIDTH), index)] * 2
    in_specs += [pl.BlockSpec((tm, A_WIDTH), lambda i: (jnp.maximum(i - npt, 0), 0))] * (2 * N_DIL)
    in_specs += [pl.BlockSpec((tm, RNN_WIDTH), lambda i: (i, 0)), pl.BlockSpec((tm, d), lambda i: (i, 0))]
    scratch = [pltpu.VMEM((A_WIDTH // LANES, tm, LANES), F32)] * 4 + [pltpu.VMEM((tm, A_WIDTH), F32)]
    return _mix_out_call(functools.partial(_even_out_body, n_prompt_tiles=npt), in_specs,
                         list(attn_p) + list(attn_s) + [y_b, x], consts, m, d, tm, scratch, "even_out")


def _odd_out(o_c, proj2, x, consts, tm):
    m, d = x.shape
    in_specs = [pl.BlockSpec((tm, C_V_W), lambda i: (i, 0)),
                pl.BlockSpec((tm, C_V_W), lambda i: (i, Z_COL // C_V_W)),
                pl.BlockSpec((tm, d), lambda i: (i, 0))]
    return _mix_out_call(_odd_out_body, in_specs, [o_c, proj2, x], consts, m, d, tm, [], "odd_out")


def _moe_slots_body(route_ref, start_ref, o_ref):
    lane = lax.broadcasted_iota(jnp.int32, route_ref.shape, 1)
    r = route_ref[...]
    starts = start_ref[...]

    def slot(expert, pos):
        return jnp.sum(jnp.where(lane == expert.astype(jnp.int32), starts, 0.0), axis=-1, keepdims=True) + pos

    s0 = slot(r[:, 0:1], r[:, 4:5])
    s1 = slot(r[:, 1:2], r[:, 5:6])
    o_ref[...] = jnp.where(lane == 0, s0, jnp.where(lane == 1, s1, 0.0)).astype(jnp.int32)


def _moe_dispatch_body(s0_ref, s1_ref, fill_ref, used_ref, hn_ref, xs_ref, zero_ref, fill_sem, row_sem):
    i = pl.program_id(0)
    tm = hn_ref.shape[0]

    def fill_copy(e):
        return pltpu.make_async_copy(zero_ref, xs_ref.at[pl.ds(pl.multiple_of(fill_ref[e], MOE_TILE), MOE_TILE)], fill_sem)

    @pl.when(i == 0)
    def _():
        zero_ref[...] = jnp.zeros_like(zero_ref)
        for e in range(N_EXPERTS):
            @pl.when(used_ref[e] > 0)
            def _():
                fill_copy(e).start()
        for e in range(N_EXPERTS):
            @pl.when(used_ref[e] > 0)
            def _():
                fill_copy(e).wait()

    base = i * tm

    def scatter_row(j, carry):
        _row_copy(hn_ref, j, xs_ref, s0_ref[base + j], row_sem).start()
        _row_copy(hn_ref, j, xs_ref, s1_ref[base + j], row_sem).start(priority=1)
        return carry

    lax.fori_loop(0, tm, scatter_row, 0, unroll=8)
    for _ in range(2):
        pltpu.make_async_copy(hn_ref, xs_ref.at[pl.ds(0, tm)], row_sem).wait()


def _moe_ffn_body(te_ref, nu_ref, xs_ref, wg_ref, wu_ref, wd_ref, o_ref):
    del te_ref

    @pl.when(pl.program_id(0) < nu_ref[0])
    def _():
        xa, xb = _unpack_bf16_pair(xs_ref[...])
        half = xa.shape[1]
        hg = _dot(xa, wg_ref[0, :half]) + _dot(xb, wg_ref[0, half:])
        hu = _dot(xa, wu_ref[0, :half]) + _dot(xb, wu_ref[0, half:])
        o_ref[...] = _dot((_silu(hg) * hu).astype(BF16), wd_ref[0])

    @pl.when(pl.program_id(0) >= nu_ref[0])
    def _():
        o_ref[...] = jnp.zeros_like(o_ref)


def _moe(hn, route, counts, w_gate, w_up, w_down, tm):
    m = hn.shape[0]
    d = w_gate.shape[1]
    tile = MOE_TILE
    n_slots = 2 * m + N_EXPERTS * tile
    n_tiles = n_slots // tile
    assert n_slots < 1 << 24, "slots must stay exact in f32"
    cnt = counts[0, N_GROUPS:N_GROUPS + N_EXPERTS].astype(jnp.int32)
    padded = ((cnt + tile - 1) // tile) * tile
    pad_end = jnp.cumsum(padded)
    pad_start = (pad_end - padded).astype(jnp.int32)
    starts = jnp.zeros((1, ROUTE_LANES), F32).at[0, :N_EXPERTS].set(pad_start.astype(F32))
    slots = pl.pallas_call(
        _moe_slots_body, grid=(m // tm,),
        in_specs=[pl.BlockSpec((tm, ROUTE_LANES), lambda i: (i, 0)), pl.BlockSpec((1, ROUTE_LANES), lambda i: (0, 0))],
        out_specs=pl.BlockSpec((tm, ROUTE_LANES), lambda i: (i, 0)),
        out_shape=jax.ShapeDtypeStruct((m, ROUTE_LANES), jnp.int32),
        compiler_params=_cparams(1), name="moe_slots",
    )(route, starts)
    slot0 = slots[:, 0]
    slot1 = slots[:, 1]
    tile_start = jnp.arange(n_tiles, dtype=jnp.int32) * tile
    tile_expert = jnp.sum((tile_start[:, None] >= pad_end[None, :]).astype(jnp.int32), axis=1)
    tile_expert = jnp.minimum(tile_expert, N_EXPERTS - 1).astype(jnp.int32)
    n_used = (pad_end[-1:] // tile).astype(jnp.int32)
    fill_start = jnp.maximum(pad_end - tile, 0).astype(jnp.int32)

    xs = pl.pallas_call(
        _moe_dispatch_body,
        grid_spec=pltpu.PrefetchScalarGridSpec(
            num_scalar_prefetch=4, grid=(m // tm,),
            in_specs=[pl.BlockSpec((tm, d // 2), lambda i, *_: (i, 0))],
            out_specs=pl.BlockSpec(memory_space=pl.ANY),
            scratch_shapes=[pltpu.VMEM((tile, d // 2), jnp.uint32), pltpu.SemaphoreType.DMA(()),
                            pltpu.SemaphoreType.DMA(())]),
        out_shape=jax.ShapeDtypeStruct((n_slots, d // 2), jnp.uint32),
        compiler_params=_cparams(1), name="moe_dispatch",
    )(slot0, slot1, fill_start, cnt, hn)

    def used(i, nu):
        return jnp.minimum(i, jnp.maximum(nu[0] - 1, 0))

    ys = pl.pallas_call(
        _moe_ffn_body,
        grid_spec=pltpu.PrefetchScalarGridSpec(
            num_scalar_prefetch=2, grid=(n_tiles,),
            in_specs=[pl.BlockSpec((tile, d // 2), lambda i, te, nu: (used(i, nu), 0)),
                      pl.BlockSpec((1, d, EXPERT_FF), lambda i, te, nu: (te[used(i, nu)], 0, 0)),
                      pl.BlockSpec((1, d, EXPERT_FF), lambda i, te, nu: (te[used(i, nu)], 0, 0)),
                      pl.BlockSpec((1, EXPERT_FF, d), lambda i, te, nu: (te[used(i, nu)], 0, 0))],
            out_specs=pl.BlockSpec((tile, d), lambda i, te, nu: (i, 0))),
        out_shape=jax.ShapeDtypeStruct((n_slots, d), F32),
        compiler_params=_cparams(1), name="moe_ffn",
    )(tile_expert, n_used, xs, w_gate, w_up, w_down)

    return route, ys, slot0, slot1


def _gdn_prep_math(xe, tt, cw_ref, bg, av_ref):
    acc = xe[8:8 + tt] * cw_ref[CONV_W - 1:CONV_W]
    for j in range(1, CONV_W):
        acc = acc + pltpu.roll(xe, j, 0)[8:8 + tt] * cw_ref[CONV_W - 1 - j:CONV_W - j]
    qkv = _silu(acc)
    outs = []
    for h in range(2 * C_QK_HEADS):
        xh = qkv[:, h * C_DK:(h + 1) * C_DK]
        xh = xh * lax.rsqrt(jnp.sum(xh * xh, axis=-1, keepdims=True) + EPS)
        if h < C_QK_HEADS:
            xh = xh * (C_DK ** -0.5)
        outs.append(xh)
    qk = jnp.concatenate(outs, axis=-1)
    v = qkv[:, 2 * C_QK_W:]
    lane = lax.broadcasted_iota(jnp.int32, bg.shape, 1)
    gdec = -jnp.exp(av_ref[0:1]) * _softplus(bg + av_ref[1:2])
    bgo = jnp.where(lane < C_V_HEADS, jax.nn.sigmoid(bg), gdec)
    return qk, v, bgo


def _unit_lower_inverses(l_bds, size):
    n = l_bds[0].shape[0]
    ri = lax.broadcasted_iota(jnp.int32, (n, n), 0)
    ci = lax.broadcasted_iota(jnp.int32, (n, n), 1)
    eye = jnp.where(ri == ci, 1.0, 0.0).astype(F32)
    base = ri // GDN_BASE == ci // GDN_BASE
    ps = [jnp.where(base, -l, 0.0) for l in l_bds]
    ts = [eye + p for p in ps]
    s = 2
    while s < GDN_BASE:
        ps = [_dot(p.astype(BF16), p.astype(BF16)) for p in ps]
        ts = [t + _dot(t.astype(BF16), p.astype(BF16)) for t, p in zip(ts, ps)]
        s *= 2
    s = GDN_BASE
    while s < size:
        lower_left = ((ri // s) % 2 == 1) & (ci // s == ri // s - 1)
        tbs = [t.astype(BF16) for t in ts]
        mids = [_dot(jnp.where(lower_left, l, 0.0).astype(BF16), tb).astype(BF16) for l, tb in zip(l_bds, tbs)]
        ts = [t - _dot(tb, mid) for t, tb, mid in zip(ts, tbs, mids)]
        s *= 2
    return ts


def _gdn_body(*refs, chunk, pack, n_par, has_state):
    c = chunk
    x_refs = refs[0:2 * n_par:2]
    bg_refs = refs[1:2 * n_par:2]
    pos = 2 * n_par
    if has_state:
        c0_ref = refs[pos]
        pos += 1
    cw_ref, av_ref, nw_ref = refs[pos:pos + 3]
    pos += 3
    if has_state:
        s0_ref = refs[pos]
        pos += 2
    o_ref, sout_ref, s_ref = refs[pos:pos + 3]
    tail_ref = None if has_state else refs[pos + 3]

    @pl.when(pl.program_id(1) == 0)
    def _():
        if has_state:
            s_ref[...] = s0_ref[...]
        else:
            s_ref[...] = jnp.zeros_like(s_ref)
            tail_ref[...] = jnp.zeros_like(tail_ref)

    rep = C_V_HEADS // C_QK_HEADS
    ti = lax.broadcasted_iota(jnp.int32, (c, c), 0)
    tj = lax.broadcasted_iota(jnp.int32, (c, c), 1)
    tril = jnp.where(ti >= tj, 1.0, 0.0).astype(F32)
    n = pack * c
    ri = lax.broadcasted_iota(jnp.int32, (n, n), 0)
    ci = lax.broadcasted_iota(jnp.int32, (n, n), 1)
    same = ri // c == ci // c
    incl = same & (ri >= ci)
    strict = same & (ri > ci)

    def col(a, heads):
        return jnp.concatenate([a[:, h:h + 1] for h in heads], axis=0)

    items, g_ends = [], []
    l_bds, qk_bds, rhss, qgs, kends = [], [], [], [], []
    for k in range(n_par):
        x = x_refs[k][...]
        prev = c0_ref[k] if has_state else tail_ref[k]
        if not has_state:
            tail_ref[k] = x[c - 8:c]
        qk, v, bgo = _gdn_prep_math(jnp.concatenate([prev, x], axis=0), c, cw_ref, bg_refs[k][...], av_ref)
        beta = bgo[:, 0:C_V_HEADS]
        gc = _dot_f32(tril, bgo[:, C_V_HEADS:2 * C_V_HEADS])
        g_last = gc[c - 1:c]
        gam = jnp.exp(gc)
        kdec = jnp.exp(g_last - gc)
        g_ends.append(jnp.exp(g_last))
        for p0 in range(0, C_V_HEADS, pack):
            heads = range(p0, p0 + pack)
            items.append((k, heads))
            g_col = col(gc, heads)
            g_row = jnp.sum(jnp.where(ri == ci, g_col, 0.0), axis=0, keepdims=True)
            b_col = col(beta, heads)
            gam_col = col(gam, heads)
            k_st = jnp.concatenate([qk[:, C_QK_W + (h // rep) * C_DK:C_QK_W + (h // rep + 1) * C_DK] for h in heads], axis=0)
            q_st = jnp.concatenate([qk[:, (h // rep) * C_DK:(h // rep + 1) * C_DK] for h in heads], axis=0)
            v_st = jnp.concatenate([v[:, h * C_DV:(h + 1) * C_DV] for h in heads], axis=0)
            kb = k_st.astype(BF16)
            decay = jnp.exp(jnp.where(incl, g_col - g_row, -jnp.inf))
            l_bds.append(jnp.where(strict, b_col * _dot_nt(kb, kb) * decay, 0.0))
            qk_bds.append((_dot_nt(q_st.astype(BF16), kb) * decay).astype(BF16))
            rhss.append(jnp.concatenate([b_col * v_st, (b_col * gam_col) * k_st], axis=1).astype(BF16))
            qgs.append(q_st * gam_col)
            kends.append((k_st * col(kdec, heads)).astype(BF16))
    t_invs = _unit_lower_inverses(l_bds, c)
    uws = [_dot(t.astype(BF16), rhs) for t, rhs in zip(t_invs, rhss)]
    u_sts, q_sts = [], []
    for (k, heads), uw, qg in zip(items, uws, qgs):
        us, qs = [], []
        for i, h in enumerate(heads):
            rows = slice(i * c, (i + 1) * c)
            lhs = jnp.concatenate([uw[rows, C_DV:], qg[rows]], axis=0).astype(BF16)
            ws = _dot(lhs, s_ref[k, h].astype(BF16))
            us.append(uw[rows, 0:C_DV] - ws[0:c])
            qs.append(ws[c:])
        u_sts.append(jnp.concatenate(us, axis=0).astype(BF16))
        q_sts.append(jnp.concatenate(qs, axis=0))
    o_sts = [q + _dot(qk_bd, ub) for q, qk_bd, ub in zip(q_sts, qk_bds, u_sts)]
    for (k, heads), o_st, ub, kendb in zip(items, o_sts, u_sts, kends):
        for i, h in enumerate(heads):
            rows = slice(i * c, (i + 1) * c)
            s_ref[k, h] = s_ref[k, h] * g_ends[k][:, h:h + 1] + _dot_tn(kendb[rows], ub[rows])
            o_h = _rms(o_st[rows], nw_ref[...])
            if has_state:
                o_ref[k * c:(k + 1) * c, h * C_DV:(h + 1) * C_DV] = o_h
            else:
                o_ref[k, 0, :, h * C_DV:(h + 1) * C_DV] = o_h

    @pl.when(pl.program_id(1) == pl.num_programs(1) - 1)
    def _():
        sout_ref[...] = s_ref[...]


def _largest_divisor(n, candidates):
    return next(c for c in candidates if n % c == 0)


def _gdn(proj, conv0, conv_w, a_log, dt_bias, onorm_w, s0, n_batch, seq, n_seq, t_new, chunk):
    mp = n_batch * seq
    nc = seq // chunk
    av = jnp.zeros((2, 128), F32)
    av = av.at[0, C_V_HEADS:2 * C_V_HEADS].set(a_log).at[1, C_V_HEADS:2 * C_V_HEADS].set(dt_bias)
    nw = onorm_w.reshape(1, C_DV).astype(F32)
    bg_col = BG_COL // 128
    state = (C_V_HEADS, C_DK, C_DV)
    consts = [conv_w, av, nw]
    const_specs = [pl.BlockSpec(a.shape, lambda b, i: (0, 0)) for a in consts]

    par_p = _largest_divisor(n_batch, (2, 1))
    seq_specs = []
    for k in range(par_p):
        seq_specs += [pl.BlockSpec((chunk, C_CONV_DIM), lambda b, i, k=k: ((b * par_p + k) * nc + i, 0)),
                      pl.BlockSpec((chunk, 128), lambda b, i, k=k: ((b * par_p + k) * nc + i, bg_col))]
    extra = -(-(n_seq * t_new) // seq)
    o4, s_p = pl.pallas_call(
        functools.partial(_gdn_body, chunk=chunk, pack=256 // chunk, n_par=par_p, has_state=False),
        grid=(n_batch // par_p, nc),
        in_specs=seq_specs + const_specs,
        out_specs=[pl.BlockSpec((par_p, 1, chunk, C_V_W), lambda b, i: (b, i, 0, 0)),
                   pl.BlockSpec((par_p,) + state, lambda b, i: (b, 0, 0, 0))],
        out_shape=[jax.ShapeDtypeStruct((n_batch + extra, nc, chunk, C_V_W), F32),
                   jax.ShapeDtypeStruct((n_batch,) + state, F32)],
        scratch_shapes=[pltpu.VMEM((par_p,) + state, F32), pltpu.VMEM((par_p, 8, C_CONV_DIM), F32)],
        compiler_params=_cparams(2), name="gdn_prompt",
    )(*([proj, proj] * par_p), *consts)
    o = o4.reshape((n_batch + extra) * seq, C_V_W)

    par_s = _largest_divisor(n_seq, (4, 2, 1))
    blk0 = mp // t_new
    c0p = jnp.pad(conv0, ((0, 0), (8 - (CONV_W - 1), 0), (0, 0)))
    seq_specs = []
    for k in range(par_s):
        seq_specs += [pl.BlockSpec((t_new, C_CONV_DIM), lambda b, i, k=k: (blk0 + b * par_s + k, 0)),
                      pl.BlockSpec((t_new, 128), lambda b, i, k=k: (blk0 + b * par_s + k, bg_col))]
    n_in = 2 * par_s + 1 + len(consts) + 1
    o, s_s = pl.pallas_call(
        functools.partial(_gdn_body, chunk=t_new, pack=C_V_HEADS, n_par=par_s, has_state=True),
        grid=(n_seq // par_s, 1),
        in_specs=seq_specs + [pl.BlockSpec((par_s, 8, C_CONV_DIM), lambda b, i: (b, 0, 0))] + const_specs
                 + [pl.BlockSpec((par_s,) + state, lambda b, i: (b, 0, 0, 0)), pl.BlockSpec(memory_space=pl.ANY)],
        out_specs=[pl.BlockSpec((par_s * t_new, C_V_W), lambda b, i: (blk0 // par_s + b, 0)),
                   pl.BlockSpec((par_s,) + state, lambda b, i: (b, 0, 0, 0))],
        out_shape=[jax.ShapeDtypeStruct(o.shape, F32), jax.ShapeDtypeStruct((n_seq,) + state, F32)],
        scratch_shapes=[pltpu.VMEM((par_s,) + state, F32)],
        input_output_aliases={n_in: 0},
        compiler_params=_cparams(2), name="gdn_sample",
    )(*([proj, proj] * par_s), c0p, *consts, s0, o)
    return o, s_p, s_s


def _odd_out_body(o_ref, z_ref, x_ref, wo_ref, gf_ref, wr_ref, br_ref, x1_ref, hn_ref, route_ref, counts_ref, run_ref):
    y = (o_ref[...] * _silu(z_ref[...])).astype(BF16)
    x1 = x_ref[...] + _dot(y, wo_ref[...])
    x1_ref[...] = x1
    _route(x1, gf_ref, wr_ref, br_ref, hn_ref, route_ref, counts_ref, run_ref)


def _final_body(s0_ref, s1_ref, x_ref, route_ref, ys_ref, g_ref, y_ref, buf_ref, sems, *, tile0):
    x = _moe_residual(x_ref, route_ref, ys_ref, s0_ref, s1_ref, buf_ref, sems, tile0)
    y_ref[...] = _rms(x, g_ref[...])


def _final_norm(x, moe, g, tm, row0, n_rows):
    route, ys, slot0, slot1 = moe
    d = x.shape[1]
    blk0 = row0 // tm
    return pl.pallas_call(
        functools.partial(_final_body, tile0=blk0),
        grid_spec=pltpu.PrefetchScalarGridSpec(
            num_scalar_prefetch=2, grid=(n_rows // tm,),
            in_specs=[pl.BlockSpec((tm, d), lambda i, *_: (blk0 + i, 0)),
                      pl.BlockSpec((tm, ROUTE_LANES), lambda i, *_: (blk0 + i, 0)),
                      pl.BlockSpec(memory_space=pl.ANY), pl.BlockSpec((1, d), lambda i, *_: (0, 0))],
            out_specs=pl.BlockSpec((tm, d), lambda i, *_: (i, 0)),
            scratch_shapes=_moe_scratch(tm, d)),
        out_shape=jax.ShapeDtypeStruct((n_rows, d), F32),
        compiler_params=_cparams(1), name="final_norm",
    )(slot0, slot1, x, route, ys, g.reshape(1, d))


def kernel(x_prompt, x_sample, cache_a_g0_kv, cache_a_g1_kv, cache_a_g2_kv, state_b_h, state_b_conv, state_c_S, state_c_conv, t5_bias, norm_mix, norm_ffn, norm_final, e_w_in, e_conv_w, e_conv_b, e_rg_wa, e_rg_ba, e_rg_wx, e_rg_bx, e_rg_lambda, e_w_out, o_w_in, o_conv_w, o_a_log, o_dt_bias, o_onorm_w, o_w_out, moe_rg_w, moe_rg_b, moe_re_w, moe_re_b, moe_w_gate, moe_w_up, moe_w_down):
    n_batch, seq, d = x_prompt.shape
    n_seq, t_new, _ = x_sample.shape
    mp = n_batch * seq
    ms = n_seq * t_new
    m = mp + ms
    assert t_new == 8 and seq % (DIL_PAIRS[2][1] * A_BLOCK) == 0
    assert e_w_in.shape[0] == 1 and o_w_in.shape[0] == 1
    tm = _row_tile(mp, ms)
    x = jnp.concatenate([x_prompt.reshape(mp, d), x_sample.reshape(ms, d)], axis=0)

    def moe_weights(layer):
        shp = (N_EXPERTS, d, EXPERT_FF)
        return (moe_w_gate[layer].reshape(shp).astype(BF16), moe_w_up[layer].reshape(shp).astype(BF16),
                moe_w_down[layer].reshape(N_EXPERTS, EXPERT_FF, d).astype(BF16))

    def prompt_tail(a, keep, c0, c1):
        return jnp.stack([lax.slice(a, ((b + 1) * seq - keep, c0), ((b + 1) * seq, c1)) for b in range(n_batch)])

    def sample_rows(a, keep, c0, c1):
        return lax.slice(a, (mp, c0), (m, c1)).reshape(n_seq, t_new, c1 - c0)[:, t_new - keep:]

    proj, *qkv_rm = _norm_proj_even(x, norm_mix[0], e_w_in[0].astype(BF16), tm, n_batch, seq)
    attn_p = []
    for g, (_, dil) in enumerate(DIL_PAIRS):
        attn_p.extend(_attn_prompt(qkv_rm[g], t5_bias[:, g], g, dil, n_batch, seq))
    new_a = []
    for g, (win, _) in enumerate(DIL_PAIRS):
        c0 = g * 3 * A_WIDTH + A_WIDTH
        keep = min(win, seq)
        new_a.append(prompt_tail(proj, keep, c0, c0 + 2 * A_WIDTH).reshape(1, n_batch, keep, 2, A_HEADS, A_HEAD_DIM))
        new_a.append(sample_rows(proj, t_new, c0, c0 + 2 * A_WIDTH).reshape(1, n_seq, t_new, 2, A_HEADS, A_HEAD_DIM))
    attn_s = _attn_sample(proj, (cache_a_g0_kv, cache_a_g1_kv, cache_a_g2_kv), t5_bias, n_seq, t_new, mp)
    y_b, bh_p, bh_s = _rglru(proj, state_b_conv[0], state_b_h[0], e_conv_w[0], e_conv_b[0], e_rg_wa[0], e_rg_ba[0],
                             e_rg_wx[0], e_rg_bx[0], e_rg_lambda[0], n_batch, seq, n_seq, t_new)
    wr, br = _router_weights(moe_rg_w[0], moe_rg_b[0], moe_re_w[0], moe_re_b[0])
    x1, hn, route, counts = _even_out(attn_p, attn_s, y_b, x, [e_w_out[0].astype(BF16), norm_ffn[0].reshape(1, d), wr, br],
                                      tm, n_batch, seq)
    y_moe = _moe(hn, route, counts, *moe_weights(0), tm)

    w_in1 = jnp.pad(o_w_in[0], ((0, 0), (0, ODD_IN_PAD - ODD_IN))).astype(BF16)
    x2, proj2 = _norm_proj(x1, y_moe, norm_mix[1], w_in1, tm, 896)
    o_c, cs_p, cs_s = _gdn(proj2, state_c_conv[0], o_conv_w[0], o_a_log[0], o_dt_bias[0], o_onorm_w[0], state_c_S[0],
                           n_batch, seq, n_seq, t_new, 64)
    wr, br = _router_weights(moe_rg_w[1], moe_rg_b[1], moe_re_w[1], moe_re_b[1])
    x3, hn, route, counts = _odd_out(o_c, proj2, x2, [o_w_out[0].astype(BF16), norm_ffn[1].reshape(1, d), wr, br], tm)
    y_moe = _moe(hn, route, counts, *moe_weights(1), tm)
    y_p = _final_norm(x3, y_moe, norm_final, tm, 0, mp).reshape(n_batch, seq, d)
    y_s = _final_norm(x3, y_moe, norm_final, tm, mp, ms).reshape(n_seq, t_new, d)

    keep = CONV_W - 1
    bconv_p = prompt_tail(proj, keep, XB_COL, XB_COL + RNN_WIDTH)[None]
    bconv_s = sample_rows(proj, keep, XB_COL, XB_COL + RNN_WIDTH)[None]
    cconv_p = prompt_tail(proj2, keep, 0, C_CONV_DIM)[None]
    cconv_s = sample_rows(proj2, keep, 0, C_CONV_DIM)[None]
    return (y_p, y_s, *new_a, bh_p[None], bh_s[None], bconv_p, bconv_s, cs_p[None], cs_s[None], cconv_p, cconv_s)
```

```python
import functools
import math

import jax
import jax.numpy as jnp
import numpy as np
from jax import lax
from jax.experimental import pallas as pl
from jax.experimental.pallas import tpu as pltpu

F32 = jnp.float32
BF16 = jnp.bfloat16
EPS = 1e-6
NEG_INF = -1e30

D_MODEL = 1024
DIL_PAIRS = ((128, 1), (512, 4), (2048, 16))
N_DIL = 3
A_HEADS = 8
A_HEAD_DIM = 64
A_WIDTH = A_HEADS * A_HEAD_DIM
A_BLOCK = 128
SPAN = 128
NUM_BUCKETS = 32
MAX_DISTANCE = 2048
RNN_WIDTH = 512
RNN_BLOCKS = 8
CONV_W = 4
RG_C = 8.0
EVEN_IN = N_DIL * 3 * A_WIDTH + 2 * RNN_WIDTH
XB_COL = N_DIL * 3 * A_WIDTH
GB_COL = XB_COL + RNN_WIDTH
C_QK_HEADS = 8
C_V_HEADS = 16
C_DK = 128
C_DV = 128
C_QK_W = C_QK_HEADS * C_DK
C_V_W = C_V_HEADS * C_DV
C_CONV_DIM = 2 * C_QK_W + C_V_W
ODD_IN = C_CONV_DIM + C_V_W + 2 * C_V_HEADS
ODD_IN_PAD = 6272
Z_COL = C_CONV_DIM
BG_COL = C_CONV_DIM + C_V_W
GDN_BASE = 8
N_GROUPS = 4
EXPERTS_PER_GROUP = 8
N_EXPERTS = N_GROUPS * EXPERTS_PER_GROUP
EXPERT_FF = 256
ROUTE_LANES = 128
MOE_TILE = 256

LANES = 128
VMEM_LIMIT = 56 * 1024 * 1024


def _cparams(n_grid):
    return pltpu.CompilerParams(dimension_semantics=("arbitrary",) * n_grid,
                                vmem_limit_bytes=VMEM_LIMIT)


def _rms(x, g):
    return x * lax.rsqrt(jnp.mean(x * x, axis=-1, keepdims=True) + EPS) * g


def _silu(x):
    return x * jax.nn.sigmoid(x)


def _softplus(x):
    return jnp.maximum(x, 0.0) + jnp.log1p(jnp.exp(-jnp.abs(x)))


def _dot(a, b):
    return jnp.dot(a, b, preferred_element_type=F32)


def _dot_nt(a, b):
    return lax.dot_general(a, b, (((1,), (1,)), ((), ())), preferred_element_type=F32)


def _dot_tn(a, b):
    return lax.dot_general(a, b, (((0,), (0,)), ((), ())), preferred_element_type=F32)


def _dot_f32(a, b):
    return jnp.dot(a, b, preferred_element_type=F32, precision=lax.Precision.HIGHEST)


def _row_tile(*counts):
    for t in (256, 128, 64, 32, 16, 8):
        if all(c % t == 0 for c in counts):
            return t
    raise ValueError("token counts must be multiples of 8")


def _row_copy(src_ref, src_row, dst_ref, dst_row, sem):
    return pltpu.make_async_copy(src_ref.at[pl.ds(src_row, 1)], dst_ref.at[pl.ds(dst_row, 1)], sem)


def _moe_residual(x_ref, route_ref, ys_ref, s0_ref, s1_ref, buf_ref, sems, tile0):
    i = pl.program_id(0)
    tm = x_ref.shape[0]

    def start(step, slot):
        base = (tile0 + step) * tm

        def gather_row(j, carry):
            _row_copy(ys_ref, s0_ref[base + j], buf_ref.at[slot, 0], j, sems.at[slot]).start()
            _row_copy(ys_ref, s1_ref[base + j], buf_ref.at[slot, 1], j, sems.at[slot]).start()
            return carry

        lax.fori_loop(0, tm, gather_row, 0, unroll=8)

    @pl.when(i == 0)
    def _():
        start(0, 0)

    @pl.when(i + 1 < pl.num_programs(0))
    def _():
        start(i + 1, (i + 1) % 2)

    slot = i % 2
    for k in range(2):
        pltpu.make_async_copy(ys_ref.at[pl.ds(0, tm)], buf_ref.at[slot, k], sems.at[slot]).wait()
    return x_ref[...] + route_ref[:, 2:3] * buf_ref[slot, 0] + route_ref[:, 3:4] * buf_ref[slot, 1]


def _moe_scratch(tm, d):
    return [pltpu.VMEM((2, 2, tm, d), F32), pltpu.SemaphoreType.DMA((2,))]


def _norm_proj_body(s0_ref, s1_ref, x_ref, route_ref, ys_ref, g_ref, w_ref, x_out_ref, o_ref, buf_ref, sems, *, col_chunk):
    x = _moe_residual(x_ref, route_ref, ys_ref, s0_ref, s1_ref, buf_ref, sems, 0)
    x_out_ref[...] = x
    hb = _rms(x, g_ref[...]).astype(BF16)
    for c0 in range(0, o_ref.shape[1], col_chunk):
        o_ref[:, c0:c0 + col_chunk] = _dot(hb, w_ref[:, c0:c0 + col_chunk])


def _norm_proj(x, moe, g, w, tm, col_chunk):
    route, ys, slot0, slot1 = moe
    m, d = x.shape
    n = w.shape[1]
    row = pl.BlockSpec((tm, d), lambda i, *_: (i, 0))
    return pl.pallas_call(
        functools.partial(_norm_proj_body, col_chunk=col_chunk),
        grid_spec=pltpu.PrefetchScalarGridSpec(
            num_scalar_prefetch=2, grid=(m // tm,),
            in_specs=[row, pl.BlockSpec((tm, ROUTE_LANES), lambda i, *_: (i, 0)), pl.BlockSpec(memory_space=pl.ANY),
                      pl.BlockSpec((1, d), lambda i, *_: (0, 0)), pl.BlockSpec((d, n), lambda i, *_: (0, 0))],
            out_specs=[row, pl.BlockSpec((tm, n), lambda i, *_: (i, 0))],
            scratch_shapes=_moe_scratch(tm, d)),
        out_shape=[jax.ShapeDtypeStruct((m, d), F32), jax.ShapeDtypeStruct((m, n), F32)],
        compiler_params=_cparams(1), name="norm_proj",
    )(slot0, slot1, x, route, ys, g.reshape(1, d), w)


def _norm_proj_even_body(x_ref, g_ref, w_ref, o_ref, rm0_ref, rm1_ref, rm2_ref, lane_ref):
    tm = x_ref.shape[0]
    hb = _rms(x_ref[...], g_ref[...]).astype(BF16)
    rm_refs = (rm0_ref, rm1_ref, rm2_ref)
    for c in range(EVEN_IN // A_WIDTH):
        cols = slice(c * A_WIDTH, (c + 1) * A_WIDTH)
        res = _dot(hb, w_ref[:, cols])
        o_ref[:, cols] = res
        if c < 3 * N_DIL:
            g, j = divmod(c, 3)
            dil = DIL_PAIRS[g][1]
            if dil == 1:
                rm_refs[g][0, 0, :, j * A_WIDTH:(j + 1) * A_WIDTH] = res.astype(BF16)
            else:
                for t in range(A_WIDTH // LANES):
                    lane_ref[t] = res[:, t * LANES:(t + 1) * LANES]
                for r in range(dil):
                    part = [lane_ref[t, pl.ds(r, tm // dil, stride=dil), :] for t in range(A_WIDTH // LANES)]
                    rm_refs[g][0, r, :, j * A_WIDTH:(j + 1) * A_WIDTH] = jnp.concatenate(part, axis=1).astype(BF16)


def _norm_proj_even(x, g, w, tm, n_batch, seq):
    m, d = x.shape
    n = w.shape[1]
    tps = seq // tm
    npt = n_batch * tps

    def rm_index(i):
        return (jnp.where(i < npt, i // tps, n_batch), 0, jnp.where(i < npt, i % tps, i - npt), 0)

    rm_shapes = [jax.ShapeDtypeStruct((n_batch + 1, dil, seq // dil, 3 * A_WIDTH), BF16) for _, dil in DIL_PAIRS]
    rm_specs = [pl.BlockSpec((1, dil, tm // dil, 3 * A_WIDTH), rm_index) for _, dil in DIL_PAIRS]
    return pl.pallas_call(
        _norm_proj_even_body,
        grid=(m // tm,),
        in_specs=[pl.BlockSpec((tm, d), lambda i: (i, 0)), pl.BlockSpec((1, d), lambda i: (0, 0)),
                  pl.BlockSpec((d, n), lambda i: (0, 0))],
        out_specs=[pl.BlockSpec((tm, n), lambda i: (i, 0))] + rm_specs,
        out_shape=[jax.ShapeDtypeStruct((m, n), F32)] + rm_shapes,
        scratch_shapes=[pltpu.VMEM((A_WIDTH // LANES, tm, LANES), F32)],
        compiler_params=_cparams(1), name="norm_proj_even",
    )(x, g.reshape(1, d), w)


def _t5_bucket(dist):
    max_exact = NUM_BUCKETS // 2
    d = np.maximum(dist, 1).astype(np.float32)
    large = max_exact + (np.log(d / max_exact) / np.log(MAX_DISTANCE / max_exact)
                         * (NUM_BUCKETS - max_exact)).astype(np.int32)
    large = np.minimum(large, NUM_BUCKETS - 1)
    return np.where(dist < max_exact, dist, large).astype(np.int32)


def _bucket_lookup(tab, buckets):
    onehot = jnp.asarray(buckets[..., None, None] == np.arange(NUM_BUCKETS)[:, None])
    return jnp.sum(jnp.where(onehot, tab.astype(F32), 0.0), axis=-2)


def _prompt_bias(tab, dil):
    qi = np.arange(A_BLOCK)[:, None]
    km = np.arange(2 * A_BLOCK)[None, :]
    delta = A_BLOCK + qi - km
    valid = (delta >= 0) & (delta <= SPAN)
    bias = _bucket_lookup(tab, _t5_bucket(np.clip(delta, 0, SPAN) * dil))
    bias = jnp.where(valid[..., None], bias, NEG_INF)
    return jnp.transpose(bias, (2, 0, 1))


def _attn_prompt_body(q_ref, kp_ref, ko_ref, vp_ref, vo_ref, bias_ref, o_ref, lse_ref):
    first = pl.program_id(2) == 0
    scale = A_HEAD_DIM ** -0.5
    n_sub = q_ref.shape[2] // A_BLOCK
    k_all = jnp.concatenate([kp_ref[0, 0], ko_ref[0, 0]], axis=0)
    v_all = jnp.concatenate([vp_ref[0, 0], vo_ref[0, 0]], axis=0)
    km = lax.broadcasted_iota(jnp.int32, (1, 2 * A_BLOCK), 1)
    no_prev = jnp.logical_and(first, km < A_BLOCK)
    heads_per_tile = LANES // A_HEAD_DIM
    head_of_lane = lax.broadcasted_iota(jnp.int32, (1, LANES), 1) // A_HEAD_DIM
    for sub in range(n_sub):
        rows = slice(sub * A_BLOCK, (sub + 1) * A_BLOCK)
        keys = slice(sub * A_BLOCK, (sub + 2) * A_BLOCK)
        for t in range(A_WIDTH // LANES):
            sl = slice(t * LANES, (t + 1) * LANES)
            q_t, k_t, v_t = q_ref[0, 0, rows, sl], k_all[keys, sl], v_all[keys, sl]
            o_t = jnp.zeros((A_BLOCK, LANES), F32)
            lse_t = jnp.zeros((A_BLOCK, LANES), F32)
            for j in range(heads_per_tile):
                mine = head_of_lane == j
                s = _dot_nt(jnp.where(mine, q_t, jnp.zeros_like(q_t)), k_t) * scale + bias_ref[t * heads_per_tile + j]
                if sub == 0:
                    s = jnp.where(no_prev, NEG_INF, s)
                m = jnp.max(s, axis=-1, keepdims=True)
                p = jnp.exp(s - m)
                den = jnp.sum(p, axis=-1, keepdims=True)
                o_t = jnp.where(mine, _dot(p.astype(BF16), v_t) / den, o_t)
                lse_t = jnp.where(mine, m + jnp.log(den), lse_t)
            o_ref[0, 0, rows, sl] = o_t
            lse_ref[0, 0, rows, sl] = lse_t


def _attn_prompt(qkv_rm, tab, g, dil, n_batch, seq):
    sub_len = seq // dil
    n_sub = _largest_divisor(sub_len // A_BLOCK, (2, 1))
    rows = n_sub * A_BLOCK
    nb = sub_len // rows

    def spec(j, prev):
        if prev:
            return pl.BlockSpec((1, 1, A_BLOCK, A_WIDTH), lambda b, r, i: (b, r, jnp.maximum(i * n_sub - 1, 0), j))
        return pl.BlockSpec((1, 1, rows, A_WIDTH), lambda b, r, i: (b, r, i, j))

    out_spec = pl.BlockSpec((1, 1, rows, A_WIDTH), lambda b, r, i: (b, r, i, 0))
    out_sds = jax.ShapeDtypeStruct((n_batch, dil, sub_len, A_WIDTH), F32)
    return pl.pallas_call(
        _attn_prompt_body,
        grid=(n_batch, dil, nb),
        in_specs=[spec(0, False), spec(1, True), spec(1, False), spec(2, True), spec(2, False),
                  pl.BlockSpec((A_HEADS, A_BLOCK, 2 * A_BLOCK), lambda b, r, i: (0, 0, 0))],
        out_specs=[out_spec, out_spec], out_shape=[out_sds, out_sds],
        compiler_params=_cparams(3), name="attn_prompt_g%d" % g,
    )(qkv_rm, qkv_rm, qkv_rm, qkv_rm, qkv_rm, _prompt_bias(tab, dil))


def _sample_bias(tab, dil, cache_len, key_index, t_new):
    t = np.arange(t_new)[:, None]
    dist = cache_len + t - key_index[None, :]
    valid = (dist >= 0) & (dist % dil == 0) & (dist <= SPAN * dil)
    bias = _bucket_lookup(tab, _t5_bucket(np.clip(dist, 0, SPAN * dil)))
    bias = jnp.where(valid[..., None], bias, NEG_INF)
    return jnp.transpose(bias, (2, 0, 1)).reshape(A_HEADS * t_new, key_index.shape[0])


def _attn_sample_body(new_ref, c0_ref, c1_ref, c2_ref, b0_ref, b1_ref, b2_ref, bn_ref, *out_refs, t_new):
    scale = A_HEAD_DIM ** -0.5
    rows = A_HEADS * t_new
    head_of_row = lax.broadcasted_iota(jnp.int32, (rows, A_WIDTH), 0) // t_new
    head_of_lane = lax.broadcasted_iota(jnp.int32, (rows, A_WIDTH), 1) // A_HEAD_DIM
    own = head_of_row == head_of_lane
    caches = (c0_ref, c1_ref, c2_ref)
    biases = (b0_ref, b1_ref, b2_ref)
    for g in range(N_DIL):
        base = g * 3 * A_WIDTH
        cache_len = caches[g].shape[-1]
        q = new_ref[:, base:base + A_WIDTH]
        k_new = new_ref[:, base + A_WIDTH:base + 2 * A_WIDTH].astype(BF16)
        v_new = new_ref[:, base + 2 * A_WIDTH:base + 3 * A_WIDTH].astype(BF16)
        q_bd = jnp.where(own, jnp.concatenate([q] * A_HEADS, axis=0), 0.0).astype(BF16)
        k_t = caches[g][0, 0, 0].reshape(A_WIDTH, cache_len).astype(BF16)
        v_t = caches[g][0, 0, 1].reshape(A_WIDTH, cache_len).astype(BF16)
        s_c = _dot(q_bd, k_t) * scale + biases[g][...]
        s_n = _dot_nt(q_bd, k_new) * scale + bn_ref[g]
        m = jnp.maximum(s_c.max(axis=-1, keepdims=True), s_n.max(axis=-1, keepdims=True))
        p_c = jnp.exp(s_c - m)
        p_n = jnp.exp(s_n - m)
        den = jnp.sum(p_c, axis=-1, keepdims=True) + jnp.sum(p_n, axis=-1, keepdims=True)
        acc = _dot_nt(p_c.astype(BF16), v_t) + _dot(p_n.astype(BF16), v_new)
        acc = jnp.where(own, acc / den, 0.0)
        lse = jnp.where(own, m + jnp.log(den), 0.0)
        o = acc[0:t_new]
        l = lse[0:t_new]
        for h in range(1, A_HEADS):
            o = o + acc[h * t_new:(h + 1) * t_new]
            l = l + lse[h * t_new:(h + 1) * t_new]
        out_refs[2 * g][...] = o
        out_refs[2 * g + 1][...] = l


def _attn_sample(proj, caches, t5_bias, n_seq, t_new, row0):
    cache_lens = [c.shape[2] for c in caches]
    caches_t = [jnp.transpose(c, (0, 1, 3, 4, 5, 2)) for c in caches]
    biases = [_sample_bias(t5_bias[:, g], DIL_PAIRS[g][1], cache_lens[g], np.arange(cache_lens[g]), t_new)
              for g in range(N_DIL)]
    bias_new = jnp.stack([_sample_bias(t5_bias[:, g], DIL_PAIRS[g][1], cache_lens[g],
                                       cache_lens[g] + np.arange(t_new), t_new) for g in range(N_DIL)])
    blk0 = row0 // t_new
    full = lambda a: pl.BlockSpec(a.shape, lambda b: (0,) * a.ndim)
    out_spec = pl.BlockSpec((t_new, A_WIDTH), lambda b: (b, 0))
    cache_specs = [pl.BlockSpec((1, 1, 2, A_HEADS, A_HEAD_DIM, n), lambda b: (0, b, 0, 0, 0, 0)) for n in cache_lens]
    return pl.pallas_call(
        functools.partial(_attn_sample_body, t_new=t_new),
        grid=(n_seq,),
        in_specs=[pl.BlockSpec((t_new, N_DIL * 3 * A_WIDTH), lambda b: (blk0 + b, 0))] + cache_specs
                 + [full(biases[0]), full(biases[1]), full(biases[2]), full(bias_new)],
        out_specs=[out_spec] * (2 * N_DIL),
        out_shape=[jax.ShapeDtypeStruct((n_seq * t_new, A_WIDTH), F32)] * (2 * N_DIL),
        compiler_params=_cparams(1), name="attn_sample",
    )(proj, *caches_t, *biases, bias_new)


def _shift_rows(x, s, fill, axis):
    t = lax.broadcasted_iota(jnp.int32, x.shape, axis)
    return jnp.where(t >= s, pltpu.roll(x, s, axis), fill)


def _linear_scan(a, b, axis):
    n = a.shape[axis]
    s = 1
    while s < n:
        b = b + a * _shift_rows(b, s, 0.0, axis)
        a = a * _shift_rows(a, s, 1.0, axis)
        s *= 2
    return a, b


def _rglru_gates(xc, wa_ref, wx_ref, vec_ref):
    xcb = xc.astype(BF16)
    r = jax.nn.sigmoid(_dot(xcb, wa_ref[...]) + vec_ref[1:2])
    ig = jax.nn.sigmoid(_dot(xcb, wx_ref[...]) + vec_ref[2:3])
    log_a = -RG_C * r * _softplus(-vec_ref[3:4])
    a = jnp.exp(log_a)
    b = jnp.sqrt(1.0 - jnp.exp(2.0 * log_a)) * (ig * xc)
    return a, b


def _gelu(x):
    return 0.5 * x * (1.0 + jnp.tanh(math.sqrt(2.0 / math.pi) * (x + 0.044715 * (x * x * x))))


def _rglru_prompt_body(xb_ref, gb_ref, cw_ref, wa_ref, wx_ref, vec_ref, y_ref, hl_ref, tail_ref, h_ref):
    @pl.when(pl.program_id(1) == 0)
    def _():
        tail_ref[...] = jnp.zeros_like(tail_ref)
        h_ref[...] = jnp.zeros_like(h_ref)

    x = xb_ref[...]
    tt = x.shape[0]
    xe = jnp.concatenate([tail_ref[...], x], axis=0)
    xc = vec_ref[0:1] + x * cw_ref[CONV_W - 1:CONV_W]
    for j in range(1, CONV_W):
        xc = xc + pltpu.roll(xe, j, 0)[8:8 + tt] * cw_ref[CONV_W - 1 - j:CONV_W - j]
    tail_ref[...] = x[tt - 8:tt]
    a, b = _rglru_gates(xc, wa_ref, wx_ref, vec_ref)
    a_cum, h = _linear_scan(a, b, 0)
    h = h + a_cum * h_ref[...]
    h_ref[...] = h[tt - 1:tt]
    hl_ref[0] = h[tt - 1:tt]
    y_ref[...] = h * _gelu(gb_ref[...])


def _rglru_sample_body(xb_ref, gb_ref, c0_ref, h0_ref, cw_ref, wa_ref, wx_ref, vec_ref, prev_ref, y_ref, hl_ref, *, t_new):
    del prev_ref
    x = xb_ref[...]
    rows = x.shape[0]
    ns = rows // t_new
    x3 = x.reshape(ns, t_new, RNN_WIDTH)
    xe = jnp.concatenate([c0_ref[...], x3], axis=1)
    xc = vec_ref[0:1] + x3 * cw_ref[CONV_W - 1:CONV_W]
    for j in range(1, CONV_W):
        xc = xc + pltpu.roll(xe, j, 1)[:, 8:8 + t_new] * cw_ref[CONV_W - 1 - j:CONV_W - j]
    a, b = _rglru_gates(xc.reshape(rows, RNN_WIDTH), wa_ref, wx_ref, vec_ref)
    a_cum, h = _linear_scan(a.reshape(ns, t_new, RNN_WIDTH), b.reshape(ns, t_new, RNN_WIDTH), 1)
    h = h + a_cum * h0_ref[...][:, None, :]
    hl_ref[...] = h[:, t_new - 1, :]
    y_ref[...] = h.reshape(rows, RNN_WIDTH) * _gelu(gb_ref[...])


def _block_diag(w):
    nb, bi, bo = w.shape
    eye = jnp.eye(nb, dtype=w.dtype)
    return (w[:, :, None, :] * eye[:, None, :, None]).reshape(nb * bi, nb * bo)


def _rglru(proj, conv0, h0, conv_w, conv_b, wa, ba, wx, bx, lam, n_batch, seq, n_seq, t_new):
    m_total = proj.shape[0]
    mp = n_batch * seq
    wa_bd = _block_diag(wa).astype(BF16)
    wx_bd = _block_diag(wx).astype(BF16)
    vec = jnp.stack([conv_b, ba, bx, lam]).astype(F32)
    xcol = XB_COL // RNN_WIDTH
    gcol = GB_COL // RNN_WIDTH
    full2 = lambda a, nd: pl.BlockSpec(a.shape, lambda *_: (0,) * a.ndim)
    tt = _row_tile(seq)
    nt = seq // tt
    w_specs2 = [pl.BlockSpec(a.shape, lambda b, i: (0, 0)) for a in (conv_w, wa_bd, wx_bd, vec)]
    y, hl_p = pl.pallas_call(
        _rglru_prompt_body,
        grid=(n_batch, nt),
        in_specs=[pl.BlockSpec((tt, RNN_WIDTH), lambda b, i: (b * nt + i, xcol)),
                  pl.BlockSpec((tt, RNN_WIDTH), lambda b, i: (b * nt + i, gcol))] + w_specs2,
        out_specs=[pl.BlockSpec((tt, RNN_WIDTH), lambda b, i: (b * nt + i, 0)),
                   pl.BlockSpec((1, 1, RNN_WIDTH), lambda b, i: (b, 0, 0))],
        out_shape=[jax.ShapeDtypeStruct((m_total, RNN_WIDTH), F32),
                   jax.ShapeDtypeStruct((n_batch, 1, RNN_WIDTH), F32)],
        scratch_shapes=[pltpu.VMEM((8, RNN_WIDTH), F32), pltpu.VMEM((1, RNN_WIDTH), F32)],
        compiler_params=_cparams(2), name="rglru_prompt",
    )(proj, proj, conv_w, wa_bd, wx_bd, vec)
    del full2
    ts = 32 if n_seq % 32 == 0 else 8
    rows = ts * t_new
    blk0 = mp // rows
    c0p = jnp.pad(conv0, ((0, 0), (8 - (CONV_W - 1), 0), (0, 0)))
    w_specs1 = [pl.BlockSpec(a.shape, lambda i: (0, 0)) for a in (conv_w, wa_bd, wx_bd, vec)]
    y, hl_s = pl.pallas_call(
        functools.partial(_rglru_sample_body, t_new=t_new),
        grid=(n_seq // ts,),
        in_specs=[pl.BlockSpec((rows, RNN_WIDTH), lambda i: (blk0 + i, xcol)),
                  pl.BlockSpec((rows, RNN_WIDTH), lambda i: (blk0 + i, gcol)),
                  pl.BlockSpec((ts, 8, RNN_WIDTH), lambda i: (i, 0, 0)),
                  pl.BlockSpec((ts, RNN_WIDTH), lambda i: (i, 0))] + w_specs1
                 + [pl.BlockSpec(memory_space=pl.ANY)],
        out_specs=[pl.BlockSpec((rows, RNN_WIDTH), lambda i: (blk0 + i, 0)),
                   pl.BlockSpec((ts, RNN_WIDTH), lambda i: (i, 0))],
        out_shape=[jax.ShapeDtypeStruct((m_total, RNN_WIDTH), F32),
                   jax.ShapeDtypeStruct((n_seq, RNN_WIDTH), F32)],
        input_output_aliases={8: 0},
        compiler_params=_cparams(1), name="rglru_sample",
    )(proj, proj, c0p, h0, conv_w, wa_bd, wx_bd, vec, y)
    return y, hl_p.reshape(n_batch, RNN_WIDTH), hl_s


def _pack_bf16_pair(x):
    w = x.shape[1] // 2
    hi = lax.bitcast_convert_type(x[:, :w].astype(BF16).astype(F32), jnp.uint32)
    lo = lax.bitcast_convert_type(x[:, w:].astype(BF16).astype(F32), jnp.uint32)
    return hi | (lo >> 16)


def _unpack_bf16_pair(p):
    hi = lax.bitcast_convert_type(p & jnp.uint32(0xFFFF0000), F32).astype(BF16)
    lo = lax.bitcast_convert_type(p << 16, F32).astype(BF16)
    return hi, lo


def _route(x1, gf_ref, wr_ref, br_ref, hn_ref, route_ref, counts_ref, run_ref):
    @pl.when(pl.program_id(0) == 0)
    def _():
        run_ref[...] = jnp.zeros_like(run_ref)

    hn = _rms(x1, gf_ref[...])
    hn_ref[...] = _pack_bf16_pair(hn)
    hn_hi = hn.astype(BF16)
    hn_lo = (hn - hn_hi.astype(F32)).astype(BF16)
    logits = _dot(hn_hi, wr_ref[0]) + (_dot(hn_hi, wr_ref[1]) + _dot(hn_lo, wr_ref[0])) + br_ref[...]
    lane = lax.broadcasted_iota(jnp.int32, logits.shape, 1)
    is_coarse = lane < N_GROUPS
    coarse = jnp.where(is_coarse, logits, -jnp.inf)
    cmax = jnp.max(coarse, axis=-1, keepdims=True)
    grp = jnp.min(jnp.where(coarse == cmax, lane, ROUTE_LANES), axis=-1, keepdims=True)
    p_grp = 1.0 / jnp.sum(jnp.where(is_coarse, jnp.exp(logits - cmax), 0.0), axis=-1, keepdims=True)
    expert = lane - N_GROUPS
    in_grp = (lane >= N_GROUPS) & (expert < N_EXPERTS) & (expert // EXPERTS_PER_GROUP == grp)
    fine = jnp.where(in_grp, logits, -jnp.inf)
    v1 = jnp.max(fine, axis=-1, keepdims=True)
    i1 = jnp.min(jnp.where(fine == v1, lane, ROUTE_LANES), axis=-1, keepdims=True)
    fine2 = jnp.where(lane == i1, -jnp.inf, fine)
    v2 = jnp.max(fine2, axis=-1, keepdims=True)
    i2 = jnp.min(jnp.where(fine2 == v2, lane, ROUTE_LANES), axis=-1, keepdims=True)
    e2 = jnp.exp(v2 - v1)
    w1 = p_grp / (1.0 + e2)
    w2 = p_grp * e2 / (1.0 + e2)
    tm = logits.shape[0]
    sel = jnp.where(lane == i1, 1.0, jnp.where(lane == i2, 1.0, 0.0))
    ri = lax.broadcasted_iota(jnp.int32, (tm, tm), 0)
    ci = lax.broadcasted_iota(jnp.int32, (tm, tm), 1)
    earlier = jnp.where(ri > ci, 1.0, 0.0).astype(BF16)
    before = _dot(earlier, sel.astype(BF16)) + run_ref[...]
    r1 = jnp.sum(jnp.where(lane == i1, before, 0.0), axis=-1, keepdims=True)
    r2 = jnp.sum(jnp.where(lane == i2, before, 0.0), axis=-1, keepdims=True)
    run_ref[...] = run_ref[...] + jnp.sum(sel, axis=0, keepdims=True)
    counts_ref[...] = run_ref[...]
    route = jnp.where(lane == 0, (i1 - N_GROUPS).astype(F32), 0.0)
    route = jnp.where(lane == 1, (i2 - N_GROUPS).astype(F32), route)
    route = jnp.where(lane == 2, w1, route)
    route = jnp.where(lane == 3, w2, route)
    route = jnp.where(lane == 4, r1, route)
    route = jnp.where(lane == 5, r2, route)
    route_ref[...] = route


def _merge_groups(os, ls):
    mx = jnp.maximum(jnp.maximum(ls[0], ls[1]), ls[2])
    es = [jnp.exp(l - mx) for l in ls]
    return (es[0] * os[0] + es[1] * os[1] + es[2] * os[2]) / (es[0] + es[1] + es[2])


def _even_out_body(o0, l0, o1, l1, o2, l2, so0, sl0, so1, sl1, so2, sl2, yb_ref, x_ref, wo_ref, gf_ref, wr_ref, br_ref,
                   x1_ref, hn_ref, route_ref, counts_ref, n1o, n1l, n2o, n2l, oa_ref, run_ref, *, n_prompt_tiles):
    i = pl.program_id(0)
    tm = x_ref.shape[0]

    @pl.when(i < n_prompt_tiles)
    def _():
        nat = []
        for src, dst, dil in ((o1, n1o, DIL_PAIRS[1][1]), (l1, n1l, DIL_PAIRS[1][1]),
                              (o2, n2o, DIL_PAIRS[2][1]), (l2, n2l, DIL_PAIRS[2][1])):
            for r in range(dil):
                blk = src[0, r]
                for t in range(A_WIDTH // LANES):
                    dst[t, pl.ds(r, tm // dil, stride=dil), :] = blk[:, t * LANES:(t + 1) * LANES]
            nat.append(jnp.concatenate([dst[t] for t in range(A_WIDTH // LANES)], axis=1))
        oa_ref[...] = _merge_groups((o0[0, 0], nat[0], nat[2]), (l0[0, 0], nat[1], nat[3]))

    @pl.when(i >= n_prompt_tiles)
    def _():
        oa_ref[...] = _merge_groups((so0[...], so1[...], so2[...]), (sl0[...], sl1[...], sl2[...]))

    cat = jnp.concatenate([oa_ref[...], yb_ref[...]], axis=-1).astype(BF16)
    x1 = x_ref[...] + _dot(cat, wo_ref[...])
    x1_ref[...] = x1
    _route(x1, gf_ref, wr_ref, br_ref, hn_ref, route_ref, counts_ref, run_ref)


def _router_weights(rg_w, rg_b, re_w, re_b):
    d = rg_w.shape[0]
    wr = jnp.concatenate([rg_w, re_w.reshape(d, N_EXPERTS)], axis=1)
    br = jnp.concatenate([rg_b, re_b.reshape(N_EXPERTS)])
    pad = ROUTE_LANES - wr.shape[1]
    wr = jnp.pad(wr, ((0, 0), (0, pad))).astype(F32)
    wr_hi = wr.astype(BF16)
    wr_lo = (wr - wr_hi.astype(F32)).astype(BF16)
    return jnp.stack([wr_hi, wr_lo]), jnp.pad(br, (0, pad)).reshape(1, ROUTE_LANES).astype(F32)


def _mix_out_call(body, in_specs, args, consts, m, d, tm, scratch, name):
    const_spec = lambda a: pl.BlockSpec(a.shape, lambda i: (0,) * a.ndim)
    return pl.pallas_call(
        body, grid=(m // tm,), in_specs=list(in_specs) + [const_spec(c) for c in consts],
        out_specs=[pl.BlockSpec((tm, d), lambda i: (i, 0)), pl.BlockSpec((tm, d // 2), lambda i: (i, 0)),
                   pl.BlockSpec((tm, ROUTE_LANES), lambda i: (i, 0)), pl.BlockSpec((1, ROUTE_LANES), lambda i: (0, 0))],
        out_shape=[jax.ShapeDtypeStruct((m, d), F32), jax.ShapeDtypeStruct((m, d // 2), jnp.uint32),
                   jax.ShapeDtypeStruct((m, ROUTE_LANES), F32), jax.ShapeDtypeStruct((1, ROUTE_LANES), F32)],
        scratch_shapes=list(scratch) + [pltpu.VMEM((1, ROUTE_LANES), F32)],
        compiler_params=_cparams(1), name=name,
    )(*args, *consts)


def _even_out(attn_p, attn_s, y_b, x, consts, tm, n_batch, seq):
    m, d = x.shape
    tps = seq // tm
    npt = n_batch * tps
    in_specs = []
    for g, (_, dil) in enumerate(DIL_PAIRS):
        def index(i):
            return (jnp.minimum(i // tps, n_batch - 1), 0, jnp.where(i < npt, i % tps, 0), 0)
        in_specs += [pl.BlockSpec((1, dil, tm // dil, A_WIDTH), index)] * 2
    in_specs += [pl.BlockSpec((tm, A_WIDTH), lambda i: (jnp.maximum(i - npt, 0), 0))] * (2 * N_DIL)
    in_specs += [pl.BlockSpec((tm, RNN_WIDTH), lambda i: (i, 0)), pl.BlockSpec((tm, d), lambda i: (i, 0))]
    scratch = [pltpu.VMEM((A_WIDTH // LANES, tm, LANES), F32)] * 4 + [pltpu.VMEM((tm, A_WIDTH), F32)]
    return _mix_out_call(functools.partial(_even_out_body, n_prompt_tiles=npt), in_specs,
                         list(attn_p) + list(attn_s) + [y_b, x], consts, m, d, tm, scratch, "even_out")


def _odd_out(o_c, proj2, x, consts, tm):
    m, d = x.shape
    in_specs = [pl.BlockSpec((tm, C_V_W), lambda i: (i, 0)),
                pl.BlockSpec((tm, C_V_W), lambda i: (i, Z_COL // C_V_W)),
                pl.BlockSpec((tm, d), lambda i: (i, 0))]
    return _mix_out_call(_odd_out_body, in_specs, [o_c, proj2, x], consts, m, d, tm, [], "odd_out")


def _moe_slots_body(route_ref, start_ref, o_ref):
    lane = lax.broadcasted_iota(jnp.int32, route_ref.shape, 1)
    r = route_ref[...]
    starts = start_ref[...]

    def slot(expert, pos):
        return jnp.sum(jnp.where(lane == expert.astype(jnp.int32), starts, 0.0), axis=-1, keepdims=True) + pos

    s0 = slot(r[:, 0:1], r[:, 4:5])
    s1 = slot(r[:, 1:2], r[:, 5:6])
    o_ref[...] = jnp.where(lane == 0, s0, jnp.where(lane == 1, s1, 0.0)).astype(jnp.int32)


def _moe_dispatch_body(s0_ref, s1_ref, fill_ref, used_ref, hn_ref, xs_ref, zero_ref, fill_sem, row_sem):
    i = pl.program_id(0)
    tm = hn_ref.shape[0]

    def fill_copy(e):
        return pltpu.make_async_copy(zero_ref, xs_ref.at[pl.ds(pl.multiple_of(fill_ref[e], MOE_TILE), MOE_TILE)], fill_sem)

    @pl.when(i == 0)
    def _():
        zero_ref[...] = jnp.zeros_like(zero_ref)
        for e in range(N_EXPERTS):
            @pl.when(used_ref[e] > 0)
            def _():
                fill_copy(e).start()
        for e in range(N_EXPERTS):
            @pl.when(used_ref[e] > 0)
            def _():
                fill_copy(e).wait()

    base = i * tm

    def scatter_row(j, carry):
        _row_copy(hn_ref, j, xs_ref, s0_ref[base + j], row_sem).start()
        _row_copy(hn_ref, j, xs_ref, s1_ref[base + j], row_sem).start(priority=1)
        return carry

    lax.fori_loop(0, tm, scatter_row, 0, unroll=8)
    for _ in range(2):
        pltpu.make_async_copy(hn_ref, xs_ref.at[pl.ds(0, tm)], row_sem).wait()


def _moe_ffn_body(te_ref, nu_ref, xs_ref, wg_ref, wu_ref, wd_ref, o_ref):
    del te_ref

    @pl.when(pl.program_id(0) < nu_ref[0])
    def _():
        xa, xb = _unpack_bf16_pair(xs_ref[...])
        half = xa.shape[1]
        hg = _dot(xa, wg_ref[0, :half]) + _dot(xb, wg_ref[0, half:])
        hu = _dot(xa, wu_ref[0, :half]) + _dot(xb, wu_ref[0, half:])
        o_ref[...] = _dot((_silu(hg) * hu).astype(BF16), wd_ref[0])

    @pl.when(pl.program_id(0) >= nu_ref[0])
    def _():
        o_ref[...] = jnp.zeros_like(o_ref)


def _moe(hn, route, counts, w_gate, w_up, w_down, tm):
    m = hn.shape[0]
    d = w_gate.shape[1]
    tile = MOE_TILE
    n_slots = 2 * m + N_EXPERTS * tile
    n_tiles = n_slots // tile
    assert n_slots < 1 << 24, "slots must stay exact in f32"
    cnt = counts[0, N_GROUPS:N_GROUPS + N_EXPERTS].astype(jnp.int32)
    padded = ((cnt + tile - 1) // tile) * tile
    pad_end = jnp.cumsum(padded)
    pad_start = (pad_end - padded).astype(jnp.int32)
    starts = jnp.zeros((1, ROUTE_LANES), F32).at[0, :N_EXPERTS].set(pad_start.astype(F32))
    ts = _largest_divisor(m, (1024, 512, 256, 128, 64, 32, 16, 8))
    slots = pl.pallas_call(
        _moe_slots_body, grid=(m // ts,),
        in_specs=[pl.BlockSpec((ts, ROUTE_LANES), lambda i: (i, 0)), pl.BlockSpec((1, ROUTE_LANES), lambda i: (0, 0))],
        out_specs=pl.BlockSpec((ts, ROUTE_LANES), lambda i: (i, 0)),
        out_shape=jax.ShapeDtypeStruct((m, ROUTE_LANES), jnp.int32),
        compiler_params=_cparams(1), name="moe_slots",
    )(route, starts)
    slot0 = slots[:, 0]
    slot1 = slots[:, 1]
    tile_start = jnp.arange(n_tiles, dtype=jnp.int32) * tile
    tile_expert = jnp.sum((tile_start[:, None] >= pad_end[None, :]).astype(jnp.int32), axis=1)
    tile_expert = jnp.minimum(tile_expert, N_EXPERTS - 1).astype(jnp.int32)
    n_used = (pad_end[-1:] // tile).astype(jnp.int32)
    fill_start = jnp.maximum(pad_end - tile, 0).astype(jnp.int32)

    xs = pl.pallas_call(
        _moe_dispatch_body,
        grid_spec=pltpu.PrefetchScalarGridSpec(
            num_scalar_prefetch=4, grid=(m // tm,),
            in_specs=[pl.BlockSpec((tm, d // 2), lambda i, *_: (i, 0))],
            out_specs=pl.BlockSpec(memory_space=pl.ANY),
            scratch_shapes=[pltpu.VMEM((tile, d // 2), jnp.uint32), pltpu.SemaphoreType.DMA(()),
                            pltpu.SemaphoreType.DMA(())]),
        out_shape=jax.ShapeDtypeStruct((n_slots, d // 2), jnp.uint32),
        compiler_params=_cparams(1), name="moe_dispatch",
    )(slot0, slot1, fill_start, cnt, hn)

    def used(i, nu):
        return jnp.minimum(i, jnp.maximum(nu[0] - 1, 0))

    ys = pl.pallas_call(
        _moe_ffn_body,
        grid_spec=pltpu.PrefetchScalarGridSpec(
            num_scalar_prefetch=2, grid=(n_tiles,),
            in_specs=[pl.BlockSpec((tile, d // 2), lambda i, te, nu: (used(i, nu), 0)),
                      pl.BlockSpec((1, d, EXPERT_FF), lambda i, te, nu: (te[used(i, nu)], 0, 0)),
                      pl.BlockSpec((1, d, EXPERT_FF), lambda i, te, nu: (te[used(i, nu)], 0, 0)),
                      pl.BlockSpec((1, EXPERT_FF, d), lambda i, te, nu: (te[used(i, nu)], 0, 0))],
            out_specs=pl.BlockSpec((tile, d), lambda i, te, nu: (i, 0))),
        out_shape=jax.ShapeDtypeStruct((n_slots, d), F32),
        compiler_params=_cparams(1), name="moe_ffn",
    )(tile_expert, n_used, xs, w_gate, w_up, w_down)

    return route, ys, slot0, slot1


def _gdn_prep_math(xe, tt, cw_ref, bg, av_ref):
    acc = xe[8:8 + tt] * cw_ref[CONV_W - 1:CONV_W]
    for j in range(1, CONV_W):
        acc = acc + pltpu.roll(xe, j, 0)[8:8 + tt] * cw_ref[CONV_W - 1 - j:CONV_W - j]
    qkv = _silu(acc)
    outs = []
    for h in range(2 * C_QK_HEADS):
        xh = qkv[:, h * C_DK:(h + 1) * C_DK]
        xh = xh * lax.rsqrt(jnp.sum(xh * xh, axis=-1, keepdims=True) + EPS)
        if h < C_QK_HEADS:
            xh = xh * (C_DK ** -0.5)
        outs.append(xh)
    qk = jnp.concatenate(outs, axis=-1)
    v = qkv[:, 2 * C_QK_W:]
    lane = lax.broadcasted_iota(jnp.int32, bg.shape, 1)
    gdec = -jnp.exp(av_ref[0:1]) * _softplus(bg + av_ref[1:2])
    bgo = jnp.where(lane < C_V_HEADS, jax.nn.sigmoid(bg), gdec)
    return qk, v, bgo


def _unit_lower_inverses(l_bds, size):
    n = l_bds[0].shape[0]
    ri = lax.broadcasted_iota(jnp.int32, (n, n), 0)
    ci = lax.broadcasted_iota(jnp.int32, (n, n), 1)
    eye = jnp.where(ri == ci, 1.0, 0.0).astype(F32)
    base = ri // GDN_BASE == ci // GDN_BASE
    ps = [jnp.where(base, -l, 0.0) for l in l_bds]
    ts = [eye + p for p in ps]
    s = 2
    while s < GDN_BASE:
        ps = [_dot(p.astype(BF16), p.astype(BF16)) for p in ps]
        ts = [t + _dot(t.astype(BF16), p.astype(BF16)) for t, p in zip(ts, ps)]
        s *= 2
    lbs = [l.astype(BF16) for l in l_bds]
    tbs = [t.astype(BF16) for t in ts]
    s = GDN_BASE
    while s < size:
        lower_left = ((ri // s) % 2 == 1) & (ci // s == ri // s - 1)
        mids = [_dot(jnp.where(lower_left, lb, jnp.zeros_like(lb)), tb).astype(BF16) for lb, tb in zip(lbs, tbs)]
        tbs = [tb - _dot(tb, mid).astype(BF16) for tb, mid in zip(tbs, mids)]
        s *= 2
    return tbs


def _gdn_body(*refs, chunk, pack, n_par, has_state):
    c = chunk
    x_refs = refs[0:2 * n_par:2]
    bg_refs = refs[1:2 * n_par:2]
    pos = 2 * n_par
    if has_state:
        c0_ref = refs[pos]
        pos += 1
    cw_ref, av_ref, nw_ref = refs[pos:pos + 3]
    pos += 3
    if has_state:
        s0_ref = refs[pos]
        pos += 2
    o_ref, sout_ref, s_ref = refs[pos:pos + 3]
    tail_ref = None if has_state else refs[pos + 3]

    @pl.when(pl.program_id(1) == 0)
    def _():
        if has_state:
            s_ref[...] = s0_ref[...]
        else:
            s_ref[...] = jnp.zeros_like(s_ref)
            tail_ref[...] = jnp.zeros_like(tail_ref)

    rep = C_V_HEADS // C_QK_HEADS
    ti = lax.broadcasted_iota(jnp.int32, (c, c), 0)
    tj = lax.broadcasted_iota(jnp.int32, (c, c), 1)
    tril = jnp.where(ti >= tj, 1.0, 0.0).astype(F32)
    n = pack * c
    ri = lax.broadcasted_iota(jnp.int32, (n, n), 0)
    ci = lax.broadcasted_iota(jnp.int32, (n, n), 1)
    same = ri // c == ci // c
    incl = same & (ri >= ci)
    strict = same & (ri > ci)

    def col(a, heads):
        return jnp.concatenate([a[:, h:h + 1] for h in heads], axis=0)

    items, g_ends = [], []
    l_bds, qk_bds, rhss, qgs, kends = [], [], [], [], []
    for k in range(n_par):
        x = x_refs[k][...]
        prev = c0_ref[k] if has_state else tail_ref[k]
        if not has_state:
            tail_ref[k] = x[c - 8:c]
        qk, v, bgo = _gdn_prep_math(jnp.concatenate([prev, x], axis=0), c, cw_ref, bg_refs[k][...], av_ref)
        beta = bgo[:, 0:C_V_HEADS]
        gc = _dot_f32(tril, bgo[:, C_V_HEADS:2 * C_V_HEADS])
        g_last = gc[c - 1:c]
        gam = jnp.exp(gc)
        kdec = jnp.exp(g_last - gc)
        g_ends.append(jnp.exp(g_last))
        for p0 in range(0, C_V_HEADS, pack):
            heads = range(p0, p0 + pack)
            items.append((k, heads))
            g_col = col(gc, heads)
            g_row = jnp.sum(jnp.where(ri == ci, g_col, 0.0), axis=0, keepdims=True)
            b_col = col(beta, heads)
            gam_col = col(gam, heads)
            k_st = jnp.concatenate([qk[:, C_QK_W + (h // rep) * C_DK:C_QK_W + (h // rep + 1) * C_DK] for h in heads], axis=0)
            q_st = jnp.concatenate([qk[:, (h // rep) * C_DK:(h // rep + 1) * C_DK] for h in heads], axis=0)
            v_st = jnp.concatenate([v[:, h * C_DV:(h + 1) * C_DV] for h in heads], axis=0)
            kb = k_st.astype(BF16)
            decay = jnp.exp(jnp.where(incl, g_col - g_row, -jnp.inf))
            l_bds.append(jnp.where(strict, b_col * _dot_nt(kb, kb) * decay, 0.0))
            qk_bds.append((_dot_nt(q_st.astype(BF16), kb) * decay).astype(BF16))
            rhss.append(jnp.concatenate([b_col * v_st, (b_col * gam_col) * k_st], axis=1).astype(BF16))
            qgs.append(q_st * gam_col)
            kends.append((k_st * col(kdec, heads)).astype(BF16))
    t_invs = _unit_lower_inverses(l_bds, c)
    uws = [_dot(t.astype(BF16), rhs) for t, rhs in zip(t_invs, rhss)]
    u_sts, q_sts = [], []
    for (k, heads), uw, qg in zip(items, uws, qgs):
        us, qs = [], []
        for i, h in enumerate(heads):
            rows = slice(i * c, (i + 1) * c)
            lhs = jnp.concatenate([uw[rows, C_DV:], qg[rows]], axis=0).astype(BF16)
            ws = _dot(lhs, s_ref[k, h].astype(BF16))
            us.append(uw[rows, 0:C_DV] - ws[0:c])
            qs.append(ws[c:])
        u_sts.append(jnp.concatenate(us, axis=0).astype(BF16))
        q_sts.append(jnp.concatenate(qs, axis=0))
    o_sts = [q + _dot(qk_bd, ub) for q, qk_bd, ub in zip(q_sts, qk_bds, u_sts)]
    for (k, heads), o_st, ub, kendb in zip(items, o_sts, u_sts, kends):
        for i, h in enumerate(heads):
            rows = slice(i * c, (i + 1) * c)
            s_ref[k, h] = s_ref[k, h] * g_ends[k][:, h:h + 1] + _dot_tn(kendb[rows], ub[rows])
            o_h = _rms(o_st[rows], nw_ref[...])
            if has_state:
                o_ref[k * c:(k + 1) * c, h * C_DV:(h + 1) * C_DV] = o_h
            else:
                o_ref[k, 0, :, h * C_DV:(h + 1) * C_DV] = o_h

    @pl.when(pl.program_id(1) == pl.num_programs(1) - 1)
    def _():
        sout_ref[...] = s_ref[...]


def _largest_divisor(n, candidates):
    return next(c for c in candidates if n % c == 0)


def _gdn(proj, conv0, conv_w, a_log, dt_bias, onorm_w, s0, n_batch, seq, n_seq, t_new, chunk):
    mp = n_batch * seq
    nc = seq // chunk
    av = jnp.zeros((2, 128), F32)
    av = av.at[0, C_V_HEADS:2 * C_V_HEADS].set(a_log).at[1, C_V_HEADS:2 * C_V_HEADS].set(dt_bias)
    nw = onorm_w.reshape(1, C_DV).astype(F32)
    bg_col = BG_COL // 128
    state = (C_V_HEADS, C_DK, C_DV)
    consts = [conv_w, av, nw]
    const_specs = [pl.BlockSpec(a.shape, lambda b, i: (0, 0)) for a in consts]

    par_p = _largest_divisor(n_batch, (2, 1))
    seq_specs = []
    for k in range(par_p):
        seq_specs += [pl.BlockSpec((chunk, C_CONV_DIM), lambda b, i, k=k: ((b * par_p + k) * nc + i, 0)),
                      pl.BlockSpec((chunk, 128), lambda b, i, k=k: ((b * par_p + k) * nc + i, bg_col))]
    extra = -(-(n_seq * t_new) // seq)
    o4, s_p = pl.pallas_call(
        functools.partial(_gdn_body, chunk=chunk, pack=256 // chunk, n_par=par_p, has_state=False),
        grid=(n_batch // par_p, nc),
        in_specs=seq_specs + const_specs,
        out_specs=[pl.BlockSpec((par_p, 1, chunk, C_V_W), lambda b, i: (b, i, 0, 0)),
                   pl.BlockSpec((par_p,) + state, lambda b, i: (b, 0, 0, 0))],
        out_shape=[jax.ShapeDtypeStruct((n_batch + extra, nc, chunk, C_V_W), F32),
                   jax.ShapeDtypeStruct((n_batch,) + state, F32)],
        scratch_shapes=[pltpu.VMEM((par_p,) + state, F32), pltpu.VMEM((par_p, 8, C_CONV_DIM), F32)],
        compiler_params=_cparams(2), name="gdn_prompt",
    )(*([proj, proj] * par_p), *consts)
    o = o4.reshape((n_batch + extra) * seq, C_V_W)

    par_s = _largest_divisor(n_seq, (4, 2, 1))
    blk0 = mp // t_new
    c0p = jnp.pad(conv0, ((0, 0), (8 - (CONV_W - 1), 0), (0, 0)))
    seq_specs = []
    for k in range(par_s):
        seq_specs += [pl.BlockSpec((t_new, C_CONV_DIM), lambda b, i, k=k: (blk0 + b * par_s + k, 0)),
                      pl.BlockSpec((t_new, 128), lambda b, i, k=k: (blk0 + b * par_s + k, bg_col))]
    n_in = 2 * par_s + 1 + len(consts) + 1
    o, s_s = pl.pallas_call(
        functools.partial(_gdn_body, chunk=t_new, pack=C_V_HEADS, n_par=par_s, has_state=True),
        grid=(n_seq // par_s, 1),
        in_specs=seq_specs + [pl.BlockSpec((par_s, 8, C_CONV_DIM), lambda b, i: (b, 0, 0))] + const_specs
                 + [pl.BlockSpec((par_s,) + state, lambda b, i: (b, 0, 0, 0)), pl.BlockSpec(memory_space=pl.ANY)],
        out_specs=[pl.BlockSpec((par_s * t_new, C_V_W), lambda b, i: (blk0 // par_s + b, 0)),
                   pl.BlockSpec((par_s,) + state, lambda b, i: (b, 0, 0, 0))],
        out_shape=[jax.ShapeDtypeStruct(o.shape, F32), jax.ShapeDtypeStruct((n_seq,) + state, F32)],
        scratch_shapes=[pltpu.VMEM((par_s,) + state, F32)],
        input_output_aliases={n_in: 0},
        compiler_params=_cparams(2), name="gdn_sample",
    )(*([proj, proj] * par_s), c0p, *consts, s0, o)
    return o, s_p, s_s


def _odd_out_body(o_ref, z_ref, x_ref, wo_ref, gf_ref, wr_ref, br_ref, x1_ref, hn_ref, route_ref, counts_ref, run_ref):
    y = (o_ref[...] * _silu(z_ref[...])).astype(BF16)
    x1 = x_ref[...] + _dot(y, wo_ref[...])
    x1_ref[...] = x1
    _route(x1, gf_ref, wr_ref, br_ref, hn_ref, route_ref, counts_ref, run_ref)


def _final_body(s0_ref, s1_ref, x_ref, route_ref, ys_ref, g_ref, y_ref, buf_ref, sems, *, tile0):
    x = _moe_residual(x_ref, route_ref, ys_ref, s0_ref, s1_ref, buf_ref, sems, tile0)
    y_ref[...] = _rms(x, g_ref[...])


def _final_norm(x, moe, g, tm, row0, n_rows):
    route, ys, slot0, slot1 = moe
    d = x.shape[1]
    blk0 = row0 // tm
    return pl.pallas_call(
        functools.partial(_final_body, tile0=blk0),
        grid_spec=pltpu.PrefetchScalarGridSpec(
            num_scalar_prefetch=2, grid=(n_rows // tm,),
            in_specs=[pl.BlockSpec((tm, d), lambda i, *_: (blk0 + i, 0)),
                      pl.BlockSpec((tm, ROUTE_LANES), lambda i, *_: (blk0 + i, 0)),
                      pl.BlockSpec(memory_space=pl.ANY), pl.BlockSpec((1, d), lambda i, *_: (0, 0))],
            out_specs=pl.BlockSpec((tm, d), lambda i, *_: (i, 0)),
            scratch_shapes=_moe_scratch(tm, d)),
        out_shape=jax.ShapeDtypeStruct((n_rows, d), F32),
        compiler_params=_cparams(1), name="final_norm",
    )(slot0, slot1, x, route, ys, g.reshape(1, d))


def kernel(x_prompt, x_sample, cache_a_g0_kv, cache_a_g1_kv, cache_a_g2_kv, state_b_h, state_b_conv, state_c_S, state_c_conv, t5_bias, norm_mix, norm_ffn, norm_final, e_w_in, e_conv_w, e_conv_b, e_rg_wa, e_rg_ba, e_rg_wx, e_rg_bx, e_rg_lambda, e_w_out, o_w_in, o_conv_w, o_a_log, o_dt_bias, o_onorm_w, o_w_out, moe_rg_w, moe_rg_b, moe_re_w, moe_re_b, moe_w_gate, moe_w_up, moe_w_down):
    n_batch, seq, d = x_prompt.shape
    n_seq, t_new, _ = x_sample.shape
    mp = n_batch * seq
    ms = n_seq * t_new
    m = mp + ms
    assert t_new == 8 and seq % (DIL_PAIRS[2][1] * A_BLOCK) == 0
    assert e_w_in.shape[0] == 1 and o_w_in.shape[0] == 1
    tm = _row_tile(mp, ms)
    x = jnp.concatenate([x_prompt.reshape(mp, d), x_sample.reshape(ms, d)], axis=0)

    def moe_weights(layer):
        shp = (N_EXPERTS, d, EXPERT_FF)
        return (moe_w_gate[layer].reshape(shp).astype(BF16), moe_w_up[layer].reshape(shp).astype(BF16),
                moe_w_down[layer].reshape(N_EXPERTS, EXPERT_FF, d).astype(BF16))

    def prompt_tail(a, keep, c0, c1):
        return jnp.stack([lax.slice(a, ((b + 1) * seq - keep, c0), ((b + 1) * seq, c1)) for b in range(n_batch)])

    def sample_rows(a, keep, c0, c1):
        return lax.slice(a, (mp, c0), (m, c1)).reshape(n_seq, t_new, c1 - c0)[:, t_new - keep:]

    proj, *qkv_rm = _norm_proj_even(x, norm_mix[0], e_w_in[0].astype(BF16), tm, n_batch, seq)
    attn_p = []
    for g, (_, dil) in enumerate(DIL_PAIRS):
        attn_p.extend(_attn_prompt(qkv_rm[g], t5_bias[:, g], g, dil, n_batch, seq))
    new_a = []
    for g, (win, _) in enumerate(DIL_PAIRS):
        c0 = g * 3 * A_WIDTH + A_WIDTH
        keep = min(win, seq)
        new_a.append(prompt_tail(proj, keep, c0, c0 + 2 * A_WIDTH).reshape(1, n_batch, keep, 2, A_HEADS, A_HEAD_DIM))
        new_a.append(sample_rows(proj, t_new, c0, c0 + 2 * A_WIDTH).reshape(1, n_seq, t_new, 2, A_HEADS, A_HEAD_DIM))
    attn_s = _attn_sample(proj, (cache_a_g0_kv, cache_a_g1_kv, cache_a_g2_kv), t5_bias, n_seq, t_new, mp)
    y_b, bh_p, bh_s = _rglru(proj, state_b_conv[0], state_b_h[0], e_conv_w[0], e_conv_b[0], e_rg_wa[0], e_rg_ba[0],
                             e_rg_wx[0], e_rg_bx[0], e_rg_lambda[0], n_batch, seq, n_seq, t_new)
    wr, br = _router_weights(moe_rg_w[0], moe_rg_b[0], moe_re_w[0], moe_re_b[0])
    x1, hn, route, counts = _even_out(attn_p, attn_s, y_b, x, [e_w_out[0].astype(BF16), norm_ffn[0].reshape(1, d), wr, br],
                                      tm, n_batch, seq)
    y_moe = _moe(hn, route, counts, *moe_weights(0), tm)

    w_in1 = jnp.pad(o_w_in[0], ((0, 0), (0, ODD_IN_PAD - ODD_IN))).astype(BF16)
    x2, proj2 = _norm_proj(x1, y_moe, norm_mix[1], w_in1, tm, 896)
    o_c, cs_p, cs_s = _gdn(proj2, state_c_conv[0], o_conv_w[0], o_a_log[0], o_dt_bias[0], o_onorm_w[0], state_c_S[0],
                           n_batch, seq, n_seq, t_new, 64)
    wr, br = _router_weights(moe_rg_w[1], moe_rg_b[1], moe_re_w[1], moe_re_b[1])
    x3, hn, route, counts = _odd_out(o_c, proj2, x2, [o_w_out[0].astype(BF16), norm_ffn[1].reshape(1, d), wr, br], tm)
    y_moe = _moe(hn, route, counts, *moe_weights(1), tm)
    y_p = _final_norm(x3, y_moe, norm_final, tm, 0, mp).reshape(n_batch, seq, d)
    y_s = _final_norm(x3, y_moe, norm_final, tm, mp, ms).reshape(n_seq, t_new, d)

    keep = CONV_W - 1
    bconv_p = prompt_tail(proj, keep, XB_COL, XB_COL + RNN_WIDTH)[None]
    bconv_s = sample_rows(proj, keep, XB_COL, XB_COL + RNN_WIDTH)[None]
    cconv_p = prompt_tail(proj2, keep, 0, C_CONV_DIM)[None]
    cconv_s = sample_rows(proj2, keep, 0, C_CONV_DIM)[None]
    return (y_p, y_s, *new_a, bh_p[None], bh_s[None], bconv_p, bconv_s, cs_p[None], cs_s[None], cconv_p, cconv_s)
```

```python
import functools
import math

import jax
import jax.numpy as jnp
import numpy as np
from jax import lax
from jax.experimental import pallas as pl
from jax.experimental.pallas import tpu as pltpu

F32 = jnp.float32
BF16 = jnp.bfloat16
EPS = 1e-6
NEG_INF = -1e30

D_MODEL = 1024
DIL_PAIRS = ((128, 1), (512, 4), (2048, 16))
N_DIL = 3
A_HEADS = 8
A_HEAD_DIM = 64
A_WIDTH = A_HEADS * A_HEAD_DIM
A_BLOCK = 128
SPAN = 128
NUM_BUCKETS = 32
MAX_DISTANCE = 2048
RNN_WIDTH = 512
RNN_BLOCKS = 8
CONV_W = 4
RG_C = 8.0
EVEN_IN = N_DIL * 3 * A_WIDTH + 2 * RNN_WIDTH
XB_COL = N_DIL * 3 * A_WIDTH
GB_COL = XB_COL + RNN_WIDTH
C_QK_HEADS = 8
C_V_HEADS = 16
C_DK = 128
C_DV = 128
C_QK_W = C_QK_HEADS * C_DK
C_V_W = C_V_HEADS * C_DV
C_CONV_DIM = 2 * C_QK_W + C_V_W
ODD_IN = C_CONV_DIM + C_V_W + 2 * C_V_HEADS
ODD_IN_PAD = 6272
Z_COL = C_CONV_DIM
BG_COL = C_CONV_DIM + C_V_W
GDN_BASE = 8
N_GROUPS = 4
EXPERTS_PER_GROUP = 8
N_EXPERTS = N_GROUPS * EXPERTS_PER_GROUP
EXPERT_FF = 256
ROUTE_LANES = 128
MOE_TILE = 256

LANES = 128
VMEM_LIMIT = 56 * 1024 * 1024


def _cparams(n_grid):
    return pltpu.CompilerParams(dimension_semantics=("arbitrary",) * n_grid,
                                vmem_limit_bytes=VMEM_LIMIT)


def _rms(x, g):
    return x * lax.rsqrt(jnp.mean(x * x, axis=-1, keepdims=True) + EPS) * g


def _silu(x):
    return x * jax.nn.sigmoid(x)


def _softplus(x):
    return jnp.maximum(x, 0.0) + jnp.log1p(jnp.exp(-jnp.abs(x)))


def _dot(a, b):
    return jnp.dot(a, b, preferred_element_type=F32)


def _dot_nt(a, b):
    return lax.dot_general(a, b, (((1,), (1,)), ((), ())), preferred_element_type=F32)


def _dot_tn(a, b):
    return lax.dot_general(a, b, (((0,), (0,)), ((), ())), preferred_element_type=F32)


def _dot_f32(a, b):
    return jnp.dot(a, b, preferred_element_type=F32, precision=lax.Precision.HIGHEST)


def _row_tile(*counts):
    for t in (256, 128, 64, 32, 16, 8):
        if all(c % t == 0 for c in counts):
            return t
    raise ValueError("token counts must be multiples of 8")


def _row_copy(src_ref, src_row, dst_ref, dst_row, sem):
    return pltpu.make_async_copy(src_ref.at[pl.ds(src_row, 1)], dst_ref.at[pl.ds(dst_row, 1)], sem)


def _moe_residual(x_ref, route_ref, ys_ref, s0_ref, s1_ref, buf_ref, sems, tile0):
    i = pl.program_id(0)
    tm = x_ref.shape[0]

    def start(step, slot):
        base = (tile0 + step) * tm

        def gather_row(j, carry):
            _row_copy(ys_ref, s0_ref[base + j], buf_ref.at[slot, 0], j, sems.at[slot]).start()
            _row_copy(ys_ref, s1_ref[base + j], buf_ref.at[slot, 1], j, sems.at[slot]).start()
            return carry

        lax.fori_loop(0, tm, gather_row, 0, unroll=8)

    @pl.when(i == 0)
    def _():
        start(0, 0)

    @pl.when(i + 1 < pl.num_programs(0))
    def _():
        start(i + 1, (i + 1) % 2)

    slot = i % 2
    for k in range(2):
        pltpu.make_async_copy(ys_ref.at[pl.ds(0, tm)], buf_ref.at[slot, k], sems.at[slot]).wait()
    return x_ref[...] + route_ref[:, 2:3] * buf_ref[slot, 0] + route_ref[:, 3:4] * buf_ref[slot, 1]


def _moe_scratch(tm, d):
    return [pltpu.VMEM((2, 2, tm, d), F32), pltpu.SemaphoreType.DMA((2,))]


def _norm_proj_body(s0_ref, s1_ref, x_ref, route_ref, ys_ref, g_ref, w_ref, x_out_ref, o_ref, buf_ref, sems, *, col_chunk):
    x = _moe_residual(x_ref, route_ref, ys_ref, s0_ref, s1_ref, buf_ref, sems, 0)
    x_out_ref[...] = x
    hb = _rms(x, g_ref[...]).astype(BF16)
    for c0 in range(0, o_ref.shape[1], col_chunk):
        o_ref[:, c0:c0 + col_chunk] = _dot(hb, w_ref[:, c0:c0 + col_chunk])


def _norm_proj(x, moe, g, w, tm, col_chunk):
    route, ys, slot0, slot1 = moe
    m, d = x.shape
    n = w.shape[1]
    row = pl.BlockSpec((tm, d), lambda i, *_: (i, 0))
    return pl.pallas_call(
        functools.partial(_norm_proj_body, col_chunk=col_chunk),
        grid_spec=pltpu.PrefetchScalarGridSpec(
            num_scalar_prefetch=2, grid=(m // tm,),
            in_specs=[row, pl.BlockSpec((tm, ROUTE_LANES), lambda i, *_: (i, 0)), pl.BlockSpec(memory_space=pl.ANY),
                      pl.BlockSpec((1, d), lambda i, *_: (0, 0)), pl.BlockSpec((d, n), lambda i, *_: (0, 0))],
            out_specs=[row, pl.BlockSpec((tm, n), lambda i, *_: (i, 0))],
            scratch_shapes=_moe_scratch(tm, d)),
        out_shape=[jax.ShapeDtypeStruct((m, d), F32), jax.ShapeDtypeStruct((m, n), F32)],
        compiler_params=_cparams(1), name="norm_proj",
    )(slot0, slot1, x, route, ys, g.reshape(1, d), w)


def _norm_proj_even_body(x_ref, g_ref, w_ref, o_ref, rm0_ref, rm1_ref, rm2_ref, lane_ref):
    tm = x_ref.shape[0]
    hb = _rms(x_ref[...], g_ref[...]).astype(BF16)
    rm_refs = (rm0_ref, rm1_ref, rm2_ref)
    for c in range(EVEN_IN // A_WIDTH):
        cols = slice(c * A_WIDTH, (c + 1) * A_WIDTH)
        res = _dot(hb, w_ref[:, cols])
        o_ref[:, cols] = res
        if c < 3 * N_DIL:
            g, j = divmod(c, 3)
            dil = DIL_PAIRS[g][1]
            if dil == 1:
                rm_refs[g][0, 0, :, j * A_WIDTH:(j + 1) * A_WIDTH] = res.astype(BF16)
            else:
                for t in range(A_WIDTH // LANES):
                    lane_ref[t] = res[:, t * LANES:(t + 1) * LANES]
                for r in range(dil):
                    part = [lane_ref[t, pl.ds(r, tm // dil, stride=dil), :] for t in range(A_WIDTH // LANES)]
                    rm_refs[g][0, r, :, j * A_WIDTH:(j + 1) * A_WIDTH] = jnp.concatenate(part, axis=1).astype(BF16)


def _norm_proj_even(x, g, w, tm, n_batch, seq):
    m, d = x.shape
    n = w.shape[1]
    tps = seq // tm
    npt = n_batch * tps

    def rm_index(i):
        return (jnp.where(i < npt, i // tps, n_batch), 0, jnp.where(i < npt, i % tps, i - npt), 0)

    rm_shapes = [jax.ShapeDtypeStruct((n_batch + 1, dil, seq // dil, 3 * A_WIDTH), BF16) for _, dil in DIL_PAIRS]
    rm_specs = [pl.BlockSpec((1, dil, tm // dil, 3 * A_WIDTH), rm_index) for _, dil in DIL_PAIRS]
    return pl.pallas_call(
        _norm_proj_even_body,
        grid=(m // tm,),
        in_specs=[pl.BlockSpec((tm, d), lambda i: (i, 0)), pl.BlockSpec((1, d), lambda i: (0, 0)),
                  pl.BlockSpec((d, n), lambda i: (0, 0))],
        out_specs=[pl.BlockSpec((tm, n), lambda i: (i, 0))] + rm_specs,
        out_shape=[jax.ShapeDtypeStruct((m, n), F32)] + rm_shapes,
        scratch_shapes=[pltpu.VMEM((A_WIDTH // LANES, tm, LANES), F32)],
        compiler_params=_cparams(1), name="norm_proj_even",
    )(x, g.reshape(1, d), w)


def _t5_bucket(dist):
    max_exact = NUM_BUCKETS // 2
    d = np.maximum(dist, 1).astype(np.float32)
    large = max_exact + (np.log(d / max_exact) / np.log(MAX_DISTANCE / max_exact)
                         * (NUM_BUCKETS - max_exact)).astype(np.int32)
    large = np.minimum(large, NUM_BUCKETS - 1)
    return np.where(dist < max_exact, dist, large).astype(np.int32)


def _bucket_lookup(tab, buckets):
    onehot = jnp.asarray(buckets[..., None, None] == np.arange(NUM_BUCKETS)[:, None])
    return jnp.sum(jnp.where(onehot, tab.astype(F32), 0.0), axis=-2)


def _prompt_bias(tab, dil):
    qi = np.arange(A_BLOCK)[:, None]
    km = np.arange(2 * A_BLOCK)[None, :]
    delta = A_BLOCK + qi - km
    valid = (delta >= 0) & (delta <= SPAN)
    bias = _bucket_lookup(tab, _t5_bucket(np.clip(delta, 0, SPAN) * dil))
    bias = jnp.where(valid[..., None], bias, NEG_INF)
    return jnp.transpose(bias, (2, 0, 1))


def _attn_prompt_body(q_ref, kp_ref, ko_ref, vp_ref, vo_ref, bias_ref, o_ref, lse_ref):
    first = pl.program_id(2) == 0
    scale = A_HEAD_DIM ** -0.5
    n_sub = q_ref.shape[2] // A_BLOCK
    k_all = jnp.concatenate([kp_ref[0, 0], ko_ref[0, 0]], axis=0)
    v_all = jnp.concatenate([vp_ref[0, 0], vo_ref[0, 0]], axis=0)
    km = lax.broadcasted_iota(jnp.int32, (1, 2 * A_BLOCK), 1)
    no_prev = jnp.logical_and(first, km < A_BLOCK)
    heads_per_tile = LANES // A_HEAD_DIM
    head_of_lane = lax.broadcasted_iota(jnp.int32, (1, LANES), 1) // A_HEAD_DIM
    for sub in range(n_sub):
        rows = slice(sub * A_BLOCK, (sub + 1) * A_BLOCK)
        keys = slice(sub * A_BLOCK, (sub + 2) * A_BLOCK)
        for t in range(A_WIDTH // LANES):
            sl = slice(t * LANES, (t + 1) * LANES)
            q_t, k_t, v_t = q_ref[0, 0, rows, sl], k_all[keys, sl], v_all[keys, sl]
            o_t = jnp.zeros((A_BLOCK, LANES), F32)
            lse_t = jnp.zeros((A_BLOCK, LANES), F32)
            for j in range(heads_per_tile):
                mine = head_of_lane == j
                s = _dot_nt(jnp.where(mine, q_t, jnp.zeros_like(q_t)), k_t) * scale + bias_ref[t * heads_per_tile + j]
                if sub == 0:
                    s = jnp.where(no_prev, NEG_INF, s)
                m = jnp.max(s, axis=-1, keepdims=True)
                p = jnp.exp(s - m)
                den = jnp.sum(p, axis=-1, keepdims=True)
                o_t = jnp.where(mine, _dot(p.astype(BF16), v_t) / den, o_t)
                lse_t = jnp.where(mine, m + jnp.log(den), lse_t)
            o_ref[0, 0, rows, sl] = o_t
            lse_ref[0, 0, rows, sl] = lse_t


def _attn_prompt(qkv_rm, tab, g, dil, n_batch, seq):
    sub_len = seq // dil
    n_sub = _largest_divisor(sub_len // A_BLOCK, (4, 2, 1))
    rows = n_sub * A_BLOCK
    nb = sub_len // rows

    def spec(j, prev):
        if prev:
            return pl.BlockSpec((1, 1, A_BLOCK, A_WIDTH), lambda b, r, i: (b, r, jnp.maximum(i * n_sub - 1, 0), j))
        return pl.BlockSpec((1, 1, rows, A_WIDTH), lambda b, r, i: (b, r, i, j))

    out_spec = pl.BlockSpec((1, 1, rows, A_WIDTH), lambda b, r, i: (b, r, i, 0))
    out_sds = jax.ShapeDtypeStruct((n_batch, dil, sub_len, A_WIDTH), F32)
    return pl.pallas_call(
        _attn_prompt_body,
        grid=(n_batch, dil, nb),
        in_specs=[spec(0, False), spec(1, True), spec(1, False), spec(2, True), spec(2, False),
                  pl.BlockSpec((A_HEADS, A_BLOCK, 2 * A_BLOCK), lambda b, r, i: (0, 0, 0))],
        out_specs=[out_spec, out_spec], out_shape=[out_sds, out_sds],
        compiler_params=_cparams(3), name="attn_prompt_g%d" % g,
    )(qkv_rm, qkv_rm, qkv_rm, qkv_rm, qkv_rm, _prompt_bias(tab, dil))


def _sample_bias(tab, dil, cache_len, key_index, t_new):
    t = np.arange(t_new)[:, None]
    dist = cache_len + t - key_index[None, :]
    valid = (dist >= 0) & (dist % dil == 0) & (dist <= SPAN * dil)
    bias = _bucket_lookup(tab, _t5_bucket(np.clip(dist, 0, SPAN * dil)))
    bias = jnp.where(valid[..., None], bias, NEG_INF)
    return jnp.transpose(bias, (2, 0, 1)).reshape(A_HEADS * t_new, key_index.shape[0])


def _attn_sample_body(new_ref, c0_ref, c1_ref, c2_ref, b0_ref, b1_ref, b2_ref, bn_ref, *out_refs, t_new):
    scale = A_HEAD_DIM ** -0.5
    rows = A_HEADS * t_new
    head_of_row = lax.broadcasted_iota(jnp.int32, (rows, A_WIDTH), 0) // t_new
    head_of_lane = lax.broadcasted_iota(jnp.int32, (rows, A_WIDTH), 1) // A_HEAD_DIM
    own = head_of_row == head_of_lane
    caches = (c0_ref, c1_ref, c2_ref)
    biases = (b0_ref, b1_ref, b2_ref)
    for g in range(N_DIL):
        base = g * 3 * A_WIDTH
        cache_len = caches[g].shape[-1]
        q = new_ref[:, base:base + A_WIDTH]
        k_new = new_ref[:, base + A_WIDTH:base + 2 * A_WIDTH].astype(BF16)
        v_new = new_ref[:, base + 2 * A_WIDTH:base + 3 * A_WIDTH].astype(BF16)
        q_bd = jnp.where(own, jnp.concatenate([q] * A_HEADS, axis=0), 0.0).astype(BF16)
        k_t = caches[g][0, 0, 0].reshape(A_WIDTH, cache_len).astype(BF16)
        v_t = caches[g][0, 0, 1].reshape(A_WIDTH, cache_len).astype(BF16)
        s_c = _dot(q_bd, k_t) * scale + biases[g][...]
        s_n = _dot_nt(q_bd, k_new) * scale + bn_ref[g]
        m = jnp.maximum(s_c.max(axis=-1, keepdims=True), s_n.max(axis=-1, keepdims=True))
        p_c = jnp.exp(s_c - m)
        p_n = jnp.exp(s_n - m)
        den = jnp.sum(p_c, axis=-1, keepdims=True) + jnp.sum(p_n, axis=-1, keepdims=True)
        acc = _dot_nt(p_c.astype(BF16), v_t) + _dot(p_n.astype(BF16), v_new)
        acc = jnp.where(own, acc / den, 0.0)
        lse = jnp.where(own, m + jnp.log(den), 0.0)
        o = acc[0:t_new]
        l = lse[0:t_new]
        for h in range(1, A_HEADS):
            o = o + acc[h * t_new:(h + 1) * t_new]
            l = l + lse[h * t_new:(h + 1) * t_new]
        out_refs[2 * g][...] = o
        out_refs[2 * g + 1][...] = l


def _attn_sample(proj, caches, t5_bias, n_seq, t_new, row0):
    cache_lens = [c.shape[2] for c in caches]
    caches_t = [jnp.transpose(c, (0, 1, 3, 4, 5, 2)) for c in caches]
    biases = [_sample_bias(t5_bias[:, g], DIL_PAIRS[g][1], cache_lens[g], np.arange(cache_lens[g]), t_new)
              for g in range(N_DIL)]
    bias_new = jnp.stack([_sample_bias(t5_bias[:, g], DIL_PAIRS[g][1], cache_lens[g],
                                       cache_lens[g] + np.arange(t_new), t_new) for g in range(N_DIL)])
    blk0 = row0 // t_new
    full = lambda a: pl.BlockSpec(a.shape, lambda b: (0,) * a.ndim)
    out_spec = pl.BlockSpec((t_new, A_WIDTH), lambda b: (b, 0))
    cache_specs = [pl.BlockSpec((1, 1, 2, A_HEADS, A_HEAD_DIM, n), lambda b: (0, b, 0, 0, 0, 0)) for n in cache_lens]
    return pl.pallas_call(
        functools.partial(_attn_sample_body, t_new=t_new),
        grid=(n_seq,),
        in_specs=[pl.BlockSpec((t_new, N_DIL * 3 * A_WIDTH), lambda b: (blk0 + b, 0))] + cache_specs
                 + [full(biases[0]), full(biases[1]), full(biases[2]), full(bias_new)],
        out_specs=[out_spec] * (2 * N_DIL),
        out_shape=[jax.ShapeDtypeStruct((n_seq * t_new, A_WIDTH), F32)] * (2 * N_DIL),
        compiler_params=_cparams(1), name="attn_sample",
    )(proj, *caches_t, *biases, bias_new)


def _shift_rows(x, s, fill, axis):
    t = lax.broadcasted_iota(jnp.int32, x.shape, axis)
    return jnp.where(t >= s, pltpu.roll(x, s, axis), fill)


def _linear_scan(a, b, axis):
    n = a.shape[axis]
    s = 1
    while s < n:
        b = b + a * _shift_rows(b, s, 0.0, axis)
        a = a * _shift_rows(a, s, 1.0, axis)
        s *= 2
    return a, b


def _rglru_gates(xc, wa_ref, wx_ref, vec_ref):
    xcb = xc.astype(BF16)
    r = jax.nn.sigmoid(_dot(xcb, wa_ref[...]) + vec_ref[1:2])
    ig = jax.nn.sigmoid(_dot(xcb, wx_ref[...]) + vec_ref[2:3])
    log_a = -RG_C * r * _softplus(-vec_ref[3:4])
    a = jnp.exp(log_a)
    b = jnp.sqrt(1.0 - jnp.exp(2.0 * log_a)) * (ig * xc)
    return a, b


def _gelu(x):
    return 0.5 * x * (1.0 + jnp.tanh(math.sqrt(2.0 / math.pi) * (x + 0.044715 * (x * x * x))))


def _rglru_prompt_body(xb_ref, gb_ref, cw_ref, wa_ref, wx_ref, vec_ref, y_ref, hl_ref, tail_ref, h_ref):
    @pl.when(pl.program_id(1) == 0)
    def _():
        tail_ref[...] = jnp.zeros_like(tail_ref)
        h_ref[...] = jnp.zeros_like(h_ref)

    x = xb_ref[...]
    tt = x.shape[0]
    xe = jnp.concatenate([tail_ref[...], x], axis=0)
    xc = vec_ref[0:1] + x * cw_ref[CONV_W - 1:CONV_W]
    for j in range(1, CONV_W):
        xc = xc + pltpu.roll(xe, j, 0)[8:8 + tt] * cw_ref[CONV_W - 1 - j:CONV_W - j]
    tail_ref[...] = x[tt - 8:tt]
    a, b = _rglru_gates(xc, wa_ref, wx_ref, vec_ref)
    a_cum, h = _linear_scan(a, b, 0)
    h = h + a_cum * h_ref[...]
    h_ref[...] = h[tt - 1:tt]
    hl_ref[0] = h[tt - 1:tt]
    y_ref[...] = h * _gelu(gb_ref[...])


def _rglru_sample_body(xb_ref, gb_ref, c0_ref, h0_ref, cw_ref, wa_ref, wx_ref, vec_ref, prev_ref, y_ref, hl_ref, *, t_new):
    del prev_ref
    x = xb_ref[...]
    rows = x.shape[0]
    ns = rows // t_new
    x3 = x.reshape(ns, t_new, RNN_WIDTH)
    xe = jnp.concatenate([c0_ref[...], x3], axis=1)
    xc = vec_ref[0:1] + x3 * cw_ref[CONV_W - 1:CONV_W]
    for j in range(1, CONV_W):
        xc = xc + pltpu.roll(xe, j, 1)[:, 8:8 + t_new] * cw_ref[CONV_W - 1 - j:CONV_W - j]
    a, b = _rglru_gates(xc.reshape(rows, RNN_WIDTH), wa_ref, wx_ref, vec_ref)
    a_cum, h = _linear_scan(a.reshape(ns, t_new, RNN_WIDTH), b.reshape(ns, t_new, RNN_WIDTH), 1)
    h = h + a_cum * h0_ref[...][:, None, :]
    hl_ref[...] = h[:, t_new - 1, :]
    y_ref[...] = h.reshape(rows, RNN_WIDTH) * _gelu(gb_ref[...])


def _block_diag(w):
    nb, bi, bo = w.shape
    eye = jnp.eye(nb, dtype=w.dtype)
    return (w[:, :, None, :] * eye[:, None, :, None]).reshape(nb * bi, nb * bo)


def _rglru(proj, conv0, h0, conv_w, conv_b, wa, ba, wx, bx, lam, n_batch, seq, n_seq, t_new):
    m_total = proj.shape[0]
    mp = n_batch * seq
    wa_bd = _block_diag(wa).astype(BF16)
    wx_bd = _block_diag(wx).astype(BF16)
    vec = jnp.stack([conv_b, ba, bx, lam]).astype(F32)
    xcol = XB_COL // RNN_WIDTH
    gcol = GB_COL // RNN_WIDTH
    full2 = lambda a, nd: pl.BlockSpec(a.shape, lambda *_: (0,) * a.ndim)
    tt = _row_tile(seq)
    nt = seq // tt
    w_specs2 = [pl.BlockSpec(a.shape, lambda b, i: (0, 0)) for a in (conv_w, wa_bd, wx_bd, vec)]
    y, hl_p = pl.pallas_call(
        _rglru_prompt_body,
        grid=(n_batch, nt),
        in_specs=[pl.BlockSpec((tt, RNN_WIDTH), lambda b, i: (b * nt + i, xcol)),
                  pl.BlockSpec((tt, RNN_WIDTH), lambda b, i: (b * nt + i, gcol))] + w_specs2,
        out_specs=[pl.BlockSpec((tt, RNN_WIDTH), lambda b, i: (b * nt + i, 0)),
                   pl.BlockSpec((1, 1, RNN_WIDTH), lambda b, i: (b, 0, 0))],
        out_shape=[jax.ShapeDtypeStruct((m_total, RNN_WIDTH), F32),
                   jax.ShapeDtypeStruct((n_batch, 1, RNN_WIDTH), F32)],
        scratch_shapes=[pltpu.VMEM((8, RNN_WIDTH), F32), pltpu.VMEM((1, RNN_WIDTH), F32)],
        compiler_params=_cparams(2), name="rglru_prompt",
    )(proj, proj, conv_w, wa_bd, wx_bd, vec)
    del full2
    ts = 32 if n_seq % 32 == 0 else 8
    rows = ts * t_new
    blk0 = mp // rows
    c0p = jnp.pad(conv0, ((0, 0), (8 - (CONV_W - 1), 0), (0, 0)))
    w_specs1 = [pl.BlockSpec(a.shape, lambda i: (0, 0)) for a in (conv_w, wa_bd, wx_bd, vec)]
    y, hl_s = pl.pallas_call(
        functools.partial(_rglru_sample_body, t_new=t_new),
        grid=(n_seq // ts,),
        in_specs=[pl.BlockSpec((rows, RNN_WIDTH), lambda i: (blk0 + i, xcol)),
                  pl.BlockSpec((rows, RNN_WIDTH), lambda i: (blk0 + i, gcol)),
                  pl.BlockSpec((ts, 8, RNN_WIDTH), lambda i: (i, 0, 0)),
                  pl.BlockSpec((ts, RNN_WIDTH), lambda i: (i, 0))] + w_specs1
                 + [pl.BlockSpec(memory_space=pl.ANY)],
        out_specs=[pl.BlockSpec((rows, RNN_WIDTH), lambda i: (blk0 + i, 0)),
                   pl.BlockSpec((ts, RNN_WIDTH), lambda i: (i, 0))],
        out_shape=[jax.ShapeDtypeStruct((m_total, RNN_WIDTH), F32),
                   jax.ShapeDtypeStruct((n_seq, RNN_WIDTH), F32)],
        input_output_aliases={8: 0},
        compiler_params=_cparams(1), name="rglru_sample",
    )(proj, proj, c0p, h0, conv_w, wa_bd, wx_bd, vec, y)
    return y, hl_p.reshape(n_batch, RNN_WIDTH), hl_s


def _pack_bf16_pair(x):
    w = x.shape[1] // 2
    hi = lax.bitcast_convert_type(x[:, :w].astype(BF16).astype(F32), jnp.uint32)
    lo = lax.bitcast_convert_type(x[:, w:].astype(BF16).astype(F32), jnp.uint32)
    return hi | (lo >> 16)


def _unpack_bf16_pair(p):
    hi = lax.bitcast_convert_type(p & jnp.uint32(0xFFFF0000), F32).astype(BF16)
    lo = lax.bitcast_convert_type(p << 16, F32).astype(BF16)
    return hi, lo


def _route(x1, gf_ref, wr_ref, br_ref, hn_ref, route_ref, counts_ref, run_ref):
    @pl.when(pl.program_id(0) == 0)
    def _():
        run_ref[...] = jnp.zeros_like(run_ref)

    hn = _rms(x1, gf_ref[...])
    hn_ref[...] = _pack_bf16_pair(hn)
    hn_hi = hn.astype(BF16)
    hn_lo = (hn - hn_hi.astype(F32)).astype(BF16)
    logits = _dot(hn_hi, wr_ref[0]) + (_dot(hn_hi, wr_ref[1]) + _dot(hn_lo, wr_ref[0])) + br_ref[...]
    lane = lax.broadcasted_iota(jnp.int32, logits.shape, 1)
    is_coarse = lane < N_GROUPS
    coarse = jnp.where(is_coarse, logits, -jnp.inf)
    cmax = jnp.max(coarse, axis=-1, keepdims=True)
    grp = jnp.min(jnp.where(coarse == cmax, lane, ROUTE_LANES), axis=-1, keepdims=True)
    p_grp = 1.0 / jnp.sum(jnp.where(is_coarse, jnp.exp(logits - cmax), 0.0), axis=-1, keepdims=True)
    expert = lane - N_GROUPS
    in_grp = (lane >= N_GROUPS) & (expert < N_EXPERTS) & (expert // EXPERTS_PER_GROUP == grp)
    fine = jnp.where(in_grp, logits, -jnp.inf)
    v1 = jnp.max(fine, axis=-1, keepdims=True)
    i1 = jnp.min(jnp.where(fine == v1, lane, ROUTE_LANES), axis=-1, keepdims=True)
    fine2 = jnp.where(lane == i1, -jnp.inf, fine)
    v2 = jnp.max(fine2, axis=-1, keepdims=True)
    i2 = jnp.min(jnp.where(fine2 == v2, lane, ROUTE_LANES), axis=-1, keepdims=True)
    e2 = jnp.exp(v2 - v1)
    w1 = p_grp / (1.0 + e2)
    w2 = p_grp * e2 / (1.0 + e2)
    tm = logits.shape[0]
    sel = jnp.where(lane == i1, 1.0, jnp.where(lane == i2, 1.0, 0.0))
    ri = lax.broadcasted_iota(jnp.int32, (tm, tm), 0)
    ci = lax.broadcasted_iota(jnp.int32, (tm, tm), 1)
    earlier = jnp.where(ri > ci, 1.0, 0.0).astype(BF16)
    before = _dot(earlier, sel.astype(BF16)) + run_ref[...]
    r1 = jnp.sum(jnp.where(lane == i1, before, 0.0), axis=-1, keepdims=True)
    r2 = jnp.sum(jnp.where(lane == i2, before, 0.0), axis=-1, keepdims=True)
    run_ref[...] = run_ref[...] + jnp.sum(sel, axis=0, keepdims=True)
    counts_ref[...] = run_ref[...]
    route = jnp.where(lane == 0, (i1 - N_GROUPS).astype(F32), 0.0)
    route = jnp.where(lane == 1, (i2 - N_GROUPS).astype(F32), route)
    route = jnp.where(lane == 2, w1, route)
    route = jnp.where(lane == 3, w2, route)
    route = jnp.where(lane == 4, r1, route)
    route = jnp.where(lane == 5, r2, route)
    route_ref[...] = route


def _merge_groups(os, ls):
    mx = jnp.maximum(jnp.maximum(ls[0], ls[1]), ls[2])
    es = [jnp.exp(l - mx) for l in ls]
    return (es[0] * os[0] + es[1] * os[1] + es[2] * os[2]) / (es[0] + es[1] + es[2])


def _even_out_body(o0, l0, o1, l1, o2, l2, so0, sl0, so1, sl1, so2, sl2, yb_ref, x_ref, wo_ref, gf_ref, wr_ref, br_ref,
                   x1_ref, hn_ref, route_ref, counts_ref, n1o, n1l, n2o, n2l, oa_ref, run_ref, *, n_prompt_tiles):
    i = pl.program_id(0)
    tm = x_ref.shape[0]

    @pl.when(i < n_prompt_tiles)
    def _():
        nat = []
        for src, dst, dil in ((o1, n1o, DIL_PAIRS[1][1]), (l1, n1l, DIL_PAIRS[1][1]),
                              (o2, n2o, DIL_PAIRS[2][1]), (l2, n2l, DIL_PAIRS[2][1])):
            for r in range(dil):
                blk = src[0, r]
                for t in range(A_WIDTH // LANES):
                    dst[t, pl.ds(r, tm // dil, stride=dil), :] = blk[:, t * LANES:(t + 1) * LANES]
            nat.append(jnp.concatenate([dst[t] for t in range(A_WIDTH // LANES)], axis=1))
        oa_ref[...] = _merge_groups((o0[0, 0], nat[0], nat[2]), (l0[0, 0], nat[1], nat[3]))

    @pl.when(i >= n_prompt_tiles)
    def _():
        oa_ref[...] = _merge_groups((so0[...], so1[...], so2[...]), (sl0[...], sl1[...], sl2[...]))

    cat = jnp.concatenate([oa_ref[...], yb_ref[...]], axis=-1).astype(BF16)
    x1 = x_ref[...] + _dot(cat, wo_ref[...])
    x1_ref[...] = x1
    _route(x1, gf_ref, wr_ref, br_ref, hn_ref, route_ref, counts_ref, run_ref)


def _router_weights(rg_w, rg_b, re_w, re_b):
    d = rg_w.shape[0]
    wr = jnp.concatenate([rg_w, re_w.reshape(d, N_EXPERTS)], axis=1)
    br = jnp.concatenate([rg_b, re_b.reshape(N_EXPERTS)])
    pad = ROUTE_LANES - wr.shape[1]
    wr = jnp.pad(wr, ((0, 0), (0, pad))).astype(F32)
    wr_hi = wr.astype(BF16)
    wr_lo = (wr - wr_hi.astype(F32)).astype(BF16)
    return jnp.stack([wr_hi, wr_lo]), jnp.pad(br, (0, pad)).reshape(1, ROUTE_LANES).astype(F32)


def _mix_out_call(body, in_specs, args, consts, m, d, tm, scratch, name):
    const_spec = lambda a: pl.BlockSpec(a.shape, lambda i: (0,) * a.ndim)
    return pl.pallas_call(
        body, grid=(m // tm,), in_specs=list(in_specs) + [const_spec(c) for c in consts],
        out_specs=[pl.BlockSpec((tm, d), lambda i: (i, 0)), pl.BlockSpec((tm, d // 2), lambda i: (i, 0)),
                   pl.BlockSpec((tm, ROUTE_LANES), lambda i: (i, 0)), pl.BlockSpec((1, ROUTE_LANES), lambda i: (0, 0))],
        out_shape=[jax.ShapeDtypeStruct((m, d), F32), jax.ShapeDtypeStruct((m, d // 2), jnp.uint32),
                   jax.ShapeDtypeStruct((m, ROUTE_LANES), F32), jax.ShapeDtypeStruct((1, ROUTE_LANES), F32)],
        scratch_shapes=list(scratch) + [pltpu.VMEM((1, ROUTE_LANES), F32)],
        compiler_params=_cparams(1), name=name,
    )(*args, *consts)


def _even_out(attn_p, attn_s, y_b, x, consts, tm, n_batch, seq):
    m, d = x.shape
    tps = seq // tm
    npt = n_batch * tps
    in_specs = []
    for g, (_, dil) in enumerate(DIL_PAIRS):
        def index(i):
            return (jnp.minimum(i // tps, n_batch - 1), 0, jnp.where(i < npt, i % tps, 0), 0)
        in_specs += [pl.BlockSpec((1, dil, tm // dil, A_WIDTH), index)] * 2
    in_specs += [pl.BlockSpec((tm, A_WIDTH), lambda i: (jnp.maximum(i - npt, 0), 0))] * (2 * N_DIL)
    in_specs += [pl.BlockSpec((tm, RNN_WIDTH), lambda i: (i, 0)), pl.BlockSpec((tm, d), lambda i: (i, 0))]
    scratch = [pltpu.VMEM((A_WIDTH // LANES, tm, LANES), F32)] * 4 + [pltpu.VMEM((tm, A_WIDTH), F32)]
    return _mix_out_call(functools.partial(_even_out_body, n_prompt_tiles=npt), in_specs,
                         list(attn_p) + list(attn_s) + [y_b, x], consts, m, d, tm, scratch, "even_out")


def _odd_out(o_c, proj2, x, consts, tm):
    m, d = x.shape
    in_specs = [pl.BlockSpec((tm, C_V_W), lambda i: (i, 0)),
                pl.BlockSpec((tm, C_V_W), lambda i: (i, Z_COL // C_V_W)),
                pl.BlockSpec((tm, d), lambda i: (i, 0))]
    return _mix_out_call(_odd_out_body, in_specs, [o_c, proj2, x], consts, m, d, tm, [], "odd_out")


def _moe_slots_body(route_ref, start_ref, o_ref):
    lane = lax.broadcasted_iota(jnp.int32, route_ref.shape, 1)
    r = route_ref[...]
    starts = start_ref[...]

    def slot(expert, pos):
        return jnp.sum(jnp.where(lane == expert.astype(jnp.int32), starts, 0.0), axis=-1, keepdims=True) + pos

    s0 = slot(r[:, 0:1], r[:, 4:5])
    s1 = slot(r[:, 1:2], r[:, 5:6])
    o_ref[...] = jnp.where(lane == 0, s0, jnp.where(lane == 1, s1, 0.0)).astype(jnp.int32)


def _moe_dispatch_body(s0_ref, s1_ref, fill_ref, used_ref, hn_ref, xs_ref, zero_ref, fill_sem, row_sem):
    i = pl.program_id(0)
    tm = hn_ref.shape[0]

    def fill_copy(e):
        return pltpu.make_async_copy(zero_ref, xs_ref.at[pl.ds(pl.multiple_of(fill_ref[e], MOE_TILE), MOE_TILE)], fill_sem)

    @pl.when(i == 0)
    def _():
        zero_ref[...] = jnp.zeros_like(zero_ref)
        for e in range(N_EXPERTS):
            @pl.when(used_ref[e] > 0)
            def _():
                fill_copy(e).start()
        for e in range(N_EXPERTS):
            @pl.when(used_ref[e] > 0)
            def _():
                fill_copy(e).wait()

    base = i * tm

    def scatter_row(j, carry):
        _row_copy(hn_ref, j, xs_ref, s0_ref[base + j], row_sem).start()
        _row_copy(hn_ref, j, xs_ref, s1_ref[base + j], row_sem).start(priority=1)
        return carry

    lax.fori_loop(0, tm, scatter_row, 0, unroll=8)
    for _ in range(2):
        pltpu.make_async_copy(hn_ref, xs_ref.at[pl.ds(0, tm)], row_sem).wait()


def _moe_ffn_body(te_ref, nu_ref, xs_ref, wg_ref, wu_ref, wd_ref, o_ref):
    del te_ref

    @pl.when(pl.program_id(0) < nu_ref[0])
    def _():
        xa, xb = _unpack_bf16_pair(xs_ref[...])
        half = xa.shape[1]
        hg = _dot(xa, wg_ref[0, :half]) + _dot(xb, wg_ref[0, half:])
        hu = _dot(xa, wu_ref[0, :half]) + _dot(xb, wu_ref[0, half:])
        o_ref[...] = _dot((_silu(hg) * hu).astype(BF16), wd_ref[0])

    @pl.when(pl.program_id(0) >= nu_ref[0])
    def _():
        o_ref[...] = jnp.zeros_like(o_ref)


def _moe(hn, route, counts, w_gate, w_up, w_down, tm):
    m = hn.shape[0]
    d = w_gate.shape[1]
    tile = MOE_TILE
    n_slots = 2 * m + N_EXPERTS * tile
    n_tiles = n_slots // tile
    assert n_slots < 1 << 24, "slots must stay exact in f32"
    cnt = counts[0, N_GROUPS:N_GROUPS + N_EXPERTS].astype(jnp.int32)
    padded = ((cnt + tile - 1) // tile) * tile
    pad_end = jnp.cumsum(padded)
    pad_start = (pad_end - padded).astype(jnp.int32)
    starts = jnp.zeros((1, ROUTE_LANES), F32).at[0, :N_EXPERTS].set(pad_start.astype(F32))
    ts = _largest_divisor(m, (1024, 512, 256, 128, 64, 32, 16, 8))
    slots = pl.pallas_call(
        _moe_slots_body, grid=(m // ts,),
        in_specs=[pl.BlockSpec((ts, ROUTE_LANES), lambda i: (i, 0)), pl.BlockSpec((1, ROUTE_LANES), lambda i: (0, 0))],
        out_specs=pl.BlockSpec((ts, ROUTE_LANES), lambda i: (i, 0)),
        out_shape=jax.ShapeDtypeStruct((m, ROUTE_LANES), jnp.int32),
        compiler_params=_cparams(1), name="moe_slots",
    )(route, starts)
    slot0 = slots[:, 0]
    slot1 = slots[:, 1]
    tile_start = jnp.arange(n_tiles, dtype=jnp.int32) * tile
    tile_expert = jnp.sum((tile_start[:, None] >= pad_end[None, :]).astype(jnp.int32), axis=1)
    tile_expert = jnp.minimum(tile_expert, N_EXPERTS - 1).astype(jnp.int32)
    n_used = (pad_end[-1:] // tile).astype(jnp.int32)
    fill_start = jnp.maximum(pad_end - tile, 0).astype(jnp.int32)

    xs = pl.pallas_call(
        _moe_dispatch_body,
        grid_spec=pltpu.PrefetchScalarGridSpec(
            num_scalar_prefetch=4, grid=(m // tm,),
            in_specs=[pl.BlockSpec((tm, d // 2), lambda i, *_: (i, 0))],
            out_specs=pl.BlockSpec(memory_space=pl.ANY),
            scratch_shapes=[pltpu.VMEM((tile, d // 2), jnp.uint32), pltpu.SemaphoreType.DMA(()),
                            pltpu.SemaphoreType.DMA(())]),
        out_shape=jax.ShapeDtypeStruct((n_slots, d // 2), jnp.uint32),
        compiler_params=_cparams(1), name="moe_dispatch",
    )(slot0, slot1, fill_start, cnt, hn)

    def used(i, nu):
        return jnp.minimum(i, jnp.maximum(nu[0] - 1, 0))

    ys = pl.pallas_call(
        _moe_ffn_body,
        grid_spec=pltpu.PrefetchScalarGridSpec(
            num_scalar_prefetch=2, grid=(n_tiles,),
            in_specs=[pl.BlockSpec((tile, d // 2), lambda i, te, nu: (used(i, nu), 0)),
                      pl.BlockSpec((1, d, EXPERT_FF), lambda i, te, nu: (te[used(i, nu)], 0, 0)),
                      pl.BlockSpec((1, d, EXPERT_FF), lambda i, te, nu: (te[used(i, nu)], 0, 0)),
                      pl.BlockSpec((1, EXPERT_FF, d), lambda i, te, nu: (te[used(i, nu)], 0, 0))],
            out_specs=pl.BlockSpec((tile, d), lambda i, te, nu: (i, 0))),
        out_shape=jax.ShapeDtypeStruct((n_slots, d), F32),
        compiler_params=_cparams(1), name="moe_ffn",
    )(tile_expert, n_used, xs, w_gate, w_up, w_down)

    return route, ys, slot0, slot1


def _gdn_prep_math(xe, tt, cw_ref, bg, av_ref):
    acc = xe[8:8 + tt] * cw_ref[CONV_W - 1:CONV_W]
    for j in range(1, CONV_W):
        acc = acc + pltpu.roll(xe, j, 0)[8:8 + tt] * cw_ref[CONV_W - 1 - j:CONV_W - j]
    qkv = _silu(acc)
    outs = []
    for h in range(2 * C_QK_HEADS):
        xh = qkv[:, h * C_DK:(h + 1) * C_DK]
        xh = xh * lax.rsqrt(jnp.sum(xh * xh, axis=-1, keepdims=True) + EPS)
        if h < C_QK_HEADS:
            xh = xh * (C_DK ** -0.5)
        outs.append(xh)
    qk = jnp.concatenate(outs, axis=-1)
    v = qkv[:, 2 * C_QK_W:]
    lane = lax.broadcasted_iota(jnp.int32, bg.shape, 1)
    gdec = -jnp.exp(av_ref[0:1]) * _softplus(bg + av_ref[1:2])
    bgo = jnp.where(lane < C_V_HEADS, jax.nn.sigmoid(bg), gdec)
    return qk, v, bgo


def _unit_lower_inverses(l_bds, size):
    n = l_bds[0].shape[0]
    ri = lax.broadcasted_iota(jnp.int32, (n, n), 0)
    ci = lax.broadcasted_iota(jnp.int32, (n, n), 1)
    eye = jnp.where(ri == ci, 1.0, 0.0).astype(F32)
    base = ri // GDN_BASE == ci // GDN_BASE
    ps = [jnp.where(base, -l, 0.0) for l in l_bds]
    ts = [eye + p for p in ps]
    s = 2
    while s < GDN_BASE:
        ps = [_dot(p.astype(BF16), p.astype(BF16)) for p in ps]
        ts = [t + _dot(t.astype(BF16), p.astype(BF16)) for t, p in zip(ts, ps)]
        s *= 2
    lbs = [l.astype(BF16) for l in l_bds]
    tbs = [t.astype(BF16) for t in ts]
    s = GDN_BASE
    while s < size:
        lower_left = ((ri // s) % 2 == 1) & (ci // s == ri // s - 1)
        mids = [_dot(jnp.where(lower_left, lb, jnp.zeros_like(lb)), tb).astype(BF16) for lb, tb in zip(lbs, tbs)]
        tbs = [tb - _dot(tb, mid).astype(BF16) for tb, mid in zip(tbs, mids)]
        s *= 2
    return tbs


def _gdn_body(*refs, chunk, pack, n_par, has_state):
    c = chunk
    x_refs = refs[0:2 * n_par:2]
    bg_refs = refs[1:2 * n_par:2]
    pos = 2 * n_par
    if has_state:
        c0_ref = refs[pos]
        pos += 1
    cw_ref, av_ref, nw_ref = refs[pos:pos + 3]
    pos += 3
    if has_state:
        s0_ref = refs[pos]
        pos += 2
    o_ref, sout_ref, s_ref = refs[pos:pos + 3]
    tail_ref = None if has_state else refs[pos + 3]

    @pl.when(pl.program_id(1) == 0)
    def _():
        if has_state:
            s_ref[...] = s0_ref[...]
        else:
            s_ref[...] = jnp.zeros_like(s_ref)
            tail_ref[...] = jnp.zeros_like(tail_ref)

    rep = C_V_HEADS // C_QK_HEADS
    ti = lax.broadcasted_iota(jnp.int32, (c, c), 0)
    tj = lax.broadcasted_iota(jnp.int32, (c, c), 1)
    tril = jnp.where(ti >= tj, 1.0, 0.0).astype(F32)
    n = pack * c
    ri = lax.broadcasted_iota(jnp.int32, (n, n), 0)
    ci = lax.broadcasted_iota(jnp.int32, (n, n), 1)
    same = ri // c == ci // c
    incl = same & (ri >= ci)
    strict = same & (ri > ci)

    def col(a, heads):
        return jnp.concatenate([a[:, h:h + 1] for h in heads], axis=0)

    items, g_ends = [], []
    l_bds, qk_bds, rhss, qgs, kends = [], [], [], [], []
    for k in range(n_par):
        x = x_refs[k][...]
        prev = c0_ref[k] if has_state else tail_ref[k]
        if not has_state:
            tail_ref[k] = x[c - 8:c]
        qk, v, bgo = _gdn_prep_math(jnp.concatenate([prev, x], axis=0), c, cw_ref, bg_refs[k][...], av_ref)
        beta = bgo[:, 0:C_V_HEADS]
        gc = _dot_f32(tril, bgo[:, C_V_HEADS:2 * C_V_HEADS])
        g_last = gc[c - 1:c]
        gam = jnp.exp(gc)
        kdec = jnp.exp(g_last - gc)
        g_ends.append(jnp.exp(g_last))
        for p0 in range(0, C_V_HEADS, pack):
            heads = range(p0, p0 + pack)
            items.append((k, heads))
            g_col = col(gc, heads)
            g_row = jnp.sum(jnp.where(ri == ci, g_col, 0.0), axis=0, keepdims=True)
            b_col = col(beta, heads)
            gam_col = col(gam, heads)
            k_st = jnp.concatenate([qk[:, C_QK_W + (h // rep) * C_DK:C_QK_W + (h // rep + 1) * C_DK] for h in heads], axis=0)
            q_st = jnp.concatenate([qk[:, (h // rep) * C_DK:(h // rep + 1) * C_DK] for h in heads], axis=0)
            v_st = jnp.concatenate([v[:, h * C_DV:(h + 1) * C_DV] for h in heads], axis=0)
            kb = k_st.astype(BF16)
            decay = jnp.exp(jnp.where(incl, g_col - g_row, -jnp.inf))
            l_bds.append(jnp.where(strict, b_col * _dot_nt(kb, kb) * decay, 0.0))
            qk_bds.append((_dot_nt(q_st.astype(BF16), kb) * decay).astype(BF16))
            rhss.append(jnp.concatenate([b_col * v_st, (b_col * gam_col) * k_st], axis=1).astype(BF16))
            qgs.append(q_st * gam_col)
            kends.append((k_st * col(kdec, heads)).astype(BF16))
    t_invs = _unit_lower_inverses(l_bds, c)
    uws = [_dot(t.astype(BF16), rhs) for t, rhs in zip(t_invs, rhss)]
    u_sts, q_sts = [], []
    for (k, heads), uw, qg in zip(items, uws, qgs):
        us, qs = [], []
        for i, h in enumerate(heads):
            rows = slice(i * c, (i + 1) * c)
            lhs = jnp.concatenate([uw[rows, C_DV:], qg[rows]], axis=0).astype(BF16)
            ws = _dot(lhs, s_ref[k, h].astype(BF16))
            us.append(uw[rows, 0:C_DV] - ws[0:c])
            qs.append(ws[c:])
        u_sts.append(jnp.concatenate(us, axis=0).astype(BF16))
        q_sts.append(jnp.concatenate(qs, axis=0))
    o_sts = [q + _dot(qk_bd, ub) for q, qk_bd, ub in zip(q_sts, qk_bds, u_sts)]
    for (k, heads), o_st, ub, kendb in zip(items, o_sts, u_sts, kends):
        for i, h in enumerate(heads):
            rows = slice(i * c, (i + 1) * c)
            s_ref[k, h] = s_ref[k, h] * g_ends[k][:, h:h + 1] + _dot_tn(kendb[rows], ub[rows])
            o_h = _rms(o_st[rows], nw_ref[...])
            if has_state:
                o_ref[k * c:(k + 1) * c, h * C_DV:(h + 1) * C_DV] = o_h
            else:
                o_ref[k, 0, :, h * C_DV:(h + 1) * C_DV] = o_h

    @pl.when(pl.program_id(1) == pl.num_programs(1) - 1)
    def _():
        sout_ref[...] = s_ref[...]


def _largest_divisor(n, candidates):
    return next(c for c in candidates if n % c == 0)


def _gdn(proj, conv0, conv_w, a_log, dt_bias, onorm_w, s0, n_batch, seq, n_seq, t_new, chunk):
    mp = n_batch * seq
    nc = seq // chunk
    av = jnp.zeros((2, 128), F32)
    av = av.at[0, C_V_HEADS:2 * C_V_HEADS].set(a_log).at[1, C_V_HEADS:2 * C_V_HEADS].set(dt_bias)
    nw = onorm_w.reshape(1, C_DV).astype(F32)
    bg_col = BG_COL // 128
    state = (C_V_HEADS, C_DK, C_DV)
    consts = [conv_w, av, nw]
    const_specs = [pl.BlockSpec(a.shape, lambda b, i: (0, 0)) for a in consts]

    par_p = _largest_divisor(n_batch, (2, 1))
    seq_specs = []
    for k in range(par_p):
        seq_specs += [pl.BlockSpec((chunk, C_CONV_DIM), lambda b, i, k=k: ((b * par_p + k) * nc + i, 0)),
                      pl.BlockSpec((chunk, 128), lambda b, i, k=k: ((b * par_p + k) * nc + i, bg_col))]
    extra = -(-(n_seq * t_new) // seq)
    o4, s_p = pl.pallas_call(
        functools.partial(_gdn_body, chunk=chunk, pack=256 // chunk, n_par=par_p, has_state=False),
        grid=(n_batch // par_p, nc),
        in_specs=seq_specs + const_specs,
        out_specs=[pl.BlockSpec((par_p, 1, chunk, C_V_W), lambda b, i: (b, i, 0, 0)),
                   pl.BlockSpec((par_p,) + state, lambda b, i: (b, 0, 0, 0))],
        out_shape=[jax.ShapeDtypeStruct((n_batch + extra, nc, chunk, C_V_W), F32),
                   jax.ShapeDtypeStruct((n_batch,) + state, F32)],
        scratch_shapes=[pltpu.VMEM((par_p,) + state, F32), pltpu.VMEM((par_p, 8, C_CONV_DIM), F32)],
        compiler_params=_cparams(2), name="gdn_prompt",
    )(*([proj, proj] * par_p), *consts)
    o = o4.reshape((n_batch + extra) * seq, C_V_W)

    par_s = _largest_divisor(n_seq, (4, 2, 1))
    blk0 = mp // t_new
    c0p = jnp.pad(conv0, ((0, 0), (8 - (CONV_W - 1), 0), (0, 0)))
    seq_specs = []
    for k in range(par_s):
        seq_specs += [pl.BlockSpec((t_new, C_CONV_DIM), lambda b, i, k=k: (blk0 + b * par_s + k, 0)),
                      pl.BlockSpec((t_new, 128), lambda b, i, k=k: (blk0 + b * par_s + k, bg_col))]
    n_in = 2 * par_s + 1 + len(consts) + 1
    o, s_s = pl.pallas_call(
        functools.partial(_gdn_body, chunk=t_new, pack=C_V_HEADS, n_par=par_s, has_state=True),
        grid=(n_seq // par_s, 1),
        in_specs=seq_specs + [pl.BlockSpec((par_s, 8, C_CONV_DIM), lambda b, i: (b, 0, 0))] + const_specs
                 + [pl.BlockSpec((par_s,) + state, lambda b, i: (b, 0, 0, 0)), pl.BlockSpec(memory_space=pl.ANY)],
        out_specs=[pl.BlockSpec((par_s * t_new, C_V_W), lambda b, i: (blk0 // par_s + b, 0)),
                   pl.BlockSpec((par_s,) + state, lambda b, i: (b, 0, 0, 0))],
        out_shape=[jax.ShapeDtypeStruct(o.shape, F32), jax.ShapeDtypeStruct((n_seq,) + state, F32)],
        scratch_shapes=[pltpu.VMEM((par_s,) + state, F32)],
        input_output_aliases={n_in: 0},
        compiler_params=_cparams(2), name="gdn_sample",
    )(*([proj, proj] * par_s), c0p, *consts, s0, o)
    return o, s_p, s_s


def _odd_out_body(o_ref, z_ref, x_ref, wo_ref, gf_ref, wr_ref, br_ref, x1_ref, hn_ref, route_ref, counts_ref, run_ref):
    y = (o_ref[...] * _silu(z_ref[...])).astype(BF16)
    x1 = x_ref[...] + _dot(y, wo_ref[...])
    x1_ref[...] = x1
    _route(x1, gf_ref, wr_ref, br_ref, hn_ref, route_ref, counts_ref, run_ref)


def _final_body(s0_ref, s1_ref, x_ref, route_ref, ys_ref, g_ref, y_ref, buf_ref, sems, *, tile0):
    x = _moe_residual(x_ref, route_ref, ys_ref, s0_ref, s1_ref, buf_ref, sems, tile0)
    y_ref[...] = _rms(x, g_ref[...])


def _final_norm(x, moe, g, tm, row0, n_rows):
    route, ys, slot0, slot1 = moe
    d = x.shape[1]
    blk0 = row0 // tm
    return pl.pallas_call(
        functools.partial(_final_body, tile0=blk0),
        grid_spec=pltpu.PrefetchScalarGridSpec(
            num_scalar_prefetch=2, grid=(n_rows // tm,),
            in_specs=[pl.BlockSpec((tm, d), lambda i, *_: (blk0 + i, 0)),
                      pl.BlockSpec((tm, ROUTE_LANES), lambda i, *_: (blk0 + i, 0)),
                      pl.BlockSpec(memory_space=pl.ANY), pl.BlockSpec((1, d), lambda i, *_: (0, 0))],
            out_specs=pl.BlockSpec((tm, d), lambda i, *_: (i, 0)),
            scratch_shapes=_moe_scratch(tm, d)),
        out_shape=jax.ShapeDtypeStruct((n_rows, d), F32),
        compiler_params=_cparams(1), name="final_norm",
    )(slot0, slot1, x, route, ys, g.reshape(1, d))


def kernel(x_prompt, x_sample, cache_a_g0_kv, cache_a_g1_kv, cache_a_g2_kv, state_b_h, state_b_conv, state_c_S, state_c_conv, t5_bias, norm_mix, norm_ffn, norm_final, e_w_in, e_conv_w, e_conv_b, e_rg_wa, e_rg_ba, e_rg_wx, e_rg_bx, e_rg_lambda, e_w_out, o_w_in, o_conv_w, o_a_log, o_dt_bias, o_onorm_w, o_w_out, moe_rg_w, moe_rg_b, moe_re_w, moe_re_b, moe_w_gate, moe_w_up, moe_w_down):
    n_batch, seq, d = x_prompt.shape
    n_seq, t_new, _ = x_sample.shape
    mp = n_batch * seq
    ms = n_seq * t_new
    m = mp + ms
    assert t_new == 8 and seq % (DIL_PAIRS[2][1] * A_BLOCK) == 0
    assert e_w_in.shape[0] == 1 and o_w_in.shape[0] == 1
    tm = _row_tile(mp, ms)
    x = jnp.concatenate([x_prompt.reshape(mp, d), x_sample.reshape(ms, d)], axis=0)

    def moe_weights(layer):
        shp = (N_EXPERTS, d, EXPERT_FF)
        return (moe_w_gate[layer].reshape(shp).astype(BF16), moe_w_up[layer].reshape(shp).astype(BF16),
                moe_w_down[layer].reshape(N_EXPERTS, EXPERT_FF, d).astype(BF16))

    def prompt_tail(a, keep, c0, c1):
        return jnp.stack([lax.slice(a, ((b + 1) * seq - keep, c0), ((b + 1) * seq, c1)) for b in range(n_batch)])

    def sample_rows(a, keep, c0, c1):
        return lax.slice(a, (mp, c0), (m, c1)).reshape(n_seq, t_new, c1 - c0)[:, t_new - keep:]

    proj, *qkv_rm = _norm_proj_even(x, norm_mix[0], e_w_in[0].astype(BF16), tm, n_batch, seq)
    attn_p = []
    for g, (_, dil) in enumerate(DIL_PAIRS):
        attn_p.extend(_attn_prompt(qkv_rm[g], t5_bias[:, g], g, dil, n_batch, seq))
    new_a = []
    for g, (win, _) in enumerate(DIL_PAIRS):
        c0 = g * 3 * A_WIDTH + A_WIDTH
        keep = min(win, seq)
        new_a.append(prompt_tail(proj, keep, c0, c0 + 2 * A_WIDTH).reshape(1, n_batch, keep, 2, A_HEADS, A_HEAD_DIM))
        new_a.append(sample_rows(proj, t_new, c0, c0 + 2 * A_WIDTH).reshape(1, n_seq, t_new, 2, A_HEADS, A_HEAD_DIM))
    attn_s = _attn_sample(proj, (cache_a_g0_kv, cache_a_g1_kv, cache_a_g2_kv), t5_bias, n_seq, t_new, mp)
    y_b, bh_p, bh_s = _rglru(proj, state_b_conv[0], state_b_h[0], e_conv_w[0], e_conv_b[0], e_rg_wa[0], e_rg_ba[0],
                             e_rg_wx[0], e_rg_bx[0], e_rg_lambda[0], n_batch, seq, n_seq, t_new)
    wr, br = _router_weights(moe_rg_w[0], moe_rg_b[0], moe_re_w[0], moe_re_b[0])
    x1, hn, route, counts = _even_out(attn_p, attn_s, y_b, x, [e_w_out[0].astype(BF16), norm_ffn[0].reshape(1, d), wr, br],
                                      tm, n_batch, seq)
    y_moe = _moe(hn, route, counts, *moe_weights(0), tm)

    w_in1 = jnp.pad(o_w_in[0], ((0, 0), (0, ODD_IN_PAD - ODD_IN))).astype(BF16)
    x2, proj2 = _norm_proj(x1, y_moe, norm_mix[1], w_in1, tm, 896)
    o_c, cs_p, cs_s = _gdn(proj2, state_c_conv[0], o_conv_w[0], o_a_log[0], o_dt_bias[0], o_onorm_w[0], state_c_S[0],
                           n_batch, seq, n_seq, t_new, 64)
    wr, br = _router_weights(moe_rg_w[1], moe_rg_b[1], moe_re_w[1], moe_re_b[1])
    x3, hn, route, counts = _odd_out(o_c, proj2, x2, [o_w_out[0].astype(BF16), norm_ffn[1].reshape(1, d), wr, br], tm)
    y_moe = _moe(hn, route, counts, *moe_weights(1), tm)
    y_p = _final_norm(x3, y_moe, norm_final, tm, 0, mp).reshape(n_batch, seq, d)
    y_s = _final_norm(x3, y_moe, norm_final, tm, mp, ms).reshape(n_seq, t_new, d)

    keep = CONV_W - 1
    bconv_p = prompt_tail(proj, keep, XB_COL, XB_COL + RNN_WIDTH)[None]
    bconv_s = sample_rows(proj, keep, XB_COL, XB_COL + RNN_WIDTH)[None]
    cconv_p = prompt_tail(proj2, keep, 0, C_CONV_DIM)[None]
    cconv_s = sample_rows(proj2, keep, 0, C_CONV_DIM)[None]
    return (y_p, y_s, *new_a, bh_p[None], bh_s[None], bconv_p, bconv_s, cs_p[None], cs_s[None], cconv_p, cconv_s)
```
